```python
import math
import jax, jax.numpy as jnp
from jax import lax
import numpy as np

D_MODEL = 1024
BATCH = 8
SEQ = 8192
DEPTH = 4

MEM_LEN = 256
D_MIX = 2 * D_MODEL
D_S5 = D_MIX // 2
D_SSD = D_MIX - D_S5
S5_GROUP = 16
S5_GROUPS = D_S5 // S5_GROUP
S5_STATE = 64
SSD_HEADDIM = 64
SSD_HEADS = D_SSD // SSD_HEADDIM
SSD_GROUPS = 4
SSD_HPG = SSD_HEADS // SSD_GROUPS
SSD_STATE = 128
SSD_CONV = 4
SSD_CHUNK = 128
D_CONV_CH = D_SSD + 2 * SSD_GROUPS * SSD_STATE
D_IN_PROJ = D_S5 + D_SSD + D_CONV_CH + SSD_HEADS
XA_HEADS = 4
XA_HEAD_DIM = D_MODEL // XA_HEADS
D_FF = 4 * D_MODEL
EPS = 1e-5

kernel_name = "hybrid_s5_ssd_xattn_trunk"


def rms_norm(x, g):
    xf = x.astype(jnp.float32)
    y = xf * lax.rsqrt(jnp.mean(xf * xf, axis=-1, keepdims=True) + EPS)
    return (y * g.astype(jnp.float32)).astype(x.dtype)


def s5_mixer(u, a_re, a_im, log_dt, b_re, b_im, c_re, c_im, d, w_glu):
    bsz, seq, _ = u.shape
    f32 = jnp.float32
    uf = u.astype(f32).reshape(bsz, seq, S5_GROUPS, S5_GROUP)
    ar, ai = a_re.astype(f32), a_im.astype(f32)
    dt = jnp.exp(log_dt.astype(f32))[:, None]
    mag = jnp.exp(dt * ar)
    abar_r, abar_i = mag * jnp.cos(dt * ai), mag * jnp.sin(dt * ai)
    den = ar * ar + ai * ai
    zr, zi = abar_r - 1.0, abar_i
    fr = (zr * ar + zi * ai) / den
    fi = (zi * ar - zr * ai) / den
    br, bi = b_re.astype(f32), b_im.astype(f32)
    bbar_r = fr[..., None] * br - fi[..., None] * bi
    bbar_i = fr[..., None] * bi + fi[..., None] * br
    drive_r = jnp.einsum('bsgh,gph->bsgp', uf, bbar_r)
    drive_i = jnp.einsum('bsgh,gph->bsgp', uf, bbar_i)
    a_r = jnp.broadcast_to(abar_r, (1, seq) + abar_r.shape)
    a_i = jnp.broadcast_to(abar_i, (1, seq) + abar_i.shape)

    def combine(left, right):
        a1r, a1i, b1r, b1i = left
        a2r, a2i, b2r, b2i = right
        return (a2r * a1r - a2i * a1i,
                a2r * a1i + a2i * a1r,
                a2r * b1r - a2i * b1i + b2r,
                a2r * b1i + a2i * b1r + b2i)

    _, _, s_r, s_i = lax.associative_scan(combine, (a_r, a_i, drive_r, drive_i), axis=1)
    y = (jnp.einsum('bsgp,ghp->bsgh', s_r, c_re.astype(f32))
         - jnp.einsum('bsgp,ghp->bsgh', s_i, c_im.astype(f32))
         + d.astype(f32) * uf)
    y = jax.nn.gelu(y.reshape(bsz, seq, D_S5))
    y = y * jax.nn.sigmoid(y @ w_glu.astype(f32))
    return y.astype(u.dtype)


def segsum(a):
    t = a.shape[-1]
    aa = jnp.broadcast_to(a[..., :, None], a.shape + (t,))
    strict = jnp.tril(jnp.ones((t, t), dtype=bool), -1)
    cs = jnp.cumsum(jnp.where(strict, aa, 0.0), axis=-2)
    incl = jnp.tril(jnp.ones((t, t), dtype=bool), 0)
    return jnp.where(incl, cs, -jnp.inf)


def ssd_chunked(x, a, bm, cm):
    bsz, seq, g, r, p = x.shape
    n = bm.shape[-1]
    nc, L = seq // SSD_CHUNK, SSD_CHUNK
    x = x.reshape(bsz, nc, L, g, r, p)
    bm = bm.reshape(bsz, nc, L, g, n)
    cm = cm.reshape(bsz, nc, L, g, n)
    a = a.reshape(bsz, nc, L, g, r).transpose(0, 3, 4, 1, 2)
    a_cum = jnp.cumsum(a, axis=-1)
    lmat = jnp.exp(segsum(a))
    cb = jnp.einsum('bclgn,bcsgn->bcgls', cm, bm)
    y_diag = jnp.einsum('bcgls,bgrcls,bcsgrp->bclgrp', cb, lmat, x)
    decay_states = jnp.exp(a_cum[..., -1:] - a_cum)
    states = jnp.einsum('bcsgn,bgrcs,bcsgrp->bcgrpn', bm, decay_states, x)
    states = jnp.concatenate([jnp.zeros_like(states[:, :1]), states], axis=1)
    chunk_tot = jnp.pad(a_cum[..., -1], ((0, 0), (0, 0), (0, 0), (1, 0)))
    decay_chunk = jnp.exp(segsum(chunk_tot))
    new_states = jnp.einsum('bgrzc,bcgrpn->bzgrpn', decay_chunk, states)
    states_prev = new_states[:, :-1]
    y_off = jnp.einsum('bclgn,bcgrpn,bgrcl->bclgrp', cm, states_prev, jnp.exp(a_cum))
    return (y_diag + y_off).reshape(bsz, seq, g, r, p)


def ssd_mixer(z, xbc, dt_raw, conv_w, conv_b, dt_bias, a_log, d_skip, norm_g):
    bsz, seq, _ = xbc.shape
    f32 = jnp.float32
    pad = jnp.pad(xbc, ((0, 0), (SSD_CONV - 1, 0), (0, 0)))
    conv = conv_b + sum(pad[:, k:k + seq] * conv_w[k] for k in range(SSD_CONV))
    xbc = jax.nn.silu(conv).astype(f32)
    xs, bm, cm = jnp.split(xbc, [D_SSD, D_SSD + SSD_GROUPS * SSD_STATE], axis=-1)
    xs = xs.reshape(bsz, seq, SSD_GROUPS, SSD_HPG, SSD_HEADDIM)
    bm = bm.reshape(bsz, seq, SSD_GROUPS, SSD_STATE)
    cm = cm.reshape(bsz, seq, SSD_GROUPS, SSD_STATE)
    dt = jax.nn.softplus(dt_raw.astype(f32) + dt_bias.astype(f32))
    dt = dt.reshape(bsz, seq, SSD_GROUPS, SSD_HPG)
    a = -jnp.exp(a_log.astype(f32)).reshape(SSD_GROUPS, SSD_HPG)
    y = ssd_chunked(xs * dt[..., None], dt * a, bm, cm)
    y = y + d_skip.astype(f32).reshape(SSD_GROUPS, SSD_HPG, 1) * xs
    y = y.reshape(bsz, seq, D_SSD).astype(z.dtype)
    return rms_norm(y * jax.nn.silu(z), norm_g)


def cross_attention(h, m, wq, wk, wv, wo):
    bsz, seq, _ = h.shape
    q = (h @ wq).reshape(bsz, seq, XA_HEADS, XA_HEAD_DIM)
    k = (m @ wk).reshape(bsz, -1, XA_HEADS, XA_HEAD_DIM)
    v = (m @ wv).reshape(bsz, -1, XA_HEADS, XA_HEAD_DIM)
    scores = jnp.einsum('bshd,bmhd->bhsm', q, k).astype(jnp.float32) * (XA_HEAD_DIM ** -0.5)
    probs = jax.nn.softmax(scores, axis=-1).astype(v.dtype)
    o = jnp.einsum('bhsm,bmhd->bshd', probs, v).reshape(bsz, seq, D_MODEL)
    return o @ wo


def _fwd_setup_inputs(seed: int = 0) -> dict:
    key = jax.random.key(seed)
    ks = jax.random.split(key, 32)
    nrm = jax.random.normal
    f32 = jnp.float32
    Lr = DEPTH
    n_idx = jnp.arange(S5_STATE, dtype=f32)
    log_dt_lo, log_dt_hi = math.log(1e-3), math.log(1e-1)
    ssd_dt = jnp.exp(jax.random.uniform(ks[12], (Lr, SSD_HEADS), f32, log_dt_lo, log_dt_hi))
    return {
        "x": nrm(ks[0], (BATCH, SEQ, D_MODEL), f32),
        "mem": nrm(ks[1], (BATCH, MEM_LEN, D_MODEL), f32),
        "norm_mix": 1.0 + 0.01 * nrm(ks[2], (Lr, D_MODEL), f32),
        "w_in": nrm(ks[3], (Lr, D_MODEL, D_IN_PROJ), f32) * D_MODEL ** -0.5,
        "s5_a_re": -0.5 + 0.01 * nrm(ks[4], (Lr, S5_GROUPS, S5_STATE), f32),
        "s5_a_im": math.pi * n_idx + 0.01 * nrm(ks[5], (Lr, S5_GROUPS, S5_STATE), f32),
        "s5_log_dt": jax.random.uniform(ks[6], (Lr, S5_GROUPS), f32, log_dt_lo, log_dt_hi),
        "s5_b_re": nrm(ks[7], (Lr, S5_GROUPS, S5_STATE, S5_GROUP), f32) * (2 * S5_GROUP) ** -0.5,
        "s5_b_im": nrm(ks[8], (Lr, S5_GROUPS, S5_STATE, S5_GROUP), f32) * (2 * S5_GROUP) ** -0.5,
        "s5_c_re": nrm(ks[9], (Lr, S5_GROUPS, S5_GROUP, S5_STATE), f32) * S5_STATE ** -0.5,
        "s5_c_im": nrm(ks[10], (Lr, S5_GROUPS, S5_GROUP, S5_STATE), f32) * S5_STATE ** -0.5,
        "s5_d": nrm(ks[11], (Lr, S5_GROUPS, S5_GROUP), f32),
        "s5_w_glu": nrm(ks[13], (Lr, D_S5, D_S5), f32) * D_S5 ** -0.5,
        "ssd_conv_w": nrm(ks[14], (Lr, SSD_CONV, D_CONV_CH), f32) * SSD_CONV ** -0.5,
        "ssd_conv_b": 0.01 * nrm(ks[15], (Lr, D_CONV_CH), f32),
        "ssd_dt_bias": ssd_dt + jnp.log(-jnp.expm1(-ssd_dt)),
        "ssd_a_log": jnp.log(jax.random.uniform(ks[16], (Lr, SSD_HEADS), f32, 1.0, 16.0)),
        "ssd_d": 1.0 + 0.01 * nrm(ks[17], (Lr, SSD_HEADS), f32),
        "ssd_norm": 1.0 + 0.01 * nrm(ks[18], (Lr, D_SSD), f32),
        "w_out": nrm(ks[19], (Lr, D_MIX, D_MODEL), f32) * D_MIX ** -0.5,
        "norm_xattn": 1.0 + 0.01 * nrm(ks[20], (Lr, D_MODEL), f32),
        "norm_mem": 1.0 + 0.01 * nrm(ks[21], (Lr, D_MODEL), f32),
        "xa_wq": nrm(ks[22], (Lr, D_MODEL, D_MODEL), f32) * D_MODEL ** -0.5,
        "xa_wk": nrm(ks[23], (Lr, D_MODEL, D_MODEL), f32) * D_MODEL ** -0.5,
        "xa_wv": nrm(ks[24], (Lr, D_MODEL, D_MODEL), f32) * D_MODEL ** -0.5,
        "xa_wo": nrm(ks[25], (Lr, D_MODEL, D_MODEL), f32) * D_MODEL ** -0.5,
        "norm_mlp": 1.0 + 0.01 * nrm(ks[26], (Lr, D_MODEL), f32),
        "mlp_w1": nrm(ks[27], (Lr, D_MODEL, D_FF), f32) * D_MODEL ** -0.5,
        "mlp_w2": nrm(ks[28], (Lr, D_FF, D_MODEL), f32) * D_FF ** -0.5,
        "norm_final": 1.0 + 0.01 * nrm(ks[29], (D_MODEL,), f32),
    }


def _fwd_reference(x, mem, norm_mix, w_in, s5_a_re, s5_a_im, s5_log_dt, s5_b_re, s5_b_im,
              s5_c_re, s5_c_im, s5_d, s5_w_glu, ssd_conv_w, ssd_conv_b, ssd_dt_bias,
              ssd_a_log, ssd_d, ssd_norm, w_out, norm_xattn, norm_mem, xa_wq, xa_wk,
              xa_wv, xa_wo, norm_mlp, mlp_w1, mlp_w2, norm_final):
    splits = [D_S5, D_S5 + D_SSD, D_S5 + D_SSD + D_CONV_CH]
    for l in range(DEPTH):
        h = rms_norm(x, norm_mix[l])
        proj = h @ w_in[l]
        u, z, xbc, dt_raw = jnp.split(proj, splits, axis=-1)
        y_s5 = s5_mixer(u, s5_a_re[l], s5_a_im[l], s5_log_dt[l], s5_b_re[l], s5_b_im[l],
                        s5_c_re[l], s5_c_im[l], s5_d[l], s5_w_glu[l])
        y_ssd = ssd_mixer(z, xbc, dt_raw, ssd_conv_w[l], ssd_conv_b[l], ssd_dt_bias[l],
                          ssd_a_log[l], ssd_d[l], ssd_norm[l])
        x = x + jnp.concatenate([y_s5, y_ssd], axis=-1) @ w_out[l]
        h = rms_norm(x, norm_xattn[l])
        m = rms_norm(mem, norm_mem[l])
        x = x + cross_attention(h, m, xa_wq[l], xa_wk[l], xa_wv[l], xa_wo[l])
        h = rms_norm(x, norm_mlp[l])
        x = x + jnp.square(jax.nn.relu(h @ mlp_w1[l])) @ mlp_w2[l]
    return rms_norm(x, norm_final)


import jax as _jax
import jax.numpy as _jnp

TWIN_FORMAT = 'train_step'
FWD_PARAMS = ['x', 'mem', 'norm_mix', 'w_in', 's5_a_re', 's5_a_im', 's5_log_dt', 's5_b_re', 's5_b_im', 's5_c_re', 's5_c_im', 's5_d', 's5_w_glu', 'ssd_conv_w', 'ssd_conv_b', 'ssd_dt_bias', 'ssd_a_log', 'ssd_d', 'ssd_norm', 'w_out', 'norm_xattn', 'norm_mem', 'xa_wq', 'xa_wk', 'xa_wv', 'xa_wo', 'norm_mlp', 'mlp_w1', 'mlp_w2', 'norm_final']
TWIN_WEIGHTS = ['norm_mix', 'w_in', 's5_a_re', 's5_a_im', 's5_log_dt', 's5_b_re', 's5_b_im', 's5_c_re', 's5_c_im', 's5_d', 's5_w_glu', 'ssd_conv_w', 'ssd_conv_b', 'ssd_dt_bias', 'ssd_a_log', 'ssd_d', 'ssd_norm', 'w_out', 'norm_xattn', 'norm_mem', 'xa_wq', 'xa_wk', 'xa_wv', 'xa_wo', 'norm_mlp', 'mlp_w1', 'mlp_w2', 'norm_final']
TWIN_DIFF_INPUT = 'x'
TWIN_INPUTS = ['x', 'mem', 'norm_mix', 'w_in', 's5_a_re', 's5_a_im', 's5_log_dt', 's5_b_re', 's5_b_im', 's5_c_re', 's5_c_im', 's5_d', 's5_w_glu', 'ssd_conv_w', 'ssd_conv_b', 'ssd_dt_bias', 'ssd_a_log', 'ssd_d', 'ssd_norm', 'w_out', 'norm_xattn', 'norm_mem', 'xa_wq', 'xa_wk', 'xa_wv', 'xa_wo', 'norm_mlp', 'mlp_w1', 'mlp_w2', 'norm_final', 'loss_target', 'm_norm_mix', 'm_w_in', 'm_s5_a_re', 'm_s5_a_im', 'm_s5_log_dt', 'm_s5_b_re', 'm_s5_b_im', 'm_s5_c_re', 'm_s5_c_im', 'm_s5_d', 'm_s5_w_glu', 'm_ssd_conv_w', 'm_ssd_conv_b', 'm_ssd_dt_bias', 'm_ssd_a_log', 'm_ssd_d', 'm_ssd_norm', 'm_w_out', 'm_norm_xattn', 'm_norm_mem', 'm_xa_wq', 'm_xa_wk', 'm_xa_wv', 'm_xa_wo', 'm_norm_mlp', 'm_mlp_w1', 'm_mlp_w2', 'm_norm_final', 'v_norm_mix', 'v_w_in', 'v_s5_a_re', 'v_s5_a_im', 'v_s5_log_dt', 'v_s5_b_re', 'v_s5_b_im', 'v_s5_c_re', 'v_s5_c_im', 'v_s5_d', 'v_s5_w_glu', 'v_ssd_conv_w', 'v_ssd_conv_b', 'v_ssd_dt_bias', 'v_ssd_a_log', 'v_ssd_d', 'v_ssd_norm', 'v_w_out', 'v_norm_xattn', 'v_norm_mem', 'v_xa_wq', 'v_xa_wk', 'v_xa_wv', 'v_xa_wo', 'v_norm_mlp', 'v_mlp_w1', 'v_mlp_w2', 'v_norm_final']
TWIN_OUTPUTS = ['loss', 'grad_x', 'grad_norm_mix', 'grad_w_in', 'grad_s5_a_re', 'grad_s5_a_im', 'grad_s5_log_dt', 'grad_s5_b_re', 'grad_s5_b_im', 'grad_s5_c_re', 'grad_s5_c_im', 'grad_s5_d', 'grad_s5_w_glu', 'grad_ssd_conv_w', 'grad_ssd_conv_b', 'grad_ssd_dt_bias', 'grad_ssd_a_log', 'grad_ssd_d', 'grad_ssd_norm', 'grad_w_out', 'grad_norm_xattn', 'grad_norm_mem', 'grad_xa_wq', 'grad_xa_wk', 'grad_xa_wv', 'grad_xa_wo', 'grad_norm_mlp', 'grad_mlp_w1', 'grad_mlp_w2', 'grad_norm_final', 'delta_norm_mix', 'delta_w_in', 'delta_s5_a_re', 'delta_s5_a_im', 'delta_s5_log_dt', 'delta_s5_b_re', 'delta_s5_b_im', 'delta_s5_c_re', 'delta_s5_c_im', 'delta_s5_d', 'delta_s5_w_glu', 'delta_ssd_conv_w', 'delta_ssd_conv_b', 'delta_ssd_dt_bias', 'delta_ssd_a_log', 'delta_ssd_d', 'delta_ssd_norm', 'delta_w_out', 'delta_norm_xattn', 'delta_norm_mem', 'delta_xa_wq', 'delta_xa_wk', 'delta_xa_wv', 'delta_xa_wo', 'delta_norm_mlp', 'delta_mlp_w1', 'delta_mlp_w2', 'delta_norm_final', 'new_m_norm_mix', 'new_m_w_in', 'new_m_s5_a_re', 'new_m_s5_a_im', 'new_m_s5_log_dt', 'new_m_s5_b_re', 'new_m_s5_b_im', 'new_m_s5_c_re', 'new_m_s5_c_im', 'new_m_s5_d', 'new_m_s5_w_glu', 'new_m_ssd_conv_w', 'new_m_ssd_conv_b', 'new_m_ssd_dt_bias', 'new_m_ssd_a_log', 'new_m_ssd_d', 'new_m_ssd_norm', 'new_m_w_out', 'new_m_norm_xattn', 'new_m_norm_mem', 'new_m_xa_wq', 'new_m_xa_wk', 'new_m_xa_wv', 'new_m_xa_wo', 'new_m_norm_mlp', 'new_m_mlp_w1', 'new_m_mlp_w2', 'new_m_norm_final', 'new_v_norm_mix', 'new_v_w_in', 'new_v_s5_a_re', 'new_v_s5_a_im', 'new_v_s5_log_dt', 'new_v_s5_b_re', 'new_v_s5_b_im', 'new_v_s5_c_re', 'new_v_s5_c_im', 'new_v_s5_d', 'new_v_s5_w_glu', 'new_v_ssd_conv_w', 'new_v_ssd_conv_b', 'new_v_ssd_dt_bias', 'new_v_ssd_a_log', 'new_v_ssd_d', 'new_v_ssd_norm', 'new_v_w_out', 'new_v_norm_xattn', 'new_v_norm_mem', 'new_v_xa_wq', 'new_v_xa_wk', 'new_v_xa_wv', 'new_v_xa_wo', 'new_v_norm_mlp', 'new_v_mlp_w1', 'new_v_mlp_w2', 'new_v_norm_final']
TWIN_LEAF_KINDS = {'loss': 'loss', 'grad_x': 'grad_x', 'grad_norm_mix': 'grad_w', 'grad_w_in': 'grad_w', 'grad_s5_a_re': 'grad_w', 'grad_s5_a_im': 'grad_w', 'grad_s5_log_dt': 'grad_w', 'grad_s5_b_re': 'grad_w', 'grad_s5_b_im': 'grad_w', 'grad_s5_c_re': 'grad_w', 'grad_s5_c_im': 'grad_w', 'grad_s5_d': 'grad_w', 'grad_s5_w_glu': 'grad_w', 'grad_ssd_conv_w': 'grad_w', 'grad_ssd_conv_b': 'grad_w', 'grad_ssd_dt_bias': 'grad_w', 'grad_ssd_a_log': 'grad_w', 'grad_ssd_d': 'grad_w', 'grad_ssd_norm': 'grad_w', 'grad_w_out': 'grad_w', 'grad_norm_xattn': 'grad_w', 'grad_norm_mem': 'grad_w', 'grad_xa_wq': 'grad_w', 'grad_xa_wk': 'grad_w', 'grad_xa_wv': 'grad_w', 'grad_xa_wo': 'grad_w', 'grad_norm_mlp': 'grad_w', 'grad_mlp_w1': 'grad_w', 'grad_mlp_w2': 'grad_w', 'grad_norm_final': 'grad_w', 'delta_norm_mix': 'delta_w', 'delta_w_in': 'delta_w', 'delta_s5_a_re': 'delta_w', 'delta_s5_a_im': 'delta_w', 'delta_s5_log_dt': 'delta_w', 'delta_s5_b_re': 'delta_w', 'delta_s5_b_im': 'delta_w', 'delta_s5_c_re': 'delta_w', 'delta_s5_c_im': 'delta_w', 'delta_s5_d': 'delta_w', 'delta_s5_w_glu': 'delta_w', 'delta_ssd_conv_w': 'delta_w', 'delta_ssd_conv_b': 'delta_w', 'delta_ssd_dt_bias': 'delta_w', 'delta_ssd_a_log': 'delta_w', 'delta_ssd_d': 'delta_w', 'delta_ssd_norm': 'delta_w', 'delta_w_out': 'delta_w', 'delta_norm_xattn': 'delta_w', 'delta_norm_mem': 'delta_w', 'delta_xa_wq': 'delta_w', 'delta_xa_wk': 'delta_w', 'delta_xa_wv': 'delta_w', 'delta_xa_wo': 'delta_w', 'delta_norm_mlp': 'delta_w', 'delta_mlp_w1': 'delta_w', 'delta_mlp_w2': 'delta_w', 'delta_norm_final': 'delta_w', 'new_m_norm_mix': 'new_m', 'new_m_w_in': 'new_m', 'new_m_s5_a_re': 'new_m', 'new_m_s5_a_im': 'new_m', 'new_m_s5_log_dt': 'new_m', 'new_m_s5_b_re': 'new_m', 'new_m_s5_b_im': 'new_m', 'new_m_s5_c_re': 'new_m', 'new_m_s5_c_im': 'new_m', 'new_m_s5_d': 'new_m', 'new_m_s5_w_glu': 'new_m', 'new_m_ssd_conv_w': 'new_m', 'new_m_ssd_conv_b': 'new_m', 'new_m_ssd_dt_bias': 'new_m', 'new_m_ssd_a_log': 'new_m', 'new_m_ssd_d': 'new_m', 'new_m_ssd_norm': 'new_m', 'new_m_w_out': 'new_m', 'new_m_norm_xattn': 'new_m', 'new_m_norm_mem': 'new_m', 'new_m_xa_wq': 'new_m', 'new_m_xa_wk': 'new_m', 'new_m_xa_wv': 'new_m', 'new_m_xa_wo': 'new_m', 'new_m_norm_mlp': 'new_m', 'new_m_mlp_w1': 'new_m', 'new_m_mlp_w2': 'new_m', 'new_m_norm_final': 'new_m', 'new_v_norm_mix': 'new_v', 'new_v_w_in': 'new_v', 'new_v_s5_a_re': 'new_v', 'new_v_s5_a_im': 'new_v', 'new_v_s5_log_dt': 'new_v', 'new_v_s5_b_re': 'new_v', 'new_v_s5_b_im': 'new_v', 'new_v_s5_c_re': 'new_v', 'new_v_s5_c_im': 'new_v', 'new_v_s5_d': 'new_v', 'new_v_s5_w_glu': 'new_v', 'new_v_ssd_conv_w': 'new_v', 'new_v_ssd_conv_b': 'new_v', 'new_v_ssd_dt_bias': 'new_v', 'new_v_ssd_a_log': 'new_v', 'new_v_ssd_d': 'new_v', 'new_v_ssd_norm': 'new_v', 'new_v_w_out': 'new_v', 'new_v_norm_xattn': 'new_v', 'new_v_norm_mem': 'new_v', 'new_v_xa_wq': 'new_v', 'new_v_xa_wk': 'new_v', 'new_v_xa_wv': 'new_v', 'new_v_xa_wo': 'new_v', 'new_v_norm_mlp': 'new_v', 'new_v_mlp_w1': 'new_v', 'new_v_mlp_w2': 'new_v', 'new_v_norm_final': 'new_v'}


def _forward(args):
    return _fwd_reference(*[args[k] for k in FWD_PARAMS])


def _output_shape():
    def fwd():
        inp = _fwd_setup_inputs(0)
        return _fwd_reference(*[inp[k] for k in FWD_PARAMS])
    out = _jax.eval_shape(fwd)
    return out.shape, out.dtype

N_MICROBATCH = 1
ADAM_LR = 0.001
ADAM_B1 = 0.9
ADAM_B2 = 0.999
ADAM_EPS = 1e-08
ADAM_WD = 0.01
ADAM_STEP = 10
PER_EXAMPLE_BATCH_AXIS = {'x': 0, 'mem': 0, 'loss_target': 0}
SHARED_INPUTS = []
_WEIGHT_DTYPES = {'norm_mix': _jnp.float32, 'w_in': _jnp.float32, 's5_a_re': _jnp.float32, 's5_a_im': _jnp.float32, 's5_log_dt': _jnp.float32, 's5_b_re': _jnp.float32, 's5_b_im': _jnp.float32, 's5_c_re': _jnp.float32, 's5_c_im': _jnp.float32, 's5_d': _jnp.float32, 's5_w_glu': _jnp.float32, 'ssd_conv_w': _jnp.float32, 'ssd_conv_b': _jnp.float32, 'ssd_dt_bias': _jnp.float32, 'ssd_a_log': _jnp.float32, 'ssd_d': _jnp.float32, 'ssd_norm': _jnp.float32, 'w_out': _jnp.float32, 'norm_xattn': _jnp.float32, 'norm_mem': _jnp.float32, 'xa_wq': _jnp.float32, 'xa_wk': _jnp.float32, 'xa_wv': _jnp.float32, 'xa_wo': _jnp.float32, 'norm_mlp': _jnp.float32, 'mlp_w1': _jnp.float32, 'mlp_w2': _jnp.float32, 'norm_final': _jnp.float32}
MOMENT_SCALE = {'norm_mix': 1.969546e-01, 'w_in': 9.726220e-02, 's5_a_re': 4.768900e-03, 's5_a_im': 5.538719e-03, 's5_log_dt': 2.705430e+00, 's5_b_re': 2.750270e-03, 's5_b_im': 2.776279e-03, 's5_c_re': 3.675609e-03, 's5_c_im': 3.670854e-03, 's5_d': 5.041259e-02, 's5_w_glu': 1.295679e-02, 'ssd_conv_w': 9.898034e-02, 'ssd_conv_b': 1.390268e-01, 'ssd_dt_bias': 2.056384e-01, 'ssd_a_log': 4.930895e-01, 'ssd_d': 7.057435e-01, 'ssd_norm': 1.353010e-01, 'w_out': 1.364529e-01, 'norm_xattn': 1.885430e-02, 'norm_mem': 2.799634e-02, 'xa_wq': 1.849501e-02, 'xa_wk': 1.858505e-02, 'xa_wv': 1.929426e-02, 'xa_wo': 1.928859e-02, 'norm_mlp': 1.865163e-01, 'mlp_w1': 9.256056e-02, 'mlp_w2': 1.746738e-01, 'norm_final': 6.578461e+01}


def _to_microbatches(a, axis):
    t = _jnp.moveaxis(a, axis, 0)
    t = t.reshape((N_MICROBATCH, t.shape[0] // N_MICROBATCH) + t.shape[1:])
    return _jnp.moveaxis(t, 1, axis + 1)


def setup_inputs(seed: int = 0) -> dict:
    inp = _fwd_setup_inputs(seed)
    key = _jax.random.fold_in(_jax.random.key(seed), 7919)
    shape, _ = _output_shape()
    out = dict(inp)
    out["loss_target"] = _jax.random.normal(_jax.random.fold_in(key, 0), shape, _jnp.float32)
    for i, name in enumerate(TWIN_WEIGHTS):
        w = inp[name].astype(_jnp.float32)
        if MOMENT_SCALE is None:
            s = _jnp.sqrt(_jnp.mean(_jnp.square(w)) + 1e-30)
        else:
            s = MOMENT_SCALE[name]
        km, kv = _jax.random.split(_jax.random.fold_in(key, i + 1))
        out[name] = w
        out["m_" + name] = s * _jax.random.normal(km, w.shape, _jnp.float32)
        out["v_" + name] = (s * s) * _jax.random.uniform(kv, w.shape, _jnp.float32, 0.5, 1.5)
    if N_MICROBATCH > 1:
        for name, axis in PER_EXAMPLE_BATCH_AXIS.items():
            out[name] = _to_microbatches(out[name], axis)
    return {'x': out['x'], 'mem': out['mem'], 'norm_mix': out['norm_mix'], 'w_in': out['w_in'], 's5_a_re': out['s5_a_re'], 's5_a_im': out['s5_a_im'], 's5_log_dt': out['s5_log_dt'], 's5_b_re': out['s5_b_re'], 's5_b_im': out['s5_b_im'], 's5_c_re': out['s5_c_re'], 's5_c_im': out['s5_c_im'], 's5_d': out['s5_d'], 's5_w_glu': out['s5_w_glu'], 'ssd_conv_w': out['ssd_conv_w'], 'ssd_conv_b': out['ssd_conv_b'], 'ssd_dt_bias': out['ssd_dt_bias'], 'ssd_a_log': out['ssd_a_log'], 'ssd_d': out['ssd_d'], 'ssd_norm': out['ssd_norm'], 'w_out': out['w_out'], 'norm_xattn': out['norm_xattn'], 'norm_mem': out['norm_mem'], 'xa_wq': out['xa_wq'], 'xa_wk': out['xa_wk'], 'xa_wv': out['xa_wv'], 'xa_wo': out['xa_wo'], 'norm_mlp': out['norm_mlp'], 'mlp_w1': out['mlp_w1'], 'mlp_w2': out['mlp_w2'], 'norm_final': out['norm_final'], 'loss_target': out['loss_target'], 'm_norm_mix': out['m_norm_mix'], 'm_w_in': out['m_w_in'], 'm_s5_a_re': out['m_s5_a_re'], 'm_s5_a_im': out['m_s5_a_im'], 'm_s5_log_dt': out['m_s5_log_dt'], 'm_s5_b_re': out['m_s5_b_re'], 'm_s5_b_im': out['m_s5_b_im'], 'm_s5_c_re': out['m_s5_c_re'], 'm_s5_c_im': out['m_s5_c_im'], 'm_s5_d': out['m_s5_d'], 'm_s5_w_glu': out['m_s5_w_glu'], 'm_ssd_conv_w': out['m_ssd_conv_w'], 'm_ssd_conv_b': out['m_ssd_conv_b'], 'm_ssd_dt_bias': out['m_ssd_dt_bias'], 'm_ssd_a_log': out['m_ssd_a_log'], 'm_ssd_d': out['m_ssd_d'], 'm_ssd_norm': out['m_ssd_norm'], 'm_w_out': out['m_w_out'], 'm_norm_xattn': out['m_norm_xattn'], 'm_norm_mem': out['m_norm_mem'], 'm_xa_wq': out['m_xa_wq'], 'm_xa_wk': out['m_xa_wk'], 'm_xa_wv': out['m_xa_wv'], 'm_xa_wo': out['m_xa_wo'], 'm_norm_mlp': out['m_norm_mlp'], 'm_mlp_w1': out['m_mlp_w1'], 'm_mlp_w2': out['m_mlp_w2'], 'm_norm_final': out['m_norm_final'], 'v_norm_mix': out['v_norm_mix'], 'v_w_in': out['v_w_in'], 'v_s5_a_re': out['v_s5_a_re'], 'v_s5_a_im': out['v_s5_a_im'], 'v_s5_log_dt': out['v_s5_log_dt'], 'v_s5_b_re': out['v_s5_b_re'], 'v_s5_b_im': out['v_s5_b_im'], 'v_s5_c_re': out['v_s5_c_re'], 'v_s5_c_im': out['v_s5_c_im'], 'v_s5_d': out['v_s5_d'], 'v_s5_w_glu': out['v_s5_w_glu'], 'v_ssd_conv_w': out['v_ssd_conv_w'], 'v_ssd_conv_b': out['v_ssd_conv_b'], 'v_ssd_dt_bias': out['v_ssd_dt_bias'], 'v_ssd_a_log': out['v_ssd_a_log'], 'v_ssd_d': out['v_ssd_d'], 'v_ssd_norm': out['v_ssd_norm'], 'v_w_out': out['v_w_out'], 'v_norm_xattn': out['v_norm_xattn'], 'v_norm_mem': out['v_norm_mem'], 'v_xa_wq': out['v_xa_wq'], 'v_xa_wk': out['v_xa_wk'], 'v_xa_wv': out['v_xa_wv'], 'v_xa_wo': out['v_xa_wo'], 'v_norm_mlp': out['v_norm_mlp'], 'v_mlp_w1': out['v_mlp_w1'], 'v_mlp_w2': out['v_mlp_w2'], 'v_norm_final': out['v_norm_final']}


def _loss(weights, diff, rest, loss_target):
    with _jax.named_scope("forward"):
        args = {**rest, TWIN_DIFF_INPUT: diff, **{k: w.astype(_WEIGHT_DTYPES[k]) for k, w in weights.items()}}
        y = _forward(args)
    with _jax.named_scope("loss_head"):
        err = _jnp.square(y.astype(_jnp.float32) - loss_target)
        return 0.5 * _jnp.sum(_jnp.mean(err, axis=-1)) if err.ndim else 0.5 * err


def _adamw(w, g, m, v):
    m = ADAM_B1 * m + (1.0 - ADAM_B1) * g
    v = ADAM_B2 * v + (1.0 - ADAM_B2) * _jnp.square(g)
    m_hat = m / (1.0 - ADAM_B1 ** ADAM_STEP)
    v_hat = v / (1.0 - ADAM_B2 ** ADAM_STEP)
    delta = -ADAM_LR * (m_hat / (_jnp.sqrt(v_hat) + ADAM_EPS) + ADAM_WD * w)
    return delta, m, v


def reference(x, mem, norm_mix, w_in, s5_a_re, s5_a_im, s5_log_dt, s5_b_re, s5_b_im, s5_c_re, s5_c_im, s5_d, s5_w_glu, ssd_conv_w, ssd_conv_b, ssd_dt_bias, ssd_a_log, ssd_d, ssd_norm, w_out, norm_xattn, norm_mem, xa_wq, xa_wk, xa_wv, xa_wo, norm_mlp, mlp_w1, mlp_w2, norm_final, loss_target, m_norm_mix, m_w_in, m_s5_a_re, m_s5_a_im, m_s5_log_dt, m_s5_b_re, m_s5_b_im, m_s5_c_re, m_s5_c_im, m_s5_d, m_s5_w_glu, m_ssd_conv_w, m_ssd_conv_b, m_ssd_dt_bias, m_ssd_a_log, m_ssd_d, m_ssd_norm, m_w_out, m_norm_xattn, m_norm_mem, m_xa_wq, m_xa_wk, m_xa_wv, m_xa_wo, m_norm_mlp, m_mlp_w1, m_mlp_w2, m_norm_final, v_norm_mix, v_w_in, v_s5_a_re, v_s5_a_im, v_s5_log_dt, v_s5_b_re, v_s5_b_im, v_s5_c_re, v_s5_c_im, v_s5_d, v_s5_w_glu, v_ssd_conv_w, v_ssd_conv_b, v_ssd_dt_bias, v_ssd_a_log, v_ssd_d, v_ssd_norm, v_w_out, v_norm_xattn, v_norm_mem, v_xa_wq, v_xa_wk, v_xa_wv, v_xa_wo, v_norm_mlp, v_mlp_w1, v_mlp_w2, v_norm_final):
    given = dict(x=x, mem=mem, norm_mix=norm_mix, w_in=w_in, s5_a_re=s5_a_re, s5_a_im=s5_a_im, s5_log_dt=s5_log_dt, s5_b_re=s5_b_re, s5_b_im=s5_b_im, s5_c_re=s5_c_re, s5_c_im=s5_c_im, s5_d=s5_d, s5_w_glu=s5_w_glu, ssd_conv_w=ssd_conv_w, ssd_conv_b=ssd_conv_b, ssd_dt_bias=ssd_dt_bias, ssd_a_log=ssd_a_log, ssd_d=ssd_d, ssd_norm=ssd_norm, w_out=w_out, norm_xattn=norm_xattn, norm_mem=norm_mem, xa_wq=xa_wq, xa_wk=xa_wk, xa_wv=xa_wv, xa_wo=xa_wo, norm_mlp=norm_mlp, mlp_w1=mlp_w1, mlp_w2=mlp_w2, norm_final=norm_final, loss_target=loss_target, m_norm_mix=m_norm_mix, m_w_in=m_w_in, m_s5_a_re=m_s5_a_re, m_s5_a_im=m_s5_a_im, m_s5_log_dt=m_s5_log_dt, m_s5_b_re=m_s5_b_re, m_s5_b_im=m_s5_b_im, m_s5_c_re=m_s5_c_re, m_s5_c_im=m_s5_c_im, m_s5_d=m_s5_d, m_s5_w_glu=m_s5_w_glu, m_ssd_conv_w=m_ssd_conv_w, m_ssd_conv_b=m_ssd_conv_b, m_ssd_dt_bias=m_ssd_dt_bias, m_ssd_a_log=m_ssd_a_log, m_ssd_d=m_ssd_d, m_ssd_norm=m_ssd_norm, m_w_out=m_w_out, m_norm_xattn=m_norm_xattn, m_norm_mem=m_norm_mem, m_xa_wq=m_xa_wq, m_xa_wk=m_xa_wk, m_xa_wv=m_xa_wv, m_xa_wo=m_xa_wo, m_norm_mlp=m_norm_mlp, m_mlp_w1=m_mlp_w1, m_mlp_w2=m_mlp_w2, m_norm_final=m_norm_final, v_norm_mix=v_norm_mix, v_w_in=v_w_in, v_s5_a_re=v_s5_a_re, v_s5_a_im=v_s5_a_im, v_s5_log_dt=v_s5_log_dt, v_s5_b_re=v_s5_b_re, v_s5_b_im=v_s5_b_im, v_s5_c_re=v_s5_c_re, v_s5_c_im=v_s5_c_im, v_s5_d=v_s5_d, v_s5_w_glu=v_s5_w_glu, v_ssd_conv_w=v_ssd_conv_w, v_ssd_conv_b=v_ssd_conv_b, v_ssd_dt_bias=v_ssd_dt_bias, v_ssd_a_log=v_ssd_a_log, v_ssd_d=v_ssd_d, v_ssd_norm=v_ssd_norm, v_w_out=v_w_out, v_norm_xattn=v_norm_xattn, v_norm_mem=v_norm_mem, v_xa_wq=v_xa_wq, v_xa_wk=v_xa_wk, v_xa_wv=v_xa_wv, v_xa_wo=v_xa_wo, v_norm_mlp=v_norm_mlp, v_mlp_w1=v_mlp_w1, v_mlp_w2=v_mlp_w2, v_norm_final=v_norm_final)
    weights = {n: given[n] for n in TWIN_WEIGHTS}
    shared = {n: given[n] for n in SHARED_INPUTS}
    per_example = {n: given[n] for n in ['x', 'mem']}
    grad_fn = _jax.value_and_grad(_loss, argnums=(0, 1))

    def one_microbatch(ex, loss_target):
        ex = dict(ex)
        diff = ex.pop(TWIN_DIFF_INPUT)
        return grad_fn(weights, diff, {**shared, **ex}, loss_target)

    if N_MICROBATCH == 1:
        loss, (grad_w, grad_x) = one_microbatch(per_example, given["loss_target"])
    else:
        def body(carry, xs):
            loss_sum, grad_sum = carry
            l_k, (gw_k, gx_k) = one_microbatch(xs[0], xs[1])
            with _jax.named_scope("update"):
                return (loss_sum + l_k, _jax.tree.map(_jnp.add, grad_sum, gw_k)), gx_k

        init = (_jnp.zeros((), _jnp.float32), _jax.tree.map(_jnp.zeros_like, weights))
        (loss, grad_w), grad_x = _jax.lax.scan(body, init, (per_example, given["loss_target"]))
    with _jax.named_scope("update"):
        delta_w, new_m, new_v = {}, {}, {}
        for n in TWIN_WEIGHTS:
            delta_w[n], new_m[n], new_v[n] = _adamw(weights[n], grad_w[n], given["m_" + n], given["v_" + n])
    return (loss, grad_x, *[grad_w[n] for n in TWIN_WEIGHTS], *[delta_w[n] for n in TWIN_WEIGHTS],
            *[new_m[n] for n in TWIN_WEIGHTS], *[new_v[n] for n in TWIN_WEIGHTS])
```

```python
import functools
import math

import jax
import jax.numpy as jnp
from jax import lax
from jax.experimental import pallas as pl
from jax.experimental.pallas import tpu as pltpu

F32 = jnp.float32
MXU = jnp.bfloat16
HI = lax.Precision.HIGHEST

D_MODEL = 1024
DEPTH = 4
MEM_LEN = 256
D_S5 = 1024
D_SSD = 1024
S5_GROUP = 16
S5_GROUPS = 64
S5_STATE = 64
S5_CH = S5_GROUPS * S5_STATE
S5_NB = 4
S5_GPB = S5_GROUPS // S5_NB
SSD_HEADDIM = 64
SSD_HEADS = 16
SSD_GROUPS = 4
SSD_STATE = 128
SSD_CONV = 4
SSD_CHUNK = 128
SSD_BC = SSD_GROUPS * SSD_STATE
D_CONV_CH = 2048
D_MAIN = 4096
D_IN_PROJ = D_MAIN + SSD_HEADS
HP = 128
XA_HEADS = 4
XA_HEAD_DIM = 256
D_FF = 4096
EPS = 1e-5
N_DEV = 8
AXES = ("x", "y", "c")

ADAM_LR = 0.001
ADAM_B1 = 0.9
ADAM_B2 = 0.999
ADAM_EPS = 1e-08
ADAM_WD = 0.01
ADAM_STEP = 10

VMEM_LIMIT = 56 * 1024 * 1024


def _params(*sem):
    return pltpu.CompilerParams(dimension_semantics=sem, vmem_limit_bytes=VMEM_LIMIT)


def _tile(n, pref, quantum=128):
    t = (min(pref, n) // quantum) * quantum
    while t >= quantum:
        if n % t == 0:
            return t
        t -= quantum
    return n


def _sds(shape, dtype):
    return jax.ShapeDtypeStruct(tuple(shape), dtype)


def _sigmoid(x):
    return 1.0 / (1.0 + jnp.exp(-x))


def _silu(x):
    return x * _sigmoid(x)


def _silu_grad(x):
    s = _sigmoid(x)
    return s * (1.0 + x * (1.0 - s))


_GELU_C = math.sqrt(2.0 / math.pi)


def _gelu(x):
    return 0.5 * x * (1.0 + jnp.tanh(_GELU_C * (x + 0.044715 * x * x * x)))


def _gelu_grad(x):
    th = jnp.tanh(_GELU_C * (x + 0.044715 * x * x * x))
    return 0.5 * (1.0 + th) + 0.5 * x * (1.0 - th * th) * _GELU_C * (1.0 + 3.0 * 0.044715 * x * x)


def _softplus(x):
    return jnp.maximum(x, 0.0) + jnp.log(1.0 + jnp.exp(-jnp.abs(x)))


_NN = (((1,), (0,)), ((), ()))
_NT = (((1,), (1,)), ((), ()))
_TN = (((0,), (0,)), ((), ()))


def _dot(a, b, dims=_NN, precision=None):
    return lax.dot_general(a, b, dims, precision=precision, preferred_element_type=F32)


def _rows8(v):
    return jnp.broadcast_to(v, (8, v.shape[1]))


def _mm(a, b, *, ta=False, tb=False, extras=(), epi=None, out_dtypes=(F32,), tm=1024, tn=1024, tk=512, name):
    m, k = (a.shape[1], a.shape[0]) if ta else a.shape
    n = b.shape[0] if tb else b.shape[1]
    assert k == (b.shape[1] if tb else b.shape[0]), (a.shape, b.shape, ta, tb)
    tm, tn, tk = _tile(m, tm), _tile(n, tn), _tile(k, tk)
    nk = k // tk
    n_ex, n_out = len(extras), len(out_dtypes)
    dims = (((0,) if ta else (1,), (1,) if tb else (0,)), ((), ()))

    def body(a_ref, b_ref, *rest):
        ex_refs, out_refs, acc = rest[:n_ex], rest[n_ex:n_ex + n_out], rest[-1]
        kk = pl.program_id(2)

        @pl.when(kk == 0)
        def _():
            acc[...] = jnp.zeros_like(acc)

        acc[...] += _dot(a_ref[...].astype(MXU), b_ref[...].astype(MXU), dims)

        @pl.when(kk == nk - 1)
        def _():
            outs = epi(acc[...], *[e[...] for e in ex_refs]) if epi is not None else (acc[...],)
            for o, r in zip(outs, out_refs, strict=True):
                r[...] = o.astype(r.dtype)

    a_spec = pl.BlockSpec((tk, tm), lambda i, j, kk: (kk, i)) if ta else pl.BlockSpec((tm, tk), lambda i, j, kk: (i, kk))
    b_spec = pl.BlockSpec((tn, tk), lambda i, j, kk: (j, kk)) if tb else pl.BlockSpec((tk, tn), lambda i, j, kk: (kk, j))
    mn_spec = pl.BlockSpec((tm, tn), lambda i, j, kk: (i, j))
    outs = pl.pallas_call(
        body,
        name=name,
        grid=(m // tm, n // tn, nk),
        in_specs=[a_spec, b_spec] + [mn_spec] * n_ex,
        out_specs=[mn_spec] * n_out,
        out_shape=[_sds((m, n), dt) for dt in out_dtypes],
        scratch_shapes=[pltpu.VMEM((tm, tn), F32)],
        compiler_params=_params("parallel", "parallel", "arbitrary"),
    )(a, b, *extras)
    return outs[0] if n_out == 1 else outs


def _bdmm(terms, *, nb, ko, no, extras=(), vecs=(), epi=None, out_dtypes=(F32,), tm=512, name):
    t = terms[0][0].shape[0]
    tm = _tile(t, tm)
    n_t, n_ex, n_v, n_out = len(terms), len(extras), len(vecs), len(out_dtypes)

    def body(*refs):
        a_refs, w_refs = refs[0:2 * n_t:2], refs[1:2 * n_t:2]
        ex_refs = refs[2 * n_t:2 * n_t + n_ex + n_v]
        out_refs = refs[2 * n_t + n_ex + n_v:]
        prods = [_dot(a[...].astype(MXU), w[...].astype(MXU)) for a, w in zip(a_refs, w_refs)]
        outs = epi(prods, *[e[...] for e in ex_refs]) if epi is not None else (prods[0],)
        for o, r in zip(outs, out_refs, strict=True):
            r[...] = o.astype(r.dtype)

    in_specs, args = [], []
    for a, w in terms:
        in_specs.append(pl.BlockSpec((tm, ko), lambda j, i: (i, j)))
        in_specs.append(pl.BlockSpec((None, ko, no), lambda j, i: (j, 0, 0)))
        args += [a, w]
    o_spec = pl.BlockSpec((tm, no), lambda j, i: (i, j))
    v_spec = pl.BlockSpec((1, no), lambda j, i: (0, j))
    outs = pl.pallas_call(
        body,
        name=name,
        grid=(nb, t // tm),
        in_specs=in_specs + [o_spec] * n_ex + [v_spec] * n_v,
        out_specs=[o_spec] * n_out,
        out_shape=[_sds((t, nb * no), dt) for dt in out_dtypes],
        compiler_params=_params("parallel", "parallel"),
    )(*args, *extras, *vecs)
    return outs[0] if n_out == 1 else outs


def _bdmm_tn(a, ka, b, nbc, *, nb, tt=512, name):
    t = a.shape[0]
    tt = _tile(t, tt)
    nt = t // tt

    def body(a_ref, b_ref, o_ref, acc):
        s = pl.program_id(1)

        @pl.when(s == 0)
        def _():
            acc[...] = jnp.zeros_like(acc)

        acc[...] += _dot(a_ref[...].astype(MXU), b_ref[...].astype(MXU), _TN)

        @pl.when(s == nt - 1)
        def _():
            o_ref[...] = acc[...]

    return pl.pallas_call(
        body,
        name=name,
        grid=(nb, nt),
        in_specs=[pl.BlockSpec((tt, ka), lambda j, s: (s, j)), pl.BlockSpec((tt, nbc), lambda j, s: (s, j))],
        out_specs=pl.BlockSpec((None, ka, nbc), lambda j, s: (j, 0, 0)),
        out_shape=_sds((nb, ka, nbc), F32),
        scratch_shapes=[pltpu.VMEM((ka, nbc), F32)],
        compiler_params=_params("parallel", "arbitrary"),
    )(a, b)


def _row(tb, w, cb=0):
    return pl.BlockSpec((tb, w), lambda i: (i, cb))


def _const(shape):
    return pl.BlockSpec(shape, lambda i: (0,) * len(shape))


def _rmsnorm_fwd(x, g, *, name):
    t, d = x.shape
    tb = _tile(t, 512, 8)

    def body(x_ref, g_ref, h_ref):
        xv = x_ref[...]
        r = lax.rsqrt(jnp.mean(xv * xv, axis=-1, keepdims=True) + EPS)
        h_ref[...] = (xv * r * g_ref[...]).astype(h_ref.dtype)

    return pl.pallas_call(
        body, name=name, grid=(t // tb,), in_specs=[_row(tb, d), _const((1, d))], out_specs=_row(tb, d),
        out_shape=_sds((t, d), MXU), compiler_params=_params("parallel"),
    )(x, g.reshape(1, d))


def _rmsnorm_bwd(x, g, dh, dres, *, name):
    t, d = x.shape
    tb = _tile(t, 256, 8)
    has_res = dres is not None

    def body(x_ref, g_ref, dh_ref, *rest):
        dx_ref, dg_ref = rest[-2:]

        @pl.when(pl.program_id(0) == 0)
        def _():
            dg_ref[...] = jnp.zeros_like(dg_ref)

        xv = x_ref[...]
        r = lax.rsqrt(jnp.mean(xv * xv, axis=-1, keepdims=True) + EPS)
        xh = xv * r
        dhv = dh_ref[...].astype(F32)
        dg_ref[...] += jnp.sum(dhv * xh, axis=0, keepdims=True)
        dxh = dhv * g_ref[...]
        dx = r * (dxh - xh * jnp.mean(dxh * xh, axis=-1, keepdims=True))
        if has_res:
            dx = dx + rest[0][...]
        dx_ref[...] = dx

    ins = [x, g.reshape(1, d), dh] + ([dres] if has_res else [])
    return pl.pallas_call(
        body, name=name, grid=(t // tb,),
        in_specs=[_row(tb, d), _const((1, d)), _row(tb, d)] + ([_row(tb, d)] if has_res else []),
        out_specs=[_row(tb, d), _const((1, d))],
        out_shape=[_sds((t, d), F32), _sds((1, d), F32)],
        compiler_params=_params("arbitrary"),
    )(*ins)


def _loss_head(x, g, target):
    t, d = x.shape
    tb = _tile(t, 256, 8)

    def body(x_ref, g_ref, tg_ref, loss_ref, dx_ref, dg_ref):
        @pl.when(pl.program_id(0) == 0)
        def _():
            dg_ref[...] = jnp.zeros_like(dg_ref)
            loss_ref[...] = jnp.zeros_like(loss_ref)

        xv, gv = x_ref[...], g_ref[...]
        r = lax.rsqrt(jnp.mean(xv * xv, axis=-1, keepdims=True) + EPS)
        xh = xv * r
        err = xh * gv - tg_ref[...]
        loss_ref[...] += 0.5 * jnp.sum(jnp.mean(err * err, axis=-1, keepdims=True), axis=0, keepdims=True)
        dy = err * (1.0 / d)
        dg_ref[...] += jnp.sum(dy * xh, axis=0, keepdims=True)
        dxh = dy * gv
        dx_ref[...] = r * (dxh - xh * jnp.mean(dxh * xh, axis=-1, keepdims=True))

    return pl.pallas_call(
        body, name="loss_head", grid=(t // tb,),
        in_specs=[_row(tb, d), _const((1, d)), _row(tb, d)],
        out_specs=[_const((1, HP)), _row(tb, d), _const((1, d))],
        out_shape=[_sds((1, HP), F32), _sds((t, d), F32), _sds((1, d), F32)],
        compiler_params=_params("arbitrary"),
    )(x, g.reshape(1, d), target)


def _s5_discretise(ar, ai, ldt, br, bi, rep):
    dt = jnp.exp(ldt)
    mag = jnp.exp(dt * ar)
    abar_r, abar_i = mag * jnp.cos(dt * ai), mag * jnp.sin(dt * ai)
    den = ar * ar + ai * ai
    zr, zi = abar_r - 1.0, abar_i
    fr = (zr * ar + zi * ai) / den
    fi = (zi * ar - zr * ai) / den
    fr_e, fi_e = _dot(fr, rep, precision=HI), _dot(fi, rep, precision=HI)
    return abar_r, abar_i, fr_e * br - fi_e * bi, fr_e * bi + fi_e * br


def _s5_prep(ar, ai, ldt, br, bi, rep):
    g, p = ar.shape
    ph = br.shape[1]

    def body(ar_ref, ai_ref, ldt_ref, br_ref, bi_ref, rep_ref, o0, o1, o2, o3):
        outs = _s5_discretise(ar_ref[...], ai_ref[...], ldt_ref[...], br_ref[...], bi_ref[...], rep_ref[...])
        for o, v in zip((o0, o1, o2, o3), outs):
            o[...] = v

    return pl.pallas_call(
        body, name="s5_prep",
        out_shape=[_sds((g, p), F32), _sds((g, p), F32), _sds((g, ph), F32), _sds((g, ph), F32)],
        compiler_params=pltpu.CompilerParams(vmem_limit_bytes=VMEM_LIMIT),
    )(ar, ai, ldt, br, bi, rep)


def _s5_prep_bwd(ar, ai, ldt, br, bi, rep, d_abar_r, d_abar_i, d_bbar_r, d_bbar_i):
    g, p = ar.shape
    ph = br.shape[1]

    def body(ar_ref, ai_ref, ldt_ref, br_ref, bi_ref, rep_ref, c0, c1, c2, c3, o0, o1, o2, o3, o4):
        rep_v = rep_ref[...]
        _, vjp = jax.vjp(lambda a, b, c, d, e: _s5_discretise(a, b, c, d, e, rep_v),
                         ar_ref[...], ai_ref[...], ldt_ref[...], br_ref[...], bi_ref[...])
        grads = vjp((c0[...], c1[...], c2[...], c3[...]))
        for o, v in zip((o0, o1, o2, o3, o4), grads):
            o[...] = v

    return pl.pallas_call(
        body, name="s5_prep_bwd",
        out_shape=[_sds((g, p), F32), _sds((g, p), F32), _sds((g, 1), F32), _sds((g, ph), F32), _sds((g, ph), F32)],
        compiler_params=pltpu.CompilerParams(vmem_limit_bytes=VMEM_LIMIT),
    )(ar, ai, ldt, br, bi, rep, d_abar_r, d_abar_i, d_bbar_r, d_bbar_i)


def _s5_scan(ar, ai, dr, di, *, reverse, name):
    t, ch = dr.shape
    tb = _tile(t, 256, 8)
    tc = _tile(ch, 512)
    nt = t // tb
    steps = int(math.log2(tb))
    assert 1 << steps == tb
    edge_in, edge_out = (tb - 1, 0) if reverse else (0, tb - 1)

    def body(ar_ref, ai_ref, dr_ref, di_ref, sr_ref, si_ref, cr, ci):
        @pl.when(pl.program_id(1) == 0)
        def _():
            cr[...] = jnp.zeros_like(cr)
            ci[...] = jnp.zeros_like(ci)

        pr = ar_ref[...]
        pi = -ai_ref[...] if reverse else ai_ref[...]
        xr, xi = dr_ref[...], di_ref[...]
        row = lax.broadcasted_iota(jnp.int32, (tb, tc), 0)
        crv, civ = cr[...], ci[...]
        xr = jnp.where(row == edge_in, xr + (pr * crv - pi * civ), xr)
        xi = jnp.where(row == edge_in, xi + (pr * civ + pi * crv), xi)
        for s in range(steps):
            sh = 1 << s
            if reverse:
                keep = row < tb - sh
                qr, qi = pltpu.roll(xr, tb - sh, 0), pltpu.roll(xi, tb - sh, 0)
            else:
                keep = row >= sh
                qr, qi = pltpu.roll(xr, sh, 0), pltpu.roll(xi, sh, 0)
            qr, qi = jnp.where(keep, qr, 0.0), jnp.where(keep, qi, 0.0)
            xr, xi = xr + (pr * qr - pi * qi), xi + (pr * qi + pi * qr)
            pr, pi = pr * pr - pi * pi, 2.0 * pr * pi
        sr_ref[...] = xr
        si_ref[...] = xi
        cr[...] = xr[edge_out:edge_out + 1, :]
        ci[...] = xi[edge_out:edge_out + 1, :]

    if reverse:
        blk = pl.BlockSpec((tb, tc), lambda c, s: (nt - 1 - s, c))
    else:
        blk = pl.BlockSpec((tb, tc), lambda c, s: (s, c))
    vec = pl.BlockSpec((1, tc), lambda c, s: (0, c))
    return pl.pallas_call(
        body, name=name, grid=(ch // tc, nt),
        in_specs=[vec, vec, blk, blk], out_specs=[blk, blk],
        out_shape=[_sds((t, ch), F32), _sds((t, ch), F32)],
        scratch_shapes=[pltpu.VMEM((1, tc), F32), pltpu.VMEM((1, tc), F32)],
        compiler_params=_params("parallel", "arbitrary"),
    )(ar, ai, dr, di)


def _s5_dabar(sr, si, lr, li):
    t, ch = sr.shape
    tb = _tile(t, 256, 8)
    tc = _tile(ch, 512)

    def body(sr_ref, si_ref, lr_ref, li_ref, gr_ref, gi_ref, cr, ci):
        @pl.when(pl.program_id(1) == 0)
        def _():
            cr[...] = jnp.zeros_like(cr)
            ci[...] = jnp.zeros_like(ci)
            gr_ref[...] = jnp.zeros_like(gr_ref)
            gi_ref[...] = jnp.zeros_like(gi_ref)

        srv, siv = sr_ref[...], si_ref[...]
        row = lax.broadcasted_iota(jnp.int32, (tb, tc), 0)
        pr = jnp.where(row == 0, cr[...], pltpu.roll(srv, 1, 0))
        pi = jnp.where(row == 0, ci[...], pltpu.roll(siv, 1, 0))
        lrv, liv = lr_ref[...], li_ref[...]
        gr_ref[...] += jnp.sum(pr * lrv + pi * liv, axis=0, keepdims=True)
        gi_ref[...] += jnp.sum(pr * liv - pi * lrv, axis=0, keepdims=True)
        cr[...] = srv[tb - 1:tb, :]
        ci[...] = siv[tb - 1:tb, :]

    blk = pl.BlockSpec((tb, tc), lambda c, s: (s, c))
    vec = pl.BlockSpec((1, tc), lambda c, s: (0, c))
    return pl.pallas_call(
        body, name="s5_dabar", grid=(ch // tc, t // tb),
        in_specs=[blk] * 4, out_specs=[vec, vec],
        out_shape=[_sds((1, ch), F32), _sds((1, ch), F32)],
        scratch_shapes=[pltpu.VMEM((1, tc), F32), pltpu.VMEM((1, tc), F32)],
        compiler_params=_params("parallel", "arbitrary"),
    )(sr, si, lr, li)


def _colsum_prod(a, b, w, *, name):
    t = a.shape[0]
    tb = _tile(t, 512, 8)

    def body(a_ref, b_ref, o_ref):
        @pl.when(pl.program_id(0) == 0)
        def _():
            o_ref[...] = jnp.zeros_like(o_ref)

        o_ref[...] += jnp.sum(a_ref[...].astype(F32) * b_ref[...].astype(F32), axis=0, keepdims=True)

    return pl.pallas_call(
        body, name=name, grid=(t // tb,), in_specs=[_row(tb, w), _row(tb, w)],
        out_specs=_const((1, w)), out_shape=_sds((1, w), F32), compiler_params=_params("arbitrary"),
    )(a, b)


def _s5_gate_bwd(d_ycat, gp, ypre):
    t, d = gp.shape
    tb = _tile(t, 256, 8)

    def body(do_ref, gp_ref, yp_ref, o_ref):
        sg = _sigmoid(gp_ref[...])
        o_ref[...] = (do_ref[...] * _gelu(yp_ref[...]) * sg * (1.0 - sg)).astype(o_ref.dtype)

    return pl.pallas_call(
        body, name="s5_gate_bwd", grid=(t // tb,), in_specs=[_row(tb, d), _row(tb, d), _row(tb, d)],
        out_specs=_row(tb, d), out_shape=_sds((t, d), MXU), compiler_params=_params("parallel"),
    )(d_ycat, gp, ypre)


_CONV_CW = 512
_CONV_OFF = (D_MAIN - D_CONV_CH) // _CONV_CW


def _conv_fwd(proj, w, b):
    t = proj.shape[0]
    tb = _tile(t, 256, 8)
    cw, off = _CONV_CW, _CONV_OFF

    def body(cur_ref, prev_ref, w_ref, b_ref, o_ref, ext):
        first = pl.program_id(1) == 0
        ext[0:8, :] = jnp.where(first, 0.0, prev_ref[...])
        ext[8:tb + 8, :] = cur_ref[...]
        acc = jnp.broadcast_to(b_ref[...], (tb, cw))
        for j in range(SSD_CONV):
            acc = acc + w_ref[SSD_CONV - 1 - j:SSD_CONV - j, :] * ext[8 - j:8 - j + tb, :]
        o_ref[...] = acc

    return pl.pallas_call(
        body, name="ssd_conv_fwd", grid=(D_CONV_CH // cw, t // tb),
        in_specs=[
            pl.BlockSpec((tb, cw), lambda j, i: (i, j + off)),
            pl.BlockSpec((8, cw), lambda j, i: (jnp.maximum(i * (tb // 8) - 1, 0), j + off)),
            pl.BlockSpec((SSD_CONV, cw), lambda j, i: (0, j)),
            pl.BlockSpec((1, cw), lambda j, i: (0, j)),
        ],
        out_specs=pl.BlockSpec((tb, cw), lambda j, i: (i, j)),
        out_shape=_sds((t, D_CONV_CH), F32),
        scratch_shapes=[pltpu.VMEM((tb + 8, cw), F32)],
        compiler_params=_params("parallel", "arbitrary"),
    )(proj, proj, w, b.reshape(1, D_CONV_CH))


def _conv_bwd(proj, w, d_conv):
    t = proj.shape[0]
    tb = _tile(t, 256, 8)
    cw, off = _CONV_CW, _CONV_OFF
    nt = t // tb

    def body(cur_ref, prev_ref, w_ref, dc_ref, dnext_ref, dx_ref, dw_ref, db_ref, ext, dext):
        i = pl.program_id(1)

        @pl.when(i == 0)
        def _():
            dw_ref[...] = jnp.zeros_like(dw_ref)
            db_ref[...] = jnp.zeros_like(db_ref)

        ext[0:8, :] = jnp.where(i == 0, 0.0, prev_ref[...])
        ext[8:tb + 8, :] = cur_ref[...]
        dcv = dc_ref[...]
        dext[0:tb, :] = dcv
        dext[tb:tb + 8, :] = jnp.where(i == nt - 1, 0.0, dnext_ref[...])
        dx = jnp.zeros((tb, cw), F32)
        for j in range(SSD_CONV):
            dx = dx + w_ref[SSD_CONV - 1 - j:SSD_CONV - j, :] * dext[j:j + tb, :]
            dw_ref[SSD_CONV - 1 - j:SSD_CONV - j, :] += jnp.sum(dcv * ext[8 - j:8 - j + tb, :], axis=0, keepdims=True)
        dx_ref[...] = dx
        db_ref[...] += jnp.sum(dcv, axis=0, keepdims=True)

    return pl.pallas_call(
        body, name="ssd_conv_bwd", grid=(D_CONV_CH // cw, nt),
        in_specs=[
            pl.BlockSpec((tb, cw), lambda j, i: (i, j + off)),
            pl.BlockSpec((8, cw), lambda j, i: (jnp.maximum(i * (tb // 8) - 1, 0), j + off)),
            pl.BlockSpec((SSD_CONV, cw), lambda j, i: (0, j)),
            pl.BlockSpec((tb, cw), lambda j, i: (i, j)),
            pl.BlockSpec((8, cw), lambda j, i: (jnp.minimum((i + 1) * (tb // 8), t // 8 - 1), j)),
        ],
        out_specs=[
            pl.BlockSpec((tb, cw), lambda j, i: (i, j)),
            pl.BlockSpec((SSD_CONV, cw), lambda j, i: (0, j)),
            pl.BlockSpec((1, cw), lambda j, i: (0, j)),
        ],
        out_shape=[_sds((t, D_CONV_CH), F32), _sds((SSD_CONV, D_CONV_CH), F32), _sds((1, D_CONV_CH), F32)],
        scratch_shapes=[pltpu.VMEM((tb + 8, cw), F32), pltpu.VMEM((tb + 8, cw), F32)],
        compiler_params=_params("parallel", "arbitrary"),
    )(proj, proj, w, d_conv, d_conv)


def _ssd_consts():
    head = jnp.arange(HP)[:, None]
    lane = jnp.arange(D_SSD)[None, :]
    expand = ((lane // SSD_HEADDIM) == head).astype(F32)
    ll = jnp.arange(SSD_CHUNK)
    tri = (ll[:, None] >= ll[None, :]).astype(F32)
    return expand, expand.T, tri, jnp.eye(HP, dtype=F32)


def _ssd_chunk_terms(cp, dtr, par, expand, tri):
    ln = SSD_CHUNK
    xbc = _silu(cp)
    xs, bm, cm = xbc[:, :D_SSD], xbc[:, D_SSD:D_SSD + SSD_BC], xbc[:, D_SSD + SSD_BC:]
    dt = _softplus(dtr + par[0:1, :])
    a = -jnp.exp(par[1:2, :])
    da = dt * a
    acum = _dot(tri, da, precision=HI)
    acum_t = _dot(da, tri, (((0,), (1,)), ((), ())), precision=HI)
    atot = acum[ln - 1:ln, :]
    dt_e = _dot(dt, expand, precision=HI)
    eac_e = _dot(jnp.exp(acum), expand, precision=HI)
    dec_e = _dot(jnp.exp(atot - acum), expand, precision=HI)
    eat_e = _dot(_rows8(jnp.exp(atot)), expand, precision=HI)[0:1, :]
    dsk_e = _dot(_rows8(par[2:3, :]), expand, precision=HI)[0:1, :]
    return dict(xs=xs, bm=bm, cm=cm, dt=dt, a=a, acum=acum, acum_t=acum_t, dt_e=dt_e, eac_e=eac_e,
                dec_e=dec_e, eat_e=eat_e, dsk_e=dsk_e)


def _decay_matrix(acum, acum_t, h, mask):
    diff = acum[:, h:h + 1] - acum_t[h:h + 1, :]
    return jnp.where(mask, jnp.exp(jnp.minimum(diff, 0.0)), 0.0)


def _ssd_fwd(proj, conv_pre, dtr, par, gnorm, consts):
    t = proj.shape[0]
    ln = SSD_CHUNK
    nc = t // ln
    expand, _, tri, _ = consts
    hd2 = 2 * SSD_HEADDIM

    def body(cp_ref, z_ref, dtr_ref, par_ref, g_ref, e_ref, tri_ref, out_ref, y_ref, st_ref, state):
        @pl.when(pl.program_id(0) == 0)
        def _():
            state[...] = jnp.zeros_like(state)

        st_ref[...] = state[...]
        c = _ssd_chunk_terms(cp_ref[...], dtr_ref[...], par_ref[...], e_ref[...], tri_ref[...])
        xdt = c["xs"] * c["dt_e"]
        xb, xd = xdt.astype(MXU), (xdt * c["dec_e"]).astype(MXU)
        bb, cb = c["bm"].astype(MXU), c["cm"].astype(MXU)
        mask = lax.broadcasted_iota(jnp.int32, (ln, ln), 0) >= lax.broadcasted_iota(jnp.int32, (ln, ln), 1)
        left = lax.broadcasted_iota(jnp.int32, (ln, hd2), 1) < SSD_HEADDIM
        for g in range(SSD_GROUPS):
            nsl = slice(g * SSD_STATE, (g + 1) * SSD_STATE)
            gsl = slice(g * 256, (g + 1) * 256)
            bg, cg = bb[:, nsl], cb[:, nsl]
            cbm = _dot(cg, bg, _NT)
            st_g = state[:, gsl]
            for pair in range(2):
                h0 = g * 4 + pair * 2
                psl = slice(h0 * SSD_HEADDIM, (h0 + 2) * SSD_HEADDIM)
                m0 = (cbm * _decay_matrix(c["acum"], c["acum_t"], h0, mask)).astype(MXU)
                m1 = (cbm * _decay_matrix(c["acum"], c["acum_t"], h0 + 1, mask)).astype(MXU)
                y_ref[:, psl] = jnp.where(left, _dot(m0, xb[:, psl]), _dot(m1, xb[:, psl]))
            y_ref[:, gsl] += _dot(cg, st_g.astype(MXU)) * c["eac_e"][:, gsl]
            state[:, gsl] = st_g * c["eat_e"][:, gsl] + _dot(bg, xd[:, gsl], _TN)
        y = y_ref[...] + c["dsk_e"] * c["xs"]
        y_ref[...] = y
        y2 = y * _silu(z_ref[...])
        r = lax.rsqrt(jnp.mean(y2 * y2, axis=-1, keepdims=True) + EPS)
        out_ref[...] = (y2 * r * g_ref[...]).astype(out_ref.dtype)

    return pl.pallas_call(
        body, name="ssd_fwd", grid=(nc,),
        in_specs=[_row(ln, D_CONV_CH), _row(ln, D_SSD, 1), _row(ln, HP), _const((8, HP)), _const((1, D_SSD)),
                  _const((HP, D_SSD)), _const((ln, ln))],
        out_specs=[_row(ln, D_SSD), _row(ln, D_SSD), pl.BlockSpec((None, SSD_STATE, D_SSD), lambda i: (i, 0, 0))],
        out_shape=[_sds((t, D_SSD), MXU), _sds((t, D_SSD), F32), _sds((nc, SSD_STATE, D_SSD), F32)],
        scratch_shapes=[pltpu.VMEM((SSD_STATE, D_SSD), F32)],
        compiler_params=_params("arbitrary"),
    )(conv_pre, proj, dtr, par, gnorm.reshape(1, D_SSD), expand, tri)


def _ssd_bwd(proj, conv_pre, dtr, par, gnorm, y, states, d_ycat, consts):
    t = proj.shape[0]
    ln = SSD_CHUNK
    nc = t // ln
    expand, expand_t, tri, eye = consts
    hd2 = 2 * SSD_HEADDIM

    def body(cp_ref, z_ref, dtr_ref, par_ref, g_ref, y_ref, st_ref, do_ref, e_ref, et_ref, tri_ref, eye_ref,
             dcp_ref, dz_ref, ddt_ref, dg_ref, dpar_ref, dstate, dx_buf, lane_buf, tot_buf, colsum):
        @pl.when(pl.program_id(0) == 0)
        def _():
            dstate[...] = jnp.zeros_like(dstate)
            dg_ref[...] = jnp.zeros_like(dg_ref)
            dpar_ref[...] = jnp.zeros_like(dpar_ref)

        cpv, et_v, tri_v, par_v = cp_ref[...], et_ref[...], tri_ref[...], par_ref[...]
        c = _ssd_chunk_terms(cpv, dtr_ref[...], par_v, e_ref[...], tri_v)
        xs = c["xs"]
        zv, yv, dov = z_ref[...], y_ref[...], do_ref[...]
        sz = _silu(zv)
        y2 = yv * sz
        r = lax.rsqrt(jnp.mean(y2 * y2, axis=-1, keepdims=True) + EPS)
        yh = y2 * r
        dg_ref[...] += jnp.sum(dov * yh, axis=0, keepdims=True)
        dyh = dov * g_ref[...]
        dy2 = r * (dyh - yh * jnp.mean(dyh * yh, axis=-1, keepdims=True))
        dz_ref[...] = dy2 * yv * _silu_grad(zv)
        dy = dy2 * sz

        xdt = xs * c["dt_e"]
        xdf = xdt * c["dec_e"]
        xb, xd = xdt.astype(MXU), xdf.astype(MXU)
        bb, cb = c["bm"].astype(MXU), c["cm"].astype(MXU)
        dyb, dye = dy.astype(MXU), (dy * c["eac_e"]).astype(MXU)
        mask = lax.broadcasted_iota(jnp.int32, (ln, ln), 0) >= lax.broadcasted_iota(jnp.int32, (ln, ln), 1)
        left = lax.broadcasted_iota(jnp.int32, (ln, hd2), 1) < SSD_HEADDIM
        lane_hp = lax.broadcasted_iota(jnp.int32, (ln, HP), 1)
        d_acum = jnp.zeros((ln, HP), F32)
        colsum[...] = jnp.zeros_like(colsum)
        tot_buf[...] = jnp.zeros_like(tot_buf)
        for g in range(SSD_GROUPS):
            nsl = slice(g * SSD_STATE, (g + 1) * SSD_STATE)
            gsl = slice(g * 256, (g + 1) * 256)
            bg, cg = bb[:, nsl], cb[:, nsl]
            cbm = _dot(cg, bg, _NT)
            st_g = st_ref[:, gsl]
            dst_g = dstate[:, gsl]
            stb, dstb = st_g.astype(MXU), dst_g.astype(MXU)
            y_off = _dot(cg, stb)
            bds = _dot(bg, dstb)
            dcb = jnp.zeros((ln, ln), F32)
            for pair in range(2):
                h0 = g * 4 + pair * 2
                psl = slice(h0 * SSD_HEADDIM, (h0 + 2) * SSD_HEADDIM)
                xp, dyp = xb[:, psl], dyb[:, psl]
                dxp = []
                for k in range(2):
                    h = h0 + k
                    lm = _decay_matrix(c["acum"], c["acum_t"], h, mask)
                    mm = cbm * lm
                    half = left if k == 0 else jnp.logical_not(left)
                    dm = _dot(jnp.where(half, dyp, jnp.zeros_like(dyp)), xp, _NT)
                    dcb = dcb + dm * lm
                    gm = dm * mm
                    d_acum = d_acum + jnp.where(lane_hp == h, jnp.sum(gm, axis=1, keepdims=True), 0.0)
                    colsum[h:h + 1, :] = jnp.sum(gm, axis=0, keepdims=True)
                    dxp.append(_dot(mm.astype(MXU), dyp, _TN))
                dx_buf[:, psl] = jnp.where(left, dxp[0], dxp[1])
            dx_buf[:, gsl] += bds * c["dec_e"][:, gsl]
            dcbb = dcb.astype(MXU)
            dc_g = _dot(dcbb, bg) + _dot(dye[:, gsl], stb, _NT)
            db_g = _dot(dcbb, cg, _TN) + _dot(xd[:, gsl], dstb, _NT)
            dcp_ref[:, D_SSD + g * SSD_STATE:D_SSD + (g + 1) * SSD_STATE] = db_g
            dcp_ref[:, D_SSD + SSD_BC + g * SSD_STATE:D_SSD + SSD_BC + (g + 1) * SSD_STATE] = dc_g
            dec_term = xdf[:, gsl] * bds
            lane_buf[:, gsl] = dy[:, gsl] * y_off * c["eac_e"][:, gsl] - dec_term
            tot_buf[0:1, gsl] = (jnp.sum(st_g * dst_g, axis=0, keepdims=True) * c["eat_e"][:, gsl]
                                 + jnp.sum(dec_term, axis=0, keepdims=True))
            dstate[:, gsl] = dst_g * c["eat_e"][:, gsl] + _dot(cg, dye[:, gsl], _TN)
        dx_tot = dx_buf[...]
        d_acum = d_acum + _dot(lane_buf[...], et_v, precision=HI) - _dot(colsum[...], eye_ref[...], _TN, precision=HI)
        d_atot = _dot(tot_buf[...], et_v, precision=HI)[0:1, :]
        row_hp = lax.broadcasted_iota(jnp.int32, (ln, HP), 0)
        d_acum = d_acum + jnp.where(row_hp == ln - 1, d_atot, 0.0)
        d_da = _dot(tri_v, d_acum, _TN, precision=HI)
        d_dt = d_da * c["a"] + _dot(dx_tot * xs, et_v, precision=HI)
        d_dtr = d_dt * _sigmoid(dtr_ref[...] + par_v[0:1, :])
        ddt_ref[...] = d_dtr
        dpar_ref[0:1, :] += jnp.sum(d_dtr, axis=0, keepdims=True)
        dpar_ref[1:2, :] += jnp.sum(d_da * c["dt"], axis=0, keepdims=True) * c["a"]
        dpar_ref[2:3, :] += _dot(_rows8(jnp.sum(dy * xs, axis=0, keepdims=True)), et_v, precision=HI)[0:1, :]
        dcp_ref[:, 0:D_SSD] = dx_tot * c["dt_e"] + dy * c["dsk_e"]
        dcp_ref[...] = dcp_ref[...] * _silu_grad(cpv)

    rev = lambda i: (nc - 1 - i, 0)
    rev1 = lambda i: (nc - 1 - i, 1)
    return pl.pallas_call(
        body, name="ssd_bwd", grid=(nc,),
        in_specs=[pl.BlockSpec((ln, D_CONV_CH), rev), pl.BlockSpec((ln, D_SSD), rev1), pl.BlockSpec((ln, HP), rev),
                  _const((8, HP)), _const((1, D_SSD)), pl.BlockSpec((ln, D_SSD), rev),
                  pl.BlockSpec((None, SSD_STATE, D_SSD), lambda i: (nc - 1 - i, 0, 0)),
                  pl.BlockSpec((ln, D_SSD), rev1),
                  _const((HP, D_SSD)), _const((D_SSD, HP)), _const((ln, ln)), _const((HP, HP))],
        out_specs=[pl.BlockSpec((ln, D_CONV_CH), rev), pl.BlockSpec((ln, D_SSD), rev), pl.BlockSpec((ln, HP), rev),
                   _const((1, D_SSD)), _const((8, HP))],
        out_shape=[_sds((t, D_CONV_CH), F32), _sds((t, D_SSD), F32), _sds((t, HP), F32), _sds((1, D_SSD), F32),
                   _sds((8, HP), F32)],
        scratch_shapes=[pltpu.VMEM((SSD_STATE, D_SSD), F32), pltpu.VMEM((ln, D_SSD), F32), pltpu.VMEM((ln, D_SSD), F32),
                        pltpu.VMEM((8, D_SSD), F32), pltpu.VMEM((HP, ln), F32)],
        compiler_params=_params("arbitrary"),
    )(conv_pre, proj, dtr, par, gnorm.reshape(1, D_SSD), y, states, d_ycat, expand, expand_t, tri, eye)


def _softmax_rows(s):
    e = jnp.exp(s - jnp.max(s, axis=-1, keepdims=True))
    return e / jnp.sum(e, axis=-1, keepdims=True)


def _attn_fwd(q, k, v):
    t, d = q.shape
    mlen = k.shape[0]
    tq = _tile(t, 512, 8)
    scale = XA_HEAD_DIM ** -0.5

    def body(q_ref, k_ref, v_ref, o_ref):
        for h in range(XA_HEADS):
            sl = slice(h * XA_HEAD_DIM, (h + 1) * XA_HEAD_DIM)
            p = _softmax_rows(_dot(q_ref[:, sl], k_ref[:, sl], _NT) * scale)
            o_ref[:, sl] = _dot(p.astype(MXU), v_ref[:, sl]).astype(o_ref.dtype)

    return pl.pallas_call(
        body, name="xattn_fwd", grid=(t // tq,),
        in_specs=[_row(tq, d), _const((mlen, d)), _const((mlen, d))], out_specs=_row(tq, d),
        out_shape=_sds((t, d), MXU), compiler_params=_params("parallel"),
    )(q, k, v)


def _attn_bwd(q, k, v, do):
    t, d = q.shape
    mlen = k.shape[0]
    tq = _tile(t, 512, 8)
    scale = XA_HEAD_DIM ** -0.5

    def body(q_ref, k_ref, v_ref, do_ref, dq_ref, dk_ref, dv_ref):
        @pl.when(pl.program_id(0) == 0)
        def _():
            dk_ref[...] = jnp.zeros_like(dk_ref)
            dv_ref[...] = jnp.zeros_like(dv_ref)

        for h in range(XA_HEADS):
            sl = slice(h * XA_HEAD_DIM, (h + 1) * XA_HEAD_DIM)
            qh, kh, vh, doh = q_ref[:, sl], k_ref[:, sl], v_ref[:, sl], do_ref[:, sl]
            p = _softmax_rows(_dot(qh, kh, _NT) * scale)
            dp = _dot(doh, vh, _NT)
            dv_ref[:, sl] += _dot(p.astype(MXU), doh, _TN)
            ds = (p * (dp - jnp.sum(p * dp, axis=-1, keepdims=True)) * scale).astype(MXU)
            dq_ref[:, sl] = _dot(ds, kh).astype(dq_ref.dtype)
            dk_ref[:, sl] += _dot(ds, qh, _TN)

    return pl.pallas_call(
        body, name="xattn_bwd", grid=(t // tq,),
        in_specs=[_row(tq, d), _const((mlen, d)), _const((mlen, d)), _row(tq, d)],
        out_specs=[_row(tq, d), _const((mlen, d)), _const((mlen, d))],
        out_shape=[_sds((t, d), MXU), _sds((mlen, d), F32), _sds((mlen, d), F32)],
        compiler_params=_params("arbitrary"),
    )(q, k, v, do)


def _lane_view(a):
    if a.ndim >= 2 and a.shape[-1] >= 128:
        return a.reshape(-1, a.shape[-1])
    if a.size % 128 == 0:
        return a.reshape(-1, 128)
    return a.reshape(1, -1)


def _adamw(w, g, m, v, *, name):
    shape = w.shape
    w2, g2, m2, v2 = (_lane_view(a) for a in (w, g.reshape(shape), m, v))
    r, c = w2.shape
    tr = _tile(r, max(8, (1 << 18) // max(c, 128)), 8)

    def body(w_ref, g_ref, m_ref, v_ref, d_ref, mo_ref, vo_ref):
        gv = g_ref[...]
        mn = ADAM_B1 * m_ref[...] + (1.0 - ADAM_B1) * gv
        vn = ADAM_B2 * v_ref[...] + (1.0 - ADAM_B2) * (gv * gv)
        m_hat = mn / (1.0 - ADAM_B1 ** ADAM_STEP)
        v_hat = vn / (1.0 - ADAM_B2 ** ADAM_STEP)
        d_ref[...] = -ADAM_LR * (m_hat / (jnp.sqrt(v_hat) + ADAM_EPS) + ADAM_WD * w_ref[...])
        mo_ref[...] = mn
        vo_ref[...] = vn

    outs = pl.pallas_call(
        body, name=name, grid=(r // tr,), in_specs=[_row(tr, c)] * 4, out_specs=[_row(tr, c)] * 3,
        out_shape=[_sds((r, c), F32)] * 3, compiler_params=_params("parallel"),
    )(w2, g2, m2, v2)
    return tuple(o.reshape(shape) for o in outs)


_HBM = pl.BlockSpec(memory_space=pltpu.HBM)


def _peer(k, xx, yy, cc):
    px = 1 - xx if k & 4 else xx
    py = 1 - yy if k & 2 else yy
    pc = 1 - cc if k & 1 else cc
    return (px, py, pc), 4 * px + 2 * py + pc


def _exchange(x, *, scatter, name):
    shape = x.shape[1:] if scatter else x.shape

    def body(x_ref, o_ref, send_sems, recv_sems, local_sem):
        xx, yy, cc = (lax.axis_index(a) for a in AXES)
        me = 4 * xx + 2 * yy + cc
        local = pltpu.make_async_copy(x_ref.at[me] if scatter else x_ref, o_ref.at[me], local_sem)
        local.start()
        sends = []
        for k in range(1, N_DEV):
            peer, pid = _peer(k, xx, yy, cc)
            cp = pltpu.make_async_remote_copy(
                src_ref=x_ref.at[pid] if scatter else x_ref, dst_ref=o_ref.at[me], send_sem=send_sems.at[k - 1],
                recv_sem=recv_sems.at[k - 1], device_id=peer, device_id_type=pl.DeviceIdType.MESH)
            cp.start()
            sends.append(cp)
        for k in range(1, N_DEV):
            peer, pid = _peer(k, xx, yy, cc)
            pltpu.make_async_remote_copy(
                src_ref=x_ref.at[pid] if scatter else x_ref, dst_ref=o_ref.at[pid], send_sem=send_sems.at[k - 1],
                recv_sem=recv_sems.at[k - 1], device_id=peer, device_id_type=pl.DeviceIdType.MESH).wait_recv()
        for cp in sends:
            cp.wait_send()
        local.wait()

    return pl.pallas_call(
        body, name=name, in_specs=[_HBM], out_specs=_HBM, out_shape=_sds((N_DEV,) + tuple(shape), x.dtype),
        scratch_shapes=[pltpu.SemaphoreType.DMA((N_DEV - 1,)), pltpu.SemaphoreType.DMA((N_DEV - 1,)),
                        pltpu.SemaphoreType.DMA(())],
    )(x)


def _sum_slots(x, *, name):
    _, r, c = x.shape
    tr = _tile(r, max(8, (1 << 17) // c), 8)

    def body(x_ref, o_ref):
        acc = x_ref[0]
        for d in range(1, N_DEV):
            acc = acc + x_ref[d]
        o_ref[...] = acc

    return pl.pallas_call(
        body, name=name, grid=(r // tr,), in_specs=[pl.BlockSpec((N_DEV, tr, c), lambda i: (0, i, 0))],
        out_specs=_row(tr, c), out_shape=_sds((r, c), F32), compiler_params=_params("parallel"),
    )(x)


def _s5_layouts(bbar_r, bbar_i, c_re, c_im):
    eye = jnp.eye(S5_GPB, dtype=F32)

    def b_blocks(bbar):
        bb = bbar.reshape(S5_NB, S5_GPB, S5_STATE, S5_GROUP)
        return jnp.einsum("jgph,gk->jghkp", bb, eye).reshape(S5_NB, S5_GPB * S5_GROUP, S5_GPB * S5_STATE)

    def c_blocks(cc):
        c4 = cc.reshape(S5_NB, S5_GPB, S5_GROUP, S5_STATE)
        return jnp.einsum("jghp,gk->jgpkh", c4, eye).reshape(S5_NB, S5_GPB * S5_STATE, S5_GPB * S5_GROUP)

    bre, bim, cre, cim = b_blocks(bbar_r), b_blocks(bbar_i), c_blocks(c_re), c_blocks(c_im)
    cast = lambda a: a.astype(MXU)
    sw = lambda a: jnp.swapaxes(a, 1, 2).astype(MXU)
    return dict(bre=cast(bre), bim=cast(bim), cre=cast(cre), cim=cast(cim), bre_t=sw(bre), bim_t=sw(bim), cre_t=sw(cre),
                cim_t=sw(cim))


def _b_diag(db):
    d5 = db.reshape(S5_NB, S5_GPB, S5_GROUP, S5_GPB, S5_STATE)
    return jnp.einsum("jghgp->jgph", d5).reshape(S5_GROUPS, S5_STATE * S5_GROUP)


def _c_diag(dc):
    d5 = dc.reshape(S5_NB, S5_GPB, S5_STATE, S5_GPB, S5_GROUP)
    return jnp.einsum("jgpgh->jghp", d5).reshape(S5_GROUPS, S5_GROUP, S5_STATE)


def _head_rows(*vecs):
    par = jnp.zeros((8, HP), F32)
    for i, v in enumerate(vecs):
        par = par.at[i, :SSD_HEADS].set(v.astype(F32))
    return par


def _add(acc, r):
    return (acc + r,)


def _layer_fwd(x, mem, w, consts):
    s = {"x": x}
    rep = consts["rep"]
    s["h1"] = h1 = _rmsnorm_fwd(x, w["norm_mix"], name="norm_mix_fwd")
    s["proj"] = proj = _mm(h1, w["w_main"], name="in_proj")
    s["dtr"] = dtr = _mm(h1, w["w_dt"], name="dt_proj")
    ar, ai, ldt = w["s5_a_re"], w["s5_a_im"], w["s5_log_dt"].reshape(S5_GROUPS, 1)
    br, bi = w["s5_b_re"].reshape(S5_GROUPS, -1), w["s5_b_im"].reshape(S5_GROUPS, -1)
    abar_r, abar_i, bbar_r, bbar_i = _s5_prep(ar, ai, ldt, br, bi, rep)
    s["abar"] = abar = (abar_r.reshape(1, S5_CH), abar_i.reshape(1, S5_CH))
    s["lay"] = lay = _s5_layouts(bbar_r, bbar_i, w["s5_c_re"], w["s5_c_im"])
    drv_r, drv_i = _bdmm([(proj, lay["bre"]), (proj, lay["bim"])], nb=S5_NB, ko=256, no=1024,
                         epi=lambda p: (p[0], p[1]), out_dtypes=(F32, F32), name="s5_drive")
    s["sr"], s["si"] = sr, si = _s5_scan(*abar, drv_r, drv_i, reverse=False, name="s5_scan_fwd")
    d_vec = w["s5_d"].reshape(1, D_S5)
    s["ypre"], s["yg"] = ypre, yg = _bdmm(
        [(sr, lay["cre"]), (si, lay["cim"])], nb=S5_NB, ko=1024, no=256, extras=[proj], vecs=[d_vec],
        epi=lambda p, u, dv: ((yp := p[0] - p[1] + dv * u), _gelu(yp)), out_dtypes=(F32, MXU), name="s5_readout")
    s["gp"], out_s5 = _mm(yg, w["s5_w_glu"], extras=[ypre], epi=lambda acc, yp: (acc, _gelu(yp) * _sigmoid(acc)),
                          out_dtypes=(F32, MXU), name="s5_glu")
    s["conv_pre"] = conv_pre = _conv_fwd(proj, w["ssd_conv_w"], w["ssd_conv_b"])
    s["par"] = par = _head_rows(w["ssd_dt_bias"], w["ssd_a_log"], w["ssd_d"])
    out_ssd, s["y_ssd"], s["states"] = _ssd_fwd(proj, conv_pre, dtr, par, w["ssd_norm"], consts["ssd"])
    s["ycat"] = ycat = jnp.concatenate([out_s5, out_ssd], axis=1)
    s["x1"] = x1 = _mm(ycat, w["w_out"], extras=[x], epi=_add, name="out_proj")
    s["hq"] = hq = _rmsnorm_fwd(x1, w["norm_xattn"], name="norm_xattn_fwd")
    s["mn"] = mn = _rmsnorm_fwd(mem, w["norm_mem"], name="norm_mem_fwd")
    s["q"] = q = _mm(hq, w["xa_wq"], out_dtypes=(MXU,), name="xa_q")
    s["k"] = k = _mm(mn, w["xa_wk"], out_dtypes=(MXU,), name="xa_k")
    s["v"] = v = _mm(mn, w["xa_wv"], out_dtypes=(MXU,), name="xa_v")
    s["o"] = o = _attn_fwd(q, k, v)
    s["x2"] = x2 = _mm(o, w["xa_wo"], extras=[x1], epi=_add, name="xa_o")
    s["hm"] = hm = _rmsnorm_fwd(x2, w["norm_mlp"], name="norm_mlp_fwd")
    s["a"], s["act"] = _mm(hm, w["mlp_w1"], epi=lambda acc: (acc, jnp.square(jnp.maximum(acc, 0.0))),
                           out_dtypes=(F32, MXU), name="mlp_up")
    x3 = _mm(s["act"], w["mlp_w2"], extras=[x2], epi=_add, name="mlp_down")
    return x3, s


def _layer_bwd(dx3, mem, w, s, consts):
    g = {}
    rep = consts["rep"]
    d_a = _mm(dx3, w["mlp_w2"], tb=True, extras=[s["a"]], epi=lambda acc, av: (acc * (2.0 * jnp.maximum(av, 0.0)),),
              out_dtypes=(MXU,), name="mlp_down_dx")
    g["mlp_w2"] = _mm(s["act"], dx3, ta=True, name="mlp_down_dw")
    g["mlp_w1"] = _mm(s["hm"], d_a, ta=True, name="mlp_up_dw")
    d_hm = _mm(d_a, w["mlp_w1"], tb=True, name="mlp_up_dx")
    dx2, g["norm_mlp"] = _rmsnorm_bwd(s["x2"], w["norm_mlp"], d_hm, dx3, name="norm_mlp_bwd")
    d_o = _mm(dx2, w["xa_wo"], tb=True, out_dtypes=(MXU,), name="xa_o_dx")
    g["xa_wo"] = _mm(s["o"], dx2, ta=True, name="xa_o_dw")
    dq, dk, dv = _attn_bwd(s["q"], s["k"], s["v"], d_o)
    g["xa_wq"] = _mm(s["hq"], dq, ta=True, name="xa_q_dw")
    d_hq = _mm(dq, w["xa_wq"], tb=True, name="xa_q_dx")
    dx1, g["norm_xattn"] = _rmsnorm_bwd(s["x1"], w["norm_xattn"], d_hq, dx2, name="norm_xattn_bwd")
    g["xa_wk"] = _mm(s["mn"], dk, ta=True, name="xa_k_dw")
    g["xa_wv"] = _mm(s["mn"], dv, ta=True, name="xa_v_dw")
    d_mn_v = _mm(dv, w["xa_wv"], tb=True, name="xa_v_dx")
    d_mn = _mm(dk, w["xa_wk"], tb=True, extras=[d_mn_v], epi=_add, name="xa_k_dx")
    _, g["norm_mem"] = _rmsnorm_bwd(mem, w["norm_mem"], d_mn, None, name="norm_mem_bwd")
    d_ycat = _mm(dx1, w["w_out"], tb=True, name="out_proj_dx")
    g["w_out"] = _mm(s["ycat"], dx1, ta=True, name="out_proj_dw")
    lay, proj, ypre = s["lay"], s["proj"], s["ypre"]
    d_gp = _s5_gate_bwd(d_ycat, s["gp"], ypre)
    g["s5_w_glu"] = _mm(s["yg"], d_gp, ta=True, name="s5_glu_dw")
    d_ypre = _mm(d_gp, w["s5_w_glu"], tb=True, extras=[d_ycat, s["gp"], ypre],
                 epi=lambda acc, do, gp, yp: ((acc + do * _sigmoid(gp)) * _gelu_grad(yp),), name="s5_glu_dx")
    ds_r, ds_i = _bdmm([(d_ypre, lay["cre_t"]), (d_ypre, lay["cim_t"])], nb=S5_NB, ko=256, no=1024,
                       epi=lambda p: (p[0], -p[1]), out_dtypes=(F32, F32), name="s5_readout_ds")
    lam_r, lam_i = _s5_scan(*s["abar"], ds_r, ds_i, reverse=True, name="s5_scan_bwd")
    d_abar_r, d_abar_i = _s5_dabar(s["sr"], s["si"], lam_r, lam_i)
    d_bbar_r = _b_diag(_bdmm_tn(proj, 256, lam_r, 1024, nb=S5_NB, name="s5_drive_dw_re"))
    d_bbar_i = _b_diag(_bdmm_tn(proj, 256, lam_i, 1024, nb=S5_NB, name="s5_drive_dw_im"))
    g["s5_c_re"] = _c_diag(_bdmm_tn(s["sr"], 1024, d_ypre, 256, nb=S5_NB, name="s5_readout_dw_re"))
    g["s5_c_im"] = -_c_diag(_bdmm_tn(s["si"], 1024, d_ypre, 256, nb=S5_NB, name="s5_readout_dw_im"))
    g["s5_d"] = _colsum_prod(d_ypre, proj, D_S5, name="s5_d_dw").reshape(S5_GROUPS, S5_GROUP)
    du = _bdmm([(lam_r, lay["bre_t"]), (lam_i, lay["bim_t"])], nb=S5_NB, ko=1024, no=256, extras=[d_ypre],
               vecs=[w["s5_d"].reshape(1, D_S5)], epi=lambda p, dyp, dv: (p[0] + p[1] + dv * dyp,), name="s5_drive_du")
    ar, ai, ldt = w["s5_a_re"], w["s5_a_im"], w["s5_log_dt"].reshape(S5_GROUPS, 1)
    br, bi = w["s5_b_re"].reshape(S5_GROUPS, -1), w["s5_b_im"].reshape(S5_GROUPS, -1)
    d_ar, d_ai, d_ldt, d_br, d_bi = _s5_prep_bwd(
        ar, ai, ldt, br, bi, rep, d_abar_r.reshape(S5_GROUPS, S5_STATE), d_abar_i.reshape(S5_GROUPS, S5_STATE),
        d_bbar_r, d_bbar_i)
    g["s5_a_re"], g["s5_a_im"], g["s5_log_dt"] = d_ar, d_ai, d_ldt.reshape(S5_GROUPS)
    g["s5_b_re"] = d_br.reshape(S5_GROUPS, S5_STATE, S5_GROUP)
    g["s5_b_im"] = d_bi.reshape(S5_GROUPS, S5_STATE, S5_GROUP)
    d_cp, dz, d_dtr, g_ssd_norm, d_par = _ssd_bwd(proj, s["conv_pre"], s["dtr"], s["par"], w["ssd_norm"], s["y_ssd"],
                                                  s["states"], d_ycat, consts["ssd"])
    g["ssd_norm"] = g_ssd_norm
    g["ssd_dt_bias"], g["ssd_a_log"], g["ssd_d"] = (d_par[i, :SSD_HEADS] for i in range(3))
    d_xbc, g["ssd_conv_w"], g["ssd_conv_b"] = _conv_bwd(proj, w["ssd_conv_w"], d_cp)
    d_proj = jnp.concatenate([du, dz, d_xbc], axis=1)
    g_main = _mm(s["h1"], d_proj, ta=True, name="in_proj_dw")
    g_dt = _mm(s["h1"], d_dtr, ta=True, name="dt_proj_dw")
    g["w_in"] = jnp.concatenate([g_main, g_dt[:, :SSD_HEADS]], axis=1)
    d_h1_dt = _mm(d_dtr, w["w_dt"], tb=True, name="dt_proj_dx")
    d_h1 = _mm(d_proj, w["w_main"], tb=True, extras=[d_h1_dt], epi=_add, name="in_proj_dx")
    dx, g["norm_mix"] = _rmsnorm_bwd(s["x"], w["norm_mix"], d_h1, dx1, name="norm_mix_bwd")
    return dx, g


LAYER_WEIGHTS = ("norm_mix", "w_in", "s5_a_re", "s5_a_im", "s5_log_dt", "s5_b_re", "s5_b_im", "s5_c_re", "s5_c_im", "s5_d",
                 "s5_w_glu", "ssd_conv_w", "ssd_conv_b", "ssd_dt_bias", "ssd_a_log", "ssd_d", "ssd_norm", "w_out",
                 "norm_xattn", "norm_mem", "xa_wq", "xa_wk", "xa_wv", "xa_wo", "norm_mlp", "mlp_w1", "mlp_w2")
WEIGHTS = LAYER_WEIGHTS + ("norm_final",)


def _local_step(x, mem, target, weights):
    consts = {
        "ssd": _ssd_consts(),
        "rep": (jnp.arange(S5_STATE)[:, None] == jnp.arange(S5_STATE * S5_GROUP)[None, :] // S5_GROUP).astype(F32),
    }
    layers = []
    for l in range(DEPTH):
        w = {n: weights[n][l] for n in LAYER_WEIGHTS}
        w_in = w["w_in"]
        w["w_main"] = w_in[:, :D_MAIN]
        w["w_dt"] = jnp.pad(w_in[:, D_MAIN:], ((0, 0), (0, HP - SSD_HEADS)))
        layers.append(w)
    saved = []
    for l in range(DEPTH):
        x, s = _layer_fwd(x, mem, layers[l], consts)
        saved.append(s)
    loss, dx, g_final = _loss_head(x, weights["norm_final"], target)
    grads = [None] * DEPTH
    for l in reversed(range(DEPTH)):
        dx, grads[l] = _layer_bwd(dx, mem, layers[l], saved[l], consts)
    out = {n: jnp.stack([grads[l][n].reshape(weights[n].shape[1:]) for l in range(DEPTH)]) for n in LAYER_WEIGHTS}
    out["norm_final"] = g_final.reshape(weights["norm_final"].shape)
    return loss, dx, out


SHARDED = {"w_in": 2, "s5_w_glu": 1, "ssd_conv_w": 2, "w_out": 1, "xa_wq": 1, "xa_wk": 1, "xa_wv": 1, "xa_wo": 1,
           "mlp_w1": 2, "mlp_w2": 1}
EXACT = ("ssd_conv_w",)
REPLICATED = tuple(n for n in WEIGHTS if n not in SHARDED)
LANES = 128
PAD_ROWS = 16


def _to_rows(flat, lead=()):
    n = flat.shape[-1]
    quantum = LANES * PAD_ROWS
    padded = -(-n // quantum) * quantum
    flat = jnp.pad(flat, [(0, 0)] * len(lead) + [(0, padded - n)])
    return flat.reshape(*lead, padded // LANES, LANES)


def _gather_weights(local):
    parts = []
    for n in SHARDED:
        if n in EXACT:
            parts.append(lax.bitcast_convert_type(local[n], MXU).reshape(-1))
        else:
            parts.append(local[n].astype(MXU).reshape(-1))
    gathered = _exchange(_to_rows(jnp.concatenate(parts)), scatter=False, name="gather_weights")
    flat = gathered.reshape(N_DEV, -1)
    full, off = {}, 0
    for n, ax in SHARDED.items():
        shp = local[n].shape
        size = math.prod(shp) * (2 if n in EXACT else 1)
        seg = flat[:, off:off + size]
        off += size
        if n in EXACT:
            seg = lax.bitcast_convert_type(seg.reshape(N_DEV, *shp, 2), F32)
        else:
            seg = seg.reshape(N_DEV, *shp)
        seg = jnp.moveaxis(seg, 0, ax)
        full[n] = seg.reshape(*shp[:ax], N_DEV * shp[ax], *shp[ax + 1:])
    return full


def _scatter_grads(grads, local_shapes):
    parts = []
    for n, ax in SHARDED.items():
        shp = local_shapes[n]
        gfull = grads[n].reshape(*shp[:ax], N_DEV, shp[ax], *shp[ax + 1:])
        parts.append(jnp.moveaxis(gfull, ax, 0).reshape(N_DEV, -1))
    payload = _to_rows(jnp.concatenate(parts, axis=1), lead=(N_DEV,))
    summed = _sum_slots(_exchange(payload, scatter=True, name="scatter_grads"), name="sum_sharded_grads").reshape(-1)
    out, off = {}, 0
    for n in SHARDED:
        size = math.prod(local_shapes[n])
        out[n] = summed[off:off + size].reshape(local_shapes[n])
        off += size
    return out


def _allreduce_small(loss, grads):
    parts = [loss.reshape(-1)[:1]] + [grads[n].reshape(-1) for n in REPLICATED]
    payload = _to_rows(jnp.concatenate(parts))
    summed = _sum_slots(_exchange(payload, scatter=False, name="gather_small_grads"), name="sum_small_grads").reshape(-1)
    out, off = {}, 1
    for n in REPLICATED:
        size = grads[n].size
        out[n] = summed[off:off + size].reshape(grads[n].shape)
        off += size
    return summed[0], out


def kernel(x, mem, norm_mix, w_in, s5_a_re, s5_a_im, s5_log_dt, s5_b_re, s5_b_im, s5_c_re, s5_c_im, s5_d, s5_w_glu, ssd_conv_w, ssd_conv_b, ssd_dt_bias, ssd_a_log, ssd_d, ssd_norm, w_out, norm_xattn, norm_mem, xa_wq, xa_wk, xa_wv, xa_wo, norm_mlp, mlp_w1, mlp_w2, norm_final, loss_target, m_norm_mix, m_w_in, m_s5_a_re, m_s5_a_im, m_s5_log_dt, m_s5_b_re, m_s5_b_im, m_s5_c_re, m_s5_c_im, m_s5_d, m_s5_w_glu, m_ssd_conv_w, m_ssd_conv_b, m_ssd_dt_bias, m_ssd_a_log, m_ssd_d, m_ssd_norm, m_w_out, m_norm_xattn, m_norm_mem, m_xa_wq, m_xa_wk, m_xa_wv, m_xa_wo, m_norm_mlp, m_mlp_w1, m_mlp_w2, m_norm_final, v_norm_mix, v_w_in, v_s5_a_re, v_s5_a_im, v_s5_log_dt, v_s5_b_re, v_s5_b_im, v_s5_c_re, v_s5_c_im, v_s5_d, v_s5_w_glu, v_ssd_conv_w, v_ssd_conv_b, v_ssd_dt_bias, v_ssd_a_log, v_ssd_d, v_ssd_norm, v_w_out, v_norm_xattn, v_norm_mem, v_xa_wq, v_xa_wk, v_xa_wv, v_xa_wo, v_norm_mlp, v_mlp_w1, v_mlp_w2, v_norm_final):
    args = locals()
    local = {n: args[n] for n in WEIGHTS}
    full = dict(local)
    full.update(_gather_weights(local))
    loss, grad_x, grads = _local_step(x[0], mem[0], loss_target[0], full)
    loss, g_small = _allreduce_small(loss, grads)
    g_all = _scatter_grads(grads, {n: local[n].shape for n in SHARDED})
    g_all.update(g_small)
    delta, new_m, new_v = {}, {}, {}
    for n in WEIGHTS:
        delta[n], new_m[n], new_v[n] = _adamw(local[n], g_all[n], args["m_" + n], args["v_" + n], name="adamw_" + n)
    return (loss, grad_x[None], *[g_all[n] for n in WEIGHTS], *[delta[n] for n in WEIGHTS],
            *[new_m[n] for n in WEIGHTS], *[new_v[n] for n in WEIGHTS])
```

```python
import functools
import math

import jax
import jax.numpy as jnp
from jax import lax
from jax.experimental import pallas as pl
from jax.experimental.pallas import tpu as pltpu

F32 = jnp.float32
MXU = jnp.bfloat16
HI = lax.Precision.HIGHEST

D_MODEL = 1024
DEPTH = 4
MEM_LEN = 256
D_S5 = 1024
D_SSD = 1024
S5_GROUP = 16
S5_GROUPS = 64
S5_STATE = 64
S5_CH = S5_GROUPS * S5_STATE
S5_NB = 4
S5_GPB = S5_GROUPS // S5_NB
SSD_HEADDIM = 64
SSD_HEADS = 16
SSD_GROUPS = 4
SSD_STATE = 128
SSD_CONV = 4
SSD_CHUNK = 128
SSD_BC = SSD_GROUPS * SSD_STATE
D_CONV_CH = 2048
D_MAIN = 4096
D_IN_PROJ = D_MAIN + SSD_HEADS
HP = 128
XA_HEADS = 4
XA_HEAD_DIM = 256
D_FF = 4096
EPS = 1e-5
N_DEV = 8
AXES = ("x", "y", "c")

ADAM_LR = 0.001
ADAM_B1 = 0.9
ADAM_B2 = 0.999
ADAM_EPS = 1e-08
ADAM_WD = 0.01
ADAM_STEP = 10

VMEM_LIMIT = 56 * 1024 * 1024


def _params(*sem):
    return pltpu.CompilerParams(dimension_semantics=sem, vmem_limit_bytes=VMEM_LIMIT)


def _tile(n, pref, quantum=128):
    t = (min(pref, n) // quantum) * quantum
    while t >= quantum:
        if n % t == 0:
            return t
        t -= quantum
    return n


def _sds(shape, dtype):
    return jax.ShapeDtypeStruct(tuple(shape), dtype)


def _sigmoid(x):
    return 1.0 / (1.0 + jnp.exp(-x))


def _silu(x):
    return x * _sigmoid(x)


def _silu_grad(x):
    s = _sigmoid(x)
    return s * (1.0 + x * (1.0 - s))


_GELU_C = math.sqrt(2.0 / math.pi)


def _gelu(x):
    return 0.5 * x * (1.0 + jnp.tanh(_GELU_C * (x + 0.044715 * x * x * x)))


def _gelu_grad(x):
    th = jnp.tanh(_GELU_C * (x + 0.044715 * x * x * x))
    return 0.5 * (1.0 + th) + 0.5 * x * (1.0 - th * th) * _GELU_C * (1.0 + 3.0 * 0.044715 * x * x)


def _softplus(x):
    return jnp.maximum(x, 0.0) + jnp.log(1.0 + jnp.exp(-jnp.abs(x)))


_NN = (((1,), (0,)), ((), ()))
_NT = (((1,), (1,)), ((), ()))
_TN = (((0,), (0,)), ((), ()))


def _dot(a, b, dims=_NN, precision=None):
    return lax.dot_general(a, b, dims, precision=precision, preferred_element_type=F32)


def _rows8(v):
    return jnp.broadcast_to(v, (8, v.shape[1]))


def _mm(a, b, *, ta=False, tb=False, extras=(), epi=None, out_dtypes=(F32,), tm=1024, tn=1024, tk=1024, name):
    m, k = (a.shape[1], a.shape[0]) if ta else a.shape
    n = b.shape[0] if tb else b.shape[1]
    assert k == (b.shape[1] if tb else b.shape[0]), (a.shape, b.shape, ta, tb)
    tm, tn, tk = _tile(m, tm), _tile(n, tn), _tile(k, tk)
    nk = k // tk
    n_ex, n_out = len(extras), len(out_dtypes)
    dims = (((0,) if ta else (1,), (1,) if tb else (0,)), ((), ()))

    def body(a_ref, b_ref, *rest):
        ex_refs, out_refs = rest[:n_ex], rest[n_ex:n_ex + n_out]
        prod = _dot(a_ref[...].astype(MXU), b_ref[...].astype(MXU), dims)

        def finish(total):
            outs = epi(total, *[e[...] for e in ex_refs]) if epi is not None else (total,)
            for o, r in zip(outs, out_refs, strict=True):
                r[...] = o.astype(r.dtype)

        if nk == 1:
            finish(prod)
            return
        acc = rest[-1]
        kk = pl.program_id(2)

        @pl.when(kk == 0)
        def _():
            acc[...] = prod

        @pl.when(jnp.logical_and(kk > 0, kk < nk - 1))
        def _():
            acc[...] += prod

        @pl.when(kk == nk - 1)
        def _():
            finish(acc[...] + prod)

    a_spec = pl.BlockSpec((tk, tm), lambda i, j, kk: (kk, i)) if ta else pl.BlockSpec((tm, tk), lambda i, j, kk: (i, kk))
    b_spec = pl.BlockSpec((tn, tk), lambda i, j, kk: (j, kk)) if tb else pl.BlockSpec((tk, tn), lambda i, j, kk: (kk, j))
    mn_spec = pl.BlockSpec((tm, tn), lambda i, j, kk: (i, j))
    outs = pl.pallas_call(
        body,
        name=name,
        grid=(m // tm, n // tn, nk),
        in_specs=[a_spec, b_spec] + [mn_spec] * n_ex,
        out_specs=[mn_spec] * n_out,
        out_shape=[_sds((m, n), dt) for dt in out_dtypes],
        scratch_shapes=[pltpu.VMEM((tm, tn), F32)] if nk > 1 else [],
        compiler_params=_params("parallel", "parallel", "arbitrary"),
    )(a, b, *extras)
    return outs[0] if n_out == 1 else outs


def _bdmm(terms, *, nb, ko, no, extras=(), vecs=(), epi=None, out_dtypes=(F32,), tm=512, name):
    t = terms[0][0].shape[0]
    tm = _tile(t, tm)
    n_t, n_ex, n_v, n_out = len(terms), len(extras), len(vecs), len(out_dtypes)

    def body(*refs):
        a_refs, w_refs = refs[0:2 * n_t:2], refs[1:2 * n_t:2]
        ex_refs = refs[2 * n_t:2 * n_t + n_ex + n_v]
        out_refs = refs[2 * n_t + n_ex + n_v:]
        prods = [_dot(a[...].astype(MXU), w[...].astype(MXU)) for a, w in zip(a_refs, w_refs)]
        outs = epi(prods, *[e[...] for e in ex_refs]) if epi is not None else (prods[0],)
        for o, r in zip(outs, out_refs, strict=True):
            r[...] = o.astype(r.dtype)

    in_specs, args = [], []
    for a, w in terms:
        in_specs.append(pl.BlockSpec((tm, ko), lambda j, i: (i, j)))
        in_specs.append(pl.BlockSpec((None, ko, no), lambda j, i: (j, 0, 0)))
        args += [a, w]
    o_spec = pl.BlockSpec((tm, no), lambda j, i: (i, j))
    v_spec = pl.BlockSpec((1, no), lambda j, i: (0, j))
    outs = pl.pallas_call(
        body,
        name=name,
        grid=(nb, t // tm),
        in_specs=in_specs + [o_spec] * n_ex + [v_spec] * n_v,
        out_specs=[o_spec] * n_out,
        out_shape=[_sds((t, nb * no), dt) for dt in out_dtypes],
        compiler_params=_params("parallel", "parallel"),
    )(*args, *extras, *vecs)
    return outs[0] if n_out == 1 else outs


def _bdmm_tn(a, ka, b, nbc, *, nb, tt=512, name):
    t = a.shape[0]
    tt = _tile(t, tt)
    nt = t // tt

    def body(a_ref, b_ref, o_ref, acc):
        s = pl.program_id(1)

        @pl.when(s == 0)
        def _():
            acc[...] = jnp.zeros_like(acc)

        acc[...] += _dot(a_ref[...].astype(MXU), b_ref[...].astype(MXU), _TN)

        @pl.when(s == nt - 1)
        def _():
            o_ref[...] = acc[...]

    return pl.pallas_call(
        body,
        name=name,
        grid=(nb, nt),
        in_specs=[pl.BlockSpec((tt, ka), lambda j, s: (s, j)), pl.BlockSpec((tt, nbc), lambda j, s: (s, j))],
        out_specs=pl.BlockSpec((None, ka, nbc), lambda j, s: (j, 0, 0)),
        out_shape=_sds((nb, ka, nbc), F32),
        scratch_shapes=[pltpu.VMEM((ka, nbc), F32)],
        compiler_params=_params("parallel", "arbitrary"),
    )(a, b)


def _row(tb, w, cb=0):
    return pl.BlockSpec((tb, w), lambda i: (i, cb))


def _const(shape):
    return pl.BlockSpec(shape, lambda i: (0,) * len(shape))


def _rmsnorm_fwd(x, g, *, name):
    t, d = x.shape
    tb = _tile(t, 512, 8)

    def body(x_ref, g_ref, h_ref):
        xv = x_ref[...]
        r = lax.rsqrt(jnp.mean(xv * xv, axis=-1, keepdims=True) + EPS)
        h_ref[...] = (xv * r * g_ref[...]).astype(h_ref.dtype)

    return pl.pallas_call(
        body, name=name, grid=(t // tb,), in_specs=[_row(tb, d), _const((1, d))], out_specs=_row(tb, d),
        out_shape=_sds((t, d), MXU), compiler_params=_params("parallel"),
    )(x, g.reshape(1, d))


def _rmsnorm_bwd(x, g, dh, dres, *, name):
    t, d = x.shape
    tb = _tile(t, 256, 8)
    has_res = dres is not None

    def body(x_ref, g_ref, dh_ref, *rest):
        dx_ref, dg_ref = rest[-2:]

        @pl.when(pl.program_id(0) == 0)
        def _():
            dg_ref[...] = jnp.zeros_like(dg_ref)

        xv = x_ref[...]
        r = lax.rsqrt(jnp.mean(xv * xv, axis=-1, keepdims=True) + EPS)
        xh = xv * r
        dhv = dh_ref[...].astype(F32)
        dg_ref[...] += jnp.sum(dhv * xh, axis=0, keepdims=True)
        dxh = dhv * g_ref[...]
        dx = r * (dxh - xh * jnp.mean(dxh * xh, axis=-1, keepdims=True))
        if has_res:
            dx = dx + rest[0][...]
        dx_ref[...] = dx

    ins = [x, g.reshape(1, d), dh] + ([dres] if has_res else [])
    return pl.pallas_call(
        body, name=name, grid=(t // tb,),
        in_specs=[_row(tb, d), _const((1, d)), _row(tb, d)] + ([_row(tb, d)] if has_res else []),
        out_specs=[_row(tb, d), _const((1, d))],
        out_shape=[_sds((t, d), F32), _sds((1, d), F32)],
        compiler_params=_params("arbitrary"),
    )(*ins)


def _loss_head(x, g, target):
    t, d = x.shape
    tb = _tile(t, 256, 8)

    def body(x_ref, g_ref, tg_ref, loss_ref, dx_ref, dg_ref):
        @pl.when(pl.program_id(0) == 0)
        def _():
            dg_ref[...] = jnp.zeros_like(dg_ref)
            loss_ref[...] = jnp.zeros_like(loss_ref)

        xv, gv = x_ref[...], g_ref[...]
        r = lax.rsqrt(jnp.mean(xv * xv, axis=-1, keepdims=True) + EPS)
        xh = xv * r
        err = xh * gv - tg_ref[...]
        loss_ref[...] += 0.5 * jnp.sum(jnp.mean(err * err, axis=-1, keepdims=True), axis=0, keepdims=True)
        dy = err * (1.0 / d)
        dg_ref[...] += jnp.sum(dy * xh, axis=0, keepdims=True)
        dxh = dy * gv
        dx_ref[...] = r * (dxh - xh * jnp.mean(dxh * xh, axis=-1, keepdims=True))

    return pl.pallas_call(
        body, name="loss_head", grid=(t // tb,),
        in_specs=[_row(tb, d), _const((1, d)), _row(tb, d)],
        out_specs=[_const((1, HP)), _row(tb, d), _const((1, d))],
        out_shape=[_sds((1, HP), F32), _sds((t, d), F32), _sds((1, d), F32)],
        compiler_params=_params("arbitrary"),
    )(x, g.reshape(1, d), target)


def _s5_discretise(ar, ai, ldt, br, bi, rep):
    dt = jnp.exp(ldt)
    mag = jnp.exp(dt * ar)
    abar_r, abar_i = mag * jnp.cos(dt * ai), mag * jnp.sin(dt * ai)
    den = ar * ar + ai * ai
    zr, zi = abar_r - 1.0, abar_i
    fr = (zr * ar + zi * ai) / den
    fi = (zi * ar - zr * ai) / den
    fr_e, fi_e = _dot(fr, rep, precision=HI), _dot(fi, rep, precision=HI)
    return abar_r, abar_i, fr_e * br - fi_e * bi, fr_e * bi + fi_e * br


def _s5_prep(ar, ai, ldt, br, bi, rep):
    g, p = ar.shape
    ph = br.shape[1]

    def body(ar_ref, ai_ref, ldt_ref, br_ref, bi_ref, rep_ref, o0, o1, o2, o3):
        outs = _s5_discretise(ar_ref[...], ai_ref[...], ldt_ref[...], br_ref[...], bi_ref[...], rep_ref[...])
        for o, v in zip((o0, o1, o2, o3), outs):
            o[...] = v

    return pl.pallas_call(
        body, name="s5_prep",
        out_shape=[_sds((g, p), F32), _sds((g, p), F32), _sds((g, ph), F32), _sds((g, ph), F32)],
        compiler_params=pltpu.CompilerParams(vmem_limit_bytes=VMEM_LIMIT),
    )(ar, ai, ldt, br, bi, rep)


def _s5_prep_bwd(ar, ai, ldt, br, bi, rep, d_abar_r, d_abar_i, d_bbar_r, d_bbar_i):
    g, p = ar.shape
    ph = br.shape[1]

    def body(ar_ref, ai_ref, ldt_ref, br_ref, bi_ref, rep_ref, c0, c1, c2, c3, o0, o1, o2, o3, o4):
        rep_v = rep_ref[...]
        _, vjp = jax.vjp(lambda a, b, c, d, e: _s5_discretise(a, b, c, d, e, rep_v),
                         ar_ref[...], ai_ref[...], ldt_ref[...], br_ref[...], bi_ref[...])
        grads = vjp((c0[...], c1[...], c2[...], c3[...]))
        for o, v in zip((o0, o1, o2, o3, o4), grads):
            o[...] = v

    return pl.pallas_call(
        body, name="s5_prep_bwd",
        out_shape=[_sds((g, p), F32), _sds((g, p), F32), _sds((g, 1), F32), _sds((g, ph), F32), _sds((g, ph), F32)],
        compiler_params=pltpu.CompilerParams(vmem_limit_bytes=VMEM_LIMIT),
    )(ar, ai, ldt, br, bi, rep, d_abar_r, d_abar_i, d_bbar_r, d_bbar_i)


SCAN_ROWS = 256
SCAN_LANES = 1024


def _scan_rows(t):
    return _tile(t, SCAN_ROWS, 64)


def _interleave_rows(x):
    t, c = x.shape
    tb = _scan_rows(t)
    return x.reshape(t // tb, 8, tb // 8, c).swapaxes(1, 2).reshape(t, c)


def _deinterleave_rows(x):
    t, c = x.shape
    tb = _scan_rows(t)
    return x.reshape(t // tb, tb // 8, 8, c).swapaxes(1, 2).reshape(t, c)


def _cmul(ar, ai, br, bi):
    return ar * br - ai * bi, ar * bi + ai * br


def _segment_carries(fr, fi, ar8, ai8, c_r, c_i, seg, reverse):
    pr, pi = ar8, ai8
    for _ in range(int(math.log2(seg))):
        pr, pi = _cmul(pr, pi, pr, pi)
    row = lax.broadcasted_iota(jnp.int32, fr.shape, 0)
    edge = 7 if reverse else 0
    qr, qi = _cmul(pr, pi, c_r, c_i)
    xr, xi = jnp.where(row == edge, fr + qr, fr), jnp.where(row == edge, fi + qi, fi)
    for sh in (1, 2, 4):
        if reverse:
            keep, amount = row < 8 - sh, 8 - sh
        else:
            keep, amount = row >= sh, sh
        qr, qi = jnp.where(keep, pltpu.roll(xr, amount, 0), 0.0), jnp.where(keep, pltpu.roll(xi, amount, 0), 0.0)
        tr, ti = _cmul(pr, pi, qr, qi)
        xr, xi = xr + tr, xi + ti
        pr, pi = _cmul(pr, pi, pr, pi)
    if reverse:
        in_r, in_i = jnp.where(row == 7, c_r, pltpu.roll(xr, 7, 0)), jnp.where(row == 7, c_i, pltpu.roll(xi, 7, 0))
        return in_r, in_i, xr[0:1, :], xi[0:1, :]
    in_r, in_i = jnp.where(row == 0, c_r, pltpu.roll(xr, 1, 0)), jnp.where(row == 0, c_i, pltpu.roll(xi, 1, 0))
    return in_r, in_i, xr[7:8, :], xi[7:8, :]


def _s5_scan(ar, ai, dr, di, *, reverse, name, states=None):
    t, ch = dr.shape
    tb = _scan_rows(t)
    tc = _tile(ch, SCAN_LANES)
    nt = t // tb
    seg = tb // 8
    assert 1 << int(math.log2(seg)) == seg
    order = range(seg - 1, -1, -1) if reverse else range(seg)
    with_sum = states is not None
    assert reverse or not with_sum

    def body(*refs):
        ar_ref, ai_ref, dr_ref, di_ref = refs[:4]
        if with_sum:
            pr_ref, pi_ref, hr_ref, hi_ref, sr_ref, si_ref, gr_ref, gi_ref, cr, ci = refs[4:]
        else:
            sr_ref, si_ref, cr, ci = refs[4:]
        step = pl.program_id(1)

        @pl.when(step == 0)
        def _():
            cr[...] = jnp.zeros_like(cr)
            ci[...] = jnp.zeros_like(ci)
            if with_sum:
                gr_ref[...] = jnp.zeros_like(gr_ref)
                gi_ref[...] = jnp.zeros_like(gi_ref)

        rows = lambda j: slice(j * 8, (j + 1) * 8)
        ar8 = jnp.broadcast_to(ar_ref[...], (8, tc))
        ai8 = jnp.broadcast_to(-ai_ref[...] if reverse else ai_ref[...], (8, tc))
        fr, fi = jnp.zeros((8, tc), F32), jnp.zeros((8, tc), F32)
        for j in order:
            tr, ti = _cmul(ar8, ai8, fr, fi)
            fr, fi = tr + dr_ref[rows(j), :], ti + di_ref[rows(j), :]
        s_r, s_i, out_r, out_i = _segment_carries(fr, fi, ar8, ai8, cr[...], ci[...], seg, reverse)
        cr[...] = out_r
        ci[...] = out_i
        if with_sum:
            first_block = step == nt - 1
            row = lax.broadcasted_iota(jnp.int32, (8, tc), 0)
            acc_r, acc_i = jnp.zeros((8, tc), F32), jnp.zeros((8, tc), F32)
        for j in order:
            tr, ti = _cmul(ar8, ai8, s_r, s_i)
            s_r, s_i = tr + dr_ref[rows(j), :], ti + di_ref[rows(j), :]
            sr_ref[rows(j), :] = s_r
            si_ref[rows(j), :] = s_i
            if with_sum:
                if j > 0:
                    p_r, p_i = pr_ref[rows(j - 1), :], pi_ref[rows(j - 1), :]
                else:
                    halo_r = jnp.where(first_block, 0.0, hr_ref[7:8, :])
                    halo_i = jnp.where(first_block, 0.0, hi_ref[7:8, :])
                    p_r = jnp.where(row == 0, halo_r, pltpu.roll(pr_ref[rows(seg - 1), :], 1, 0))
                    p_i = jnp.where(row == 0, halo_i, pltpu.roll(pi_ref[rows(seg - 1), :], 1, 0))
                acc_r = acc_r + (p_r * s_r + p_i * s_i)
                acc_i = acc_i + (p_r * s_i - p_i * s_r)
        if with_sum:
            gr_ref[...] += jnp.sum(acc_r, axis=0, keepdims=True)
            gi_ref[...] += jnp.sum(acc_i, axis=0, keepdims=True)

    if reverse:
        blk = pl.BlockSpec((tb, tc), lambda c, s: (nt - 1 - s, c))
    else:
        blk = pl.BlockSpec((tb, tc), lambda c, s: (s, c))
    vec = pl.BlockSpec((1, tc), lambda c, s: (0, c))
    in_specs, args = [vec, vec, blk, blk], [ar, ai, dr, di]
    out_specs, out_shape = [blk, blk], [_sds((t, ch), F32)] * 2
    if with_sum:
        halo = pl.BlockSpec((8, tc), lambda c, s: (jnp.maximum((nt - 1 - s) * (tb // 8) - 1, 0), c))
        in_specs += [blk, blk, halo, halo]
        args += [states[0], states[1], states[0], states[1]]
        out_specs += [vec, vec]
        out_shape += [_sds((1, ch), F32)] * 2
    return pl.pallas_call(
        body, name=name, grid=(ch // tc, nt), in_specs=in_specs, out_specs=out_specs, out_shape=out_shape,
        scratch_shapes=[pltpu.VMEM((1, tc), F32), pltpu.VMEM((1, tc), F32)],
        compiler_params=_params("parallel", "arbitrary"),
    )(*args)


def _colsum_prod(a, b, w, *, name):
    t = a.shape[0]
    tb = _tile(t, 512, 8)

    def body(a_ref, b_ref, o_ref):
        @pl.when(pl.program_id(0) == 0)
        def _():
            o_ref[...] = jnp.zeros_like(o_ref)

        o_ref[...] += jnp.sum(a_ref[...].astype(F32) * b_ref[...].astype(F32), axis=0, keepdims=True)

    return pl.pallas_call(
        body, name=name, grid=(t // tb,), in_specs=[_row(tb, w), _row(tb, w)],
        out_specs=_const((1, w)), out_shape=_sds((1, w), F32), compiler_params=_params("arbitrary"),
    )(a, b)


def _s5_gate_bwd(d_ycat, gp, ypre):
    t, d = gp.shape
    tb = _tile(t, 256, 8)

    def body(do_ref, gp_ref, yp_ref, o_ref):
        sg = _sigmoid(gp_ref[...])
        o_ref[...] = (do_ref[...] * _gelu(yp_ref[...]) * sg * (1.0 - sg)).astype(o_ref.dtype)

    return pl.pallas_call(
        body, name="s5_gate_bwd", grid=(t // tb,), in_specs=[_row(tb, d), _row(tb, d), _row(tb, d)],
        out_specs=_row(tb, d), out_shape=_sds((t, d), MXU), compiler_params=_params("parallel"),
    )(d_ycat, gp, ypre)


_CONV_CW = 512
_CONV_OFF = (D_MAIN - D_CONV_CH) // _CONV_CW


def _conv_fwd(proj, w, b):
    t = proj.shape[0]
    tb = _tile(t, 256, 8)
    cw, off = _CONV_CW, _CONV_OFF

    def body(cur_ref, prev_ref, w_ref, b_ref, o_ref, ext):
        first = pl.program_id(1) == 0
        ext[0:8, :] = jnp.where(first, 0.0, prev_ref[...])
        ext[8:tb + 8, :] = cur_ref[...]
        acc = jnp.broadcast_to(b_ref[...], (tb, cw))
        for j in range(SSD_CONV):
            acc = acc + w_ref[SSD_CONV - 1 - j:SSD_CONV - j, :] * ext[8 - j:8 - j + tb, :]
        o_ref[...] = acc

    return pl.pallas_call(
        body, name="ssd_conv_fwd", grid=(D_CONV_CH // cw, t // tb),
        in_specs=[
            pl.BlockSpec((tb, cw), lambda j, i: (i, j + off)),
            pl.BlockSpec((8, cw), lambda j, i: (jnp.maximum(i * (tb // 8) - 1, 0), j + off)),
            pl.BlockSpec((SSD_CONV, cw), lambda j, i: (0, j)),
            pl.BlockSpec((1, cw), lambda j, i: (0, j)),
        ],
        out_specs=pl.BlockSpec((tb, cw), lambda j, i: (i, j)),
        out_shape=_sds((t, D_CONV_CH), F32),
        scratch_shapes=[pltpu.VMEM((tb + 8, cw), F32)],
        compiler_params=_params("parallel", "arbitrary"),
    )(proj, proj, w, b.reshape(1, D_CONV_CH))


def _conv_bwd(proj, w, d_conv):
    t = proj.shape[0]
    tb = _tile(t, 256, 8)
    cw, off = _CONV_CW, _CONV_OFF
    nt = t // tb

    def body(cur_ref, prev_ref, w_ref, dc_ref, dnext_ref, dx_ref, dw_ref, db_ref, ext, dext):
        i = pl.program_id(1)

        @pl.when(i == 0)
        def _():
            dw_ref[...] = jnp.zeros_like(dw_ref)
            db_ref[...] = jnp.zeros_like(db_ref)

        ext[0:8, :] = jnp.where(i == 0, 0.0, prev_ref[...])
        ext[8:tb + 8, :] = cur_ref[...]
        dcv = dc_ref[...]
        dext[0:tb, :] = dcv
        dext[tb:tb + 8, :] = jnp.where(i == nt - 1, 0.0, dnext_ref[...])
        dx = jnp.zeros((tb, cw), F32)
        for j in range(SSD_CONV):
            dx = dx + w_ref[SSD_CONV - 1 - j:SSD_CONV - j, :] * dext[j:j + tb, :]
            dw_ref[SSD_CONV - 1 - j:SSD_CONV - j, :] += jnp.sum(dcv * ext[8 - j:8 - j + tb, :], axis=0, keepdims=True)
        dx_ref[...] = dx
        db_ref[...] += jnp.sum(dcv, axis=0, keepdims=True)

    return pl.pallas_call(
        body, name="ssd_conv_bwd", grid=(D_CONV_CH // cw, nt),
        in_specs=[
            pl.BlockSpec((tb, cw), lambda j, i: (i, j + off)),
            pl.BlockSpec((8, cw), lambda j, i: (jnp.maximum(i * (tb // 8) - 1, 0), j + off)),
            pl.BlockSpec((SSD_CONV, cw), lambda j, i: (0, j)),
            pl.BlockSpec((tb, cw), lambda j, i: (i, j)),
            pl.BlockSpec((8, cw), lambda j, i: (jnp.minimum((i + 1) * (tb // 8), t // 8 - 1), j)),
        ],
        out_specs=[
            pl.BlockSpec((tb, cw), lambda j, i: (i, j)),
            pl.BlockSpec((SSD_CONV, cw), lambda j, i: (0, j)),
            pl.BlockSpec((1, cw), lambda j, i: (0, j)),
        ],
        out_shape=[_sds((t, D_CONV_CH), F32), _sds((SSD_CONV, D_CONV_CH), F32), _sds((1, D_CONV_CH), F32)],
        scratch_shapes=[pltpu.VMEM((tb + 8, cw), F32), pltpu.VMEM((tb + 8, cw), F32)],
        compiler_params=_params("parallel", "arbitrary"),
    )(proj, proj, w, d_conv, d_conv)


def _ssd_consts():
    head = jnp.arange(HP)[:, None]
    lane = jnp.arange(D_SSD)[None, :]
    expand = ((lane // SSD_HEADDIM) == head).astype(F32)
    ll = jnp.arange(SSD_CHUNK)
    tri = (ll[:, None] >= ll[None, :]).astype(F32)
    return expand, expand.T, tri, jnp.eye(HP, dtype=F32)


def _ssd_chunk_terms(cp, dtr, par, expand, tri):
    ln = SSD_CHUNK
    xbc = _silu(cp)
    xs, bm, cm = xbc[:, :D_SSD], xbc[:, D_SSD:D_SSD + SSD_BC], xbc[:, D_SSD + SSD_BC:]
    dt = _softplus(dtr + par[0:1, :])
    a = -jnp.exp(par[1:2, :])
    da = dt * a
    acum = _dot(tri, da, precision=HI)
    acum_t = _dot(da, tri, (((0,), (1,)), ((), ())), precision=HI)
    atot = acum[ln - 1:ln, :]
    dt_e = _dot(dt, expand, precision=HI)
    eac_e = _dot(jnp.exp(acum), expand, precision=HI)
    dec_e = _dot(jnp.exp(atot - acum), expand, precision=HI)
    eat_e = _dot(_rows8(jnp.exp(atot)), expand, precision=HI)[0:1, :]
    dsk_e = _dot(_rows8(par[2:3, :]), expand, precision=HI)[0:1, :]
    return dict(xs=xs, bm=bm, cm=cm, dt=dt, a=a, acum=acum, acum_t=acum_t, dt_e=dt_e, eac_e=eac_e,
                dec_e=dec_e, eat_e=eat_e, dsk_e=dsk_e)


def _decay_matrix(acum, acum_t, h, mask):
    diff = acum[:, h:h + 1] - acum_t[h:h + 1, :]
    return jnp.where(mask, jnp.exp(jnp.minimum(diff, 0.0)), 0.0)


def _ssd_fwd(proj, conv_pre, dtr, par, gnorm, consts):
    t = proj.shape[0]
    ln = SSD_CHUNK
    nc = t // ln
    expand, _, tri, _ = consts
    hd2 = 2 * SSD_HEADDIM

    def body(cp_ref, z_ref, dtr_ref, par_ref, g_ref, e_ref, tri_ref, out_ref, y_ref, st_ref, state):
        @pl.when(pl.program_id(0) == 0)
        def _():
            state[...] = jnp.zeros_like(state)

        st_ref[...] = state[...]
        c = _ssd_chunk_terms(cp_ref[...], dtr_ref[...], par_ref[...], e_ref[...], tri_ref[...])
        xdt = c["xs"] * c["dt_e"]
        xb, xd = xdt.astype(MXU), (xdt * c["dec_e"]).astype(MXU)
        bb, cb = c["bm"].astype(MXU), c["cm"].astype(MXU)
        mask = lax.broadcasted_iota(jnp.int32, (ln, ln), 0) >= lax.broadcasted_iota(jnp.int32, (ln, ln), 1)
        left = lax.broadcasted_iota(jnp.int32, (ln, hd2), 1) < SSD_HEADDIM
        for g in range(SSD_GROUPS):
            nsl = slice(g * SSD_STATE, (g + 1) * SSD_STATE)
            gsl = slice(g * 256, (g + 1) * 256)
            bg, cg = bb[:, nsl], cb[:, nsl]
            cbm = _dot(cg, bg, _NT)
            st_g = state[:, gsl]
            for pair in range(2):
                h0 = g * 4 + pair * 2
                psl = slice(h0 * SSD_HEADDIM, (h0 + 2) * SSD_HEADDIM)
                m0 = (cbm * _decay_matrix(c["acum"], c["acum_t"], h0, mask)).astype(MXU)
                m1 = (cbm * _decay_matrix(c["acum"], c["acum_t"], h0 + 1, mask)).astype(MXU)
                y_ref[:, psl] = jnp.where(left, _dot(m0, xb[:, psl]), _dot(m1, xb[:, psl]))
            y_ref[:, gsl] += _dot(cg, st_g.astype(MXU)) * c["eac_e"][:, gsl]
            state[:, gsl] = st_g * c["eat_e"][:, gsl] + _dot(bg, xd[:, gsl], _TN)
        y = y_ref[...] + c["dsk_e"] * c["xs"]
        y_ref[...] = y
        y2 = y * _silu(z_ref[...])
        r = lax.rsqrt(jnp.mean(y2 * y2, axis=-1, keepdims=True) + EPS)
        out_ref[...] = (y2 * r * g_ref[...]).astype(out_ref.dtype)

    return pl.pallas_call(
        body, name="ssd_fwd", grid=(nc,),
        in_specs=[_row(ln, D_CONV_CH), _row(ln, D_SSD, 1), _row(ln, HP), _const((8, HP)), _const((1, D_SSD)),
                  _const((HP, D_SSD)), _const((ln, ln))],
        out_specs=[_row(ln, D_SSD), _row(ln, D_SSD), pl.BlockSpec((None, SSD_STATE, D_SSD), lambda i: (i, 0, 0))],
        out_shape=[_sds((t, D_SSD), MXU), _sds((t, D_SSD), F32), _sds((nc, SSD_STATE, D_SSD), F32)],
        scratch_shapes=[pltpu.VMEM((SSD_STATE, D_SSD), F32)],
        compiler_params=_params("arbitrary"),
    )(conv_pre, proj, dtr, par, gnorm.reshape(1, D_SSD), expand, tri)


def _ssd_bwd(proj, conv_pre, dtr, par, gnorm, y, states, d_ycat, consts):
    t = proj.shape[0]
    ln = SSD_CHUNK
    nc = t // ln
    expand, expand_t, tri, eye = consts
    hd2 = 2 * SSD_HEADDIM

    def body(cp_ref, z_ref, dtr_ref, par_ref, g_ref, y_ref, st_ref, do_ref, e_ref, et_ref, tri_ref, eye_ref,
             dcp_ref, dz_ref, ddt_ref, dg_ref, dpar_ref, dstate, dx_buf, lane_buf, tot_buf, colsum):
        @pl.when(pl.program_id(0) == 0)
        def _():
            dstate[...] = jnp.zeros_like(dstate)
            dg_ref[...] = jnp.zeros_like(dg_ref)
            dpar_ref[...] = jnp.zeros_like(dpar_ref)

        cpv, et_v, tri_v, par_v = cp_ref[...], et_ref[...], tri_ref[...], par_ref[...]
        c = _ssd_chunk_terms(cpv, dtr_ref[...], par_v, e_ref[...], tri_v)
        xs = c["xs"]
        zv, yv, dov = z_ref[...], y_ref[...], do_ref[...]
        sz = _silu(zv)
        y2 = yv * sz
        r = lax.rsqrt(jnp.mean(y2 * y2, axis=-1, keepdims=True) + EPS)
        yh = y2 * r
        dg_ref[...] += jnp.sum(dov * yh, axis=0, keepdims=True)
        dyh = dov * g_ref[...]
        dy2 = r * (dyh - yh * jnp.mean(dyh * yh, axis=-1, keepdims=True))
        dz_ref[...] = dy2 * yv * _silu_grad(zv)
        dy = dy2 * sz

        xdt = xs * c["dt_e"]
        xdf = xdt * c["dec_e"]
        xb, xd = xdt.astype(MXU), xdf.astype(MXU)
        bb, cb = c["bm"].astype(MXU), c["cm"].astype(MXU)
        dyb, dye = dy.astype(MXU), (dy * c["eac_e"]).astype(MXU)
        mask = lax.broadcasted_iota(jnp.int32, (ln, ln), 0) >= lax.broadcasted_iota(jnp.int32, (ln, ln), 1)
        left = lax.broadcasted_iota(jnp.int32, (ln, hd2), 1) < SSD_HEADDIM
        lane_hp = lax.broadcasted_iota(jnp.int32, (ln, HP), 1)
        d_acum = jnp.zeros((ln, HP), F32)
        colsum[...] = jnp.zeros_like(colsum)
        tot_buf[...] = jnp.zeros_like(tot_buf)
        for g in range(SSD_GROUPS):
            nsl = slice(g * SSD_STATE, (g + 1) * SSD_STATE)
            gsl = slice(g * 256, (g + 1) * 256)
            bg, cg = bb[:, nsl], cb[:, nsl]
            cbm = _dot(cg, bg, _NT)
            st_g = st_ref[:, gsl]
            dst_g = dstate[:, gsl]
            stb, dstb = st_g.astype(MXU), dst_g.astype(MXU)
            y_off = _dot(cg, stb)
            bds = _dot(bg, dstb)
            dcb = jnp.zeros((ln, ln), F32)
            for pair in range(2):
                h0 = g * 4 + pair * 2
                psl = slice(h0 * SSD_HEADDIM, (h0 + 2) * SSD_HEADDIM)
                xp, dyp = xb[:, psl], dyb[:, psl]
                dxp = []
                for k in range(2):
                    h = h0 + k
                    lm = _decay_matrix(c["acum"], c["acum_t"], h, mask)
                    mm = cbm * lm
                    half = left if k == 0 else jnp.logical_not(left)
                    dm = _dot(jnp.where(half, dyp, jnp.zeros_like(dyp)), xp, _NT)
                    dcb = dcb + dm * lm
                    gm = dm * mm
                    d_acum = d_acum + jnp.where(lane_hp == h, jnp.sum(gm, axis=1, keepdims=True), 0.0)
                    colsum[h:h + 1, :] = jnp.sum(gm, axis=0, keepdims=True)
                    dxp.append(_dot(mm.astype(MXU), dyp, _TN))
                dx_buf[:, psl] = jnp.where(left, dxp[0], dxp[1])
            dx_buf[:, gsl] += bds * c["dec_e"][:, gsl]
            dcbb = dcb.astype(MXU)
            dc_g = _dot(dcbb, bg) + _dot(dye[:, gsl], stb, _NT)
            db_g = _dot(dcbb, cg, _TN) + _dot(xd[:, gsl], dstb, _NT)
            dcp_ref[:, D_SSD + g * SSD_STATE:D_SSD + (g + 1) * SSD_STATE] = db_g
            dcp_ref[:, D_SSD + SSD_BC + g * SSD_STATE:D_SSD + SSD_BC + (g + 1) * SSD_STATE] = dc_g
            dec_term = xdf[:, gsl] * bds
            lane_buf[:, gsl] = dy[:, gsl] * y_off * c["eac_e"][:, gsl] - dec_term
            tot_buf[0:1, gsl] = (jnp.sum(st_g * dst_g, axis=0, keepdims=True) * c["eat_e"][:, gsl]
                                 + jnp.sum(dec_term, axis=0, keepdims=True))
            dstate[:, gsl] = dst_g * c["eat_e"][:, gsl] + _dot(cg, dye[:, gsl], _TN)
        dx_tot = dx_buf[...]
        d_acum = d_acum + _dot(lane_buf[...], et_v, precision=HI) - _dot(colsum[...], eye_ref[...], _TN, precision=HI)
        d_atot = _dot(tot_buf[...], et_v, precision=HI)[0:1, :]
        row_hp = lax.broadcasted_iota(jnp.int32, (ln, HP), 0)
        d_acum = d_acum + jnp.where(row_hp == ln - 1, d_atot, 0.0)
        d_da = _dot(tri_v, d_acum, _TN, precision=HI)
        d_dt = d_da * c["a"] + _dot(dx_tot * xs, et_v, precision=HI)
        d_dtr = d_dt * _sigmoid(dtr_ref[...] + par_v[0:1, :])
        ddt_ref[...] = d_dtr
        dpar_ref[0:1, :] += jnp.sum(d_dtr, axis=0, keepdims=True)
        dpar_ref[1:2, :] += jnp.sum(d_da * c["dt"], axis=0, keepdims=True) * c["a"]
        dpar_ref[2:3, :] += _dot(_rows8(jnp.sum(dy * xs, axis=0, keepdims=True)), et_v, precision=HI)[0:1, :]
        dcp_ref[:, 0:D_SSD] = dx_tot * c["dt_e"] + dy * c["dsk_e"]
        dcp_ref[...] = dcp_ref[...] * _silu_grad(cpv)

    rev = lambda i: (nc - 1 - i, 0)
    rev1 = lambda i: (nc - 1 - i, 1)
    return pl.pallas_call(
        body, name="ssd_bwd", grid=(nc,),
        in_specs=[pl.BlockSpec((ln, D_CONV_CH), rev), pl.BlockSpec((ln, D_SSD), rev1), pl.BlockSpec((ln, HP), rev),
                  _const((8, HP)), _const((1, D_SSD)), pl.BlockSpec((ln, D_SSD), rev),
                  pl.BlockSpec((None, SSD_STATE, D_SSD), lambda i: (nc - 1 - i, 0, 0)),
                  pl.BlockSpec((ln, D_SSD), rev1),
                  _const((HP, D_SSD)), _const((D_SSD, HP)), _const((ln, ln)), _const((HP, HP))],
        out_specs=[pl.BlockSpec((ln, D_CONV_CH), rev), pl.BlockSpec((ln, D_SSD), rev), pl.BlockSpec((ln, HP), rev),
                   _const((1, D_SSD)), _const((8, HP))],
        out_shape=[_sds((t, D_CONV_CH), F32), _sds((t, D_SSD), F32), _sds((t, HP), F32), _sds((1, D_SSD), F32),
                   _sds((8, HP), F32)],
        scratch_shapes=[pltpu.VMEM((SSD_STATE, D_SSD), F32), pltpu.VMEM((ln, D_SSD), F32), pltpu.VMEM((ln, D_SSD), F32),
                        pltpu.VMEM((8, D_SSD), F32), pltpu.VMEM((HP, ln), F32)],
        compiler_params=_params("arbitrary"),
    )(conv_pre, proj, dtr, par, gnorm.reshape(1, D_SSD), y, states, d_ycat, expand, expand_t, tri, eye)


def _softmax_rows(s):
    e = jnp.exp(s - jnp.max(s, axis=-1, keepdims=True))
    return e / jnp.sum(e, axis=-1, keepdims=True)


def _attn_fwd(q, k, v):
    t, d = q.shape
    mlen = k.shape[0]
    tq = _tile(t, 512, 8)
    scale = XA_HEAD_DIM ** -0.5

    def body(q_ref, k_ref, v_ref, o_ref):
        for h in range(XA_HEADS):
            sl = slice(h * XA_HEAD_DIM, (h + 1) * XA_HEAD_DIM)
            p = _softmax_rows(_dot(q_ref[:, sl], k_ref[:, sl], _NT) * scale)
            o_ref[:, sl] = _dot(p.astype(MXU), v_ref[:, sl]).astype(o_ref.dtype)

    return pl.pallas_call(
        body, name="xattn_fwd", grid=(t // tq,),
        in_specs=[_row(tq, d), _const((mlen, d)), _const((mlen, d))], out_specs=_row(tq, d),
        out_shape=_sds((t, d), MXU), compiler_params=_params("parallel"),
    )(q, k, v)


def _attn_bwd(q, k, v, do):
    t, d = q.shape
    mlen = k.shape[0]
    tq = _tile(t, 512, 8)
    scale = XA_HEAD_DIM ** -0.5

    def body(q_ref, k_ref, v_ref, do_ref, dq_ref, dk_ref, dv_ref):
        @pl.when(pl.program_id(0) == 0)
        def _():
            dk_ref[...] = jnp.zeros_like(dk_ref)
            dv_ref[...] = jnp.zeros_like(dv_ref)

        for h in range(XA_HEADS):
            sl = slice(h * XA_HEAD_DIM, (h + 1) * XA_HEAD_DIM)
            qh, kh, vh, doh = q_ref[:, sl], k_ref[:, sl], v_ref[:, sl], do_ref[:, sl]
            p = _softmax_rows(_dot(qh, kh, _NT) * scale)
            dp = _dot(doh, vh, _NT)
            dv_ref[:, sl] += _dot(p.astype(MXU), doh, _TN)
            ds = (p * (dp - jnp.sum(p * dp, axis=-1, keepdims=True)) * scale).astype(MXU)
            dq_ref[:, sl] = _dot(ds, kh).astype(dq_ref.dtype)
            dk_ref[:, sl] += _dot(ds, qh, _TN)

    return pl.pallas_call(
        body, name="xattn_bwd", grid=(t // tq,),
        in_specs=[_row(tq, d), _const((mlen, d)), _const((mlen, d)), _row(tq, d)],
        out_specs=[_row(tq, d), _const((mlen, d)), _const((mlen, d))],
        out_shape=[_sds((t, d), MXU), _sds((mlen, d), F32), _sds((mlen, d), F32)],
        compiler_params=_params("arbitrary"),
    )(q, k, v, do)


def _lane_view(a):
    if a.ndim >= 2 and a.shape[-1] >= 128:
        return a.reshape(-1, a.shape[-1])
    if a.size % 128 == 0:
        return a.reshape(-1, 128)
    return a.reshape(1, -1)


def _adamw(w, g, m, v, *, name):
    shape = w.shape
    w2, g2, m2, v2 = (_lane_view(a) for a in (w, g.reshape(shape), m, v))
    r, c = w2.shape
    tr = _tile(r, max(8, (1 << 18) // max(c, 128)), 8)

    def body(w_ref, g_ref, m_ref, v_ref, d_ref, mo_ref, vo_ref):
        gv = g_ref[...]
        mn = ADAM_B1 * m_ref[...] + (1.0 - ADAM_B1) * gv
        vn = ADAM_B2 * v_ref[...] + (1.0 - ADAM_B2) * (gv * gv)
        m_hat = mn / (1.0 - ADAM_B1 ** ADAM_STEP)
        v_hat = vn / (1.0 - ADAM_B2 ** ADAM_STEP)
        d_ref[...] = -ADAM_LR * (m_hat / (jnp.sqrt(v_hat) + ADAM_EPS) + ADAM_WD * w_ref[...])
        mo_ref[...] = mn
        vo_ref[...] = vn

    outs = pl.pallas_call(
        body, name=name, grid=(r // tr,), in_specs=[_row(tr, c)] * 4, out_specs=[_row(tr, c)] * 3,
        out_shape=[_sds((r, c), F32)] * 3, compiler_params=_params("parallel"),
    )(w2, g2, m2, v2)
    return tuple(o.reshape(shape) for o in outs)


_HBM = pl.BlockSpec(memory_space=pltpu.HBM)


def _peer(k, xx, yy, cc):
    px = 1 - xx if k & 4 else xx
    py = 1 - yy if k & 2 else yy
    pc = 1 - cc if k & 1 else cc
    return (px, py, pc), 4 * px + 2 * py + pc


def _exchange(x, *, scatter, name):
    shape = x.shape[1:] if scatter else x.shape

    def body(x_ref, o_ref, send_sems, recv_sems, local_sem):
        xx, yy, cc = (lax.axis_index(a) for a in AXES)
        me = 4 * xx + 2 * yy + cc
        local = pltpu.make_async_copy(x_ref.at[me] if scatter else x_ref, o_ref.at[me], local_sem)
        local.start()
        sends = []
        for k in range(1, N_DEV):
            peer, pid = _peer(k, xx, yy, cc)
            cp = pltpu.make_async_remote_copy(
                src_ref=x_ref.at[pid] if scatter else x_ref, dst_ref=o_ref.at[me], send_sem=send_sems.at[k - 1],
                recv_sem=recv_sems.at[k - 1], device_id=peer, device_id_type=pl.DeviceIdType.MESH)
            cp.start()
            sends.append(cp)
        for k in range(1, N_DEV):
            peer, pid = _peer(k, xx, yy, cc)
            pltpu.make_async_remote_copy(
                src_ref=x_ref.at[pid] if scatter else x_ref, dst_ref=o_ref.at[pid], send_sem=send_sems.at[k - 1],
                recv_sem=recv_sems.at[k - 1], device_id=peer, device_id_type=pl.DeviceIdType.MESH).wait_recv()
        for cp in sends:
            cp.wait_send()
        local.wait()

    return pl.pallas_call(
        body, name=name, in_specs=[_HBM], out_specs=_HBM, out_shape=_sds((N_DEV,) + tuple(shape), x.dtype),
        scratch_shapes=[pltpu.SemaphoreType.DMA((N_DEV - 1,)), pltpu.SemaphoreType.DMA((N_DEV - 1,)),
                        pltpu.SemaphoreType.DMA(())],
    )(x)


def _sum_slots(x, *, name):
    _, r, c = x.shape
    tr = _tile(r, max(PAD_ROWS, (1 << 17) // c), PAD_ROWS)

    def body(x_ref, o_ref):
        acc = x_ref[0].astype(F32)
        for d in range(1, N_DEV):
            acc = acc + x_ref[d].astype(F32)
        o_ref[...] = acc

    return pl.pallas_call(
        body, name=name, grid=(r // tr,), in_specs=[pl.BlockSpec((N_DEV, tr, c), lambda i: (0, i, 0))],
        out_specs=_row(tr, c), out_shape=_sds((r, c), F32), compiler_params=_params("parallel"),
    )(x)


def _s5_layouts(bbar_r, bbar_i, c_re, c_im):
    eye = jnp.eye(S5_GPB, dtype=F32)

    def b_blocks(bbar):
        bb = bbar.reshape(S5_NB, S5_GPB, S5_STATE, S5_GROUP)
        return jnp.einsum("jgph,gk->jghkp", bb, eye).reshape(S5_NB, S5_GPB * S5_GROUP, S5_GPB * S5_STATE)

    def c_blocks(cc):
        c4 = cc.reshape(S5_NB, S5_GPB, S5_GROUP, S5_STATE)
        return jnp.einsum("jghp,gk->jgpkh", c4, eye).reshape(S5_NB, S5_GPB * S5_STATE, S5_GPB * S5_GROUP)

    bre, bim, cre, cim = b_blocks(bbar_r), b_blocks(bbar_i), c_blocks(c_re), c_blocks(c_im)
    cast = lambda a: a.astype(MXU)
    sw = lambda a: jnp.swapaxes(a, 1, 2).astype(MXU)
    return dict(bre=cast(bre), bim=cast(bim), cre=cast(cre), cim=cast(cim), bre_t=sw(bre), bim_t=sw(bim), cre_t=sw(cre),
                cim_t=sw(cim))


def _b_diag(db):
    d5 = db.reshape(S5_NB, S5_GPB, S5_GROUP, S5_GPB, S5_STATE)
    diag = jnp.stack([d5[:, g, :, g, :] for g in range(S5_GPB)], axis=1)
    return jnp.swapaxes(diag, 2, 3).reshape(S5_GROUPS, S5_STATE * S5_GROUP)


def _c_diag(dc):
    d5 = dc.reshape(S5_NB, S5_GPB, S5_STATE, S5_GPB, S5_GROUP)
    diag = jnp.stack([d5[:, g, :, g, :] for g in range(S5_GPB)], axis=1)
    return jnp.swapaxes(diag, 2, 3).reshape(S5_GROUPS, S5_GROUP, S5_STATE)


def _head_rows(*vecs):
    par = jnp.zeros((8, HP), F32)
    for i, v in enumerate(vecs):
        par = par.at[i, :SSD_HEADS].set(v.astype(F32))
    return par


def _add(acc, r):
    return (acc + r,)


def _layer_fwd(x, mem, w, consts):
    s = {"x": x}
    rep = consts["rep"]
    s["h1"] = h1 = _rmsnorm_fwd(x, w["norm_mix"], name="norm_mix_fwd")
    s["proj"] = proj = _mm(h1, w["w_main"], name="in_proj")
    s["dtr"] = dtr = _mm(h1, w["w_dt"], name="dt_proj")
    ar, ai, ldt = w["s5_a_re"], w["s5_a_im"], w["s5_log_dt"].reshape(S5_GROUPS, 1)
    br, bi = w["s5_b_re"].reshape(S5_GROUPS, -1), w["s5_b_im"].reshape(S5_GROUPS, -1)
    abar_r, abar_i, bbar_r, bbar_i = _s5_prep(ar, ai, ldt, br, bi, rep)
    s["abar"] = abar = (abar_r.reshape(1, S5_CH), abar_i.reshape(1, S5_CH))
    s["lay"] = lay = _s5_layouts(bbar_r, bbar_i, w["s5_c_re"], w["s5_c_im"])
    s["u"] = u = _interleave_rows(proj[:, :D_S5])
    drv_r, drv_i = _bdmm([(u, lay["bre"]), (u, lay["bim"])], nb=S5_NB, ko=256, no=1024,
                         epi=lambda p: (p[0], p[1]), out_dtypes=(F32, F32), name="s5_drive")
    s["sr"], s["si"] = sr, si = _s5_scan(*abar, drv_r, drv_i, reverse=False, name="s5_scan_fwd")
    d_vec = w["s5_d"].reshape(1, D_S5)
    s["ypre"], s["yg"] = ypre, yg = _bdmm(
        [(sr, lay["cre"]), (si, lay["cim"])], nb=S5_NB, ko=1024, no=256, extras=[u], vecs=[d_vec],
        epi=lambda p, uv, dv: ((yp := p[0] - p[1] + dv * uv), _gelu(yp)), out_dtypes=(F32, MXU), name="s5_readout")
    s["gp"], out_s5 = _mm(yg, w["s5_w_glu"], extras=[ypre], epi=lambda acc, yp: (acc, _gelu(yp) * _sigmoid(acc)),
                          out_dtypes=(F32, MXU), name="s5_glu")
    out_s5 = _deinterleave_rows(out_s5)
    s["conv_pre"] = conv_pre = _conv_fwd(proj, w["ssd_conv_w"], w["ssd_conv_b"])
    s["par"] = par = _head_rows(w["ssd_dt_bias"], w["ssd_a_log"], w["ssd_d"])
    out_ssd, s["y_ssd"], s["states"] = _ssd_fwd(proj, conv_pre, dtr, par, w["ssd_norm"], consts["ssd"])
    s["ycat"] = ycat = jnp.concatenate([out_s5, out_ssd], axis=1)
    s["x1"] = x1 = _mm(ycat, w["w_out"], extras=[x], epi=_add, name="out_proj")
    s["hq"] = hq = _rmsnorm_fwd(x1, w["norm_xattn"], name="norm_xattn_fwd")
    s["mn"] = mn = _rmsnorm_fwd(mem, w["norm_mem"], name="norm_mem_fwd")
    s["q"] = q = _mm(hq, w["xa_wq"], out_dtypes=(MXU,), name="xa_q")
    s["k"] = k = _mm(mn, w["xa_wk"], out_dtypes=(MXU,), name="xa_k")
    s["v"] = v = _mm(mn, w["xa_wv"], out_dtypes=(MXU,), name="xa_v")
    s["o"] = o = _attn_fwd(q, k, v)
    s["x2"] = x2 = _mm(o, w["xa_wo"], extras=[x1], epi=_add, name="xa_o")
    s["hm"] = hm = _rmsnorm_fwd(x2, w["norm_mlp"], name="norm_mlp_fwd")
    s["act"] = _mm(hm, w["mlp_w1"], epi=lambda acc: (jnp.square(jnp.maximum(acc, 0.0)),), out_dtypes=(MXU,),
                   name="mlp_up")
    x3 = _mm(s["act"], w["mlp_w2"], extras=[x2], epi=_add, name="mlp_down")
    return x3, s


def _layer_bwd(dx3, mem, w, s, consts):
    g = {}
    rep = consts["rep"]
    d_a = _mm(dx3, w["mlp_w2"], tb=True, extras=[s["act"]],
              epi=lambda acc, act: (acc * (2.0 * jnp.sqrt(act.astype(F32))),), out_dtypes=(MXU,), name="mlp_down_dx")
    g["mlp_w2"] = _mm(s["act"], dx3, ta=True, name="mlp_down_dw")
    g["mlp_w1"] = _mm(s["hm"], d_a, ta=True, name="mlp_up_dw")
    d_hm = _mm(d_a, w["mlp_w1"], tb=True, name="mlp_up_dx")
    dx2, g["norm_mlp"] = _rmsnorm_bwd(s["x2"], w["norm_mlp"], d_hm, dx3, name="norm_mlp_bwd")
    d_o = _mm(dx2, w["xa_wo"], tb=True, out_dtypes=(MXU,), name="xa_o_dx")
    g["xa_wo"] = _mm(s["o"], dx2, ta=True, name="xa_o_dw")
    dq, dk, dv = _attn_bwd(s["q"], s["k"], s["v"], d_o)
    g["xa_wq"] = _mm(s["hq"], dq, ta=True, name="xa_q_dw")
    d_hq = _mm(dq, w["xa_wq"], tb=True, name="xa_q_dx")
    dx1, g["norm_xattn"] = _rmsnorm_bwd(s["x1"], w["norm_xattn"], d_hq, dx2, name="norm_xattn_bwd")
    g["xa_wk"] = _mm(s["mn"], dk, ta=True, name="xa_k_dw")
    g["xa_wv"] = _mm(s["mn"], dv, ta=True, name="xa_v_dw")
    d_mn_v = _mm(dv, w["xa_wv"], tb=True, name="xa_v_dx")
    d_mn = _mm(dk, w["xa_wk"], tb=True, extras=[d_mn_v], epi=_add, name="xa_k_dx")
    _, g["norm_mem"] = _rmsnorm_bwd(mem, w["norm_mem"], d_mn, None, name="norm_mem_bwd")
    d_ycat = _mm(dx1, w["w_out"], tb=True, name="out_proj_dx")
    g["w_out"] = _mm(s["ycat"], dx1, ta=True, name="out_proj_dw")
    lay, proj, ypre, u = s["lay"], s["proj"], s["ypre"], s["u"]
    d_os5 = _interleave_rows(d_ycat[:, :D_S5])
    d_gp = _s5_gate_bwd(d_os5, s["gp"], ypre)
    g["s5_w_glu"] = _mm(s["yg"], d_gp, ta=True, name="s5_glu_dw")
    d_ypre = _mm(d_gp, w["s5_w_glu"], tb=True, extras=[d_os5, s["gp"], ypre],
                 epi=lambda acc, do, gp, yp: ((acc + do * _sigmoid(gp)) * _gelu_grad(yp),), name="s5_glu_dx")
    ds_r, ds_i = _bdmm([(d_ypre, lay["cre_t"]), (d_ypre, lay["cim_t"])], nb=S5_NB, ko=256, no=1024,
                       epi=lambda p: (p[0], -p[1]), out_dtypes=(F32, F32), name="s5_readout_ds")
    lam_r, lam_i, d_abar_r, d_abar_i = _s5_scan(*s["abar"], ds_r, ds_i, reverse=True, name="s5_scan_bwd",
                                                states=(s["sr"], s["si"]))
    d_bbar_r = _b_diag(_bdmm_tn(u, 256, lam_r, 1024, nb=S5_NB, name="s5_drive_dw_re"))
    d_bbar_i = _b_diag(_bdmm_tn(u, 256, lam_i, 1024, nb=S5_NB, name="s5_drive_dw_im"))
    g["s5_c_re"] = _c_diag(_bdmm_tn(s["sr"], 1024, d_ypre, 256, nb=S5_NB, name="s5_readout_dw_re"))
    g["s5_c_im"] = -_c_diag(_bdmm_tn(s["si"], 1024, d_ypre, 256, nb=S5_NB, name="s5_readout_dw_im"))
    g["s5_d"] = _colsum_prod(d_ypre, u, D_S5, name="s5_d_dw").reshape(S5_GROUPS, S5_GROUP)
    du = _bdmm([(lam_r, lay["bre_t"]), (lam_i, lay["bim_t"])], nb=S5_NB, ko=1024, no=256, extras=[d_ypre],
               vecs=[w["s5_d"].reshape(1, D_S5)], epi=lambda p, dyp, dv: (p[0] + p[1] + dv * dyp,), name="s5_drive_du")
    du = _deinterleave_rows(du)
    ar, ai, ldt = w["s5_a_re"], w["s5_a_im"], w["s5_log_dt"].reshape(S5_GROUPS, 1)
    br, bi = w["s5_b_re"].reshape(S5_GROUPS, -1), w["s5_b_im"].reshape(S5_GROUPS, -1)
    d_ar, d_ai, d_ldt, d_br, d_bi = _s5_prep_bwd(
        ar, ai, ldt, br, bi, rep, d_abar_r.reshape(S5_GROUPS, S5_STATE), d_abar_i.reshape(S5_GROUPS, S5_STATE),
        d_bbar_r, d_bbar_i)
    g["s5_a_re"], g["s5_a_im"], g["s5_log_dt"] = d_ar, d_ai, d_ldt.reshape(S5_GROUPS)
    g["s5_b_re"] = d_br.reshape(S5_GROUPS, S5_STATE, S5_GROUP)
    g["s5_b_im"] = d_bi.reshape(S5_GROUPS, S5_STATE, S5_GROUP)
    d_cp, dz, d_dtr, g_ssd_norm, d_par = _ssd_bwd(proj, s["conv_pre"], s["dtr"], s["par"], w["ssd_norm"], s["y_ssd"],
                                                  s["states"], d_ycat, consts["ssd"])
    g["ssd_norm"] = g_ssd_norm
    g["ssd_dt_bias"], g["ssd_a_log"], g["ssd_d"] = (d_par[i, :SSD_HEADS] for i in range(3))
    d_xbc, g["ssd_conv_w"], g["ssd_conv_b"] = _conv_bwd(proj, w["ssd_conv_w"], d_cp)
    d_proj = jnp.concatenate([du, dz, d_xbc], axis=1)
    g_main = _mm(s["h1"], d_proj, ta=True, name="in_proj_dw")
    g_dt = _mm(s["h1"], d_dtr, ta=True, name="dt_proj_dw")
    g["w_in"] = jnp.concatenate([g_main, g_dt[:, :SSD_HEADS]], axis=1)
    d_h1_dt = _mm(d_dtr, w["w_dt"], tb=True, name="dt_proj_dx")
    d_h1 = _mm(d_proj, w["w_main"], tb=True, extras=[d_h1_dt], epi=_add, name="in_proj_dx")
    dx, g["norm_mix"] = _rmsnorm_bwd(s["x"], w["norm_mix"], d_h1, dx1, name="norm_mix_bwd")
    return dx, g


LAYER_WEIGHTS = ("norm_mix", "w_in", "s5_a_re", "s5_a_im", "s5_log_dt", "s5_b_re", "s5_b_im", "s5_c_re", "s5_c_im", "s5_d",
                 "s5_w_glu", "ssd_conv_w", "ssd_conv_b", "ssd_dt_bias", "ssd_a_log", "ssd_d", "ssd_norm", "w_out",
                 "norm_xattn", "norm_mem", "xa_wq", "xa_wk", "xa_wv", "xa_wo", "norm_mlp", "mlp_w1", "mlp_w2")
WEIGHTS = LAYER_WEIGHTS + ("norm_final",)


def _local_step(x, mem, target, weights):
    consts = {
        "ssd": _ssd_consts(),
        "rep": (jnp.arange(S5_STATE)[:, None] == jnp.arange(S5_STATE * S5_GROUP)[None, :] // S5_GROUP).astype(F32),
    }
    layers = []
    for l in range(DEPTH):
        w = {n: weights[n][l] for n in LAYER_WEIGHTS}
        w_in = w["w_in"]
        w["w_main"] = w_in[:, :D_MAIN]
        w["w_dt"] = jnp.pad(w_in[:, D_MAIN:], ((0, 0), (0, HP - SSD_HEADS)))
        layers.append(w)
    saved = []
    for l in range(DEPTH):
        x, s = _layer_fwd(x, mem, layers[l], consts)
        saved.append(s)
    loss, dx, g_final = _loss_head(x, weights["norm_final"], target)
    grads = [None] * DEPTH
    for l in reversed(range(DEPTH)):
        dx, grads[l] = _layer_bwd(dx, mem, layers[l], saved[l], consts)
    out = {n: jnp.stack([grads[l][n].reshape(weights[n].shape[1:]) for l in range(DEPTH)]) for n in LAYER_WEIGHTS}
    out["norm_final"] = g_final.reshape(weights["norm_final"].shape)
    return loss, dx, out


SHARDED = {"w_in": 2, "s5_w_glu": 1, "ssd_conv_w": 2, "w_out": 1, "xa_wq": 1, "xa_wk": 1, "xa_wv": 1, "xa_wo": 1,
           "mlp_w1": 2, "mlp_w2": 1}
EXACT = ("ssd_conv_w",)
REPLICATED = tuple(n for n in WEIGHTS if n not in SHARDED)
LANES = 128
PAD_ROWS = 16


def _to_rows(flat, lead=()):
    n = flat.shape[-1]
    quantum = LANES * PAD_ROWS
    padded = -(-n // quantum) * quantum
    flat = jnp.pad(flat, [(0, 0)] * len(lead) + [(0, padded - n)])
    return flat.reshape(*lead, padded // LANES, LANES)


def _gather_weights(local):
    parts = []
    for n in SHARDED:
        if n in EXACT:
            parts.append(lax.bitcast_convert_type(local[n], MXU).reshape(-1))
        else:
            parts.append(local[n].astype(MXU).reshape(-1))
    gathered = _exchange(_to_rows(jnp.concatenate(parts)), scatter=False, name="gather_weights")
    flat = gathered.reshape(N_DEV, -1)
    full, off = {}, 0
    for n, ax in SHARDED.items():
        shp = local[n].shape
        size = math.prod(shp) * (2 if n in EXACT else 1)
        seg = flat[:, off:off + size]
        off += size
        if n in EXACT:
            seg = lax.bitcast_convert_type(seg.reshape(N_DEV, *shp, 2), F32)
        else:
            seg = seg.reshape(N_DEV, *shp)
        seg = jnp.moveaxis(seg, 0, ax)
        full[n] = seg.reshape(*shp[:ax], N_DEV * shp[ax], *shp[ax + 1:])
    return full


def _scatter_grads(grads, local_shapes):
    parts = []
    for n, ax in SHARDED.items():
        shp = local_shapes[n]
        gfull = grads[n].reshape(*shp[:ax], N_DEV, shp[ax], *shp[ax + 1:])
        parts.append(jnp.moveaxis(gfull, ax, 0).reshape(N_DEV, -1).astype(MXU))
    payload = _to_rows(jnp.concatenate(parts, axis=1), lead=(N_DEV,))
    summed = _sum_slots(_exchange(payload, scatter=True, name="scatter_grads"), name="sum_sharded_grads").reshape(-1)
    out, off = {}, 0
    for n in SHARDED:
        size = math.prod(local_shapes[n])
        out[n] = summed[off:off + size].reshape(local_shapes[n])
        off += size
    return out


def _allreduce_small(loss, grads):
    parts = [loss.reshape(-1)[:1]] + [grads[n].reshape(-1) for n in REPLICATED]
    payload = _to_rows(jnp.concatenate(parts))
    summed = _sum_slots(_exchange(payload, scatter=False, name="gather_small_grads"), name="sum_small_grads").reshape(-1)
    out, off = {}, 1
    for n in REPLICATED:
        size = grads[n].size
        out[n] = summed[off:off + size].reshape(grads[n].shape)
        off += size
    return summed[0], out


def kernel(x, mem, norm_mix, w_in, s5_a_re, s5_a_im, s5_log_dt, s5_b_re, s5_b_im, s5_c_re, s5_c_im, s5_d, s5_w_glu, ssd_conv_w, ssd_conv_b, ssd_dt_bias, ssd_a_log, ssd_d, ssd_norm, w_out, norm_xattn, norm_mem, xa_wq, xa_wk, xa_wv, xa_wo, norm_mlp, mlp_w1, mlp_w2, norm_final, loss_target, m_norm_mix, m_w_in, m_s5_a_re, m_s5_a_im, m_s5_log_dt, m_s5_b_re, m_s5_b_im, m_s5_c_re, m_s5_c_im, m_s5_d, m_s5_w_glu, m_ssd_conv_w, m_ssd_conv_b, m_ssd_dt_bias, m_ssd_a_log, m_ssd_d, m_ssd_norm, m_w_out, m_norm_xattn, m_norm_mem, m_xa_wq, m_xa_wk, m_xa_wv, m_xa_wo, m_norm_mlp, m_mlp_w1, m_mlp_w2, m_norm_final, v_norm_mix, v_w_in, v_s5_a_re, v_s5_a_im, v_s5_log_dt, v_s5_b_re, v_s5_b_im, v_s5_c_re, v_s5_c_im, v_s5_d, v_s5_w_glu, v_ssd_conv_w, v_ssd_conv_b, v_ssd_dt_bias, v_ssd_a_log, v_ssd_d, v_ssd_norm, v_w_out, v_norm_xattn, v_norm_mem, v_xa_wq, v_xa_wk, v_xa_wv, v_xa_wo, v_norm_mlp, v_mlp_w1, v_mlp_w2, v_norm_final):
    args = locals()
    local = {n: args[n] for n in WEIGHTS}
    full = dict(local)
    full.update(_gather_weights(local))
    loss, grad_x, grads = _local_step(x[0], mem[0], loss_target[0], full)
    loss, g_small = _allreduce_small(loss, grads)
    g_all = _scatter_grads(grads, {n: local[n].shape for n in SHARDED})
    g_all.update(g_small)
    delta, new_m, new_v = {}, {}, {}
    for n in WEIGHTS:
        delta[n], new_m[n], new_v[n] = _adamw(local[n], g_all[n], args["m_" + n], args["v_" + n], name="adamw_" + n)
    return (loss, grad_x[None], *[g_all[n] for n in WEIGHTS], *[delta[n] for n in WEIGHTS],
            *[new_m[n] for n in WEIGHTS], *[new_v[n] for n in WEIGHTS])
```

```python
import functools
import math

import jax
import jax.numpy as jnp
from jax import lax
from jax.experimental import pallas as pl
from jax.experimental.pallas import tpu as pltpu

F32 = jnp.float32
MXU = jnp.bfloat16
HI = lax.Precision.HIGHEST

D_MODEL = 1024
DEPTH = 4
MEM_LEN = 256
D_S5 = 1024
D_SSD = 1024
S5_GROUP = 16
S5_GROUPS = 64
S5_STATE = 64
S5_CH = S5_GROUPS * S5_STATE
S5_NB = 4
S5_GPB = S5_GROUPS // S5_NB
SSD_HEADDIM = 64
SSD_HEADS = 16
SSD_GROUPS = 4
SSD_STATE = 128
SSD_CONV = 4
SSD_CHUNK = 128
SSD_BC = SSD_GROUPS * SSD_STATE
D_CONV_CH = 2048
D_MAIN = 4096
D_IN_PROJ = D_MAIN + SSD_HEADS
HP = 128
XA_HEADS = 4
XA_HEAD_DIM = 256
D_FF = 4096
EPS = 1e-5
N_DEV = 8
AXES = ("x", "y", "c")

ADAM_LR = 0.001
ADAM_B1 = 0.9
ADAM_B2 = 0.999
ADAM_EPS = 1e-08
ADAM_WD = 0.01
ADAM_STEP = 10

VMEM_LIMIT = 56 * 1024 * 1024


def _params(*sem):
    return pltpu.CompilerParams(dimension_semantics=sem, vmem_limit_bytes=VMEM_LIMIT)


def _tile(n, pref, quantum=128):
    t = (min(pref, n) // quantum) * quantum
    while t >= quantum:
        if n % t == 0:
            return t
        t -= quantum
    return n


def _sds(shape, dtype):
    return jax.ShapeDtypeStruct(tuple(shape), dtype)


def _sigmoid(x):
    return 1.0 / (1.0 + jnp.exp(-x))


def _silu(x):
    return x * _sigmoid(x)


def _silu_grad(x):
    s = _sigmoid(x)
    return s * (1.0 + x * (1.0 - s))


_GELU_C = math.sqrt(2.0 / math.pi)


def _gelu(x):
    return 0.5 * x * (1.0 + jnp.tanh(_GELU_C * (x + 0.044715 * x * x * x)))


def _gelu_grad(x):
    th = jnp.tanh(_GELU_C * (x + 0.044715 * x * x * x))
    return 0.5 * (1.0 + th) + 0.5 * x * (1.0 - th * th) * _GELU_C * (1.0 + 3.0 * 0.044715 * x * x)


def _softplus(x):
    return jnp.maximum(x, 0.0) + jnp.log(1.0 + jnp.exp(-jnp.abs(x)))


_NN = (((1,), (0,)), ((), ()))
_NT = (((1,), (1,)), ((), ()))
_TN = (((0,), (0,)), ((), ()))


def _dot(a, b, dims=_NN, precision=None):
    return lax.dot_general(a, b, dims, precision=precision, preferred_element_type=F32)


def _rows8(v):
    return jnp.broadcast_to(v, (8, v.shape[1]))


def _mm(a, b, *, ta=False, tb=False, extras=(), epi=None, out_dtypes=(F32,), tm=1024, tn=1024, tk=1024, name):
    m, k = (a.shape[1], a.shape[0]) if ta else a.shape
    n = b.shape[0] if tb else b.shape[1]
    assert k == (b.shape[1] if tb else b.shape[0]), (a.shape, b.shape, ta, tb)
    tm, tn, tk = _tile(m, tm), _tile(n, tn), _tile(k, tk)
    nk = k // tk
    n_ex, n_out = len(extras), len(out_dtypes)
    dims = (((0,) if ta else (1,), (1,) if tb else (0,)), ((), ()))

    def body(a_ref, b_ref, *rest):
        ex_refs, out_refs = rest[:n_ex], rest[n_ex:n_ex + n_out]
        prod = _dot(a_ref[...].astype(MXU), b_ref[...].astype(MXU), dims)

        def finish(total):
            outs = epi(total, *[e[...] for e in ex_refs]) if epi is not None else (total,)
            for o, r in zip(outs, out_refs, strict=True):
                r[...] = o.astype(r.dtype)

        if nk == 1:
            finish(prod)
            return
        acc = rest[-1]
        kk = pl.program_id(2)

        @pl.when(kk == 0)
        def _():
            acc[...] = prod

        @pl.when(jnp.logical_and(kk > 0, kk < nk - 1))
        def _():
            acc[...] += prod

        @pl.when(kk == nk - 1)
        def _():
            finish(acc[...] + prod)

    a_spec = pl.BlockSpec((tk, tm), lambda i, j, kk: (kk, i)) if ta else pl.BlockSpec((tm, tk), lambda i, j, kk: (i, kk))
    b_spec = pl.BlockSpec((tn, tk), lambda i, j, kk: (j, kk)) if tb else pl.BlockSpec((tk, tn), lambda i, j, kk: (kk, j))
    mn_spec = pl.BlockSpec((tm, tn), lambda i, j, kk: (i, j))
    outs = pl.pallas_call(
        body,
        name=name,
        grid=(m // tm, n // tn, nk),
        in_specs=[a_spec, b_spec] + [mn_spec] * n_ex,
        out_specs=[mn_spec] * n_out,
        out_shape=[_sds((m, n), dt) for dt in out_dtypes],
        scratch_shapes=[pltpu.VMEM((tm, tn), F32)] if nk > 1 else [],
        compiler_params=_params("parallel", "parallel", "arbitrary"),
    )(a, b, *extras)
    return outs[0] if n_out == 1 else outs


def _bdmm(terms, *, nb, ko, no, extras=(), vecs=(), epi=None, out_dtypes=(F32,), tm=512, name):
    t = terms[0][0].shape[0]
    tm = _tile(t, tm)
    n_t, n_ex, n_v, n_out = len(terms), len(extras), len(vecs), len(out_dtypes)

    def body(*refs):
        a_refs, w_refs = refs[0:2 * n_t:2], refs[1:2 * n_t:2]
        ex_refs = refs[2 * n_t:2 * n_t + n_ex + n_v]
        out_refs = refs[2 * n_t + n_ex + n_v:]
        prods = [_dot(a[...].astype(MXU), w[...].astype(MXU)) for a, w in zip(a_refs, w_refs)]
        outs = epi(prods, *[e[...] for e in ex_refs]) if epi is not None else (prods[0],)
        for o, r in zip(outs, out_refs, strict=True):
            r[...] = o.astype(r.dtype)

    in_specs, args = [], []
    for a, w in terms:
        in_specs.append(pl.BlockSpec((tm, ko), lambda j, i: (i, j)))
        in_specs.append(pl.BlockSpec((None, ko, no), lambda j, i: (j, 0, 0)))
        args += [a, w]
    o_spec = pl.BlockSpec((tm, no), lambda j, i: (i, j))
    v_spec = pl.BlockSpec((1, no), lambda j, i: (0, j))
    outs = pl.pallas_call(
        body,
        name=name,
        grid=(nb, t // tm),
        in_specs=in_specs + [o_spec] * n_ex + [v_spec] * n_v,
        out_specs=[o_spec] * n_out,
        out_shape=[_sds((t, nb * no), dt) for dt in out_dtypes],
        compiler_params=_params("parallel", "parallel"),
    )(*args, *extras, *vecs)
    return outs[0] if n_out == 1 else outs


def _bdmm_tn(a, ka, b, nbc, *, nb, tt=512, name):
    t = a.shape[0]
    tt = _tile(t, tt)
    nt = t // tt

    def body(a_ref, b_ref, o_ref, acc):
        s = pl.program_id(1)

        @pl.when(s == 0)
        def _():
            acc[...] = jnp.zeros_like(acc)

        acc[...] += _dot(a_ref[...].astype(MXU), b_ref[...].astype(MXU), _TN)

        @pl.when(s == nt - 1)
        def _():
            o_ref[...] = acc[...]

    return pl.pallas_call(
        body,
        name=name,
        grid=(nb, nt),
        in_specs=[pl.BlockSpec((tt, ka), lambda j, s: (s, j)), pl.BlockSpec((tt, nbc), lambda j, s: (s, j))],
        out_specs=pl.BlockSpec((None, ka, nbc), lambda j, s: (j, 0, 0)),
        out_shape=_sds((nb, ka, nbc), F32),
        scratch_shapes=[pltpu.VMEM((ka, nbc), F32)],
        compiler_params=_params("parallel", "arbitrary"),
    )(a, b)


def _row(tb, w, cb=0):
    return pl.BlockSpec((tb, w), lambda i: (i, cb))


def _const(shape):
    return pl.BlockSpec(shape, lambda i: (0,) * len(shape))


def _rmsnorm_fwd(x, g, *, name):
    t, d = x.shape
    tb = _tile(t, 512, 8)

    def body(x_ref, g_ref, h_ref):
        xv = x_ref[...]
        r = lax.rsqrt(jnp.mean(xv * xv, axis=-1, keepdims=True) + EPS)
        h_ref[...] = (xv * r * g_ref[...]).astype(h_ref.dtype)

    return pl.pallas_call(
        body, name=name, grid=(t // tb,), in_specs=[_row(tb, d), _const((1, d))], out_specs=_row(tb, d),
        out_shape=_sds((t, d), MXU), compiler_params=_params("parallel"),
    )(x, g.reshape(1, d))


def _rmsnorm_bwd(x, g, dh, dres, *, name):
    t, d = x.shape
    tb = _tile(t, 256, 8)
    has_res = dres is not None

    def body(x_ref, g_ref, dh_ref, *rest):
        dx_ref, dg_ref = rest[-2:]

        @pl.when(pl.program_id(0) == 0)
        def _():
            dg_ref[...] = jnp.zeros_like(dg_ref)

        xv = x_ref[...]
        r = lax.rsqrt(jnp.mean(xv * xv, axis=-1, keepdims=True) + EPS)
        xh = xv * r
        dhv = dh_ref[...].astype(F32)
        dg_ref[...] += jnp.sum(dhv * xh, axis=0, keepdims=True)
        dxh = dhv * g_ref[...]
        dx = r * (dxh - xh * jnp.mean(dxh * xh, axis=-1, keepdims=True))
        if has_res:
            dx = dx + rest[0][...]
        dx_ref[...] = dx

    ins = [x, g.reshape(1, d), dh] + ([dres] if has_res else [])
    return pl.pallas_call(
        body, name=name, grid=(t // tb,),
        in_specs=[_row(tb, d), _const((1, d)), _row(tb, d)] + ([_row(tb, d)] if has_res else []),
        out_specs=[_row(tb, d), _const((1, d))],
        out_shape=[_sds((t, d), F32), _sds((1, d), F32)],
        compiler_params=_params("arbitrary"),
    )(*ins)


def _loss_head(x, g, target):
    t, d = x.shape
    tb = _tile(t, 256, 8)

    def body(x_ref, g_ref, tg_ref, loss_ref, dx_ref, dg_ref):
        @pl.when(pl.program_id(0) == 0)
        def _():
            dg_ref[...] = jnp.zeros_like(dg_ref)
            loss_ref[...] = jnp.zeros_like(loss_ref)

        xv, gv = x_ref[...], g_ref[...]
        r = lax.rsqrt(jnp.mean(xv * xv, axis=-1, keepdims=True) + EPS)
        xh = xv * r
        err = xh * gv - tg_ref[...]
        loss_ref[...] += 0.5 * jnp.sum(jnp.mean(err * err, axis=-1, keepdims=True), axis=0, keepdims=True)
        dy = err * (1.0 / d)
        dg_ref[...] += jnp.sum(dy * xh, axis=0, keepdims=True)
        dxh = dy * gv
        dx_ref[...] = r * (dxh - xh * jnp.mean(dxh * xh, axis=-1, keepdims=True))

    return pl.pallas_call(
        body, name="loss_head", grid=(t // tb,),
        in_specs=[_row(tb, d), _const((1, d)), _row(tb, d)],
        out_specs=[_const((1, HP)), _row(tb, d), _const((1, d))],
        out_shape=[_sds((1, HP), F32), _sds((t, d), F32), _sds((1, d), F32)],
        compiler_params=_params("arbitrary"),
    )(x, g.reshape(1, d), target)


def _s5_discretise(ar, ai, ldt, br, bi, rep):
    dt = jnp.exp(ldt)
    mag = jnp.exp(dt * ar)
    abar_r, abar_i = mag * jnp.cos(dt * ai), mag * jnp.sin(dt * ai)
    den = ar * ar + ai * ai
    zr, zi = abar_r - 1.0, abar_i
    fr = (zr * ar + zi * ai) / den
    fi = (zi * ar - zr * ai) / den
    fr_e, fi_e = _dot(fr, rep, precision=HI), _dot(fi, rep, precision=HI)
    return abar_r, abar_i, fr_e * br - fi_e * bi, fr_e * bi + fi_e * br


def _s5_prep(ar, ai, ldt, br, bi, rep):
    g, p = ar.shape
    ph = br.shape[1]

    def body(ar_ref, ai_ref, ldt_ref, br_ref, bi_ref, rep_ref, o0, o1, o2, o3):
        outs = _s5_discretise(ar_ref[...], ai_ref[...], ldt_ref[...], br_ref[...], bi_ref[...], rep_ref[...])
        for o, v in zip((o0, o1, o2, o3), outs):
            o[...] = v

    return pl.pallas_call(
        body, name="s5_prep",
        out_shape=[_sds((g, p), F32), _sds((g, p), F32), _sds((g, ph), F32), _sds((g, ph), F32)],
        compiler_params=pltpu.CompilerParams(vmem_limit_bytes=VMEM_LIMIT),
    )(ar, ai, ldt, br, bi, rep)


def _s5_prep_bwd(ar, ai, ldt, br, bi, rep, d_abar_r, d_abar_i, d_bbar_r, d_bbar_i):
    g, p = ar.shape
    ph = br.shape[1]

    def body(ar_ref, ai_ref, ldt_ref, br_ref, bi_ref, rep_ref, c0, c1, c2, c3, o0, o1, o2, o3, o4):
        rep_v = rep_ref[...]
        _, vjp = jax.vjp(lambda a, b, c, d, e: _s5_discretise(a, b, c, d, e, rep_v),
                         ar_ref[...], ai_ref[...], ldt_ref[...], br_ref[...], bi_ref[...])
        grads = vjp((c0[...], c1[...], c2[...], c3[...]))
        for o, v in zip((o0, o1, o2, o3, o4), grads):
            o[...] = v

    return pl.pallas_call(
        body, name="s5_prep_bwd",
        out_shape=[_sds((g, p), F32), _sds((g, p), F32), _sds((g, 1), F32), _sds((g, ph), F32), _sds((g, ph), F32)],
        compiler_params=pltpu.CompilerParams(vmem_limit_bytes=VMEM_LIMIT),
    )(ar, ai, ldt, br, bi, rep, d_abar_r, d_abar_i, d_bbar_r, d_bbar_i)


SCAN_ROWS = 256
SCAN_LANES = 1024


def _scan_rows(t):
    return _tile(t, SCAN_ROWS, 64)


def _interleave_rows(x):
    t, c = x.shape
    tb = _scan_rows(t)
    return x.reshape(t // tb, 8, tb // 8, c).swapaxes(1, 2).reshape(t, c)


def _deinterleave_rows(x):
    t, c = x.shape
    tb = _scan_rows(t)
    return x.reshape(t // tb, tb // 8, 8, c).swapaxes(1, 2).reshape(t, c)


def _cmul(ar, ai, br, bi):
    return ar * br - ai * bi, ar * bi + ai * br


def _segment_carries(fr, fi, ar8, ai8, c_r, c_i, seg, reverse):
    pr, pi = ar8, ai8
    for _ in range(int(math.log2(seg))):
        pr, pi = _cmul(pr, pi, pr, pi)
    row = lax.broadcasted_iota(jnp.int32, fr.shape, 0)
    edge = 7 if reverse else 0
    qr, qi = _cmul(pr, pi, c_r, c_i)
    xr, xi = jnp.where(row == edge, fr + qr, fr), jnp.where(row == edge, fi + qi, fi)
    for sh in (1, 2, 4):
        if reverse:
            keep, amount = row < 8 - sh, 8 - sh
        else:
            keep, amount = row >= sh, sh
        qr, qi = jnp.where(keep, pltpu.roll(xr, amount, 0), 0.0), jnp.where(keep, pltpu.roll(xi, amount, 0), 0.0)
        tr, ti = _cmul(pr, pi, qr, qi)
        xr, xi = xr + tr, xi + ti
        pr, pi = _cmul(pr, pi, pr, pi)
    if reverse:
        in_r, in_i = jnp.where(row == 7, c_r, pltpu.roll(xr, 7, 0)), jnp.where(row == 7, c_i, pltpu.roll(xi, 7, 0))
        return in_r, in_i, xr[0:1, :], xi[0:1, :]
    in_r, in_i = jnp.where(row == 0, c_r, pltpu.roll(xr, 1, 0)), jnp.where(row == 0, c_i, pltpu.roll(xi, 1, 0))
    return in_r, in_i, xr[7:8, :], xi[7:8, :]


def _s5_scan(ar, ai, dr, di, *, reverse, name, states=None):
    t, ch = dr.shape
    tb = _scan_rows(t)
    tc = _tile(ch, SCAN_LANES)
    nt = t // tb
    seg = tb // 8
    assert 1 << int(math.log2(seg)) == seg
    order = range(seg - 1, -1, -1) if reverse else range(seg)
    with_sum = states is not None
    assert reverse or not with_sum

    def body(*refs):
        ar_ref, ai_ref, dr_ref, di_ref = refs[:4]
        if with_sum:
            pr_ref, pi_ref, hr_ref, hi_ref, sr_ref, si_ref, gr_ref, gi_ref, cr, ci = refs[4:]
        else:
            sr_ref, si_ref, cr, ci = refs[4:]
        step = pl.program_id(1)

        @pl.when(step == 0)
        def _():
            cr[...] = jnp.zeros_like(cr)
            ci[...] = jnp.zeros_like(ci)
            if with_sum:
                gr_ref[...] = jnp.zeros_like(gr_ref)
                gi_ref[...] = jnp.zeros_like(gi_ref)

        rows = lambda j: slice(j * 8, (j + 1) * 8)
        ar8 = jnp.broadcast_to(ar_ref[...], (8, tc))
        ai8 = jnp.broadcast_to(-ai_ref[...] if reverse else ai_ref[...], (8, tc))
        fr, fi = jnp.zeros((8, tc), F32), jnp.zeros((8, tc), F32)
        for j in order:
            tr, ti = _cmul(ar8, ai8, fr, fi)
            fr, fi = tr + dr_ref[rows(j), :], ti + di_ref[rows(j), :]
        s_r, s_i, out_r, out_i = _segment_carries(fr, fi, ar8, ai8, cr[...], ci[...], seg, reverse)
        cr[...] = out_r
        ci[...] = out_i
        if with_sum:
            first_block = step == nt - 1
            row = lax.broadcasted_iota(jnp.int32, (8, tc), 0)
            acc_r, acc_i = jnp.zeros((8, tc), F32), jnp.zeros((8, tc), F32)
        for j in order:
            tr, ti = _cmul(ar8, ai8, s_r, s_i)
            s_r, s_i = tr + dr_ref[rows(j), :], ti + di_ref[rows(j), :]
            sr_ref[rows(j), :] = s_r
            si_ref[rows(j), :] = s_i
            if with_sum:
                if j > 0:
                    p_r, p_i = pr_ref[rows(j - 1), :], pi_ref[rows(j - 1), :]
                else:
                    halo_r = jnp.where(first_block, 0.0, hr_ref[7:8, :])
                    halo_i = jnp.where(first_block, 0.0, hi_ref[7:8, :])
                    p_r = jnp.where(row == 0, halo_r, pltpu.roll(pr_ref[rows(seg - 1), :], 1, 0))
                    p_i = jnp.where(row == 0, halo_i, pltpu.roll(pi_ref[rows(seg - 1), :], 1, 0))
                acc_r = acc_r + (p_r * s_r + p_i * s_i)
                acc_i = acc_i + (p_r * s_i - p_i * s_r)
        if with_sum:
            gr_ref[...] += jnp.sum(acc_r, axis=0, keepdims=True)
            gi_ref[...] += jnp.sum(acc_i, axis=0, keepdims=True)

    if reverse:
        blk = pl.BlockSpec((tb, tc), lambda c, s: (nt - 1 - s, c))
    else:
        blk = pl.BlockSpec((tb, tc), lambda c, s: (s, c))
    vec = pl.BlockSpec((1, tc), lambda c, s: (0, c))
    in_specs, args = [vec, vec, blk, blk], [ar, ai, dr, di]
    out_specs, out_shape = [blk, blk], [_sds((t, ch), F32)] * 2
    if with_sum:
        halo = pl.BlockSpec((8, tc), lambda c, s: (jnp.maximum((nt - 1 - s) * (tb // 8) - 1, 0), c))
        in_specs += [blk, blk, halo, halo]
        args += [states[0], states[1], states[0], states[1]]
        out_specs += [vec, vec]
        out_shape += [_sds((1, ch), F32)] * 2
    return pl.pallas_call(
        body, name=name, grid=(ch // tc, nt), in_specs=in_specs, out_specs=out_specs, out_shape=out_shape,
        scratch_shapes=[pltpu.VMEM((1, tc), F32), pltpu.VMEM((1, tc), F32)],
        compiler_params=_params("parallel", "arbitrary"),
    )(*args)


def _colsum_prod(a, b, w, *, name):
    t = a.shape[0]
    tb = _tile(t, 512, 8)

    def body(a_ref, b_ref, o_ref):
        @pl.when(pl.program_id(0) == 0)
        def _():
            o_ref[...] = jnp.zeros_like(o_ref)

        o_ref[...] += jnp.sum(a_ref[...].astype(F32) * b_ref[...].astype(F32), axis=0, keepdims=True)

    return pl.pallas_call(
        body, name=name, grid=(t // tb,), in_specs=[_row(tb, w), _row(tb, w)],
        out_specs=_const((1, w)), out_shape=_sds((1, w), F32), compiler_params=_params("arbitrary"),
    )(a, b)


def _s5_gate_bwd(d_ycat, gp, ypre):
    t, d = gp.shape
    tb = _tile(t, 256, 8)

    def body(do_ref, gp_ref, yp_ref, o_ref):
        sg = _sigmoid(gp_ref[...])
        o_ref[...] = (do_ref[...] * _gelu(yp_ref[...]) * sg * (1.0 - sg)).astype(o_ref.dtype)

    return pl.pallas_call(
        body, name="s5_gate_bwd", grid=(t // tb,), in_specs=[_row(tb, d), _row(tb, d), _row(tb, d)],
        out_specs=_row(tb, d), out_shape=_sds((t, d), MXU), compiler_params=_params("parallel"),
    )(d_ycat, gp, ypre)


_CONV_CW = 512
_CONV_OFF = (D_MAIN - D_CONV_CH) // _CONV_CW


def _conv_fwd(proj, w, b):
    t = proj.shape[0]
    tb = _tile(t, 256, 8)
    cw, off = _CONV_CW, _CONV_OFF

    def body(cur_ref, prev_ref, w_ref, b_ref, o_ref, ext):
        first = pl.program_id(1) == 0
        ext[0:8, :] = jnp.where(first, 0.0, prev_ref[...])
        ext[8:tb + 8, :] = cur_ref[...]
        acc = jnp.broadcast_to(b_ref[...], (tb, cw))
        for j in range(SSD_CONV):
            acc = acc + w_ref[SSD_CONV - 1 - j:SSD_CONV - j, :] * ext[8 - j:8 - j + tb, :]
        o_ref[...] = acc

    return pl.pallas_call(
        body, name="ssd_conv_fwd", grid=(D_CONV_CH // cw, t // tb),
        in_specs=[
            pl.BlockSpec((tb, cw), lambda j, i: (i, j + off)),
            pl.BlockSpec((8, cw), lambda j, i: (jnp.maximum(i * (tb // 8) - 1, 0), j + off)),
            pl.BlockSpec((SSD_CONV, cw), lambda j, i: (0, j)),
            pl.BlockSpec((1, cw), lambda j, i: (0, j)),
        ],
        out_specs=pl.BlockSpec((tb, cw), lambda j, i: (i, j)),
        out_shape=_sds((t, D_CONV_CH), F32),
        scratch_shapes=[pltpu.VMEM((tb + 8, cw), F32)],
        compiler_params=_params("parallel", "arbitrary"),
    )(proj, proj, w, b.reshape(1, D_CONV_CH))


def _conv_bwd(proj, w, d_conv):
    t = proj.shape[0]
    tb = _tile(t, 256, 8)
    cw, off = _CONV_CW, _CONV_OFF
    nt = t // tb

    def body(cur_ref, prev_ref, w_ref, dc_ref, dnext_ref, dx_ref, dw_ref, db_ref, ext, dext):
        i = pl.program_id(1)

        @pl.when(i == 0)
        def _():
            dw_ref[...] = jnp.zeros_like(dw_ref)
            db_ref[...] = jnp.zeros_like(db_ref)

        ext[0:8, :] = jnp.where(i == 0, 0.0, prev_ref[...])
        ext[8:tb + 8, :] = cur_ref[...]
        dcv = dc_ref[...]
        dext[0:tb, :] = dcv
        dext[tb:tb + 8, :] = jnp.where(i == nt - 1, 0.0, dnext_ref[...])
        dx = jnp.zeros((tb, cw), F32)
        for j in range(SSD_CONV):
            dx = dx + w_ref[SSD_CONV - 1 - j:SSD_CONV - j, :] * dext[j:j + tb, :]
            dw_ref[SSD_CONV - 1 - j:SSD_CONV - j, :] += jnp.sum(dcv * ext[8 - j:8 - j + tb, :], axis=0, keepdims=True)
        dx_ref[...] = dx
        db_ref[...] += jnp.sum(dcv, axis=0, keepdims=True)

    return pl.pallas_call(
        body, name="ssd_conv_bwd", grid=(D_CONV_CH // cw, nt),
        in_specs=[
            pl.BlockSpec((tb, cw), lambda j, i: (i, j + off)),
            pl.BlockSpec((8, cw), lambda j, i: (jnp.maximum(i * (tb // 8) - 1, 0), j + off)),
            pl.BlockSpec((SSD_CONV, cw), lambda j, i: (0, j)),
            pl.BlockSpec((tb, cw), lambda j, i: (i, j)),
            pl.BlockSpec((8, cw), lambda j, i: (jnp.minimum((i + 1) * (tb // 8), t // 8 - 1), j)),
        ],
        out_specs=[
            pl.BlockSpec((tb, cw), lambda j, i: (i, j)),
            pl.BlockSpec((SSD_CONV, cw), lambda j, i: (0, j)),
            pl.BlockSpec((1, cw), lambda j, i: (0, j)),
        ],
        out_shape=[_sds((t, D_CONV_CH), F32), _sds((SSD_CONV, D_CONV_CH), F32), _sds((1, D_CONV_CH), F32)],
        scratch_shapes=[pltpu.VMEM((tb + 8, cw), F32), pltpu.VMEM((tb + 8, cw), F32)],
        compiler_params=_params("parallel", "arbitrary"),
    )(proj, proj, w, d_conv, d_conv)


def _ssd_consts():
    head = jnp.arange(HP)[:, None]
    lane = jnp.arange(D_SSD)[None, :]
    expand = ((lane // SSD_HEADDIM) == head).astype(F32)
    ll = jnp.arange(SSD_CHUNK)
    tri = (ll[:, None] >= ll[None, :]).astype(F32)
    return expand, expand.T, tri, jnp.eye(HP, dtype=F32)


def _ssd_chunk_terms(cp, dtr, par, expand, tri):
    ln = SSD_CHUNK
    xbc = _silu(cp)
    xs, bm, cm = xbc[:, :D_SSD], xbc[:, D_SSD:D_SSD + SSD_BC], xbc[:, D_SSD + SSD_BC:]
    dt = _softplus(dtr + par[0:1, :])
    a = -jnp.exp(par[1:2, :])
    da = dt * a
    acum = _dot(tri, da, precision=HI)
    acum_t = _dot(da, tri, (((0,), (1,)), ((), ())), precision=HI)
    atot = acum[ln - 1:ln, :]
    dt_e = _dot(dt, expand, precision=HI)
    eac_e = _dot(jnp.exp(acum), expand, precision=HI)
    dec_e = _dot(jnp.exp(atot - acum), expand, precision=HI)
    eat_e = _dot(_rows8(jnp.exp(atot)), expand, precision=HI)[0:1, :]
    dsk_e = _dot(_rows8(par[2:3, :]), expand, precision=HI)[0:1, :]
    return dict(xs=xs, bm=bm, cm=cm, dt=dt, a=a, acum=acum, acum_t=acum_t, dt_e=dt_e, eac_e=eac_e,
                dec_e=dec_e, eat_e=eat_e, dsk_e=dsk_e)


def _decay_matrix(acum, acum_t, h, mask):
    diff = acum[:, h:h + 1] - acum_t[h:h + 1, :]
    return jnp.where(mask, jnp.exp(jnp.minimum(diff, 0.0)), 0.0)


def _ssd_fwd(proj, conv_pre, dtr, par, gnorm, consts):
    t = proj.shape[0]
    ln = SSD_CHUNK
    nc = t // ln
    expand, _, tri, _ = consts
    hd2 = 2 * SSD_HEADDIM

    def body(cp_ref, z_ref, dtr_ref, par_ref, g_ref, e_ref, tri_ref, out_ref, y_ref, st_ref, state):
        @pl.when(pl.program_id(0) == 0)
        def _():
            state[...] = jnp.zeros_like(state)

        st_ref[...] = state[...]
        c = _ssd_chunk_terms(cp_ref[...], dtr_ref[...], par_ref[...], e_ref[...], tri_ref[...])
        xdt = c["xs"] * c["dt_e"]
        xb, xd = xdt.astype(MXU), (xdt * c["dec_e"]).astype(MXU)
        bb, cb = c["bm"].astype(MXU), c["cm"].astype(MXU)
        mask = lax.broadcasted_iota(jnp.int32, (ln, ln), 0) >= lax.broadcasted_iota(jnp.int32, (ln, ln), 1)
        left = lax.broadcasted_iota(jnp.int32, (ln, hd2), 1) < SSD_HEADDIM
        for g in range(SSD_GROUPS):
            nsl = slice(g * SSD_STATE, (g + 1) * SSD_STATE)
            gsl = slice(g * 256, (g + 1) * 256)
            bg, cg = bb[:, nsl], cb[:, nsl]
            cbm = _dot(cg, bg, _NT)
            st_g = state[:, gsl]
            for pair in range(2):
                h0 = g * 4 + pair * 2
                psl = slice(h0 * SSD_HEADDIM, (h0 + 2) * SSD_HEADDIM)
                m0 = (cbm * _decay_matrix(c["acum"], c["acum_t"], h0, mask)).astype(MXU)
                m1 = (cbm * _decay_matrix(c["acum"], c["acum_t"], h0 + 1, mask)).astype(MXU)
                y_ref[:, psl] = jnp.where(left, _dot(m0, xb[:, psl]), _dot(m1, xb[:, psl]))
            y_ref[:, gsl] += _dot(cg, st_g.astype(MXU)) * c["eac_e"][:, gsl]
            state[:, gsl] = st_g * c["eat_e"][:, gsl] + _dot(bg, xd[:, gsl], _TN)
        y = y_ref[...] + c["dsk_e"] * c["xs"]
        y_ref[...] = y
        y2 = y * _silu(z_ref[...])
        r = lax.rsqrt(jnp.mean(y2 * y2, axis=-1, keepdims=True) + EPS)
        out_ref[...] = (y2 * r * g_ref[...]).astype(out_ref.dtype)

    return pl.pallas_call(
        body, name="ssd_fwd", grid=(nc,),
        in_specs=[_row(ln, D_CONV_CH), _row(ln, D_SSD, 1), _row(ln, HP), _const((8, HP)), _const((1, D_SSD)),
                  _const((HP, D_SSD)), _const((ln, ln))],
        out_specs=[_row(ln, D_SSD), _row(ln, D_SSD), pl.BlockSpec((None, SSD_STATE, D_SSD), lambda i: (i, 0, 0))],
        out_shape=[_sds((t, D_SSD), MXU), _sds((t, D_SSD), F32), _sds((nc, SSD_STATE, D_SSD), F32)],
        scratch_shapes=[pltpu.VMEM((SSD_STATE, D_SSD), F32)],
        compiler_params=_params("arbitrary"),
    )(conv_pre, proj, dtr, par, gnorm.reshape(1, D_SSD), expand, tri)


def _ssd_bwd(proj, conv_pre, dtr, par, gnorm, y, states, d_ycat, consts):
    t = proj.shape[0]
    ln = SSD_CHUNK
    nc = t // ln
    expand, expand_t, tri, eye = consts
    hd2 = 2 * SSD_HEADDIM

    def body(cp_ref, z_ref, dtr_ref, par_ref, g_ref, y_ref, st_ref, do_ref, e_ref, et_ref, tri_ref, eye_ref,
             dcp_ref, dz_ref, ddt_ref, dg_ref, dpar_ref, dstate, dx_buf, lane_buf, tot_buf, colsum):
        @pl.when(pl.program_id(0) == 0)
        def _():
            dstate[...] = jnp.zeros_like(dstate)
            dg_ref[...] = jnp.zeros_like(dg_ref)
            dpar_ref[...] = jnp.zeros_like(dpar_ref)

        cpv, et_v, tri_v, par_v = cp_ref[...], et_ref[...], tri_ref[...], par_ref[...]
        c = _ssd_chunk_terms(cpv, dtr_ref[...], par_v, e_ref[...], tri_v)
        xs = c["xs"]
        zv, yv, dov = z_ref[...], y_ref[...], do_ref[...]
        sz = _silu(zv)
        y2 = yv * sz
        r = lax.rsqrt(jnp.mean(y2 * y2, axis=-1, keepdims=True) + EPS)
        yh = y2 * r
        dg_ref[...] += jnp.sum(dov * yh, axis=0, keepdims=True)
        dyh = dov * g_ref[...]
        dy2 = r * (dyh - yh * jnp.mean(dyh * yh, axis=-1, keepdims=True))
        dz_ref[...] = dy2 * yv * _silu_grad(zv)
        dy = dy2 * sz

        xdt = xs * c["dt_e"]
        xdf = xdt * c["dec_e"]
        xb, xd = xdt.astype(MXU), xdf.astype(MXU)
        bb, cb = c["bm"].astype(MXU), c["cm"].astype(MXU)
        dyb, dye = dy.astype(MXU), (dy * c["eac_e"]).astype(MXU)
        mask = lax.broadcasted_iota(jnp.int32, (ln, ln), 0) >= lax.broadcasted_iota(jnp.int32, (ln, ln), 1)
        left = lax.broadcasted_iota(jnp.int32, (ln, hd2), 1) < SSD_HEADDIM
        lane_hp = lax.broadcasted_iota(jnp.int32, (ln, HP), 1)
        d_acum = jnp.zeros((ln, HP), F32)
        colsum[...] = jnp.zeros_like(colsum)
        tot_buf[...] = jnp.zeros_like(tot_buf)
        for g in range(SSD_GROUPS):
            nsl = slice(g * SSD_STATE, (g + 1) * SSD_STATE)
            gsl = slice(g * 256, (g + 1) * 256)
            bg, cg = bb[:, nsl], cb[:, nsl]
            cbm = _dot(cg, bg, _NT)
            st_g = st_ref[:, gsl]
            dst_g = dstate[:, gsl]
            stb, dstb = st_g.astype(MXU), dst_g.astype(MXU)
            y_off = _dot(cg, stb)
            bds = _dot(bg, dstb)
            dcb = jnp.zeros((ln, ln), F32)
            for pair in range(2):
                h0 = g * 4 + pair * 2
                psl = slice(h0 * SSD_HEADDIM, (h0 + 2) * SSD_HEADDIM)
                xp, dyp = xb[:, psl], dyb[:, psl]
                dxp = []
                for k in range(2):
                    h = h0 + k
                    lm = _decay_matrix(c["acum"], c["acum_t"], h, mask)
                    mm = cbm * lm
                    half = left if k == 0 else jnp.logical_not(left)
                    dm = _dot(jnp.where(half, dyp, jnp.zeros_like(dyp)), xp, _NT)
                    dcb = dcb + dm * lm
                    gm = dm * mm
                    d_acum = d_acum + jnp.where(lane_hp == h, jnp.sum(gm, axis=1, keepdims=True), 0.0)
                    colsum[h:h + 1, :] = jnp.sum(gm, axis=0, keepdims=True)
                    dxp.append(_dot(mm.astype(MXU), dyp, _TN))
                dx_buf[:, psl] = jnp.where(left, dxp[0], dxp[1])
            dx_buf[:, gsl] += bds * c["dec_e"][:, gsl]
            dcbb = dcb.astype(MXU)
            dc_g = _dot(dcbb, bg) + _dot(dye[:, gsl], stb, _NT)
            db_g = _dot(dcbb, cg, _TN) + _dot(xd[:, gsl], dstb, _NT)
            dcp_ref[:, D_SSD + g * SSD_STATE:D_SSD + (g + 1) * SSD_STATE] = db_g
            dcp_ref[:, D_SSD + SSD_BC + g * SSD_STATE:D_SSD + SSD_BC + (g + 1) * SSD_STATE] = dc_g
            dec_term = xdf[:, gsl] * bds
            lane_buf[:, gsl] = dy[:, gsl] * y_off * c["eac_e"][:, gsl] - dec_term
            tot_buf[0:1, gsl] = (jnp.sum(st_g * dst_g, axis=0, keepdims=True) * c["eat_e"][:, gsl]
                                 + jnp.sum(dec_term, axis=0, keepdims=True))
            dstate[:, gsl] = dst_g * c["eat_e"][:, gsl] + _dot(cg, dye[:, gsl], _TN)
        dx_tot = dx_buf[...]
        d_acum = d_acum + _dot(lane_buf[...], et_v, precision=HI) - _dot(colsum[...], eye_ref[...], _TN, precision=HI)
        d_atot = _dot(tot_buf[...], et_v, precision=HI)[0:1, :]
        row_hp = lax.broadcasted_iota(jnp.int32, (ln, HP), 0)
        d_acum = d_acum + jnp.where(row_hp == ln - 1, d_atot, 0.0)
        d_da = _dot(tri_v, d_acum, _TN, precision=HI)
        d_dt = d_da * c["a"] + _dot(dx_tot * xs, et_v, precision=HI)
        d_dtr = d_dt * _sigmoid(dtr_ref[...] + par_v[0:1, :])
        ddt_ref[...] = d_dtr
        dpar_ref[0:1, :] += jnp.sum(d_dtr, axis=0, keepdims=True)
        dpar_ref[1:2, :] += jnp.sum(d_da * c["dt"], axis=0, keepdims=True) * c["a"]
        dpar_ref[2:3, :] += _dot(_rows8(jnp.sum(dy * xs, axis=0, keepdims=True)), et_v, precision=HI)[0:1, :]
        dcp_ref[:, 0:D_SSD] = dx_tot * c["dt_e"] + dy * c["dsk_e"]
        dcp_ref[...] = dcp_ref[...] * _silu_grad(cpv)

    rev = lambda i: (nc - 1 - i, 0)
    rev1 = lambda i: (nc - 1 - i, 1)
    return pl.pallas_call(
        body, name="ssd_bwd", grid=(nc,),
        in_specs=[pl.BlockSpec((ln, D_CONV_CH), rev), pl.BlockSpec((ln, D_SSD), rev1), pl.BlockSpec((ln, HP), rev),
                  _const((8, HP)), _const((1, D_SSD)), pl.BlockSpec((ln, D_SSD), rev),
                  pl.BlockSpec((None, SSD_STATE, D_SSD), lambda i: (nc - 1 - i, 0, 0)),
                  pl.BlockSpec((ln, D_SSD), rev1),
                  _const((HP, D_SSD)), _const((D_SSD, HP)), _const((ln, ln)), _const((HP, HP))],
        out_specs=[pl.BlockSpec((ln, D_CONV_CH), rev), pl.BlockSpec((ln, D_SSD), rev), pl.BlockSpec((ln, HP), rev),
                   _const((1, D_SSD)), _const((8, HP))],
        out_shape=[_sds((t, D_CONV_CH), F32), _sds((t, D_SSD), F32), _sds((t, HP), F32), _sds((1, D_SSD), F32),
                   _sds((8, HP), F32)],
        scratch_shapes=[pltpu.VMEM((SSD_STATE, D_SSD), F32), pltpu.VMEM((ln, D_SSD), F32), pltpu.VMEM((ln, D_SSD), F32),
                        pltpu.VMEM((8, D_SSD), F32), pltpu.VMEM((HP, ln), F32)],
        compiler_params=_params("arbitrary"),
    )(conv_pre, proj, dtr, par, gnorm.reshape(1, D_SSD), y, states, d_ycat, expand, expand_t, tri, eye)


def _softmax_rows(s):
    e = jnp.exp(s - jnp.max(s, axis=-1, keepdims=True))
    return e / jnp.sum(e, axis=-1, keepdims=True)


def _attn_fwd(q, k, v):
    t, d = q.shape
    mlen = k.shape[0]
    tq = _tile(t, 512, 8)
    scale = XA_HEAD_DIM ** -0.5

    def body(q_ref, k_ref, v_ref, o_ref):
        for h in range(XA_HEADS):
            sl = slice(h * XA_HEAD_DIM, (h + 1) * XA_HEAD_DIM)
            p = _softmax_rows(_dot(q_ref[:, sl], k_ref[:, sl], _NT) * scale)
            o_ref[:, sl] = _dot(p.astype(MXU), v_ref[:, sl]).astype(o_ref.dtype)

    return pl.pallas_call(
        body, name="xattn_fwd", grid=(t // tq,),
        in_specs=[_row(tq, d), _const((mlen, d)), _const((mlen, d))], out_specs=_row(tq, d),
        out_shape=_sds((t, d), MXU), compiler_params=_params("parallel"),
    )(q, k, v)


def _attn_bwd(q, k, v, do):
    t, d = q.shape
    mlen = k.shape[0]
    tq = _tile(t, 512, 8)
    scale = XA_HEAD_DIM ** -0.5

    def body(q_ref, k_ref, v_ref, do_ref, dq_ref, dk_ref, dv_ref):
        @pl.when(pl.program_id(0) == 0)
        def _():
            dk_ref[...] = jnp.zeros_like(dk_ref)
            dv_ref[...] = jnp.zeros_like(dv_ref)

        for h in range(XA_HEADS):
            sl = slice(h * XA_HEAD_DIM, (h + 1) * XA_HEAD_DIM)
            qh, kh, vh, doh = q_ref[:, sl], k_ref[:, sl], v_ref[:, sl], do_ref[:, sl]
            p = _softmax_rows(_dot(qh, kh, _NT) * scale)
            dp = _dot(doh, vh, _NT)
            dv_ref[:, sl] += _dot(p.astype(MXU), doh, _TN)
            ds = (p * (dp - jnp.sum(p * dp, axis=-1, keepdims=True)) * scale).astype(MXU)
            dq_ref[:, sl] = _dot(ds, kh).astype(dq_ref.dtype)
            dk_ref[:, sl] += _dot(ds, qh, _TN)

    return pl.pallas_call(
        body, name="xattn_bwd", grid=(t // tq,),
        in_specs=[_row(tq, d), _const((mlen, d)), _const((mlen, d)), _row(tq, d)],
        out_specs=[_row(tq, d), _const((mlen, d)), _const((mlen, d))],
        out_shape=[_sds((t, d), MXU), _sds((mlen, d), F32), _sds((mlen, d), F32)],
        compiler_params=_params("arbitrary"),
    )(q, k, v, do)


def _lane_view(a):
    if a.ndim >= 2 and a.shape[-1] >= 128:
        return a.reshape(-1, a.shape[-1])
    if a.size % 128 == 0:
        return a.reshape(-1, 128)
    return a.reshape(1, -1)


def _adamw(w, g, m, v, *, name):
    shape = w.shape
    w2, g2, m2, v2 = (_lane_view(a) for a in (w, g.reshape(shape), m, v))
    r, c = w2.shape
    tr = _tile(r, max(8, (1 << 18) // max(c, 128)), 8)

    def body(w_ref, g_ref, m_ref, v_ref, d_ref, mo_ref, vo_ref):
        gv = g_ref[...]
        mn = ADAM_B1 * m_ref[...] + (1.0 - ADAM_B1) * gv
        vn = ADAM_B2 * v_ref[...] + (1.0 - ADAM_B2) * (gv * gv)
        m_hat = mn / (1.0 - ADAM_B1 ** ADAM_STEP)
        v_hat = vn / (1.0 - ADAM_B2 ** ADAM_STEP)
        d_ref[...] = -ADAM_LR * (m_hat / (jnp.sqrt(v_hat) + ADAM_EPS) + ADAM_WD * w_ref[...])
        mo_ref[...] = mn
        vo_ref[...] = vn

    outs = pl.pallas_call(
        body, name=name, grid=(r // tr,), in_specs=[_row(tr, c)] * 4, out_specs=[_row(tr, c)] * 3,
        out_shape=[_sds((r, c), F32)] * 3, compiler_params=_params("parallel"),
    )(w2, g2, m2, v2)
    return tuple(o.reshape(shape) for o in outs)


_HBM = pl.BlockSpec(memory_space=pltpu.HBM)


N_CHIPS = 4
CHIPS = ((0, 0), (0, 1), (1, 0), (1, 1))


def _pair_exchange(x, *, name):
    def body(x_ref, o_ref, send_sems, recv_sems):
        xx, yy, cc = (lax.axis_index(a) for a in AXES)
        copies = [
            pltpu.make_async_remote_copy(
                src_ref=x_ref.at[4 * px + 2 * py + (1 - cc)], dst_ref=o_ref.at[k], send_sem=send_sems.at[k],
                recv_sem=recv_sems.at[k], device_id=(xx, yy, 1 - cc), device_id_type=pl.DeviceIdType.MESH)
            for k, (px, py) in enumerate(CHIPS)]
        for cp in copies:
            cp.start()
        for cp in copies:
            cp.wait_recv()
        for cp in copies:
            cp.wait_send()

    return pl.pallas_call(
        body, name=name, in_specs=[_HBM], out_specs=_HBM, out_shape=_sds((N_CHIPS,) + tuple(x.shape[1:]), x.dtype),
        scratch_shapes=[pltpu.SemaphoreType.DMA((N_CHIPS,)), pltpu.SemaphoreType.DMA((N_CHIPS,))],
    )(x)


def _pair_add(x, got, *, name):
    _, r, c = x.shape
    tr = _tile(r, max(PAD_ROWS, (1 << 17) // c), PAD_ROWS)

    def body(x_ref, g_ref, o_ref):
        mine = jnp.where(lax.axis_index("c") == 0, x_ref[0].astype(F32), x_ref[1].astype(F32))
        o_ref[...] = (mine + g_ref[...].astype(F32)).astype(o_ref.dtype)

    return pl.pallas_call(
        body, name=name, grid=(N_CHIPS, r // tr),
        in_specs=[pl.BlockSpec((None, 2, tr, c), lambda k, i: (k, 0, i, 0)), pl.BlockSpec((None, tr, c), lambda k, i: (k, i, 0))],
        out_specs=pl.BlockSpec((None, tr, c), lambda k, i: (k, i, 0)), out_shape=_sds((N_CHIPS, r, c), x.dtype),
        compiler_params=_params("parallel", "parallel"),
    )(x.reshape(N_CHIPS, 2, r, c), got)


def _chip_exchange(x, *, name):
    def body(x_ref, o_ref, send_sems, recv_sems, local_sem):
        xx, yy, cc = (lax.axis_index(a) for a in AXES)
        mine = 2 * xx + yy
        local = pltpu.make_async_copy(x_ref.at[mine], o_ref.at[mine], local_sem)
        local.start()
        sends = []
        for j, (px, py) in enumerate([(1 - xx, yy), (xx, 1 - yy), (1 - xx, 1 - yy)]):
            cp = pltpu.make_async_remote_copy(
                src_ref=x_ref.at[2 * px + py], dst_ref=o_ref.at[mine], send_sem=send_sems.at[j], recv_sem=recv_sems.at[j],
                device_id=(px, py, cc), device_id_type=pl.DeviceIdType.MESH)
            cp.start()
            sends.append(cp)
        for j, (px, py) in enumerate([(1 - xx, yy), (xx, 1 - yy), (1 - xx, 1 - yy)]):
            pltpu.make_async_remote_copy(
                src_ref=x_ref.at[2 * px + py], dst_ref=o_ref.at[2 * px + py], send_sem=send_sems.at[j],
                recv_sem=recv_sems.at[j], device_id=(px, py, cc), device_id_type=pl.DeviceIdType.MESH).wait_recv()
        for cp in sends:
            cp.wait_send()
        local.wait()

    return pl.pallas_call(
        body, name=name, in_specs=[_HBM], out_specs=_HBM, out_shape=_sds(x.shape, x.dtype),
        scratch_shapes=[pltpu.SemaphoreType.DMA((N_CHIPS - 1,)), pltpu.SemaphoreType.DMA((N_CHIPS - 1,)),
                        pltpu.SemaphoreType.DMA(())],
    )(x)


def _reduce_scatter(x, *, name):
    chip_sums = _pair_add(x, _pair_exchange(x, name=name + "_pair"), name=name + "_pair_add")
    return _sum_slots(_chip_exchange(chip_sums, name=name + "_chips"), name=name + "_sum")


def _all_gather(x, *, name):
    def body(x_ref, o_ref, send_sems, recv_sems, local_sem):
        xx, yy, cc = (lax.axis_index(a) for a in AXES)
        me, sibling = (xx, yy, cc), (xx, yy, 1 - cc)
        chips = [(1 - xx, yy), (xx, 1 - yy), (1 - xx, 1 - yy)]

        def slot(px, py, pc):
            return o_ref.at[4 * px + 2 * py + pc]

        def copy(k, block, to, src=None):
            return pltpu.make_async_remote_copy(
                src_ref=slot(*block) if src is None else src, dst_ref=slot(*block), send_sem=send_sems.at[k],
                recv_sem=recv_sems.at[k], device_id=to, device_id_type=pl.DeviceIdType.MESH)

        local = pltpu.make_async_copy(x_ref, slot(*me), local_sem)
        local.start()
        first = [copy(0, me, sibling, src=x_ref)] + [copy(1 + j, me, (*chip, cc), src=x_ref) for j, chip in enumerate(chips)]
        for cp in first:
            cp.start()
        passed = [copy(4 + j, (*chip, cc), sibling) for j, chip in enumerate(chips)]
        for j, chip in enumerate(chips):
            copy(1 + j, (*chip, cc), me).wait_recv()
            passed[j].start()
        copy(0, sibling, me).wait_recv()
        for j, chip in enumerate(chips):
            copy(4 + j, (*chip, 1 - cc), me).wait_recv()
        for cp in first + passed:
            cp.wait_send()
        local.wait()

    return pl.pallas_call(
        body, name=name, in_specs=[_HBM], out_specs=_HBM, out_shape=_sds((N_DEV,) + tuple(x.shape), x.dtype),
        scratch_shapes=[pltpu.SemaphoreType.DMA((N_DEV - 1,)), pltpu.SemaphoreType.DMA((N_DEV - 1,)),
                        pltpu.SemaphoreType.DMA(())],
    )(x)


def _sum_slots(x, *, name):
    n, r, c = x.shape
    tr = _tile(r, max(PAD_ROWS, (1 << 17) // c), PAD_ROWS)

    def body(x_ref, o_ref):
        acc = x_ref[0].astype(F32)
        for d in range(1, n):
            acc = acc + x_ref[d].astype(F32)
        o_ref[...] = acc

    return pl.pallas_call(
        body, name=name, grid=(r // tr,), in_specs=[pl.BlockSpec((n, tr, c), lambda i: (0, i, 0))],
        out_specs=_row(tr, c), out_shape=_sds((r, c), F32), compiler_params=_params("parallel"),
    )(x)


def _s5_layouts(bbar_r, bbar_i, c_re, c_im):
    eye = jnp.eye(S5_GPB, dtype=F32)

    def b_blocks(bbar):
        bb = bbar.reshape(S5_NB, S5_GPB, S5_STATE, S5_GROUP)
        return jnp.einsum("jgph,gk->jghkp", bb, eye).reshape(S5_NB, S5_GPB * S5_GROUP, S5_GPB * S5_STATE)

    def c_blocks(cc):
        c4 = cc.reshape(S5_NB, S5_GPB, S5_GROUP, S5_STATE)
        return jnp.einsum("jghp,gk->jgpkh", c4, eye).reshape(S5_NB, S5_GPB * S5_STATE, S5_GPB * S5_GROUP)

    bre, bim, cre, cim = b_blocks(bbar_r), b_blocks(bbar_i), c_blocks(c_re), c_blocks(c_im)
    cast = lambda a: a.astype(MXU)
    sw = lambda a: jnp.swapaxes(a, 1, 2).astype(MXU)
    return dict(bre=cast(bre), bim=cast(bim), cre=cast(cre), cim=cast(cim), bre_t=sw(bre), bim_t=sw(bim), cre_t=sw(cre),
                cim_t=sw(cim))


def _b_diag(db):
    d5 = db.reshape(S5_NB, S5_GPB, S5_GROUP, S5_GPB, S5_STATE)
    diag = jnp.stack([d5[:, g, :, g, :] for g in range(S5_GPB)], axis=1)
    return jnp.swapaxes(diag, 2, 3).reshape(S5_GROUPS, S5_STATE * S5_GROUP)


def _c_diag(dc):
    d5 = dc.reshape(S5_NB, S5_GPB, S5_STATE, S5_GPB, S5_GROUP)
    diag = jnp.stack([d5[:, g, :, g, :] for g in range(S5_GPB)], axis=1)
    return jnp.swapaxes(diag, 2, 3).reshape(S5_GROUPS, S5_GROUP, S5_STATE)


def _head_rows(*vecs):
    par = jnp.zeros((8, HP), F32)
    for i, v in enumerate(vecs):
        par = par.at[i, :SSD_HEADS].set(v.astype(F32))
    return par


def _add(acc, r):
    return (acc + r,)


def _layer_fwd(x, mem, w, consts):
    s = {"x": x}
    rep = consts["rep"]
    s["h1"] = h1 = _rmsnorm_fwd(x, w["norm_mix"], name="norm_mix_fwd")
    s["proj"] = proj = _mm(h1, w["w_main"], name="in_proj")
    s["dtr"] = dtr = _mm(h1, w["w_dt"], name="dt_proj")
    ar, ai, ldt = w["s5_a_re"], w["s5_a_im"], w["s5_log_dt"].reshape(S5_GROUPS, 1)
    br, bi = w["s5_b_re"].reshape(S5_GROUPS, -1), w["s5_b_im"].reshape(S5_GROUPS, -1)
    abar_r, abar_i, bbar_r, bbar_i = _s5_prep(ar, ai, ldt, br, bi, rep)
    s["abar"] = abar = (abar_r.reshape(1, S5_CH), abar_i.reshape(1, S5_CH))
    s["lay"] = lay = _s5_layouts(bbar_r, bbar_i, w["s5_c_re"], w["s5_c_im"])
    s["u"] = u = _interleave_rows(proj[:, :D_S5])
    drv_r, drv_i = _bdmm([(u, lay["bre"]), (u, lay["bim"])], nb=S5_NB, ko=256, no=1024,
                         epi=lambda p: (p[0], p[1]), out_dtypes=(F32, F32), name="s5_drive")
    s["sr"], s["si"] = sr, si = _s5_scan(*abar, drv_r, drv_i, reverse=False, name="s5_scan_fwd")
    d_vec = w["s5_d"].reshape(1, D_S5)
    s["ypre"], s["yg"] = ypre, yg = _bdmm(
        [(sr, lay["cre"]), (si, lay["cim"])], nb=S5_NB, ko=1024, no=256, extras=[u], vecs=[d_vec],
        epi=lambda p, uv, dv: ((yp := p[0] - p[1] + dv * uv), _gelu(yp)), out_dtypes=(F32, MXU), name="s5_readout")
    s["gp"], out_s5 = _mm(yg, w["s5_w_glu"], extras=[ypre], epi=lambda acc, yp: (acc, _gelu(yp) * _sigmoid(acc)),
                          out_dtypes=(F32, MXU), name="s5_glu")
    out_s5 = _deinterleave_rows(out_s5)
    s["conv_pre"] = conv_pre = _conv_fwd(proj, w["ssd_conv_w"], w["ssd_conv_b"])
    s["par"] = par = _head_rows(w["ssd_dt_bias"], w["ssd_a_log"], w["ssd_d"])
    out_ssd, s["y_ssd"], s["states"] = _ssd_fwd(proj, conv_pre, dtr, par, w["ssd_norm"], consts["ssd"])
    s["ycat"] = ycat = jnp.concatenate([out_s5, out_ssd], axis=1)
    s["x1"] = x1 = _mm(ycat, w["w_out"], extras=[x], epi=_add, name="out_proj")
    s["hq"] = hq = _rmsnorm_fwd(x1, w["norm_xattn"], name="norm_xattn_fwd")
    s["mn"] = mn = _rmsnorm_fwd(mem, w["norm_mem"], name="norm_mem_fwd")
    s["q"] = q = _mm(hq, w["xa_wq"], out_dtypes=(MXU,), name="xa_q")
    s["k"] = k = _mm(mn, w["xa_wk"], out_dtypes=(MXU,), name="xa_k")
    s["v"] = v = _mm(mn, w["xa_wv"], out_dtypes=(MXU,), name="xa_v")
    s["o"] = o = _attn_fwd(q, k, v)
    s["x2"] = x2 = _mm(o, w["xa_wo"], extras=[x1], epi=_add, name="xa_o")
    s["hm"] = hm = _rmsnorm_fwd(x2, w["norm_mlp"], name="norm_mlp_fwd")
    s["act"] = _mm(hm, w["mlp_w1"], epi=lambda acc: (jnp.square(jnp.maximum(acc, 0.0)),), out_dtypes=(MXU,),
                   name="mlp_up")
    x3 = _mm(s["act"], w["mlp_w2"], extras=[x2], epi=_add, name="mlp_down")
    return x3, s


def _layer_bwd(dx3, mem, w, s, consts):
    g = {}
    rep = consts["rep"]
    d_a = _mm(dx3, w["mlp_w2"], tb=True, extras=[s["act"]],
              epi=lambda acc, act: (acc * (2.0 * jnp.sqrt(act.astype(F32))),), out_dtypes=(MXU,), name="mlp_down_dx")
    g["mlp_w2"] = _mm(s["act"], dx3, ta=True, name="mlp_down_dw")
    g["mlp_w1"] = _mm(s["hm"], d_a, ta=True, name="mlp_up_dw")
    d_hm = _mm(d_a, w["mlp_w1"], tb=True, name="mlp_up_dx")
    dx2, g["norm_mlp"] = _rmsnorm_bwd(s["x2"], w["norm_mlp"], d_hm, dx3, name="norm_mlp_bwd")
    d_o = _mm(dx2, w["xa_wo"], tb=True, out_dtypes=(MXU,), name="xa_o_dx")
    g["xa_wo"] = _mm(s["o"], dx2, ta=True, name="xa_o_dw")
    dq, dk, dv = _attn_bwd(s["q"], s["k"], s["v"], d_o)
    g["xa_wq"] = _mm(s["hq"], dq, ta=True, name="xa_q_dw")
    d_hq = _mm(dq, w["xa_wq"], tb=True, name="xa_q_dx")
    dx1, g["norm_xattn"] = _rmsnorm_bwd(s["x1"], w["norm_xattn"], d_hq, dx2, name="norm_xattn_bwd")
    g["xa_wk"] = _mm(s["mn"], dk, ta=True, name="xa_k_dw")
    g["xa_wv"] = _mm(s["mn"], dv, ta=True, name="xa_v_dw")
    d_mn_v = _mm(dv, w["xa_wv"], tb=True, name="xa_v_dx")
    d_mn = _mm(dk, w["xa_wk"], tb=True, extras=[d_mn_v], epi=_add, name="xa_k_dx")
    _, g["norm_mem"] = _rmsnorm_bwd(mem, w["norm_mem"], d_mn, None, name="norm_mem_bwd")
    d_ycat = _mm(dx1, w["w_out"], tb=True, name="out_proj_dx")
    g["w_out"] = _mm(s["ycat"], dx1, ta=True, name="out_proj_dw")
    lay, proj, ypre, u = s["lay"], s["proj"], s["ypre"], s["u"]
    d_os5 = _interleave_rows(d_ycat[:, :D_S5])
    d_gp = _s5_gate_bwd(d_os5, s["gp"], ypre)
    g["s5_w_glu"] = _mm(s["yg"], d_gp, ta=True, name="s5_glu_dw")
    d_ypre = _mm(d_gp, w["s5_w_glu"], tb=True, extras=[d_os5, s["gp"], ypre],
                 epi=lambda acc, do, gp, yp: ((acc + do * _sigmoid(gp)) * _gelu_grad(yp),), name="s5_glu_dx")
    ds_r, ds_i = _bdmm([(d_ypre, lay["cre_t"]), (d_ypre, lay["cim_t"])], nb=S5_NB, ko=256, no=1024,
                       epi=lambda p: (p[0], -p[1]), out_dtypes=(F32, F32), name="s5_readout_ds")
    lam_r, lam_i, d_abar_r, d_abar_i = _s5_scan(*s["abar"], ds_r, ds_i, reverse=True, name="s5_scan_bwd",
                                                states=(s["sr"], s["si"]))
    d_bbar_r = _b_diag(_bdmm_tn(u, 256, lam_r, 1024, nb=S5_NB, name="s5_drive_dw_re"))
    d_bbar_i = _b_diag(_bdmm_tn(u, 256, lam_i, 1024, nb=S5_NB, name="s5_drive_dw_im"))
    g["s5_c_re"] = _c_diag(_bdmm_tn(s["sr"], 1024, d_ypre, 256, nb=S5_NB, name="s5_readout_dw_re"))
    g["s5_c_im"] = -_c_diag(_bdmm_tn(s["si"], 1024, d_ypre, 256, nb=S5_NB, name="s5_readout_dw_im"))
    g["s5_d"] = _colsum_prod(d_ypre, u, D_S5, name="s5_d_dw").reshape(S5_GROUPS, S5_GROUP)
    du = _bdmm([(lam_r, lay["bre_t"]), (lam_i, lay["bim_t"])], nb=S5_NB, ko=1024, no=256, extras=[d_ypre],
               vecs=[w["s5_d"].reshape(1, D_S5)], epi=lambda p, dyp, dv: (p[0] + p[1] + dv * dyp,), name="s5_drive_du")
    du = _deinterleave_rows(du)
    ar, ai, ldt = w["s5_a_re"], w["s5_a_im"], w["s5_log_dt"].reshape(S5_GROUPS, 1)
    br, bi = w["s5_b_re"].reshape(S5_GROUPS, -1), w["s5_b_im"].reshape(S5_GROUPS, -1)
    d_ar, d_ai, d_ldt, d_br, d_bi = _s5_prep_bwd(
        ar, ai, ldt, br, bi, rep, d_abar_r.reshape(S5_GROUPS, S5_STATE), d_abar_i.reshape(S5_GROUPS, S5_STATE),
        d_bbar_r, d_bbar_i)
    g["s5_a_re"], g["s5_a_im"], g["s5_log_dt"] = d_ar, d_ai, d_ldt.reshape(S5_GROUPS)
    g["s5_b_re"] = d_br.reshape(S5_GROUPS, S5_STATE, S5_GROUP)
    g["s5_b_im"] = d_bi.reshape(S5_GROUPS, S5_STATE, S5_GROUP)
    d_cp, dz, d_dtr, g_ssd_norm, d_par = _ssd_bwd(proj, s["conv_pre"], s["dtr"], s["par"], w["ssd_norm"], s["y_ssd"],
                                                  s["states"], d_ycat, consts["ssd"])
    g["ssd_norm"] = g_ssd_norm
    g["ssd_dt_bias"], g["ssd_a_log"], g["ssd_d"] = (d_par[i, :SSD_HEADS] for i in range(3))
    d_xbc, g["ssd_conv_w"], g["ssd_conv_b"] = _conv_bwd(proj, w["ssd_conv_w"], d_cp)
    d_proj = jnp.concatenate([du, dz, d_xbc], axis=1)
    g_main = _mm(s["h1"], d_proj, ta=True, name="in_proj_dw")
    g_dt = _mm(s["h1"], d_dtr, ta=True, name="dt_proj_dw")
    g["w_in"] = jnp.concatenate([g_main, g_dt[:, :SSD_HEADS]], axis=1)
    d_h1_dt = _mm(d_dtr, w["w_dt"], tb=True, name="dt_proj_dx")
    d_h1 = _mm(d_proj, w["w_main"], tb=True, extras=[d_h1_dt], epi=_add, name="in_proj_dx")
    dx, g["norm_mix"] = _rmsnorm_bwd(s["x"], w["norm_mix"], d_h1, dx1, name="norm_mix_bwd")
    return dx, g


LAYER_WEIGHTS = ("norm_mix", "w_in", "s5_a_re", "s5_a_im", "s5_log_dt", "s5_b_re", "s5_b_im", "s5_c_re", "s5_c_im", "s5_d",
                 "s5_w_glu", "ssd_conv_w", "ssd_conv_b", "ssd_dt_bias", "ssd_a_log", "ssd_d", "ssd_norm", "w_out",
                 "norm_xattn", "norm_mem", "xa_wq", "xa_wk", "xa_wv", "xa_wo", "norm_mlp", "mlp_w1", "mlp_w2")
WEIGHTS = LAYER_WEIGHTS + ("norm_final",)


def _local_step(x, mem, target, weights):
    consts = {
        "ssd": _ssd_consts(),
        "rep": (jnp.arange(S5_STATE)[:, None] == jnp.arange(S5_STATE * S5_GROUP)[None, :] // S5_GROUP).astype(F32),
    }
    layers = []
    for l in range(DEPTH):
        w = {n: weights[n][l] for n in LAYER_WEIGHTS}
        w_in = w["w_in"]
        w["w_main"] = w_in[:, :D_MAIN]
        w["w_dt"] = jnp.pad(w_in[:, D_MAIN:], ((0, 0), (0, HP - SSD_HEADS)))
        layers.append(w)
    saved = []
    for l in range(DEPTH):
        x, s = _layer_fwd(x, mem, layers[l], consts)
        saved.append(s)
    loss, dx, g_final = _loss_head(x, weights["norm_final"], target)
    grads = [None] * DEPTH
    for l in reversed(range(DEPTH)):
        dx, grads[l] = _layer_bwd(dx, mem, layers[l], saved[l], consts)
    out = {n: jnp.stack([grads[l][n].reshape(weights[n].shape[1:]) for l in range(DEPTH)]) for n in LAYER_WEIGHTS}
    out["norm_final"] = g_final.reshape(weights["norm_final"].shape)
    return loss, dx, out


SHARDED = {"w_in": 2, "s5_w_glu": 1, "ssd_conv_w": 2, "w_out": 1, "xa_wq": 1, "xa_wk": 1, "xa_wv": 1, "xa_wo": 1,
           "mlp_w1": 2, "mlp_w2": 1}
EXACT = ("ssd_conv_w",)
OWN_EXCHANGE = ("w_in",)
FLAT_EXCHANGE = tuple(n for n in SHARDED if n not in OWN_EXCHANGE)
REPLICATED = tuple(n for n in WEIGHTS if n not in SHARDED)
LANES = 128
PAD_ROWS = 16


def _to_rows(flat, lead=()):
    n = flat.shape[-1]
    quantum = LANES * PAD_ROWS
    padded = -(-n // quantum) * quantum
    flat = jnp.pad(flat, [(0, 0)] * len(lead) + [(0, padded - n)])
    return flat.reshape(*lead, padded // LANES, LANES)


def _gather_weights(local):
    def assemble(n, seg):
        shp, ax = local[n].shape, SHARDED[n]
        return jnp.moveaxis(seg, 0, ax).reshape(*shp[:ax], N_DEV * shp[ax], *shp[ax + 1:])

    full = {n: assemble(n, _all_gather(local[n].astype(MXU), name="gather_" + n)) for n in OWN_EXCHANGE}
    parts = []
    for n in FLAT_EXCHANGE:
        if n in EXACT:
            parts.append(lax.bitcast_convert_type(local[n], MXU).reshape(-1))
        else:
            parts.append(local[n].astype(MXU).reshape(-1))
    gathered = _all_gather(_to_rows(jnp.concatenate(parts)), name="gather_weights")
    flat = gathered.reshape(N_DEV, -1)
    off = 0
    for n in FLAT_EXCHANGE:
        shp = local[n].shape
        size = math.prod(shp) * (2 if n in EXACT else 1)
        seg = flat[:, off:off + size]
        off += size
        if n in EXACT:
            seg = lax.bitcast_convert_type(seg.reshape(N_DEV, *shp, 2), F32)
        else:
            seg = seg.reshape(N_DEV, *shp)
        full[n] = assemble(n, seg)
    return full


def _scatter_grads(grads, local_shapes):
    def shards(n):
        shp, ax = local_shapes[n], SHARDED[n]
        gfull = grads[n].reshape(*shp[:ax], N_DEV, shp[ax], *shp[ax + 1:])
        return jnp.moveaxis(gfull, ax, 0).astype(MXU)

    out = {}
    for n in OWN_EXCHANGE:
        shp = local_shapes[n]
        out[n] = _reduce_scatter(shards(n).reshape(N_DEV, -1, shp[-1]), name="scatter_" + n).reshape(shp)
    parts = [shards(n).reshape(N_DEV, -1) for n in FLAT_EXCHANGE]
    payload = _to_rows(jnp.concatenate(parts, axis=1), lead=(N_DEV,))
    summed = _reduce_scatter(payload, name="scatter_grads").reshape(-1)
    off = 0
    for n in FLAT_EXCHANGE:
        size = math.prod(local_shapes[n])
        out[n] = summed[off:off + size].reshape(local_shapes[n])
        off += size
    return out


def _allreduce_small(loss, grads):
    parts = [loss.reshape(-1)[:1]] + [grads[n].reshape(-1) for n in REPLICATED]
    payload = _to_rows(jnp.concatenate(parts))
    summed = _sum_slots(_all_gather(payload, name="gather_small_grads"), name="sum_small_grads").reshape(-1)
    out, off = {}, 1
    for n in REPLICATED:
        size = grads[n].size
        out[n] = summed[off:off + size].reshape(grads[n].shape)
        off += size
    return summed[0], out


def kernel(x, mem, norm_mix, w_in, s5_a_re, s5_a_im, s5_log_dt, s5_b_re, s5_b_im, s5_c_re, s5_c_im, s5_d, s5_w_glu, ssd_conv_w, ssd_conv_b, ssd_dt_bias, ssd_a_log, ssd_d, ssd_norm, w_out, norm_xattn, norm_mem, xa_wq, xa_wk, xa_wv, xa_wo, norm_mlp, mlp_w1, mlp_w2, norm_final, loss_target, m_norm_mix, m_w_in, m_s5_a_re, m_s5_a_im, m_s5_log_dt, m_s5_b_re, m_s5_b_im, m_s5_c_re, m_s5_c_im, m_s5_d, m_s5_w_glu, m_ssd_conv_w, m_ssd_conv_b, m_ssd_dt_bias, m_ssd_a_log, m_ssd_d, m_ssd_norm, m_w_out, m_norm_xattn, m_norm_mem, m_xa_wq, m_xa_wk, m_xa_wv, m_xa_wo, m_norm_mlp, m_mlp_w1, m_mlp_w2, m_norm_final, v_norm_mix, v_w_in, v_s5_a_re, v_s5_a_im, v_s5_log_dt, v_s5_b_re, v_s5_b_im, v_s5_c_re, v_s5_c_im, v_s5_d, v_s5_w_glu, v_ssd_conv_w, v_ssd_conv_b, v_ssd_dt_bias, v_ssd_a_log, v_ssd_d, v_ssd_norm, v_w_out, v_norm_xattn, v_norm_mem, v_xa_wq, v_xa_wk, v_xa_wv, v_xa_wo, v_norm_mlp, v_mlp_w1, v_mlp_w2, v_norm_final):
    args = locals()
    local = {n: args[n] for n in WEIGHTS}
    full = dict(local)
    full.update(_gather_weights(local))
    loss, grad_x, grads = _local_step(x[0], mem[0], loss_target[0], full)
    loss, g_small = _allreduce_small(loss, grads)
    g_all = _scatter_grads(grads, {n: local[n].shape for n in SHARDED})
    g_all.update(g_small)
    delta, new_m, new_v = {}, {}, {}
    for n in WEIGHTS:
        delta[n], new_m[n], new_v[n] = _adamw(local[n], g_all[n], args["m_" + n], args["v_" + n], name="adamw_" + n)
    return (loss, grad_x[None], *[g_all[n] for n in WEIGHTS], *[delta[n] for n in WEIGHTS],
            *[new_m[n] for n in WEIGHTS], *[new_v[n] for n in WEIGHTS])
```

```python
import functools
import math

import jax
import jax.numpy as jnp
from jax import lax
from jax.experimental import pallas as pl
from jax.experimental.pallas import tpu as pltpu

F32 = jnp.float32
MXU = jnp.bfloat16
HI = lax.Precision.HIGHEST

D_MODEL = 1024
DEPTH = 4
MEM_LEN = 256
D_S5 = 1024
D_SSD = 1024
S5_GROUP = 16
S5_GROUPS = 64
S5_STATE = 64
S5_CH = S5_GROUPS * S5_STATE
S5_NB = 4
S5_GPB = S5_GROUPS // S5_NB
SSD_HEADDIM = 64
SSD_HEADS = 16
SSD_GROUPS = 4
SSD_STATE = 128
SSD_CONV = 4
SSD_CHUNK = 128
SSD_BC = SSD_GROUPS * SSD_STATE
D_CONV_CH = 2048
D_MAIN = 4096
D_IN_PROJ = D_MAIN + SSD_HEADS
HP = 128
XA_HEADS = 4
XA_HEAD_DIM = 256
D_FF = 4096
EPS = 1e-5
N_DEV = 8
AXES = ("x", "y", "c")

ADAM_LR = 0.001
ADAM_B1 = 0.9
ADAM_B2 = 0.999
ADAM_EPS = 1e-08
ADAM_WD = 0.01
ADAM_STEP = 10

VMEM_LIMIT = 56 * 1024 * 1024


def _params(*sem):
    return pltpu.CompilerParams(dimension_semantics=sem, vmem_limit_bytes=VMEM_LIMIT)


def _tile(n, pref, quantum=128):
    t = (min(pref, n) // quantum) * quantum
    while t >= quantum:
        if n % t == 0:
            return t
        t -= quantum
    return n


def _sds(shape, dtype):
    return jax.ShapeDtypeStruct(tuple(shape), dtype)


def _sigmoid(x):
    return 1.0 / (1.0 + jnp.exp(-x))


def _silu(x):
    return x * _sigmoid(x)


def _silu_grad(x):
    s = _sigmoid(x)
    return s * (1.0 + x * (1.0 - s))


_GELU_C = math.sqrt(2.0 / math.pi)


def _gelu(x):
    return 0.5 * x * (1.0 + jnp.tanh(_GELU_C * (x + 0.044715 * x * x * x)))


def _gelu_grad(x):
    th = jnp.tanh(_GELU_C * (x + 0.044715 * x * x * x))
    return 0.5 * (1.0 + th) + 0.5 * x * (1.0 - th * th) * _GELU_C * (1.0 + 3.0 * 0.044715 * x * x)


def _softplus(x):
    return jnp.maximum(x, 0.0) + jnp.log(1.0 + jnp.exp(-jnp.abs(x)))


_NN = (((1,), (0,)), ((), ()))
_NT = (((1,), (1,)), ((), ()))
_TN = (((0,), (0,)), ((), ()))


def _dot(a, b, dims=_NN, precision=None):
    return lax.dot_general(a, b, dims, precision=precision, preferred_element_type=F32)


def _rows8(v):
    return jnp.broadcast_to(v, (8, v.shape[1]))


def _mm(a, b, *, ta=False, tb=False, extras=(), epi=None, out_dtypes=(F32,), tm=1024, tn=1024, tk=1024, name):
    m, k = (a.shape[1], a.shape[0]) if ta else a.shape
    n = b.shape[0] if tb else b.shape[1]
    assert k == (b.shape[1] if tb else b.shape[0]), (a.shape, b.shape, ta, tb)
    tm, tn, tk = _tile(m, tm), _tile(n, tn), _tile(k, tk)
    nk = k // tk
    n_ex, n_out = len(extras), len(out_dtypes)
    dims = (((0,) if ta else (1,), (1,) if tb else (0,)), ((), ()))

    def body(a_ref, b_ref, *rest):
        ex_refs, out_refs = rest[:n_ex], rest[n_ex:n_ex + n_out]
        prod = _dot(a_ref[...].astype(MXU), b_ref[...].astype(MXU), dims)

        def finish(total):
            outs = epi(total, *[e[...] for e in ex_refs]) if epi is not None else (total,)
            for o, r in zip(outs, out_refs, strict=True):
                r[...] = o.astype(r.dtype)

        if nk == 1:
            finish(prod)
            return
        acc = rest[-1]
        kk = pl.program_id(2)

        @pl.when(kk == 0)
        def _():
            acc[...] = prod

        @pl.when(jnp.logical_and(kk > 0, kk < nk - 1))
        def _():
            acc[...] += prod

        @pl.when(kk == nk - 1)
        def _():
            finish(acc[...] + prod)

    a_spec = pl.BlockSpec((tk, tm), lambda i, j, kk: (kk, i)) if ta else pl.BlockSpec((tm, tk), lambda i, j, kk: (i, kk))
    b_spec = pl.BlockSpec((tn, tk), lambda i, j, kk: (j, kk)) if tb else pl.BlockSpec((tk, tn), lambda i, j, kk: (kk, j))
    mn_spec = pl.BlockSpec((tm, tn), lambda i, j, kk: (i, j))
    outs = pl.pallas_call(
        body,
        name=name,
        grid=(m // tm, n // tn, nk),
        in_specs=[a_spec, b_spec] + [mn_spec] * n_ex,
        out_specs=[mn_spec] * n_out,
        out_shape=[_sds((m, n), dt) for dt in out_dtypes],
        scratch_shapes=[pltpu.VMEM((tm, tn), F32)] if nk > 1 else [],
        compiler_params=_params("parallel", "parallel", "arbitrary"),
    )(a, b, *extras)
    return outs[0] if n_out == 1 else outs


def _bdmm(terms, *, nb, ko, no, extras=(), vecs=(), epi=None, out_dtypes=(F32,), tm=512, name):
    t = terms[0][0].shape[0]
    tm = _tile(t, tm)
    n_t, n_ex, n_v, n_out = len(terms), len(extras), len(vecs), len(out_dtypes)

    def body(*refs):
        a_refs, w_refs = refs[0:2 * n_t:2], refs[1:2 * n_t:2]
        ex_refs = refs[2 * n_t:2 * n_t + n_ex + n_v]
        out_refs = refs[2 * n_t + n_ex + n_v:]
        prods = [_dot(a[...].astype(MXU), w[...].astype(MXU)) for a, w in zip(a_refs, w_refs)]
        outs = epi(prods, *[e[...] for e in ex_refs]) if epi is not None else (prods[0],)
        for o, r in zip(outs, out_refs, strict=True):
            r[...] = o.astype(r.dtype)

    in_specs, args = [], []
    for a, w in terms:
        in_specs.append(pl.BlockSpec((tm, ko), lambda j, i: (i, j)))
        in_specs.append(pl.BlockSpec((None, ko, no), lambda j, i: (j, 0, 0)))
        args += [a, w]
    o_spec = pl.BlockSpec((tm, no), lambda j, i: (i, j))
    v_spec = pl.BlockSpec((1, no), lambda j, i: (0, j))
    outs = pl.pallas_call(
        body,
        name=name,
        grid=(nb, t // tm),
        in_specs=in_specs + [o_spec] * n_ex + [v_spec] * n_v,
        out_specs=[o_spec] * n_out,
        out_shape=[_sds((t, nb * no), dt) for dt in out_dtypes],
        compiler_params=_params("parallel", "parallel"),
    )(*args, *extras, *vecs)
    return outs[0] if n_out == 1 else outs


def _bdmm_tn(a, ka, b, nbc, *, nb, tt=512, name):
    t = a.shape[0]
    tt = _tile(t, tt)
    nt = t // tt

    def body(a_ref, b_ref, o_ref, acc):
        s = pl.program_id(1)

        @pl.when(s == 0)
        def _():
            acc[...] = jnp.zeros_like(acc)

        acc[...] += _dot(a_ref[...].astype(MXU), b_ref[...].astype(MXU), _TN)

        @pl.when(s == nt - 1)
        def _():
            o_ref[...] = acc[...]

    return pl.pallas_call(
        body,
        name=name,
        grid=(nb, nt),
        in_specs=[pl.BlockSpec((tt, ka), lambda j, s: (s, j)), pl.BlockSpec((tt, nbc), lambda j, s: (s, j))],
        out_specs=pl.BlockSpec((None, ka, nbc), lambda j, s: (j, 0, 0)),
        out_shape=_sds((nb, ka, nbc), F32),
        scratch_shapes=[pltpu.VMEM((ka, nbc), F32)],
        compiler_params=_params("parallel", "arbitrary"),
    )(a, b)


def _row(tb, w, cb=0):
    return pl.BlockSpec((tb, w), lambda i: (i, cb))


def _const(shape):
    return pl.BlockSpec(shape, lambda i: (0,) * len(shape))


def _rmsnorm_fwd(x, g, *, name):
    t, d = x.shape
    tb = _tile(t, 512, 8)

    def body(x_ref, g_ref, h_ref):
        xv = x_ref[...]
        r = lax.rsqrt(jnp.mean(xv * xv, axis=-1, keepdims=True) + EPS)
        h_ref[...] = (xv * r * g_ref[...]).astype(h_ref.dtype)

    return pl.pallas_call(
        body, name=name, grid=(t // tb,), in_specs=[_row(tb, d), _const((1, d))], out_specs=_row(tb, d),
        out_shape=_sds((t, d), MXU), compiler_params=_params("parallel"),
    )(x, g.reshape(1, d))


def _rmsnorm_bwd(x, g, dh, dres, *, name):
    t, d = x.shape
    tb = _tile(t, 256, 8)
    has_res = dres is not None

    def body(x_ref, g_ref, dh_ref, *rest):
        dx_ref, dg_ref = rest[-2:]

        @pl.when(pl.program_id(0) == 0)
        def _():
            dg_ref[...] = jnp.zeros_like(dg_ref)

        xv = x_ref[...]
        r = lax.rsqrt(jnp.mean(xv * xv, axis=-1, keepdims=True) + EPS)
        xh = xv * r
        dhv = dh_ref[...].astype(F32)
        dg_ref[...] += jnp.sum(dhv * xh, axis=0, keepdims=True)
        dxh = dhv * g_ref[...]
        dx = r * (dxh - xh * jnp.mean(dxh * xh, axis=-1, keepdims=True))
        if has_res:
            dx = dx + rest[0][...]
        dx_ref[...] = dx

    ins = [x, g.reshape(1, d), dh] + ([dres] if has_res else [])
    return pl.pallas_call(
        body, name=name, grid=(t // tb,),
        in_specs=[_row(tb, d), _const((1, d)), _row(tb, d)] + ([_row(tb, d)] if has_res else []),
        out_specs=[_row(tb, d), _const((1, d))],
        out_shape=[_sds((t, d), F32), _sds((1, d), F32)],
        compiler_params=_params("arbitrary"),
    )(*ins)


def _loss_head(x, g, target):
    t, d = x.shape
    tb = _tile(t, 256, 8)

    def body(x_ref, g_ref, tg_ref, loss_ref, dx_ref, dg_ref):
        @pl.when(pl.program_id(0) == 0)
        def _():
            dg_ref[...] = jnp.zeros_like(dg_ref)
            loss_ref[...] = jnp.zeros_like(loss_ref)

        xv, gv = x_ref[...], g_ref[...]
        r = lax.rsqrt(jnp.mean(xv * xv, axis=-1, keepdims=True) + EPS)
        xh = xv * r
        err = xh * gv - tg_ref[...]
        loss_ref[...] += 0.5 * jnp.sum(jnp.mean(err * err, axis=-1, keepdims=True), axis=0, keepdims=True)
        dy = err * (1.0 / d)
        dg_ref[...] += jnp.sum(dy * xh, axis=0, keepdims=True)
        dxh = dy * gv
        dx_ref[...] = r * (dxh - xh * jnp.mean(dxh * xh, axis=-1, keepdims=True))

    return pl.pallas_call(
        body, name="loss_head", grid=(t // tb,),
        in_specs=[_row(tb, d), _const((1, d)), _row(tb, d)],
        out_specs=[_const((1, HP)), _row(tb, d), _const((1, d))],
        out_shape=[_sds((1, HP), F32), _sds((t, d), F32), _sds((1, d), F32)],
        compiler_params=_params("arbitrary"),
    )(x, g.reshape(1, d), target)


def _s5_discretise(ar, ai, ldt, br, bi, rep):
    dt = jnp.exp(ldt)
    mag = jnp.exp(dt * ar)
    abar_r, abar_i = mag * jnp.cos(dt * ai), mag * jnp.sin(dt * ai)
    den = ar * ar + ai * ai
    zr, zi = abar_r - 1.0, abar_i
    fr = (zr * ar + zi * ai) / den
    fi = (zi * ar - zr * ai) / den
    fr_e, fi_e = _dot(fr, rep, precision=HI), _dot(fi, rep, precision=HI)
    return abar_r, abar_i, fr_e * br - fi_e * bi, fr_e * bi + fi_e * br


def _s5_prep(ar, ai, ldt, br, bi, rep):
    g, p = ar.shape
    ph = br.shape[1]

    def body(ar_ref, ai_ref, ldt_ref, br_ref, bi_ref, rep_ref, o0, o1, o2, o3):
        outs = _s5_discretise(ar_ref[...], ai_ref[...], ldt_ref[...], br_ref[...], bi_ref[...], rep_ref[...])
        for o, v in zip((o0, o1, o2, o3), outs):
            o[...] = v

    return pl.pallas_call(
        body, name="s5_prep",
        out_shape=[_sds((g, p), F32), _sds((g, p), F32), _sds((g, ph), F32), _sds((g, ph), F32)],
        compiler_params=pltpu.CompilerParams(vmem_limit_bytes=VMEM_LIMIT),
    )(ar, ai, ldt, br, bi, rep)


def _s5_prep_bwd(ar, ai, ldt, br, bi, rep, d_abar_r, d_abar_i, d_bbar_r, d_bbar_i):
    g, p = ar.shape
    ph = br.shape[1]

    def body(ar_ref, ai_ref, ldt_ref, br_ref, bi_ref, rep_ref, c0, c1, c2, c3, o0, o1, o2, o3, o4):
        rep_v = rep_ref[...]
        _, vjp = jax.vjp(lambda a, b, c, d, e: _s5_discretise(a, b, c, d, e, rep_v),
                         ar_ref[...], ai_ref[...], ldt_ref[...], br_ref[...], bi_ref[...])
        grads = vjp((c0[...], c1[...], c2[...], c3[...]))
        for o, v in zip((o0, o1, o2, o3, o4), grads):
            o[...] = v

    return pl.pallas_call(
        body, name="s5_prep_bwd",
        out_shape=[_sds((g, p), F32), _sds((g, p), F32), _sds((g, 1), F32), _sds((g, ph), F32), _sds((g, ph), F32)],
        compiler_params=pltpu.CompilerParams(vmem_limit_bytes=VMEM_LIMIT),
    )(ar, ai, ldt, br, bi, rep, d_abar_r, d_abar_i, d_bbar_r, d_bbar_i)


SCAN_ROWS = 512


def _scan_rows(t):
    return _tile(t, SCAN_ROWS, 64)


def _interleave_rows(x):
    t, c = x.shape
    tb = _scan_rows(t)
    return x.reshape(t // tb, 8, tb // 8, c).swapaxes(1, 2).reshape(t, c)


def _deinterleave_rows(x):
    t, c = x.shape
    tb = _scan_rows(t)
    return x.reshape(t // tb, tb // 8, 8, c).swapaxes(1, 2).reshape(t, c)


def _cmul(ar, ai, br, bi):
    return ar * br - ai * bi, ar * bi + ai * br


def _segment_carries(fr, fi, ar8, ai8, c_r, c_i, seg, reverse):
    pr, pi = ar8, ai8
    for _ in range(int(math.log2(seg))):
        pr, pi = _cmul(pr, pi, pr, pi)
    row = lax.broadcasted_iota(jnp.int32, fr.shape, 0)
    edge = 7 if reverse else 0
    qr, qi = _cmul(pr, pi, c_r, c_i)
    xr, xi = jnp.where(row == edge, fr + qr, fr), jnp.where(row == edge, fi + qi, fi)
    for sh in (1, 2, 4):
        if reverse:
            keep, amount = row < 8 - sh, 8 - sh
        else:
            keep, amount = row >= sh, sh
        qr, qi = jnp.where(keep, pltpu.roll(xr, amount, 0), 0.0), jnp.where(keep, pltpu.roll(xi, amount, 0), 0.0)
        tr, ti = _cmul(pr, pi, qr, qi)
        xr, xi = xr + tr, xi + ti
        pr, pi = _cmul(pr, pi, pr, pi)
    if reverse:
        in_r, in_i = jnp.where(row == 7, c_r, pltpu.roll(xr, 7, 0)), jnp.where(row == 7, c_i, pltpu.roll(xi, 7, 0))
        return in_r, in_i, xr[0:1, :], xi[0:1, :]
    in_r, in_i = jnp.where(row == 0, c_r, pltpu.roll(xr, 1, 0)), jnp.where(row == 0, c_i, pltpu.roll(xi, 1, 0))
    return in_r, in_i, xr[7:8, :], xi[7:8, :]


def _sweeps(ar8, ai8, dr, di, cr, ci, seg, reverse, emit):
    order = range(seg - 1, -1, -1) if reverse else range(seg)
    rows = lambda j: slice(j * 8, (j + 1) * 8)
    fr, fi = jnp.zeros(ar8.shape, F32), jnp.zeros(ar8.shape, F32)
    for j in order:
        tr, ti = _cmul(ar8, ai8, fr, fi)
        fr, fi = tr + dr[rows(j), :], ti + di[rows(j), :]
    s_r, s_i, out_r, out_i = _segment_carries(fr, fi, ar8, ai8, cr[...], ci[...], seg, reverse)
    cr[...] = out_r
    ci[...] = out_i
    for j in order:
        tr, ti = _cmul(ar8, ai8, s_r, s_i)
        s_r, s_i = tr + dr[rows(j), :], ti + di[rows(j), :]
        emit(j, s_r, s_i)


S5_BK, S5_BN = 256, 1024


def _s5_specs(tb, nt, reverse):
    t_of = (lambda s: nt - 1 - s) if reverse else (lambda s: s)
    return dict(
        small=pl.BlockSpec((tb, S5_BK), lambda c, s: (t_of(s), c)),
        wide=pl.BlockSpec((tb, S5_BN), lambda c, s: (t_of(s), c)),
        halo=pl.BlockSpec((8, S5_BN), lambda c, s: (jnp.maximum(t_of(s) * (tb // 8) - 1, 0), c)),
        vec_w=pl.BlockSpec((1, S5_BN), lambda c, s: (0, c)),
        vec_s=pl.BlockSpec((1, S5_BK), lambda c, s: (0, c)),
        w_in=pl.BlockSpec((None, S5_BK, S5_BN), lambda c, s: (c, 0, 0)),
        w_out=pl.BlockSpec((None, S5_BN, S5_BK), lambda c, s: (c, 0, 0)),
    )


def _s5_fwd(u, ar, ai, bre, bim, cre, cim, d_vec):
    t = u.shape[0]
    tb = _scan_rows(t)
    nt, seg, nb = t // tb, tb // 8, bre.shape[0]
    assert 1 << int(math.log2(seg)) == seg

    def body(u_ref, ar_ref, ai_ref, bre_ref, bim_ref, cre_ref, cim_ref, d_ref, sr_ref, si_ref, yp_ref, yg_ref,
             cr, ci, dr_s, di_s):
        @pl.when(pl.program_id(1) == 0)
        def _():
            cr[...] = jnp.zeros_like(cr)
            ci[...] = jnp.zeros_like(ci)

        uv = u_ref[...]
        ub = uv.astype(MXU)
        dr_s[...] = _dot(ub, bre_ref[...])
        di_s[...] = _dot(ub, bim_ref[...])
        ar8 = jnp.broadcast_to(ar_ref[...], (8, S5_BN))
        ai8 = jnp.broadcast_to(ai_ref[...], (8, S5_BN))

        def emit(j, s_r, s_i):
            sr_ref[j * 8:(j + 1) * 8, :] = s_r
            si_ref[j * 8:(j + 1) * 8, :] = s_i

        _sweeps(ar8, ai8, dr_s, di_s, cr, ci, seg, False, emit)
        yp = _dot(sr_ref[...].astype(MXU), cre_ref[...]) - _dot(si_ref[...].astype(MXU), cim_ref[...]) + d_ref[...] * uv
        yp_ref[...] = yp
        yg_ref[...] = _gelu(yp).astype(yg_ref.dtype)

    sp = _s5_specs(tb, nt, False)
    return pl.pallas_call(
        body, name="s5_fwd", grid=(nb, nt),
        in_specs=[sp["small"], sp["vec_w"], sp["vec_w"], sp["w_in"], sp["w_in"], sp["w_out"], sp["w_out"], sp["vec_s"]],
        out_specs=[sp["wide"], sp["wide"], sp["small"], sp["small"]],
        out_shape=[_sds((t, nb * S5_BN), F32), _sds((t, nb * S5_BN), F32), _sds((t, nb * S5_BK), F32),
                   _sds((t, nb * S5_BK), MXU)],
        scratch_shapes=[pltpu.VMEM((1, S5_BN), F32), pltpu.VMEM((1, S5_BN), F32), pltpu.VMEM((tb, S5_BN), F32),
                        pltpu.VMEM((tb, S5_BN), F32)],
        compiler_params=_params("parallel", "arbitrary"),
    )(u, ar, ai, bre, bim, cre, cim, d_vec)


def _s5_bwd(dyp, u, sr, si, ar, ai, cre_t, cim_t, bre_t, bim_t, d_vec):
    t = u.shape[0]
    tb = _scan_rows(t)
    nt, seg, nb = t // tb, tb // 8, cre_t.shape[0]

    def body(dyp_ref, u_ref, pr_ref, pi_ref, hr_ref, hi_ref, ar_ref, ai_ref, cre_ref, cim_ref, bre_ref, bim_ref, d_ref,
             du_ref, gr_ref, gi_ref, dbr_ref, dbi_ref, dcr_ref, dci_ref, dd_ref, cr, ci, dr_s, di_s, lr_s, li_s):
        step = pl.program_id(1)

        @pl.when(step == 0)
        def _():
            for r in (cr, ci, gr_ref, gi_ref, dbr_ref, dbi_ref, dcr_ref, dci_ref, dd_ref):
                r[...] = jnp.zeros_like(r)

        dyv, uv = dyp_ref[...], u_ref[...]
        dyb, ub = dyv.astype(MXU), uv.astype(MXU)
        dr_s[...] = _dot(dyb, cre_ref[...])
        di_s[...] = -_dot(dyb, cim_ref[...])
        ar8 = jnp.broadcast_to(ar_ref[...], (8, S5_BN))
        ai8 = jnp.broadcast_to(-ai_ref[...], (8, S5_BN))
        first_block = step == nt - 1
        row = lax.broadcasted_iota(jnp.int32, (8, S5_BN), 0)
        acc = [jnp.zeros((8, S5_BN), F32), jnp.zeros((8, S5_BN), F32)]

        def emit(j, s_r, s_i):
            lr_s[j * 8:(j + 1) * 8, :] = s_r
            li_s[j * 8:(j + 1) * 8, :] = s_i
            if j > 0:
                p_r, p_i = pr_ref[(j - 1) * 8:j * 8, :], pi_ref[(j - 1) * 8:j * 8, :]
            else:
                halo_r = jnp.where(first_block, 0.0, hr_ref[7:8, :])
                halo_i = jnp.where(first_block, 0.0, hi_ref[7:8, :])
                p_r = jnp.where(row == 0, halo_r, pltpu.roll(pr_ref[(seg - 1) * 8:seg * 8, :], 1, 0))
                p_i = jnp.where(row == 0, halo_i, pltpu.roll(pi_ref[(seg - 1) * 8:seg * 8, :], 1, 0))
            acc[0] = acc[0] + (p_r * s_r + p_i * s_i)
            acc[1] = acc[1] + (p_r * s_i - p_i * s_r)

        _sweeps(ar8, ai8, dr_s, di_s, cr, ci, seg, True, emit)
        gr_ref[...] += jnp.sum(acc[0], axis=0, keepdims=True)
        gi_ref[...] += jnp.sum(acc[1], axis=0, keepdims=True)
        lrb, lib = lr_s[...].astype(MXU), li_s[...].astype(MXU)
        du_ref[...] = _dot(lrb, bre_ref[...]) + _dot(lib, bim_ref[...]) + d_ref[...] * dyv
        dbr_ref[...] += _dot(ub, lrb, _TN)
        dbi_ref[...] += _dot(ub, lib, _TN)
        dcr_ref[...] += _dot(pr_ref[...].astype(MXU), dyb, _TN)
        dci_ref[...] += _dot(pi_ref[...].astype(MXU), dyb, _TN)
        dd_ref[...] += jnp.sum(dyv * uv, axis=0, keepdims=True)

    sp = _s5_specs(tb, nt, True)
    return pl.pallas_call(
        body, name="s5_bwd", grid=(nb, nt),
        in_specs=[sp["small"], sp["small"], sp["wide"], sp["wide"], sp["halo"], sp["halo"], sp["vec_w"], sp["vec_w"],
                  sp["w_in"], sp["w_in"], sp["w_out"], sp["w_out"], sp["vec_s"]],
        out_specs=[sp["small"], sp["vec_w"], sp["vec_w"], sp["w_in"], sp["w_in"], sp["w_out"], sp["w_out"], sp["vec_s"]],
        out_shape=[_sds((t, nb * S5_BK), F32), _sds((1, nb * S5_BN), F32), _sds((1, nb * S5_BN), F32),
                   _sds((nb, S5_BK, S5_BN), F32), _sds((nb, S5_BK, S5_BN), F32), _sds((nb, S5_BN, S5_BK), F32),
                   _sds((nb, S5_BN, S5_BK), F32), _sds((1, nb * S5_BK), F32)],
        scratch_shapes=[pltpu.VMEM((1, S5_BN), F32), pltpu.VMEM((1, S5_BN), F32)] + [pltpu.VMEM((tb, S5_BN), F32)] * 4,
        compiler_params=_params("parallel", "arbitrary"),
    )(dyp, u, sr, si, sr, si, ar, ai, cre_t, cim_t, bre_t, bim_t, d_vec)


def _s5_gate_bwd(d_ycat, gp, ypre):
    t, d = gp.shape
    tb = _tile(t, 256, 8)

    def body(do_ref, gp_ref, yp_ref, o_ref):
        sg = _sigmoid(gp_ref[...])
        o_ref[...] = (do_ref[...] * _gelu(yp_ref[...]) * sg * (1.0 - sg)).astype(o_ref.dtype)

    return pl.pallas_call(
        body, name="s5_gate_bwd", grid=(t // tb,), in_specs=[_row(tb, d), _row(tb, d), _row(tb, d)],
        out_specs=_row(tb, d), out_shape=_sds((t, d), MXU), compiler_params=_params("parallel"),
    )(d_ycat, gp, ypre)


_CONV_CW = 512
_CONV_OFF = (D_MAIN - D_CONV_CH) // _CONV_CW


def _conv_fwd(proj, w, b):
    t = proj.shape[0]
    tb = _tile(t, 256, 8)
    cw, off = _CONV_CW, _CONV_OFF

    def body(cur_ref, prev_ref, w_ref, b_ref, o_ref, ext):
        first = pl.program_id(1) == 0
        ext[0:8, :] = jnp.where(first, 0.0, prev_ref[...])
        ext[8:tb + 8, :] = cur_ref[...]
        acc = jnp.broadcast_to(b_ref[...], (tb, cw))
        for j in range(SSD_CONV):
            acc = acc + w_ref[SSD_CONV - 1 - j:SSD_CONV - j, :] * ext[8 - j:8 - j + tb, :]
        o_ref[...] = acc

    return pl.pallas_call(
        body, name="ssd_conv_fwd", grid=(D_CONV_CH // cw, t // tb),
        in_specs=[
            pl.BlockSpec((tb, cw), lambda j, i: (i, j + off)),
            pl.BlockSpec((8, cw), lambda j, i: (jnp.maximum(i * (tb // 8) - 1, 0), j + off)),
            pl.BlockSpec((SSD_CONV, cw), lambda j, i: (0, j)),
            pl.BlockSpec((1, cw), lambda j, i: (0, j)),
        ],
        out_specs=pl.BlockSpec((tb, cw), lambda j, i: (i, j)),
        out_shape=_sds((t, D_CONV_CH), F32),
        scratch_shapes=[pltpu.VMEM((tb + 8, cw), F32)],
        compiler_params=_params("parallel", "arbitrary"),
    )(proj, proj, w, b.reshape(1, D_CONV_CH))


def _conv_bwd(proj, w, d_conv):
    t = proj.shape[0]
    tb = _tile(t, 256, 8)
    cw, off = _CONV_CW, _CONV_OFF
    nt = t // tb

    def body(cur_ref, prev_ref, w_ref, dc_ref, dnext_ref, dx_ref, dw_ref, db_ref, ext, dext):
        i = pl.program_id(1)

        @pl.when(i == 0)
        def _():
            dw_ref[...] = jnp.zeros_like(dw_ref)
            db_ref[...] = jnp.zeros_like(db_ref)

        ext[0:8, :] = jnp.where(i == 0, 0.0, prev_ref[...])
        ext[8:tb + 8, :] = cur_ref[...]
        dcv = dc_ref[...]
        dext[0:tb, :] = dcv
        dext[tb:tb + 8, :] = jnp.where(i == nt - 1, 0.0, dnext_ref[...])
        dx = jnp.zeros((tb, cw), F32)
        for j in range(SSD_CONV):
            dx = dx + w_ref[SSD_CONV - 1 - j:SSD_CONV - j, :] * dext[j:j + tb, :]
            dw_ref[SSD_CONV - 1 - j:SSD_CONV - j, :] += jnp.sum(dcv * ext[8 - j:8 - j + tb, :], axis=0, keepdims=True)
        dx_ref[...] = dx
        db_ref[...] += jnp.sum(dcv, axis=0, keepdims=True)

    return pl.pallas_call(
        body, name="ssd_conv_bwd", grid=(D_CONV_CH // cw, nt),
        in_specs=[
            pl.BlockSpec((tb, cw), lambda j, i: (i, j + off)),
            pl.BlockSpec((8, cw), lambda j, i: (jnp.maximum(i * (tb // 8) - 1, 0), j + off)),
            pl.BlockSpec((SSD_CONV, cw), lambda j, i: (0, j)),
            pl.BlockSpec((tb, cw), lambda j, i: (i, j)),
            pl.BlockSpec((8, cw), lambda j, i: (jnp.minimum((i + 1) * (tb // 8), t // 8 - 1), j)),
        ],
        out_specs=[
            pl.BlockSpec((tb, cw), lambda j, i: (i, j)),
            pl.BlockSpec((SSD_CONV, cw), lambda j, i: (0, j)),
            pl.BlockSpec((1, cw), lambda j, i: (0, j)),
        ],
        out_shape=[_sds((t, D_CONV_CH), F32), _sds((SSD_CONV, D_CONV_CH), F32), _sds((1, D_CONV_CH), F32)],
        scratch_shapes=[pltpu.VMEM((tb + 8, cw), F32), pltpu.VMEM((tb + 8, cw), F32)],
        compiler_params=_params("parallel", "arbitrary"),
    )(proj, proj, w, d_conv, d_conv)


def _ssd_consts():
    head = jnp.arange(HP)[:, None]
    lane = jnp.arange(D_SSD)[None, :]
    expand = ((lane // SSD_HEADDIM) == head).astype(F32)
    ll = jnp.arange(SSD_CHUNK)
    tri = (ll[:, None] >= ll[None, :]).astype(F32)
    return expand, expand.T, tri, jnp.eye(HP, dtype=F32)


def _ssd_chunk_terms(cp, dtr, par, expand, tri):
    ln = SSD_CHUNK
    xbc = _silu(cp)
    xs, bm, cm = xbc[:, :D_SSD], xbc[:, D_SSD:D_SSD + SSD_BC], xbc[:, D_SSD + SSD_BC:]
    dt = _softplus(dtr + par[0:1, :])
    a = -jnp.exp(par[1:2, :])
    da = dt * a
    acum = _dot(tri, da, precision=HI)
    acum_t = _dot(da, tri, (((0,), (1,)), ((), ())), precision=HI)
    atot = acum[ln - 1:ln, :]
    dt_e = _dot(dt, expand, precision=HI)
    eac_e = _dot(jnp.exp(acum), expand, precision=HI)
    dec_e = _dot(jnp.exp(atot - acum), expand, precision=HI)
    eat_e = _dot(_rows8(jnp.exp(atot)), expand, precision=HI)[0:1, :]
    dsk_e = _dot(_rows8(par[2:3, :]), expand, precision=HI)[0:1, :]
    return dict(xs=xs, bm=bm, cm=cm, dt=dt, a=a, acum=acum, acum_t=acum_t, dt_e=dt_e, eac_e=eac_e,
                dec_e=dec_e, eat_e=eat_e, dsk_e=dsk_e)


def _decay_matrix(acum, acum_t, h, mask):
    diff = acum[:, h:h + 1] - acum_t[h:h + 1, :]
    return jnp.where(mask, jnp.exp(jnp.minimum(diff, 0.0)), 0.0)


def _ssd_fwd(proj, conv_pre, dtr, par, gnorm, consts):
    t = proj.shape[0]
    ln = SSD_CHUNK
    nc = t // ln
    expand, _, tri, _ = consts
    hd2 = 2 * SSD_HEADDIM

    def body(cp_ref, z_ref, dtr_ref, par_ref, g_ref, e_ref, tri_ref, out_ref, y_ref, st_ref, state):
        @pl.when(pl.program_id(0) == 0)
        def _():
            state[...] = jnp.zeros_like(state)

        st_ref[...] = state[...]
        c = _ssd_chunk_terms(cp_ref[...], dtr_ref[...], par_ref[...], e_ref[...], tri_ref[...])
        xdt = c["xs"] * c["dt_e"]
        xb, xd = xdt.astype(MXU), (xdt * c["dec_e"]).astype(MXU)
        bb, cb = c["bm"].astype(MXU), c["cm"].astype(MXU)
        mask = lax.broadcasted_iota(jnp.int32, (ln, ln), 0) >= lax.broadcasted_iota(jnp.int32, (ln, ln), 1)
        left = lax.broadcasted_iota(jnp.int32, (ln, hd2), 1) < SSD_HEADDIM
        for g in range(SSD_GROUPS):
            nsl = slice(g * SSD_STATE, (g + 1) * SSD_STATE)
            gsl = slice(g * 256, (g + 1) * 256)
            bg, cg = bb[:, nsl], cb[:, nsl]
            cbm = _dot(cg, bg, _NT)
            st_g = state[:, gsl]
            for pair in range(2):
                h0 = g * 4 + pair * 2
                psl = slice(h0 * SSD_HEADDIM, (h0 + 2) * SSD_HEADDIM)
                m0 = (cbm * _decay_matrix(c["acum"], c["acum_t"], h0, mask)).astype(MXU)
                m1 = (cbm * _decay_matrix(c["acum"], c["acum_t"], h0 + 1, mask)).astype(MXU)
                y_ref[:, psl] = jnp.where(left, _dot(m0, xb[:, psl]), _dot(m1, xb[:, psl]))
            y_ref[:, gsl] += _dot(cg, st_g.astype(MXU)) * c["eac_e"][:, gsl]
            state[:, gsl] = st_g * c["eat_e"][:, gsl] + _dot(bg, xd[:, gsl], _TN)
        y = y_ref[...] + c["dsk_e"] * c["xs"]
        y_ref[...] = y
        y2 = y * _silu(z_ref[...])
        r = lax.rsqrt(jnp.mean(y2 * y2, axis=-1, keepdims=True) + EPS)
        out_ref[...] = (y2 * r * g_ref[...]).astype(out_ref.dtype)

    return pl.pallas_call(
        body, name="ssd_fwd", grid=(nc,),
        in_specs=[_row(ln, D_CONV_CH), _row(ln, D_SSD, 1), _row(ln, HP), _const((8, HP)), _const((1, D_SSD)),
                  _const((HP, D_SSD)), _const((ln, ln))],
        out_specs=[_row(ln, D_SSD), _row(ln, D_SSD), pl.BlockSpec((None, SSD_STATE, D_SSD), lambda i: (i, 0, 0))],
        out_shape=[_sds((t, D_SSD), MXU), _sds((t, D_SSD), F32), _sds((nc, SSD_STATE, D_SSD), F32)],
        scratch_shapes=[pltpu.VMEM((SSD_STATE, D_SSD), F32)],
        compiler_params=_params("arbitrary"),
    )(conv_pre, proj, dtr, par, gnorm.reshape(1, D_SSD), expand, tri)


def _ssd_bwd(proj, conv_pre, dtr, par, gnorm, y, states, d_ycat, consts):
    t = proj.shape[0]
    ln = SSD_CHUNK
    nc = t // ln
    expand, expand_t, tri, eye = consts
    hd2 = 2 * SSD_HEADDIM

    def body(cp_ref, z_ref, dtr_ref, par_ref, g_ref, y_ref, st_ref, do_ref, e_ref, et_ref, tri_ref, eye_ref,
             dcp_ref, dz_ref, ddt_ref, dg_ref, dpar_ref, dstate, dx_buf, lane_buf, tot_buf, colsum):
        @pl.when(pl.program_id(0) == 0)
        def _():
            dstate[...] = jnp.zeros_like(dstate)
            dg_ref[...] = jnp.zeros_like(dg_ref)
            dpar_ref[...] = jnp.zeros_like(dpar_ref)

        cpv, et_v, tri_v, par_v = cp_ref[...], et_ref[...], tri_ref[...], par_ref[...]
        c = _ssd_chunk_terms(cpv, dtr_ref[...], par_v, e_ref[...], tri_v)
        xs = c["xs"]
        zv, yv, dov = z_ref[...], y_ref[...], do_ref[...]
        sz = _silu(zv)
        y2 = yv * sz
        r = lax.rsqrt(jnp.mean(y2 * y2, axis=-1, keepdims=True) + EPS)
        yh = y2 * r
        dg_ref[...] += jnp.sum(dov * yh, axis=0, keepdims=True)
        dyh = dov * g_ref[...]
        dy2 = r * (dyh - yh * jnp.mean(dyh * yh, axis=-1, keepdims=True))
        dz_ref[...] = dy2 * yv * _silu_grad(zv)
        dy = dy2 * sz

        xdt = xs * c["dt_e"]
        xdf = xdt * c["dec_e"]
        xb, xd = xdt.astype(MXU), xdf.astype(MXU)
        bb, cb = c["bm"].astype(MXU), c["cm"].astype(MXU)
        dyb, dye = dy.astype(MXU), (dy * c["eac_e"]).astype(MXU)
        mask = lax.broadcasted_iota(jnp.int32, (ln, ln), 0) >= lax.broadcasted_iota(jnp.int32, (ln, ln), 1)
        left = lax.broadcasted_iota(jnp.int32, (ln, hd2), 1) < SSD_HEADDIM
        lane_hp = lax.broadcasted_iota(jnp.int32, (ln, HP), 1)
        d_acum = jnp.zeros((ln, HP), F32)
        colsum[...] = jnp.zeros_like(colsum)
        tot_buf[...] = jnp.zeros_like(tot_buf)
        for g in range(SSD_GROUPS):
            nsl = slice(g * SSD_STATE, (g + 1) * SSD_STATE)
            gsl = slice(g * 256, (g + 1) * 256)
            bg, cg = bb[:, nsl], cb[:, nsl]
            cbm = _dot(cg, bg, _NT)
            st_g = st_ref[:, gsl]
            dst_g = dstate[:, gsl]
            stb, dstb = st_g.astype(MXU), dst_g.astype(MXU)
            y_off = _dot(cg, stb)
            bds = _dot(bg, dstb)
            dcb = jnp.zeros((ln, ln), F32)
            for pair in range(2):
                h0 = g * 4 + pair * 2
                psl = slice(h0 * SSD_HEADDIM, (h0 + 2) * SSD_HEADDIM)
                xp, dyp = xb[:, psl], dyb[:, psl]
                dxp = []
                for k in range(2):
                    h = h0 + k
                    lm = _decay_matrix(c["acum"], c["acum_t"], h, mask)
                    mm = cbm * lm
                    half = left if k == 0 else jnp.logical_not(left)
                    dm = _dot(jnp.where(half, dyp, jnp.zeros_like(dyp)), xp, _NT)
                    dcb = dcb + dm * lm
                    gm = dm * mm
                    d_acum = d_acum + jnp.where(lane_hp == h, jnp.sum(gm, axis=1, keepdims=True), 0.0)
                    colsum[h:h + 1, :] = jnp.sum(gm, axis=0, keepdims=True)
                    dxp.append(_dot(mm.astype(MXU), dyp, _TN))
                dx_buf[:, psl] = jnp.where(left, dxp[0], dxp[1])
            dx_buf[:, gsl] += bds * c["dec_e"][:, gsl]
            dcbb = dcb.astype(MXU)
            dc_g = _dot(dcbb, bg) + _dot(dye[:, gsl], stb, _NT)
            db_g = _dot(dcbb, cg, _TN) + _dot(xd[:, gsl], dstb, _NT)
            dcp_ref[:, D_SSD + g * SSD_STATE:D_SSD + (g + 1) * SSD_STATE] = db_g
            dcp_ref[:, D_SSD + SSD_BC + g * SSD_STATE:D_SSD + SSD_BC + (g + 1) * SSD_STATE] = dc_g
            dec_term = xdf[:, gsl] * bds
            lane_buf[:, gsl] = dy[:, gsl] * y_off * c["eac_e"][:, gsl] - dec_term
            tot_buf[0:1, gsl] = (jnp.sum(st_g * dst_g, axis=0, keepdims=True) * c["eat_e"][:, gsl]
                                 + jnp.sum(dec_term, axis=0, keepdims=True))
            dstate[:, gsl] = dst_g * c["eat_e"][:, gsl] + _dot(cg, dye[:, gsl], _TN)
        dx_tot = dx_buf[...]
        d_acum = d_acum + _dot(lane_buf[...], et_v, precision=HI) - _dot(colsum[...], eye_ref[...], _TN, precision=HI)
        d_atot = _dot(tot_buf[...], et_v, precision=HI)[0:1, :]
        row_hp = lax.broadcasted_iota(jnp.int32, (ln, HP), 0)
        d_acum = d_acum + jnp.where(row_hp == ln - 1, d_atot, 0.0)
        d_da = _dot(tri_v, d_acum, _TN, precision=HI)
        d_dt = d_da * c["a"] + _dot(dx_tot * xs, et_v, precision=HI)
        d_dtr = d_dt * _sigmoid(dtr_ref[...] + par_v[0:1, :])
        ddt_ref[...] = d_dtr
        dpar_ref[0:1, :] += jnp.sum(d_dtr, axis=0, keepdims=True)
        dpar_ref[1:2, :] += jnp.sum(d_da * c["dt"], axis=0, keepdims=True) * c["a"]
        dpar_ref[2:3, :] += _dot(_rows8(jnp.sum(dy * xs, axis=0, keepdims=True)), et_v, precision=HI)[0:1, :]
        dcp_ref[:, 0:D_SSD] = dx_tot * c["dt_e"] + dy * c["dsk_e"]
        dcp_ref[...] = dcp_ref[...] * _silu_grad(cpv)

    rev = lambda i: (nc - 1 - i, 0)
    rev1 = lambda i: (nc - 1 - i, 1)
    return pl.pallas_call(
        body, name="ssd_bwd", grid=(nc,),
        in_specs=[pl.BlockSpec((ln, D_CONV_CH), rev), pl.BlockSpec((ln, D_SSD), rev1), pl.BlockSpec((ln, HP), rev),
                  _const((8, HP)), _const((1, D_SSD)), pl.BlockSpec((ln, D_SSD), rev),
                  pl.BlockSpec((None, SSD_STATE, D_SSD), lambda i: (nc - 1 - i, 0, 0)),
                  pl.BlockSpec((ln, D_SSD), rev1),
                  _const((HP, D_SSD)), _const((D_SSD, HP)), _const((ln, ln)), _const((HP, HP))],
        out_specs=[pl.BlockSpec((ln, D_CONV_CH), rev), pl.BlockSpec((ln, D_SSD), rev), pl.BlockSpec((ln, HP), rev),
                   _const((1, D_SSD)), _const((8, HP))],
        out_shape=[_sds((t, D_CONV_CH), F32), _sds((t, D_SSD), F32), _sds((t, HP), F32), _sds((1, D_SSD), F32),
                   _sds((8, HP), F32)],
        scratch_shapes=[pltpu.VMEM((SSD_STATE, D_SSD), F32), pltpu.VMEM((ln, D_SSD), F32), pltpu.VMEM((ln, D_SSD), F32),
                        pltpu.VMEM((8, D_SSD), F32), pltpu.VMEM((HP, ln), F32)],
        compiler_params=_params("arbitrary"),
    )(conv_pre, proj, dtr, par, gnorm.reshape(1, D_SSD), y, states, d_ycat, expand, expand_t, tri, eye)


def _softmax_rows(s):
    e = jnp.exp(s - jnp.max(s, axis=-1, keepdims=True))
    return e / jnp.sum(e, axis=-1, keepdims=True)


def _attn_fwd(q, k, v):
    t, d = q.shape
    mlen = k.shape[0]
    tq = _tile(t, 512, 8)
    scale = XA_HEAD_DIM ** -0.5

    def body(q_ref, k_ref, v_ref, o_ref):
        for h in range(XA_HEADS):
            sl = slice(h * XA_HEAD_DIM, (h + 1) * XA_HEAD_DIM)
            p = _softmax_rows(_dot(q_ref[:, sl], k_ref[:, sl], _NT) * scale)
            o_ref[:, sl] = _dot(p.astype(MXU), v_ref[:, sl]).astype(o_ref.dtype)

    return pl.pallas_call(
        body, name="xattn_fwd", grid=(t // tq,),
        in_specs=[_row(tq, d), _const((mlen, d)), _const((mlen, d))], out_specs=_row(tq, d),
        out_shape=_sds((t, d), MXU), compiler_params=_params("parallel"),
    )(q, k, v)


def _attn_bwd(q, k, v, do):
    t, d = q.shape
    mlen = k.shape[0]
    tq = _tile(t, 512, 8)
    scale = XA_HEAD_DIM ** -0.5

    def body(q_ref, k_ref, v_ref, do_ref, dq_ref, dk_ref, dv_ref):
        @pl.when(pl.program_id(0) == 0)
        def _():
            dk_ref[...] = jnp.zeros_like(dk_ref)
            dv_ref[...] = jnp.zeros_like(dv_ref)

        for h in range(XA_HEADS):
            sl = slice(h * XA_HEAD_DIM, (h + 1) * XA_HEAD_DIM)
            qh, kh, vh, doh = q_ref[:, sl], k_ref[:, sl], v_ref[:, sl], do_ref[:, sl]
            p = _softmax_rows(_dot(qh, kh, _NT) * scale)
            dp = _dot(doh, vh, _NT)
            dv_ref[:, sl] += _dot(p.astype(MXU), doh, _TN)
            ds = (p * (dp - jnp.sum(p * dp, axis=-1, keepdims=True)) * scale).astype(MXU)
            dq_ref[:, sl] = _dot(ds, kh).astype(dq_ref.dtype)
            dk_ref[:, sl] += _dot(ds, qh, _TN)

    return pl.pallas_call(
        body, name="xattn_bwd", grid=(t // tq,),
        in_specs=[_row(tq, d), _const((mlen, d)), _const((mlen, d)), _row(tq, d)],
        out_specs=[_row(tq, d), _const((mlen, d)), _const((mlen, d))],
        out_shape=[_sds((t, d), MXU), _sds((mlen, d), F32), _sds((mlen, d), F32)],
        compiler_params=_params("arbitrary"),
    )(q, k, v, do)


def _lane_view(a):
    if a.ndim >= 2 and a.shape[-1] >= 128:
        return a.reshape(-1, a.shape[-1])
    if a.size % 128 == 0:
        return a.reshape(-1, 128)
    return a.reshape(1, -1)


def _adamw(w, g, m, v, *, name):
    shape = w.shape
    w2, g2, m2, v2 = (_lane_view(a) for a in (w, g.reshape(shape), m, v))
    r, c = w2.shape
    tr = _tile(r, max(8, (1 << 18) // max(c, 128)), 8)

    def body(w_ref, g_ref, m_ref, v_ref, d_ref, mo_ref, vo_ref):
        gv = g_ref[...]
        mn = ADAM_B1 * m_ref[...] + (1.0 - ADAM_B1) * gv
        vn = ADAM_B2 * v_ref[...] + (1.0 - ADAM_B2) * (gv * gv)
        m_hat = mn / (1.0 - ADAM_B1 ** ADAM_STEP)
        v_hat = vn / (1.0 - ADAM_B2 ** ADAM_STEP)
        d_ref[...] = -ADAM_LR * (m_hat / (jnp.sqrt(v_hat) + ADAM_EPS) + ADAM_WD * w_ref[...])
        mo_ref[...] = mn
        vo_ref[...] = vn

    outs = pl.pallas_call(
        body, name=name, grid=(r // tr,), in_specs=[_row(tr, c)] * 4, out_specs=[_row(tr, c)] * 3,
        out_shape=[_sds((r, c), F32)] * 3, compiler_params=_params("parallel"),
    )(w2, g2, m2, v2)
    return tuple(o.reshape(shape) for o in outs)


_HBM = pl.BlockSpec(memory_space=pltpu.HBM)


N_CHIPS = 4
CHIPS = ((0, 0), (0, 1), (1, 0), (1, 1))


def _pair_exchange(x, *, name):
    def body(x_ref, o_ref, send_sems, recv_sems):
        xx, yy, cc = (lax.axis_index(a) for a in AXES)
        copies = [
            pltpu.make_async_remote_copy(
                src_ref=x_ref.at[4 * px + 2 * py + (1 - cc)], dst_ref=o_ref.at[k], send_sem=send_sems.at[k],
                recv_sem=recv_sems.at[k], device_id=(xx, yy, 1 - cc), device_id_type=pl.DeviceIdType.MESH)
            for k, (px, py) in enumerate(CHIPS)]
        for cp in copies:
            cp.start()
        for cp in copies:
            cp.wait_recv()
        for cp in copies:
            cp.wait_send()

    return pl.pallas_call(
        body, name=name, in_specs=[_HBM], out_specs=_HBM, out_shape=_sds((N_CHIPS,) + tuple(x.shape[1:]), x.dtype),
        scratch_shapes=[pltpu.SemaphoreType.DMA((N_CHIPS,)), pltpu.SemaphoreType.DMA((N_CHIPS,))],
    )(x)


def _pair_add(x, got, *, name):
    _, r, c = x.shape
    tr = _tile(r, max(PAD_ROWS, (1 << 17) // c), PAD_ROWS)

    def body(x_ref, g_ref, o_ref):
        mine = jnp.where(lax.axis_index("c") == 0, x_ref[0].astype(F32), x_ref[1].astype(F32))
        o_ref[...] = (mine + g_ref[...].astype(F32)).astype(o_ref.dtype)

    return pl.pallas_call(
        body, name=name, grid=(N_CHIPS, r // tr),
        in_specs=[pl.BlockSpec((None, 2, tr, c), lambda k, i: (k, 0, i, 0)), pl.BlockSpec((None, tr, c), lambda k, i: (k, i, 0))],
        out_specs=pl.BlockSpec((None, tr, c), lambda k, i: (k, i, 0)), out_shape=_sds((N_CHIPS, r, c), x.dtype),
        compiler_params=_params("parallel", "parallel"),
    )(x.reshape(N_CHIPS, 2, r, c), got)


def _chip_exchange(x, *, name):
    def body(x_ref, o_ref, send_sems, recv_sems, local_sem):
        xx, yy, cc = (lax.axis_index(a) for a in AXES)
        mine = 2 * xx + yy
        local = pltpu.make_async_copy(x_ref.at[mine], o_ref.at[mine], local_sem)
        local.start()
        sends = []
        for j, (px, py) in enumerate([(1 - xx, yy), (xx, 1 - yy), (1 - xx, 1 - yy)]):
            cp = pltpu.make_async_remote_copy(
                src_ref=x_ref.at[2 * px + py], dst_ref=o_ref.at[mine], send_sem=send_sems.at[j], recv_sem=recv_sems.at[j],
                device_id=(px, py, cc), device_id_type=pl.DeviceIdType.MESH)
            cp.start()
            sends.append(cp)
        for j, (px, py) in enumerate([(1 - xx, yy), (xx, 1 - yy), (1 - xx, 1 - yy)]):
            pltpu.make_async_remote_copy(
                src_ref=x_ref.at[2 * px + py], dst_ref=o_ref.at[2 * px + py], send_sem=send_sems.at[j],
                recv_sem=recv_sems.at[j], device_id=(px, py, cc), device_id_type=pl.DeviceIdType.MESH).wait_recv()
        for cp in sends:
            cp.wait_send()
        local.wait()

    return pl.pallas_call(
        body, name=name, in_specs=[_HBM], out_specs=_HBM, out_shape=_sds(x.shape, x.dtype),
        scratch_shapes=[pltpu.SemaphoreType.DMA((N_CHIPS - 1,)), pltpu.SemaphoreType.DMA((N_CHIPS - 1,)),
                        pltpu.SemaphoreType.DMA(())],
    )(x)


def _reduce_scatter(x, *, name):
    chip_sums = _pair_add(x, _pair_exchange(x, name=name + "_pair"), name=name + "_pair_add")
    return _sum_slots(_chip_exchange(chip_sums, name=name + "_chips"), name=name + "_sum")


def _all_gather(x, *, name):
    def body(x_ref, o_ref, send_sems, recv_sems, local_sem):
        xx, yy, cc = (lax.axis_index(a) for a in AXES)
        me, sibling = (xx, yy, cc), (xx, yy, 1 - cc)
        chips = [(1 - xx, yy), (xx, 1 - yy), (1 - xx, 1 - yy)]

        def slot(px, py, pc):
            return o_ref.at[4 * px + 2 * py + pc]

        def copy(k, block, to, src=None):
            return pltpu.make_async_remote_copy(
                src_ref=slot(*block) if src is None else src, dst_ref=slot(*block), send_sem=send_sems.at[k],
                recv_sem=recv_sems.at[k], device_id=to, device_id_type=pl.DeviceIdType.MESH)

        local = pltpu.make_async_copy(x_ref, slot(*me), local_sem)
        local.start()
        first = [copy(0, me, sibling, src=x_ref)] + [copy(1 + j, me, (*chip, cc), src=x_ref) for j, chip in enumerate(chips)]
        for cp in first:
            cp.start()
        passed = [copy(4 + j, (*chip, cc), sibling) for j, chip in enumerate(chips)]
        for j, chip in enumerate(chips):
            copy(1 + j, (*chip, cc), me).wait_recv()
            passed[j].start()
        copy(0, sibling, me).wait_recv()
        for j, chip in enumerate(chips):
            copy(4 + j, (*chip, 1 - cc), me).wait_recv()
        for cp in first + passed:
            cp.wait_send()
        local.wait()

    return pl.pallas_call(
        body, name=name, in_specs=[_HBM], out_specs=_HBM, out_shape=_sds((N_DEV,) + tuple(x.shape), x.dtype),
        scratch_shapes=[pltpu.SemaphoreType.DMA((N_DEV - 1,)), pltpu.SemaphoreType.DMA((N_DEV - 1,)),
                        pltpu.SemaphoreType.DMA(())],
    )(x)


def _sum_slots(x, *, name):
    n, r, c = x.shape
    tr = _tile(r, max(PAD_ROWS, (1 << 17) // c), PAD_ROWS)

    def body(x_ref, o_ref):
        acc = x_ref[0].astype(F32)
        for d in range(1, n):
            acc = acc + x_ref[d].astype(F32)
        o_ref[...] = acc

    return pl.pallas_call(
        body, name=name, grid=(r // tr,), in_specs=[pl.BlockSpec((n, tr, c), lambda i: (0, i, 0))],
        out_specs=_row(tr, c), out_shape=_sds((r, c), F32), compiler_params=_params("parallel"),
    )(x)


def _s5_layouts(bbar_r, bbar_i, c_re, c_im):
    eye = jnp.eye(S5_GPB, dtype=F32)

    def b_blocks(bbar):
        bb = bbar.reshape(S5_NB, S5_GPB, S5_STATE, S5_GROUP)
        return jnp.einsum("jgph,gk->jghkp", bb, eye).reshape(S5_NB, S5_GPB * S5_GROUP, S5_GPB * S5_STATE)

    def c_blocks(cc):
        c4 = cc.reshape(S5_NB, S5_GPB, S5_GROUP, S5_STATE)
        return jnp.einsum("jghp,gk->jgpkh", c4, eye).reshape(S5_NB, S5_GPB * S5_STATE, S5_GPB * S5_GROUP)

    bre, bim, cre, cim = b_blocks(bbar_r), b_blocks(bbar_i), c_blocks(c_re), c_blocks(c_im)
    cast = lambda a: a.astype(MXU)
    sw = lambda a: jnp.swapaxes(a, 1, 2).astype(MXU)
    return dict(bre=cast(bre), bim=cast(bim), cre=cast(cre), cim=cast(cim), bre_t=sw(bre), bim_t=sw(bim), cre_t=sw(cre),
                cim_t=sw(cim))


def _b_diag(db):
    d5 = db.reshape(S5_NB, S5_GPB, S5_GROUP, S5_GPB, S5_STATE)
    diag = jnp.stack([d5[:, g, :, g, :] for g in range(S5_GPB)], axis=1)
    return jnp.swapaxes(diag, 2, 3).reshape(S5_GROUPS, S5_STATE * S5_GROUP)


def _c_diag(dc):
    d5 = dc.reshape(S5_NB, S5_GPB, S5_STATE, S5_GPB, S5_GROUP)
    diag = jnp.stack([d5[:, g, :, g, :] for g in range(S5_GPB)], axis=1)
    return jnp.swapaxes(diag, 2, 3).reshape(S5_GROUPS, S5_GROUP, S5_STATE)


def _head_rows(*vecs):
    par = jnp.zeros((8, HP), F32)
    for i, v in enumerate(vecs):
        par = par.at[i, :SSD_HEADS].set(v.astype(F32))
    return par


def _add(acc, r):
    return (acc + r,)


def _layer_fwd(x, mem, w, consts):
    s = {"x": x}
    rep = consts["rep"]
    s["h1"] = h1 = _rmsnorm_fwd(x, w["norm_mix"], name="norm_mix_fwd")
    s["proj"] = proj = _mm(h1, w["w_main"], name="in_proj")
    s["dtr"] = dtr = _mm(h1, w["w_dt"], name="dt_proj")
    ar, ai, ldt = w["s5_a_re"], w["s5_a_im"], w["s5_log_dt"].reshape(S5_GROUPS, 1)
    br, bi = w["s5_b_re"].reshape(S5_GROUPS, -1), w["s5_b_im"].reshape(S5_GROUPS, -1)
    abar_r, abar_i, bbar_r, bbar_i = _s5_prep(ar, ai, ldt, br, bi, rep)
    s["abar"] = abar = (abar_r.reshape(1, S5_CH), abar_i.reshape(1, S5_CH))
    s["lay"] = lay = _s5_layouts(bbar_r, bbar_i, w["s5_c_re"], w["s5_c_im"])
    s["u"] = u = _interleave_rows(proj[:, :D_S5])
    s["sr"], s["si"], s["ypre"], s["yg"] = _s5_fwd(u, *abar, lay["bre"], lay["bim"], lay["cre"], lay["cim"],
                                                   w["s5_d"].reshape(1, D_S5))
    ypre, yg = s["ypre"], s["yg"]
    s["gp"], out_s5 = _mm(yg, w["s5_w_glu"], extras=[ypre], epi=lambda acc, yp: (acc, _gelu(yp) * _sigmoid(acc)),
                          out_dtypes=(F32, MXU), name="s5_glu")
    out_s5 = _deinterleave_rows(out_s5)
    s["conv_pre"] = conv_pre = _conv_fwd(proj, w["ssd_conv_w"], w["ssd_conv_b"])
    s["par"] = par = _head_rows(w["ssd_dt_bias"], w["ssd_a_log"], w["ssd_d"])
    out_ssd, s["y_ssd"], s["states"] = _ssd_fwd(proj, conv_pre, dtr, par, w["ssd_norm"], consts["ssd"])
    s["ycat"] = ycat = jnp.concatenate([out_s5, out_ssd], axis=1)
    s["x1"] = x1 = _mm(ycat, w["w_out"], extras=[x], epi=_add, name="out_proj")
    s["hq"] = hq = _rmsnorm_fwd(x1, w["norm_xattn"], name="norm_xattn_fwd")
    s["mn"] = mn = _rmsnorm_fwd(mem, w["norm_mem"], name="norm_mem_fwd")
    s["q"] = q = _mm(hq, w["xa_wq"], out_dtypes=(MXU,), name="xa_q")
    s["k"] = k = _mm(mn, w["xa_wk"], out_dtypes=(MXU,), name="xa_k")
    s["v"] = v = _mm(mn, w["xa_wv"], out_dtypes=(MXU,), name="xa_v")
    s["o"] = o = _attn_fwd(q, k, v)
    s["x2"] = x2 = _mm(o, w["xa_wo"], extras=[x1], epi=_add, name="xa_o")
    s["hm"] = hm = _rmsnorm_fwd(x2, w["norm_mlp"], name="norm_mlp_fwd")
    s["act"] = _mm(hm, w["mlp_w1"], epi=lambda acc: (jnp.square(jnp.maximum(acc, 0.0)),), out_dtypes=(MXU,),
                   name="mlp_up")
    x3 = _mm(s["act"], w["mlp_w2"], extras=[x2], epi=_add, name="mlp_down")
    return x3, s


def _layer_bwd(dx3, mem, w, s, consts):
    g = {}
    rep = consts["rep"]
    d_a = _mm(dx3, w["mlp_w2"], tb=True, extras=[s["act"]],
              epi=lambda acc, act: (acc * (2.0 * jnp.sqrt(act.astype(F32))),), out_dtypes=(MXU,), name="mlp_down_dx")
    g["mlp_w2"] = _mm(s["act"], dx3, ta=True, name="mlp_down_dw")
    g["mlp_w1"] = _mm(s["hm"], d_a, ta=True, name="mlp_up_dw")
    d_hm = _mm(d_a, w["mlp_w1"], tb=True, name="mlp_up_dx")
    dx2, g["norm_mlp"] = _rmsnorm_bwd(s["x2"], w["norm_mlp"], d_hm, dx3, name="norm_mlp_bwd")
    d_o = _mm(dx2, w["xa_wo"], tb=True, out_dtypes=(MXU,), name="xa_o_dx")
    g["xa_wo"] = _mm(s["o"], dx2, ta=True, name="xa_o_dw")
    dq, dk, dv = _attn_bwd(s["q"], s["k"], s["v"], d_o)
    g["xa_wq"] = _mm(s["hq"], dq, ta=True, name="xa_q_dw")
    d_hq = _mm(dq, w["xa_wq"], tb=True, name="xa_q_dx")
    dx1, g["norm_xattn"] = _rmsnorm_bwd(s["x1"], w["norm_xattn"], d_hq, dx2, name="norm_xattn_bwd")
    g["xa_wk"] = _mm(s["mn"], dk, ta=True, name="xa_k_dw")
    g["xa_wv"] = _mm(s["mn"], dv, ta=True, name="xa_v_dw")
    d_mn_v = _mm(dv, w["xa_wv"], tb=True, name="xa_v_dx")
    d_mn = _mm(dk, w["xa_wk"], tb=True, extras=[d_mn_v], epi=_add, name="xa_k_dx")
    _, g["norm_mem"] = _rmsnorm_bwd(mem, w["norm_mem"], d_mn, None, name="norm_mem_bwd")
    d_ycat = _mm(dx1, w["w_out"], tb=True, name="out_proj_dx")
    g["w_out"] = _mm(s["ycat"], dx1, ta=True, name="out_proj_dw")
    lay, proj, ypre, u = s["lay"], s["proj"], s["ypre"], s["u"]
    d_os5 = _interleave_rows(d_ycat[:, :D_S5])
    d_gp = _s5_gate_bwd(d_os5, s["gp"], ypre)
    g["s5_w_glu"] = _mm(s["yg"], d_gp, ta=True, name="s5_glu_dw")
    d_ypre = _mm(d_gp, w["s5_w_glu"], tb=True, extras=[d_os5, s["gp"], ypre],
                 epi=lambda acc, do, gp, yp: ((acc + do * _sigmoid(gp)) * _gelu_grad(yp),), name="s5_glu_dx")
    du, d_abar_r, d_abar_i, db_re, db_im, dc_re, dc_im, d_d = _s5_bwd(
        d_ypre, u, s["sr"], s["si"], *s["abar"], lay["cre_t"], lay["cim_t"], lay["bre_t"], lay["bim_t"],
        w["s5_d"].reshape(1, D_S5))
    d_bbar_r, d_bbar_i = _b_diag(db_re), _b_diag(db_im)
    g["s5_c_re"], g["s5_c_im"] = _c_diag(dc_re), -_c_diag(dc_im)
    g["s5_d"] = d_d.reshape(S5_GROUPS, S5_GROUP)
    du = _deinterleave_rows(du)
    ar, ai, ldt = w["s5_a_re"], w["s5_a_im"], w["s5_log_dt"].reshape(S5_GROUPS, 1)
    br, bi = w["s5_b_re"].reshape(S5_GROUPS, -1), w["s5_b_im"].reshape(S5_GROUPS, -1)
    d_ar, d_ai, d_ldt, d_br, d_bi = _s5_prep_bwd(
        ar, ai, ldt, br, bi, rep, d_abar_r.reshape(S5_GROUPS, S5_STATE), d_abar_i.reshape(S5_GROUPS, S5_STATE),
        d_bbar_r, d_bbar_i)
    g["s5_a_re"], g["s5_a_im"], g["s5_log_dt"] = d_ar, d_ai, d_ldt.reshape(S5_GROUPS)
    g["s5_b_re"] = d_br.reshape(S5_GROUPS, S5_STATE, S5_GROUP)
    g["s5_b_im"] = d_bi.reshape(S5_GROUPS, S5_STATE, S5_GROUP)
    d_cp, dz, d_dtr, g_ssd_norm, d_par = _ssd_bwd(proj, s["conv_pre"], s["dtr"], s["par"], w["ssd_norm"], s["y_ssd"],
                                                  s["states"], d_ycat, consts["ssd"])
    g["ssd_norm"] = g_ssd_norm
    g["ssd_dt_bias"], g["ssd_a_log"], g["ssd_d"] = (d_par[i, :SSD_HEADS] for i in range(3))
    d_xbc, g["ssd_conv_w"], g["ssd_conv_b"] = _conv_bwd(proj, w["ssd_conv_w"], d_cp)
    d_proj = jnp.concatenate([du, dz, d_xbc], axis=1)
    g_main = _mm(s["h1"], d_proj, ta=True, name="in_proj_dw")
    g_dt = _mm(s["h1"], d_dtr, ta=True, name="dt_proj_dw")
    g["w_in"] = jnp.concatenate([g_main, g_dt[:, :SSD_HEADS]], axis=1)
    d_h1_dt = _mm(d_dtr, w["w_dt"], tb=True, name="dt_proj_dx")
    d_h1 = _mm(d_proj, w["w_main"], tb=True, extras=[d_h1_dt], epi=_add, name="in_proj_dx")
    dx, g["norm_mix"] = _rmsnorm_bwd(s["x"], w["norm_mix"], d_h1, dx1, name="norm_mix_bwd")
    return dx, g


LAYER_WEIGHTS = ("norm_mix", "w_in", "s5_a_re", "s5_a_im", "s5_log_dt", "s5_b_re", "s5_b_im", "s5_c_re", "s5_c_im", "s5_d",
                 "s5_w_glu", "ssd_conv_w", "ssd_conv_b", "ssd_dt_bias", "ssd_a_log", "ssd_d", "ssd_norm", "w_out",
                 "norm_xattn", "norm_mem", "xa_wq", "xa_wk", "xa_wv", "xa_wo", "norm_mlp", "mlp_w1", "mlp_w2")
WEIGHTS = LAYER_WEIGHTS + ("norm_final",)


def _local_step(x, mem, target, weights):
    consts = {
        "ssd": _ssd_consts(),
        "rep": (jnp.arange(S5_STATE)[:, None] == jnp.arange(S5_STATE * S5_GROUP)[None, :] // S5_GROUP).astype(F32),
    }
    layers = []
    for l in range(DEPTH):
        w = {n: weights[n][l] for n in LAYER_WEIGHTS}
        w_in = w["w_in"]
        w["w_main"] = w_in[:, :D_MAIN]
        w["w_dt"] = jnp.pad(w_in[:, D_MAIN:], ((0, 0), (0, HP - SSD_HEADS)))
        layers.append(w)
    saved = []
    for l in range(DEPTH):
        x, s = _layer_fwd(x, mem, layers[l], consts)
        saved.append(s)
    loss, dx, g_final = _loss_head(x, weights["norm_final"], target)
    grads = [None] * DEPTH
    for l in reversed(range(DEPTH)):
        dx, grads[l] = _layer_bwd(dx, mem, layers[l], saved[l], consts)
    out = {n: jnp.stack([grads[l][n].reshape(weights[n].shape[1:]) for l in range(DEPTH)]) for n in LAYER_WEIGHTS}
    out["norm_final"] = g_final.reshape(weights["norm_final"].shape)
    return loss, dx, out


SHARDED = {"w_in": 2, "s5_w_glu": 1, "ssd_conv_w": 2, "w_out": 1, "xa_wq": 1, "xa_wk": 1, "xa_wv": 1, "xa_wo": 1,
           "mlp_w1": 2, "mlp_w2": 1}
EXACT = ("ssd_conv_w",)
ROW_EXCHANGE = tuple(n for n, ax in SHARDED.items() if ax == 1)
OWN_EXCHANGE = tuple(n for n in SHARDED if n not in ROW_EXCHANGE)
REPLICATED = tuple(n for n in WEIGHTS if n not in SHARDED)
LANES = 128
PAD_ROWS = 16


def _to_rows(flat, lead=()):
    n = flat.shape[-1]
    quantum = LANES * PAD_ROWS
    padded = -(-n // quantum) * quantum
    flat = jnp.pad(flat, [(0, 0)] * len(lead) + [(0, padded - n)])
    return flat.reshape(*lead, padded // LANES, LANES)


def _gather_weights(local):
    def assemble(n, seg):
        shp, ax = local[n].shape, SHARDED[n]
        return jnp.moveaxis(seg, 0, ax).reshape(*shp[:ax], N_DEV * shp[ax], *shp[ax + 1:])

    full = {}
    for n in OWN_EXCHANGE:
        payload = local[n] if n in EXACT else local[n].astype(MXU)
        full[n] = assemble(n, _all_gather(payload, name="gather_" + n))
    got = _all_gather(jnp.concatenate([local[n].astype(MXU) for n in ROW_EXCHANGE], axis=1), name="gather_row_sharded")
    off = 0
    for n in ROW_EXCHANGE:
        rows = local[n].shape[1]
        full[n] = assemble(n, got[:, :, off:off + rows])
        off += rows
    return full


def _scatter_grads(grads, local_shapes):
    def shards(n):
        shp, ax = local_shapes[n], SHARDED[n]
        gfull = grads[n].reshape(*shp[:ax], N_DEV, shp[ax], *shp[ax + 1:])
        return jnp.moveaxis(gfull, ax, 0).astype(MXU)

    out = {}
    for n in OWN_EXCHANGE:
        shp = local_shapes[n]
        out[n] = _reduce_scatter(shards(n).reshape(N_DEV, -1, shp[-1]), name="scatter_" + n).reshape(shp)
    payload = jnp.concatenate([shards(n) for n in ROW_EXCHANGE], axis=2)
    summed = _reduce_scatter(payload.reshape(N_DEV, -1, payload.shape[-1]), name="scatter_row_sharded")
    summed = summed.reshape(payload.shape[1:])
    off = 0
    for n in ROW_EXCHANGE:
        rows = local_shapes[n][1]
        out[n] = summed[:, off:off + rows]
        off += rows
    return out


def _allreduce_small(loss, grads):
    parts = [loss.reshape(-1)[:1]] + [grads[n].reshape(-1) for n in REPLICATED]
    payload = _to_rows(jnp.concatenate(parts))
    summed = _sum_slots(_all_gather(payload, name="gather_small_grads"), name="sum_small_grads").reshape(-1)
    out, off = {}, 1
    for n in REPLICATED:
        size = grads[n].size
        out[n] = summed[off:off + size].reshape(grads[n].shape)
        off += size
    return summed[0], out


def kernel(x, mem, norm_mix, w_in, s5_a_re, s5_a_im, s5_log_dt, s5_b_re, s5_b_im, s5_c_re, s5_c_im, s5_d, s5_w_glu, ssd_conv_w, ssd_conv_b, ssd_dt_bias, ssd_a_log, ssd_d, ssd_norm, w_out, norm_xattn, norm_mem, xa_wq, xa_wk, xa_wv, xa_wo, norm_mlp, mlp_w1, mlp_w2, norm_final, loss_target, m_norm_mix, m_w_in, m_s5_a_re, m_s5_a_im, m_s5_log_dt, m_s5_b_re, m_s5_b_im, m_s5_c_re, m_s5_c_im, m_s5_d, m_s5_w_glu, m_ssd_conv_w, m_ssd_conv_b, m_ssd_dt_bias, m_ssd_a_log, m_ssd_d, m_ssd_norm, m_w_out, m_norm_xattn, m_norm_mem, m_xa_wq, m_xa_wk, m_xa_wv, m_xa_wo, m_norm_mlp, m_mlp_w1, m_mlp_w2, m_norm_final, v_norm_mix, v_w_in, v_s5_a_re, v_s5_a_im, v_s5_log_dt, v_s5_b_re, v_s5_b_im, v_s5_c_re, v_s5_c_im, v_s5_d, v_s5_w_glu, v_ssd_conv_w, v_ssd_conv_b, v_ssd_dt_bias, v_ssd_a_log, v_ssd_d, v_ssd_norm, v_w_out, v_norm_xattn, v_norm_mem, v_xa_wq, v_xa_wk, v_xa_wv, v_xa_wo, v_norm_mlp, v_mlp_w1, v_mlp_w2, v_norm_final):
    args = locals()
    local = {n: args[n] for n in WEIGHTS}
    full = dict(local)
    full.update(_gather_weights(local))
    loss, grad_x, grads = _local_step(x[0], mem[0], loss_target[0], full)
    loss, g_small = _allreduce_small(loss, grads)
    g_all = _scatter_grads(grads, {n: local[n].shape for n in SHARDED})
    g_all.update(g_small)
    delta, new_m, new_v = {}, {}, {}
    for n in WEIGHTS:
        delta[n], new_m[n], new_v[n] = _adamw(local[n], g_all[n], args["m_" + n], args["v_" + n], name="adamw_" + n)
    return (loss, grad_x[None], *[g_all[n] for n in WEIGHTS], *[delta[n] for n in WEIGHTS],
            *[new_m[n] for n in WEIGHTS], *[new_v[n] for n in WEIGHTS])
```

```python
import functools
import math

import jax
import jax.numpy as jnp
from jax import lax
from jax.experimental import pallas as pl
from jax.experimental.pallas import tpu as pltpu

F32 = jnp.float32
MXU = jnp.bfloat16
HI = lax.Precision.HIGHEST

D_MODEL = 1024
DEPTH = 4
MEM_LEN = 256
D_S5 = 1024
D_SSD = 1024
S5_GROUP = 16
S5_GROUPS = 64
S5_STATE = 64
S5_CH = S5_GROUPS * S5_STATE
S5_NB = 4
S5_GPB = S5_GROUPS // S5_NB
SSD_HEADDIM = 64
SSD_HEADS = 16
SSD_GROUPS = 4
SSD_STATE = 128
SSD_CONV = 4
SSD_CHUNK = 128
SSD_BC = SSD_GROUPS * SSD_STATE
D_CONV_CH = 2048
D_MAIN = 4096
D_IN_PROJ = D_MAIN + SSD_HEADS
HP = 128
XA_HEADS = 4
XA_HEAD_DIM = 256
D_FF = 4096
EPS = 1e-5
N_DEV = 8
AXES = ("x", "y", "c")

ADAM_LR = 0.001
ADAM_B1 = 0.9
ADAM_B2 = 0.999
ADAM_EPS = 1e-08
ADAM_WD = 0.01
ADAM_STEP = 10

VMEM_LIMIT = 56 * 1024 * 1024


def _params(*sem):
    return pltpu.CompilerParams(dimension_semantics=sem, vmem_limit_bytes=VMEM_LIMIT)


def _tile(n, pref, quantum=128):
    t = (min(pref, n) // quantum) * quantum
    while t >= quantum:
        if n % t == 0:
            return t
        t -= quantum
    return n


def _sds(shape, dtype):
    return jax.ShapeDtypeStruct(tuple(shape), dtype)


def _sigmoid(x):
    return 1.0 / (1.0 + jnp.exp(-x))


def _silu(x):
    return x * _sigmoid(x)


def _silu_grad(x):
    s = _sigmoid(x)
    return s * (1.0 + x * (1.0 - s))


_GELU_C = math.sqrt(2.0 / math.pi)


def _gelu(x):
    return 0.5 * x * (1.0 + jnp.tanh(_GELU_C * (x + 0.044715 * x * x * x)))


def _gelu_grad(x):
    th = jnp.tanh(_GELU_C * (x + 0.044715 * x * x * x))
    return 0.5 * (1.0 + th) + 0.5 * x * (1.0 - th * th) * _GELU_C * (1.0 + 3.0 * 0.044715 * x * x)


def _softplus(x):
    return jnp.maximum(x, 0.0) + jnp.log(1.0 + jnp.exp(-jnp.abs(x)))


_NN = (((1,), (0,)), ((), ()))
_NT = (((1,), (1,)), ((), ()))
_TN = (((0,), (0,)), ((), ()))


def _dot(a, b, dims=_NN, precision=None):
    return lax.dot_general(a, b, dims, precision=precision, preferred_element_type=F32)


def _dot_split(x, w):
    hi = x.astype(MXU)
    lo = (x - hi.astype(F32)).astype(MXU)
    return _dot(hi, w) + _dot(lo, w)


def _rows8(v):
    return jnp.broadcast_to(v, (8, v.shape[1]))


def _mm(a, b, *, ta=False, tb=False, extras=(), epi=None, out_dtypes=(F32,), tm=1024, tn=1024, tk=1024, name):
    m, k = (a.shape[1], a.shape[0]) if ta else a.shape
    n = b.shape[0] if tb else b.shape[1]
    assert k == (b.shape[1] if tb else b.shape[0]), (a.shape, b.shape, ta, tb)
    tm, tn, tk = _tile(m, tm), _tile(n, tn), _tile(k, tk)
    nk = k // tk
    n_ex, n_out = len(extras), len(out_dtypes)
    dims = (((0,) if ta else (1,), (1,) if tb else (0,)), ((), ()))

    def body(a_ref, b_ref, *rest):
        ex_refs, out_refs = rest[:n_ex], rest[n_ex:n_ex + n_out]
        prod = _dot(a_ref[...].astype(MXU), b_ref[...].astype(MXU), dims)

        def finish(total):
            outs = epi(total, *[e[...] for e in ex_refs]) if epi is not None else (total,)
            for o, r in zip(outs, out_refs, strict=True):
                r[...] = o.astype(r.dtype)

        if nk == 1:
            finish(prod)
            return
        acc = rest[-1]
        kk = pl.program_id(2)

        @pl.when(kk == 0)
        def _():
            acc[...] = prod

        @pl.when(jnp.logical_and(kk > 0, kk < nk - 1))
        def _():
            acc[...] += prod

        @pl.when(kk == nk - 1)
        def _():
            finish(acc[...] + prod)

    a_spec = pl.BlockSpec((tk, tm), lambda i, j, kk: (kk, i)) if ta else pl.BlockSpec((tm, tk), lambda i, j, kk: (i, kk))
    b_spec = pl.BlockSpec((tn, tk), lambda i, j, kk: (j, kk)) if tb else pl.BlockSpec((tk, tn), lambda i, j, kk: (kk, j))
    mn_spec = pl.BlockSpec((tm, tn), lambda i, j, kk: (i, j))
    outs = pl.pallas_call(
        body,
        name=name,
        grid=(m // tm, n // tn, nk),
        in_specs=[a_spec, b_spec] + [mn_spec] * n_ex,
        out_specs=[mn_spec] * n_out,
        out_shape=[_sds((m, n), dt) for dt in out_dtypes],
        scratch_shapes=[pltpu.VMEM((tm, tn), F32)] if nk > 1 else [],
        compiler_params=_params("parallel", "parallel", "arbitrary"),
    )(a, b, *extras)
    return outs[0] if n_out == 1 else outs


def _bdmm(terms, *, nb, ko, no, extras=(), vecs=(), epi=None, out_dtypes=(F32,), tm=512, name):
    t = terms[0][0].shape[0]
    tm = _tile(t, tm)
    n_t, n_ex, n_v, n_out = len(terms), len(extras), len(vecs), len(out_dtypes)

    def body(*refs):
        a_refs, w_refs = refs[0:2 * n_t:2], refs[1:2 * n_t:2]
        ex_refs = refs[2 * n_t:2 * n_t + n_ex + n_v]
        out_refs = refs[2 * n_t + n_ex + n_v:]
        prods = [_dot(a[...].astype(MXU), w[...].astype(MXU)) for a, w in zip(a_refs, w_refs)]
        outs = epi(prods, *[e[...] for e in ex_refs]) if epi is not None else (prods[0],)
        for o, r in zip(outs, out_refs, strict=True):
            r[...] = o.astype(r.dtype)

    in_specs, args = [], []
    for a, w in terms:
        in_specs.append(pl.BlockSpec((tm, ko), lambda j, i: (i, j)))
        in_specs.append(pl.BlockSpec((None, ko, no), lambda j, i: (j, 0, 0)))
        args += [a, w]
    o_spec = pl.BlockSpec((tm, no), lambda j, i: (i, j))
    v_spec = pl.BlockSpec((1, no), lambda j, i: (0, j))
    outs = pl.pallas_call(
        body,
        name=name,
        grid=(nb, t // tm),
        in_specs=in_specs + [o_spec] * n_ex + [v_spec] * n_v,
        out_specs=[o_spec] * n_out,
        out_shape=[_sds((t, nb * no), dt) for dt in out_dtypes],
        compiler_params=_params("parallel", "parallel"),
    )(*args, *extras, *vecs)
    return outs[0] if n_out == 1 else outs


def _bdmm_tn(a, ka, b, nbc, *, nb, tt=512, name):
    t = a.shape[0]
    tt = _tile(t, tt)
    nt = t // tt

    def body(a_ref, b_ref, o_ref, acc):
        s = pl.program_id(1)

        @pl.when(s == 0)
        def _():
            acc[...] = jnp.zeros_like(acc)

        acc[...] += _dot(a_ref[...].astype(MXU), b_ref[...].astype(MXU), _TN)

        @pl.when(s == nt - 1)
        def _():
            o_ref[...] = acc[...]

    return pl.pallas_call(
        body,
        name=name,
        grid=(nb, nt),
        in_specs=[pl.BlockSpec((tt, ka), lambda j, s: (s, j)), pl.BlockSpec((tt, nbc), lambda j, s: (s, j))],
        out_specs=pl.BlockSpec((None, ka, nbc), lambda j, s: (j, 0, 0)),
        out_shape=_sds((nb, ka, nbc), F32),
        scratch_shapes=[pltpu.VMEM((ka, nbc), F32)],
        compiler_params=_params("parallel", "arbitrary"),
    )(a, b)


def _row(tb, w, cb=0):
    return pl.BlockSpec((tb, w), lambda i: (i, cb))


def _const(shape):
    return pl.BlockSpec(shape, lambda i: (0,) * len(shape))


def _rmsnorm_fwd(x, g, *, name):
    t, d = x.shape
    tb = _tile(t, 512, 8)

    def body(x_ref, g_ref, h_ref):
        xv = x_ref[...]
        r = lax.rsqrt(jnp.mean(xv * xv, axis=-1, keepdims=True) + EPS)
        h_ref[...] = (xv * r * g_ref[...]).astype(h_ref.dtype)

    return pl.pallas_call(
        body, name=name, grid=(t // tb,), in_specs=[_row(tb, d), _const((1, d))], out_specs=_row(tb, d),
        out_shape=_sds((t, d), MXU), compiler_params=_params("parallel"),
    )(x, g.reshape(1, d))


def _rmsnorm_bwd(x, g, dh, dres, *, name):
    t, d = x.shape
    tb = _tile(t, 256, 8)
    has_res = dres is not None

    def body(x_ref, g_ref, dh_ref, *rest):
        dx_ref, dg_ref = rest[-2:]

        @pl.when(pl.program_id(0) == 0)
        def _():
            dg_ref[...] = jnp.zeros_like(dg_ref)

        xv = x_ref[...]
        r = lax.rsqrt(jnp.mean(xv * xv, axis=-1, keepdims=True) + EPS)
        xh = xv * r
        dhv = dh_ref[...].astype(F32)
        dg_ref[...] += jnp.sum(dhv * xh, axis=0, keepdims=True)
        dxh = dhv * g_ref[...]
        dx = r * (dxh - xh * jnp.mean(dxh * xh, axis=-1, keepdims=True))
        if has_res:
            dx = dx + rest[0][...]
        dx_ref[...] = dx

    ins = [x, g.reshape(1, d), dh] + ([dres] if has_res else [])
    return pl.pallas_call(
        body, name=name, grid=(t // tb,),
        in_specs=[_row(tb, d), _const((1, d)), _row(tb, d)] + ([_row(tb, d)] if has_res else []),
        out_specs=[_row(tb, d), _const((1, d))],
        out_shape=[_sds((t, d), F32), _sds((1, d), F32)],
        compiler_params=_params("arbitrary"),
    )(*ins)


def _loss_head(x, g, target):
    t, d = x.shape
    tb = _tile(t, 256, 8)

    def body(x_ref, g_ref, tg_ref, loss_ref, dx_ref, dg_ref):
        @pl.when(pl.program_id(0) == 0)
        def _():
            dg_ref[...] = jnp.zeros_like(dg_ref)
            loss_ref[...] = jnp.zeros_like(loss_ref)

        xv, gv = x_ref[...], g_ref[...]
        r = lax.rsqrt(jnp.mean(xv * xv, axis=-1, keepdims=True) + EPS)
        xh = xv * r
        err = xh * gv - tg_ref[...]
        loss_ref[...] += 0.5 * jnp.sum(jnp.mean(err * err, axis=-1, keepdims=True), axis=0, keepdims=True)
        dy = err * (1.0 / d)
        dg_ref[...] += jnp.sum(dy * xh, axis=0, keepdims=True)
        dxh = dy * gv
        dx_ref[...] = r * (dxh - xh * jnp.mean(dxh * xh, axis=-1, keepdims=True))

    return pl.pallas_call(
        body, name="loss_head", grid=(t // tb,),
        in_specs=[_row(tb, d), _const((1, d)), _row(tb, d)],
        out_specs=[_const((1, HP)), _row(tb, d), _const((1, d))],
        out_shape=[_sds((1, HP), F32), _sds((t, d), F32), _sds((1, d), F32)],
        compiler_params=_params("arbitrary"),
    )(x, g.reshape(1, d), target)


def _s5_discretise(ar, ai, ldt, br, bi, rep):
    dt = jnp.exp(ldt)
    mag = jnp.exp(dt * ar)
    abar_r, abar_i = mag * jnp.cos(dt * ai), mag * jnp.sin(dt * ai)
    den = ar * ar + ai * ai
    zr, zi = abar_r - 1.0, abar_i
    fr = (zr * ar + zi * ai) / den
    fi = (zi * ar - zr * ai) / den
    fr_e, fi_e = _dot(fr, rep, precision=HI), _dot(fi, rep, precision=HI)
    return abar_r, abar_i, fr_e * br - fi_e * bi, fr_e * bi + fi_e * br


def _s5_prep(ar, ai, ldt, br, bi, rep):
    g, p = ar.shape
    ph = br.shape[1]

    def body(ar_ref, ai_ref, ldt_ref, br_ref, bi_ref, rep_ref, o0, o1, o2, o3):
        outs = _s5_discretise(ar_ref[...], ai_ref[...], ldt_ref[...], br_ref[...], bi_ref[...], rep_ref[...])
        for o, v in zip((o0, o1, o2, o3), outs):
            o[...] = v

    return pl.pallas_call(
        body, name="s5_prep",
        out_shape=[_sds((g, p), F32), _sds((g, p), F32), _sds((g, ph), F32), _sds((g, ph), F32)],
        compiler_params=pltpu.CompilerParams(vmem_limit_bytes=VMEM_LIMIT),
    )(ar, ai, ldt, br, bi, rep)


def _s5_prep_bwd(ar, ai, ldt, br, bi, rep, d_abar_r, d_abar_i, d_bbar_r, d_bbar_i):
    g, p = ar.shape
    ph = br.shape[1]

    def body(ar_ref, ai_ref, ldt_ref, br_ref, bi_ref, rep_ref, c0, c1, c2, c3, o0, o1, o2, o3, o4):
        rep_v = rep_ref[...]
        _, vjp = jax.vjp(lambda a, b, c, d, e: _s5_discretise(a, b, c, d, e, rep_v),
                         ar_ref[...], ai_ref[...], ldt_ref[...], br_ref[...], bi_ref[...])
        grads = vjp((c0[...], c1[...], c2[...], c3[...]))
        for o, v in zip((o0, o1, o2, o3, o4), grads):
            o[...] = v

    return pl.pallas_call(
        body, name="s5_prep_bwd",
        out_shape=[_sds((g, p), F32), _sds((g, p), F32), _sds((g, 1), F32), _sds((g, ph), F32), _sds((g, ph), F32)],
        compiler_params=pltpu.CompilerParams(vmem_limit_bytes=VMEM_LIMIT),
    )(ar, ai, ldt, br, bi, rep, d_abar_r, d_abar_i, d_bbar_r, d_bbar_i)


SCAN_ROWS = 512


def _scan_rows(t):
    return _tile(t, SCAN_ROWS, 64)


def _interleave_rows(x):
    t, c = x.shape
    tb = _scan_rows(t)
    return x.reshape(t // tb, 8, tb // 8, c).swapaxes(1, 2).reshape(t, c)


def _deinterleave_rows(x):
    t, c = x.shape
    tb = _scan_rows(t)
    return x.reshape(t // tb, tb // 8, 8, c).swapaxes(1, 2).reshape(t, c)


def _cmul(ar, ai, br, bi):
    return ar * br - ai * bi, ar * bi + ai * br


def _segment_carries(fr, fi, ar8, ai8, c_r, c_i, seg, reverse):
    pr, pi = ar8, ai8
    for _ in range(int(math.log2(seg))):
        pr, pi = _cmul(pr, pi, pr, pi)
    row = lax.broadcasted_iota(jnp.int32, fr.shape, 0)
    edge = 7 if reverse else 0
    qr, qi = _cmul(pr, pi, c_r, c_i)
    xr, xi = jnp.where(row == edge, fr + qr, fr), jnp.where(row == edge, fi + qi, fi)
    for sh in (1, 2, 4):
        if reverse:
            keep, amount = row < 8 - sh, 8 - sh
        else:
            keep, amount = row >= sh, sh
        qr, qi = jnp.where(keep, pltpu.roll(xr, amount, 0), 0.0), jnp.where(keep, pltpu.roll(xi, amount, 0), 0.0)
        tr, ti = _cmul(pr, pi, qr, qi)
        xr, xi = xr + tr, xi + ti
        pr, pi = _cmul(pr, pi, pr, pi)
    if reverse:
        in_r, in_i = jnp.where(row == 7, c_r, pltpu.roll(xr, 7, 0)), jnp.where(row == 7, c_i, pltpu.roll(xi, 7, 0))
        return in_r, in_i, xr[0:1, :], xi[0:1, :]
    in_r, in_i = jnp.where(row == 0, c_r, pltpu.roll(xr, 1, 0)), jnp.where(row == 0, c_i, pltpu.roll(xi, 1, 0))
    return in_r, in_i, xr[7:8, :], xi[7:8, :]


def _sweeps(ar8, ai8, dr, di, cr, ci, seg, reverse, emit):
    order = range(seg - 1, -1, -1) if reverse else range(seg)
    rows = lambda j: slice(j * 8, (j + 1) * 8)
    fr, fi = jnp.zeros(ar8.shape, F32), jnp.zeros(ar8.shape, F32)
    for j in order:
        tr, ti = _cmul(ar8, ai8, fr, fi)
        fr, fi = tr + dr[rows(j), :], ti + di[rows(j), :]
    s_r, s_i, out_r, out_i = _segment_carries(fr, fi, ar8, ai8, cr[...], ci[...], seg, reverse)
    cr[...] = out_r
    ci[...] = out_i
    for j in order:
        tr, ti = _cmul(ar8, ai8, s_r, s_i)
        s_r, s_i = tr + dr[rows(j), :], ti + di[rows(j), :]
        emit(j, s_r, s_i)


S5_BK, S5_BN = 256, 1024


def _s5_specs(tb, nt, reverse):
    t_of = (lambda s: nt - 1 - s) if reverse else (lambda s: s)
    return dict(
        small=pl.BlockSpec((tb, S5_BK), lambda c, s: (t_of(s), c)),
        wide=pl.BlockSpec((tb, S5_BN), lambda c, s: (t_of(s), c)),
        halo=pl.BlockSpec((8, S5_BN), lambda c, s: (jnp.maximum(t_of(s) * (tb // 8) - 1, 0), c)),
        vec_w=pl.BlockSpec((1, S5_BN), lambda c, s: (0, c)),
        vec_s=pl.BlockSpec((1, S5_BK), lambda c, s: (0, c)),
        w_in=pl.BlockSpec((None, S5_BK, S5_BN), lambda c, s: (c, 0, 0)),
        w_out=pl.BlockSpec((None, S5_BN, S5_BK), lambda c, s: (c, 0, 0)),
    )


def _s5_fwd(u, ar, ai, bre, bim, cre, cim, d_vec):
    t = u.shape[0]
    tb = _scan_rows(t)
    nt, seg, nb = t // tb, tb // 8, bre.shape[0]
    assert 1 << int(math.log2(seg)) == seg

    def body(u_ref, ar_ref, ai_ref, bre_ref, bim_ref, cre_ref, cim_ref, d_ref, sr_ref, si_ref, yp_ref, yg_ref,
             cr, ci, dr_s, di_s):
        @pl.when(pl.program_id(1) == 0)
        def _():
            cr[...] = jnp.zeros_like(cr)
            ci[...] = jnp.zeros_like(ci)

        uv = u_ref[...]
        ub = uv.astype(MXU)
        dr_s[...] = _dot(ub, bre_ref[...])
        di_s[...] = _dot(ub, bim_ref[...])
        ar8 = jnp.broadcast_to(ar_ref[...], (8, S5_BN))
        ai8 = jnp.broadcast_to(ai_ref[...], (8, S5_BN))

        def emit(j, s_r, s_i):
            sr_ref[j * 8:(j + 1) * 8, :] = s_r
            si_ref[j * 8:(j + 1) * 8, :] = s_i

        _sweeps(ar8, ai8, dr_s, di_s, cr, ci, seg, False, emit)
        yp = _dot(sr_ref[...].astype(MXU), cre_ref[...]) - _dot(si_ref[...].astype(MXU), cim_ref[...]) + d_ref[...] * uv
        yp_ref[...] = yp
        yg_ref[...] = _gelu(yp).astype(yg_ref.dtype)

    sp = _s5_specs(tb, nt, False)
    return pl.pallas_call(
        body, name="s5_fwd", grid=(nb, nt),
        in_specs=[sp["small"], sp["vec_w"], sp["vec_w"], sp["w_in"], sp["w_in"], sp["w_out"], sp["w_out"], sp["vec_s"]],
        out_specs=[sp["wide"], sp["wide"], sp["small"], sp["small"]],
        out_shape=[_sds((t, nb * S5_BN), F32), _sds((t, nb * S5_BN), F32), _sds((t, nb * S5_BK), F32),
                   _sds((t, nb * S5_BK), MXU)],
        scratch_shapes=[pltpu.VMEM((1, S5_BN), F32), pltpu.VMEM((1, S5_BN), F32), pltpu.VMEM((tb, S5_BN), F32),
                        pltpu.VMEM((tb, S5_BN), F32)],
        compiler_params=_params("parallel", "arbitrary"),
    )(u, ar, ai, bre, bim, cre, cim, d_vec)


def _s5_bwd(dyp, u, sr, si, ar, ai, cre_t, cim_t, bre_t, bim_t, d_vec):
    t = u.shape[0]
    tb = _scan_rows(t)
    nt, seg, nb = t // tb, tb // 8, cre_t.shape[0]

    def body(dyp_ref, u_ref, pr_ref, pi_ref, hr_ref, hi_ref, ar_ref, ai_ref, cre_ref, cim_ref, bre_ref, bim_ref, d_ref,
             du_ref, gr_ref, gi_ref, dbr_ref, dbi_ref, dcr_ref, dci_ref, dd_ref, cr, ci, dr_s, di_s, lr_s, li_s):
        step = pl.program_id(1)

        @pl.when(step == 0)
        def _():
            for r in (cr, ci, gr_ref, gi_ref, dbr_ref, dbi_ref, dcr_ref, dci_ref, dd_ref):
                r[...] = jnp.zeros_like(r)

        dyv, uv = dyp_ref[...], u_ref[...]
        dyb, ub = dyv.astype(MXU), uv.astype(MXU)
        dr_s[...] = _dot(dyb, cre_ref[...])
        di_s[...] = -_dot(dyb, cim_ref[...])
        ar8 = jnp.broadcast_to(ar_ref[...], (8, S5_BN))
        ai8 = jnp.broadcast_to(-ai_ref[...], (8, S5_BN))
        first_block = step == nt - 1
        row = lax.broadcasted_iota(jnp.int32, (8, S5_BN), 0)
        acc = [jnp.zeros((8, S5_BN), F32), jnp.zeros((8, S5_BN), F32)]

        def emit(j, s_r, s_i):
            lr_s[j * 8:(j + 1) * 8, :] = s_r
            li_s[j * 8:(j + 1) * 8, :] = s_i
            if j > 0:
                p_r, p_i = pr_ref[(j - 1) * 8:j * 8, :], pi_ref[(j - 1) * 8:j * 8, :]
            else:
                halo_r = jnp.where(first_block, 0.0, hr_ref[7:8, :])
                halo_i = jnp.where(first_block, 0.0, hi_ref[7:8, :])
                p_r = jnp.where(row == 0, halo_r, pltpu.roll(pr_ref[(seg - 1) * 8:seg * 8, :], 1, 0))
                p_i = jnp.where(row == 0, halo_i, pltpu.roll(pi_ref[(seg - 1) * 8:seg * 8, :], 1, 0))
            acc[0] = acc[0] + (p_r * s_r + p_i * s_i)
            acc[1] = acc[1] + (p_r * s_i - p_i * s_r)

        _sweeps(ar8, ai8, dr_s, di_s, cr, ci, seg, True, emit)
        gr_ref[...] += jnp.sum(acc[0], axis=0, keepdims=True)
        gi_ref[...] += jnp.sum(acc[1], axis=0, keepdims=True)
        lrb, lib = lr_s[...].astype(MXU), li_s[...].astype(MXU)
        du_ref[...] = _dot(lrb, bre_ref[...]) + _dot(lib, bim_ref[...]) + d_ref[...] * dyv
        dbr_ref[...] += _dot(ub, lrb, _TN)
        dbi_ref[...] += _dot(ub, lib, _TN)
        dcr_ref[...] += _dot(pr_ref[...].astype(MXU), dyb, _TN)
        dci_ref[...] += _dot(pi_ref[...].astype(MXU), dyb, _TN)
        dd_ref[...] += jnp.sum(dyv * uv, axis=0, keepdims=True)

    sp = _s5_specs(tb, nt, True)
    return pl.pallas_call(
        body, name="s5_bwd", grid=(nb, nt),
        in_specs=[sp["small"], sp["small"], sp["wide"], sp["wide"], sp["halo"], sp["halo"], sp["vec_w"], sp["vec_w"],
                  sp["w_in"], sp["w_in"], sp["w_out"], sp["w_out"], sp["vec_s"]],
        out_specs=[sp["small"], sp["vec_w"], sp["vec_w"], sp["w_in"], sp["w_in"], sp["w_out"], sp["w_out"], sp["vec_s"]],
        out_shape=[_sds((t, nb * S5_BK), F32), _sds((1, nb * S5_BN), F32), _sds((1, nb * S5_BN), F32),
                   _sds((nb, S5_BK, S5_BN), F32), _sds((nb, S5_BK, S5_BN), F32), _sds((nb, S5_BN, S5_BK), F32),
                   _sds((nb, S5_BN, S5_BK), F32), _sds((1, nb * S5_BK), F32)],
        scratch_shapes=[pltpu.VMEM((1, S5_BN), F32), pltpu.VMEM((1, S5_BN), F32)] + [pltpu.VMEM((tb, S5_BN), F32)] * 4,
        compiler_params=_params("parallel", "arbitrary"),
    )(dyp, u, sr, si, sr, si, ar, ai, cre_t, cim_t, bre_t, bim_t, d_vec)


def _s5_gate_bwd(d_ycat, gp, ypre):
    t, d = gp.shape
    tb = _tile(t, 256, 8)

    def body(do_ref, gp_ref, yp_ref, o_ref):
        sg = _sigmoid(gp_ref[...])
        o_ref[...] = (do_ref[...] * _gelu(yp_ref[...]) * sg * (1.0 - sg)).astype(o_ref.dtype)

    return pl.pallas_call(
        body, name="s5_gate_bwd", grid=(t // tb,), in_specs=[_row(tb, d), _row(tb, d), _row(tb, d)],
        out_specs=_row(tb, d), out_shape=_sds((t, d), MXU), compiler_params=_params("parallel"),
    )(d_ycat, gp, ypre)


_CONV_CW = 512
_CONV_OFF = (D_MAIN - D_CONV_CH) // _CONV_CW


def _conv_bwd(proj, w, d_conv):
    t = proj.shape[0]
    tb = _tile(t, 256, 8)
    cw, off = _CONV_CW, _CONV_OFF
    nt = t // tb

    def body(cur_ref, prev_ref, w_ref, dc_ref, dnext_ref, dx_ref, dw_ref, db_ref, ext, dext):
        i = pl.program_id(1)

        @pl.when(i == 0)
        def _():
            dw_ref[...] = jnp.zeros_like(dw_ref)
            db_ref[...] = jnp.zeros_like(db_ref)

        ext[0:8, :] = jnp.where(i == 0, 0.0, prev_ref[...])
        ext[8:tb + 8, :] = cur_ref[...]
        dcv = dc_ref[...]
        dext[0:tb, :] = dcv
        dext[tb:tb + 8, :] = jnp.where(i == nt - 1, 0.0, dnext_ref[...])
        dx = jnp.zeros((tb, cw), F32)
        for j in range(SSD_CONV):
            dx = dx + w_ref[SSD_CONV - 1 - j:SSD_CONV - j, :] * dext[j:j + tb, :]
            dw_ref[SSD_CONV - 1 - j:SSD_CONV - j, :] += jnp.sum(dcv * ext[8 - j:8 - j + tb, :], axis=0, keepdims=True)
        dx_ref[...] = dx
        db_ref[...] += jnp.sum(dcv, axis=0, keepdims=True)

    return pl.pallas_call(
        body, name="ssd_conv_bwd", grid=(D_CONV_CH // cw, nt),
        in_specs=[
            pl.BlockSpec((tb, cw), lambda j, i: (i, j + off)),
            pl.BlockSpec((8, cw), lambda j, i: (jnp.maximum(i * (tb // 8) - 1, 0), j + off)),
            pl.BlockSpec((SSD_CONV, cw), lambda j, i: (0, j)),
            pl.BlockSpec((tb, cw), lambda j, i: (i, j)),
            pl.BlockSpec((8, cw), lambda j, i: (jnp.minimum((i + 1) * (tb // 8), t // 8 - 1), j)),
        ],
        out_specs=[
            pl.BlockSpec((tb, cw), lambda j, i: (i, j)),
            pl.BlockSpec((SSD_CONV, cw), lambda j, i: (0, j)),
            pl.BlockSpec((1, cw), lambda j, i: (0, j)),
        ],
        out_shape=[_sds((t, D_CONV_CH), F32), _sds((SSD_CONV, D_CONV_CH), F32), _sds((1, D_CONV_CH), F32)],
        scratch_shapes=[pltpu.VMEM((tb + 8, cw), F32), pltpu.VMEM((tb + 8, cw), F32)],
        compiler_params=_params("parallel", "arbitrary"),
    )(proj, proj, w, d_conv, d_conv)


def _ssd_consts():
    head = jnp.arange(HP)[:, None]
    lane = jnp.arange(D_SSD)[None, :]
    expand = ((lane // SSD_HEADDIM) == head).astype(MXU)
    ll = jnp.arange(SSD_CHUNK)
    tri = (ll[:, None] >= ll[None, :]).astype(F32)
    return expand, expand.T, tri, jnp.eye(HP, dtype=F32)


def _ssd_chunk_terms(cp, dtr, par, expand, tri):
    ln = SSD_CHUNK
    xbc = _silu(cp)
    xs, bm, cm = xbc[:, :D_SSD], xbc[:, D_SSD:D_SSD + SSD_BC], xbc[:, D_SSD + SSD_BC:]
    dt = _softplus(dtr + par[0:1, :])
    a = -jnp.exp(par[1:2, :])
    da = dt * a
    acum = _dot(tri, da, precision=HI)
    acum_t = _dot(da, tri, (((0,), (1,)), ((), ())), precision=HI)
    atot = acum[ln - 1:ln, :]
    dt_e = _dot_split(dt, expand)
    eac_e = _dot_split(jnp.exp(acum), expand)
    dec_e = _dot_split(jnp.exp(atot - acum), expand)
    eat_e = _dot_split(_rows8(jnp.exp(atot)), expand)[0:1, :]
    dsk_e = _dot_split(_rows8(par[2:3, :]), expand)[0:1, :]
    return dict(xs=xs, bm=bm, cm=cm, dt=dt, a=a, acum=acum, acum_t=acum_t, dt_e=dt_e, eac_e=eac_e,
                dec_e=dec_e, eat_e=eat_e, dsk_e=dsk_e)


def _decay_matrix(acum, acum_t, h, mask):
    diff = acum[:, h:h + 1] - acum_t[h:h + 1, :]
    return jnp.where(mask, jnp.exp(jnp.minimum(diff, 0.0)), 0.0)


def _ssd_fwd(proj, conv_w, conv_b, dtr, par, gnorm, consts):
    t = proj.shape[0]
    ln = SSD_CHUNK
    nc = t // ln
    expand, _, tri, _ = consts
    hd2 = 2 * SSD_HEADDIM

    def body(cur_ref, prev_ref, cw_ref, cb_ref, z_ref, dtr_ref, par_ref, g_ref, e_ref, tri_ref,
             out_ref, y_ref, st_ref, cp_ref, state, ext):
        first = pl.program_id(0) == 0

        @pl.when(first)
        def _():
            state[...] = jnp.zeros_like(state)

        st_ref[...] = state[...]
        ext[0:8, :] = jnp.where(first, 0.0, prev_ref[...])
        ext[8:ln + 8, :] = cur_ref[...]
        cpv = jnp.broadcast_to(cb_ref[...], (ln, D_CONV_CH))
        for j in range(SSD_CONV):
            cpv = cpv + cw_ref[SSD_CONV - 1 - j:SSD_CONV - j, :] * ext[8 - j:8 - j + ln, :]
        cp_ref[...] = cpv
        c = _ssd_chunk_terms(cpv, dtr_ref[...], par_ref[...], e_ref[...], tri_ref[...])
        xdt = c["xs"] * c["dt_e"]
        xb, xd = xdt.astype(MXU), (xdt * c["dec_e"]).astype(MXU)
        bb, cb = c["bm"].astype(MXU), c["cm"].astype(MXU)
        mask = lax.broadcasted_iota(jnp.int32, (ln, ln), 0) >= lax.broadcasted_iota(jnp.int32, (ln, ln), 1)
        left = lax.broadcasted_iota(jnp.int32, (ln, hd2), 1) < SSD_HEADDIM
        for g in range(SSD_GROUPS):
            nsl = slice(g * SSD_STATE, (g + 1) * SSD_STATE)
            gsl = slice(g * 256, (g + 1) * 256)
            bg, cg = bb[:, nsl], cb[:, nsl]
            cbm = _dot(cg, bg, _NT)
            st_g = state[:, gsl]
            for pair in range(2):
                h0 = g * 4 + pair * 2
                psl = slice(h0 * SSD_HEADDIM, (h0 + 2) * SSD_HEADDIM)
                m0 = (cbm * _decay_matrix(c["acum"], c["acum_t"], h0, mask)).astype(MXU)
                m1 = (cbm * _decay_matrix(c["acum"], c["acum_t"], h0 + 1, mask)).astype(MXU)
                y_ref[:, psl] = jnp.where(left, _dot(m0, xb[:, psl]), _dot(m1, xb[:, psl]))
            y_ref[:, gsl] += _dot(cg, st_g.astype(MXU)) * c["eac_e"][:, gsl]
            state[:, gsl] = st_g * c["eat_e"][:, gsl] + _dot(bg, xd[:, gsl], _TN)
        y = y_ref[...] + c["dsk_e"] * c["xs"]
        y_ref[...] = y
        y2 = y * _silu(z_ref[...])
        r = lax.rsqrt(jnp.mean(y2 * y2, axis=-1, keepdims=True) + EPS)
        out_ref[...] = (y2 * r * g_ref[...]).astype(out_ref.dtype)

    xbc_block = (D_MAIN - D_CONV_CH) // D_CONV_CH
    return pl.pallas_call(
        body, name="ssd_fwd", grid=(nc,),
        in_specs=[_row(ln, D_CONV_CH, xbc_block),
                  pl.BlockSpec((8, D_CONV_CH), lambda i: (jnp.maximum(i * (ln // 8) - 1, 0), xbc_block)),
                  _const((SSD_CONV, D_CONV_CH)), _const((1, D_CONV_CH)),
                  _row(ln, D_SSD, 1), _row(ln, HP), _const((8, HP)), _const((1, D_SSD)),
                  _const((HP, D_SSD)), _const((ln, ln))],
        out_specs=[_row(ln, D_SSD), _row(ln, D_SSD), pl.BlockSpec((None, SSD_STATE, D_SSD), lambda i: (i, 0, 0)),
                   _row(ln, D_CONV_CH)],
        out_shape=[_sds((t, D_SSD), MXU), _sds((t, D_SSD), F32), _sds((nc, SSD_STATE, D_SSD), F32),
                   _sds((t, D_CONV_CH), F32)],
        scratch_shapes=[pltpu.VMEM((SSD_STATE, D_SSD), F32), pltpu.VMEM((ln + 8, D_CONV_CH), F32)],
        compiler_params=_params("arbitrary"),
    )(proj, proj, conv_w, conv_b.reshape(1, D_CONV_CH), proj, dtr, par, gnorm.reshape(1, D_SSD), expand, tri)


def _ssd_bwd(proj, conv_pre, dtr, par, gnorm, y, states, d_ycat, consts):
    t = proj.shape[0]
    ln = SSD_CHUNK
    nc = t // ln
    expand, expand_t, tri, eye = consts
    hd2 = 2 * SSD_HEADDIM

    def body(cp_ref, z_ref, dtr_ref, par_ref, g_ref, y_ref, st_ref, do_ref, e_ref, et_ref, tri_ref, eye_ref,
             dcp_ref, dz_ref, ddt_ref, dg_ref, dpar_ref, dstate, dx_buf, lane_buf, tot_buf, colsum):
        @pl.when(pl.program_id(0) == 0)
        def _():
            dstate[...] = jnp.zeros_like(dstate)
            dg_ref[...] = jnp.zeros_like(dg_ref)
            dpar_ref[...] = jnp.zeros_like(dpar_ref)

        cpv, et_v, tri_v, par_v = cp_ref[...], et_ref[...], tri_ref[...], par_ref[...]
        c = _ssd_chunk_terms(cpv, dtr_ref[...], par_v, e_ref[...], tri_v)
        xs = c["xs"]
        zv, yv, dov = z_ref[...], y_ref[...], do_ref[...]
        sz = _silu(zv)
        y2 = yv * sz
        r = lax.rsqrt(jnp.mean(y2 * y2, axis=-1, keepdims=True) + EPS)
        yh = y2 * r
        dg_ref[...] += jnp.sum(dov * yh, axis=0, keepdims=True)
        dyh = dov * g_ref[...]
        dy2 = r * (dyh - yh * jnp.mean(dyh * yh, axis=-1, keepdims=True))
        dz_ref[...] = dy2 * yv * _silu_grad(zv)
        dy = dy2 * sz

        xdt = xs * c["dt_e"]
        xdf = xdt * c["dec_e"]
        xb, xd = xdt.astype(MXU), xdf.astype(MXU)
        bb, cb = c["bm"].astype(MXU), c["cm"].astype(MXU)
        dyb, dye = dy.astype(MXU), (dy * c["eac_e"]).astype(MXU)
        mask = lax.broadcasted_iota(jnp.int32, (ln, ln), 0) >= lax.broadcasted_iota(jnp.int32, (ln, ln), 1)
        left = lax.broadcasted_iota(jnp.int32, (ln, hd2), 1) < SSD_HEADDIM
        lane_hp = lax.broadcasted_iota(jnp.int32, (ln, HP), 1)
        d_acum = jnp.zeros((ln, HP), F32)
        colsum[...] = jnp.zeros_like(colsum)
        tot_buf[...] = jnp.zeros_like(tot_buf)
        for g in range(SSD_GROUPS):
            nsl = slice(g * SSD_STATE, (g + 1) * SSD_STATE)
            gsl = slice(g * 256, (g + 1) * 256)
            bg, cg = bb[:, nsl], cb[:, nsl]
            cbm = _dot(cg, bg, _NT)
            st_g = st_ref[:, gsl]
            dst_g = dstate[:, gsl]
            stb, dstb = st_g.astype(MXU), dst_g.astype(MXU)
            y_off = _dot(cg, stb)
            bds = _dot(bg, dstb)
            dcb = jnp.zeros((ln, ln), F32)
            for pair in range(2):
                h0 = g * 4 + pair * 2
                psl = slice(h0 * SSD_HEADDIM, (h0 + 2) * SSD_HEADDIM)
                xp, dyp = xb[:, psl], dyb[:, psl]
                dxp = []
                for k in range(2):
                    h = h0 + k
                    lm = _decay_matrix(c["acum"], c["acum_t"], h, mask)
                    mm = cbm * lm
                    half = left if k == 0 else jnp.logical_not(left)
                    dm = _dot(jnp.where(half, dyp, jnp.zeros_like(dyp)), xp, _NT)
                    dcb = dcb + dm * lm
                    gm = dm * mm
                    d_acum = d_acum + jnp.where(lane_hp == h, jnp.sum(gm, axis=1, keepdims=True), 0.0)
                    colsum[h:h + 1, :] = jnp.sum(gm, axis=0, keepdims=True)
                    dxp.append(_dot(mm.astype(MXU), dyp, _TN))
                dx_buf[:, psl] = jnp.where(left, dxp[0], dxp[1])
            dx_buf[:, gsl] += bds * c["dec_e"][:, gsl]
            dcbb = dcb.astype(MXU)
            dc_g = _dot(dcbb, bg) + _dot(dye[:, gsl], stb, _NT)
            db_g = _dot(dcbb, cg, _TN) + _dot(xd[:, gsl], dstb, _NT)
            dcp_ref[:, D_SSD + g * SSD_STATE:D_SSD + (g + 1) * SSD_STATE] = db_g
            dcp_ref[:, D_SSD + SSD_BC + g * SSD_STATE:D_SSD + SSD_BC + (g + 1) * SSD_STATE] = dc_g
            dec_term = xdf[:, gsl] * bds
            lane_buf[:, gsl] = dy[:, gsl] * y_off * c["eac_e"][:, gsl] - dec_term
            tot_buf[0:1, gsl] = (jnp.sum(st_g * dst_g, axis=0, keepdims=True) * c["eat_e"][:, gsl]
                                 + jnp.sum(dec_term, axis=0, keepdims=True))
            dstate[:, gsl] = dst_g * c["eat_e"][:, gsl] + _dot(cg, dye[:, gsl], _TN)
        dx_tot = dx_buf[...]
        d_acum = d_acum + _dot_split(lane_buf[...], et_v) - _dot(colsum[...], eye_ref[...], _TN, precision=HI)
        d_atot = _dot_split(tot_buf[...], et_v)[0:1, :]
        row_hp = lax.broadcasted_iota(jnp.int32, (ln, HP), 0)
        d_acum = d_acum + jnp.where(row_hp == ln - 1, d_atot, 0.0)
        d_da = _dot(tri_v, d_acum, _TN, precision=HI)
        d_dt = d_da * c["a"] + _dot_split(dx_tot * xs, et_v)
        d_dtr = d_dt * _sigmoid(dtr_ref[...] + par_v[0:1, :])
        ddt_ref[...] = d_dtr
        dpar_ref[0:1, :] += jnp.sum(d_dtr, axis=0, keepdims=True)
        dpar_ref[1:2, :] += jnp.sum(d_da * c["dt"], axis=0, keepdims=True) * c["a"]
        dpar_ref[2:3, :] += _dot_split(_rows8(jnp.sum(dy * xs, axis=0, keepdims=True)), et_v)[0:1, :]
        dcp_ref[:, 0:D_SSD] = dx_tot * c["dt_e"] + dy * c["dsk_e"]
        dcp_ref[...] = dcp_ref[...] * _silu_grad(cpv)

    rev = lambda i: (nc - 1 - i, 0)
    rev1 = lambda i: (nc - 1 - i, 1)
    return pl.pallas_call(
        body, name="ssd_bwd", grid=(nc,),
        in_specs=[pl.BlockSpec((ln, D_CONV_CH), rev), pl.BlockSpec((ln, D_SSD), rev1), pl.BlockSpec((ln, HP), rev),
                  _const((8, HP)), _const((1, D_SSD)), pl.BlockSpec((ln, D_SSD), rev),
                  pl.BlockSpec((None, SSD_STATE, D_SSD), lambda i: (nc - 1 - i, 0, 0)),
                  pl.BlockSpec((ln, D_SSD), rev1),
                  _const((HP, D_SSD)), _const((D_SSD, HP)), _const((ln, ln)), _const((HP, HP))],
        out_specs=[pl.BlockSpec((ln, D_CONV_CH), rev), pl.BlockSpec((ln, D_SSD), rev), pl.BlockSpec((ln, HP), rev),
                   _const((1, D_SSD)), _const((8, HP))],
        out_shape=[_sds((t, D_CONV_CH), F32), _sds((t, D_SSD), F32), _sds((t, HP), F32), _sds((1, D_SSD), F32),
                   _sds((8, HP), F32)],
        scratch_shapes=[pltpu.VMEM((SSD_STATE, D_SSD), F32), pltpu.VMEM((ln, D_SSD), F32), pltpu.VMEM((ln, D_SSD), F32),
                        pltpu.VMEM((8, D_SSD), F32), pltpu.VMEM((HP, ln), F32)],
        compiler_params=_params("arbitrary"),
    )(conv_pre, proj, dtr, par, gnorm.reshape(1, D_SSD), y, states, d_ycat, expand, expand_t, tri, eye)


def _softmax_rows(s):
    e = jnp.exp(s - jnp.max(s, axis=-1, keepdims=True))
    return e / jnp.sum(e, axis=-1, keepdims=True)


def _attn_fwd(q, k, v):
    t, d = q.shape
    mlen = k.shape[0]
    tq = _tile(t, 512, 8)
    scale = XA_HEAD_DIM ** -0.5

    def body(q_ref, k_ref, v_ref, o_ref):
        for h in range(XA_HEADS):
            sl = slice(h * XA_HEAD_DIM, (h + 1) * XA_HEAD_DIM)
            p = _softmax_rows(_dot(q_ref[:, sl], k_ref[:, sl], _NT) * scale)
            o_ref[:, sl] = _dot(p.astype(MXU), v_ref[:, sl]).astype(o_ref.dtype)

    return pl.pallas_call(
        body, name="xattn_fwd", grid=(t // tq,),
        in_specs=[_row(tq, d), _const((mlen, d)), _const((mlen, d))], out_specs=_row(tq, d),
        out_shape=_sds((t, d), MXU), compiler_params=_params("parallel"),
    )(q, k, v)


def _attn_bwd(q, k, v, do):
    t, d = q.shape
    mlen = k.shape[0]
    tq = _tile(t, 512, 8)
    scale = XA_HEAD_DIM ** -0.5

    def body(q_ref, k_ref, v_ref, do_ref, dq_ref, dk_ref, dv_ref):
        @pl.when(pl.program_id(0) == 0)
        def _():
            dk_ref[...] = jnp.zeros_like(dk_ref)
            dv_ref[...] = jnp.zeros_like(dv_ref)

        for h in range(XA_HEADS):
            sl = slice(h * XA_HEAD_DIM, (h + 1) * XA_HEAD_DIM)
            qh, kh, vh, doh = q_ref[:, sl], k_ref[:, sl], v_ref[:, sl], do_ref[:, sl]
            p = _softmax_rows(_dot(qh, kh, _NT) * scale)
            dp = _dot(doh, vh, _NT)
            dv_ref[:, sl] += _dot(p.astype(MXU), doh, _TN)
            ds = (p * (dp - jnp.sum(p * dp, axis=-1, keepdims=True)) * scale).astype(MXU)
            dq_ref[:, sl] = _dot(ds, kh).astype(dq_ref.dtype)
            dk_ref[:, sl] += _dot(ds, qh, _TN)

    return pl.pallas_call(
        body, name="xattn_bwd", grid=(t // tq,),
        in_specs=[_row(tq, d), _const((mlen, d)), _const((mlen, d)), _row(tq, d)],
        out_specs=[_row(tq, d), _const((mlen, d)), _const((mlen, d))],
        out_shape=[_sds((t, d), MXU), _sds((mlen, d), F32), _sds((mlen, d), F32)],
        compiler_params=_params("arbitrary"),
    )(q, k, v, do)


def _lane_view(a):
    if a.ndim >= 2 and a.shape[-1] >= 128:
        return a.reshape(-1, a.shape[-1])
    if a.size % 128 == 0:
        return a.reshape(-1, 128)
    return a.reshape(1, -1)


def _adamw(w, g, m, v, *, name):
    shape = w.shape
    w2, g2, m2, v2 = (_lane_view(a) for a in (w, g.reshape(shape), m, v))
    r, c = w2.shape
    tr = _tile(r, max(8, (1 << 18) // max(c, 128)), 8)

    def body(w_ref, g_ref, m_ref, v_ref, d_ref, mo_ref, vo_ref):
        gv = g_ref[...]
        mn = ADAM_B1 * m_ref[...] + (1.0 - ADAM_B1) * gv
        vn = ADAM_B2 * v_ref[...] + (1.0 - ADAM_B2) * (gv * gv)
        m_hat = mn / (1.0 - ADAM_B1 ** ADAM_STEP)
        v_hat = vn / (1.0 - ADAM_B2 ** ADAM_STEP)
        d_ref[...] = -ADAM_LR * (m_hat / (jnp.sqrt(v_hat) + ADAM_EPS) + ADAM_WD * w_ref[...])
        mo_ref[...] = mn
        vo_ref[...] = vn

    outs = pl.pallas_call(
        body, name=name, grid=(r // tr,), in_specs=[_row(tr, c)] * 4, out_specs=[_row(tr, c)] * 3,
        out_shape=[_sds((r, c), F32)] * 3, compiler_params=_params("parallel"),
    )(w2, g2, m2, v2)
    return tuple(o.reshape(shape) for o in outs)


_HBM = pl.BlockSpec(memory_space=pltpu.HBM)


N_CHIPS = 4
CHIPS = ((0, 0), (0, 1), (1, 0), (1, 1))


def _pair_exchange(x, *, name):
    def body(x_ref, o_ref, send_sems, recv_sems):
        xx, yy, cc = (lax.axis_index(a) for a in AXES)
        copies = [
            pltpu.make_async_remote_copy(
                src_ref=x_ref.at[4 * px + 2 * py + (1 - cc)], dst_ref=o_ref.at[k], send_sem=send_sems.at[k],
                recv_sem=recv_sems.at[k], device_id=(xx, yy, 1 - cc), device_id_type=pl.DeviceIdType.MESH)
            for k, (px, py) in enumerate(CHIPS)]
        for cp in copies:
            cp.start()
        for cp in copies:
            cp.wait_recv()
        for cp in copies:
            cp.wait_send()

    return pl.pallas_call(
        body, name=name, in_specs=[_HBM], out_specs=_HBM, out_shape=_sds((N_CHIPS,) + tuple(x.shape[1:]), x.dtype),
        scratch_shapes=[pltpu.SemaphoreType.DMA((N_CHIPS,)), pltpu.SemaphoreType.DMA((N_CHIPS,))],
    )(x)


def _pair_add(x, got, *, name):
    _, r, c = x.shape
    tr = _tile(r, max(PAD_ROWS, (1 << 17) // c), PAD_ROWS)

    def body(x_ref, g_ref, o_ref):
        mine = jnp.where(lax.axis_index("c") == 0, x_ref[0].astype(F32), x_ref[1].astype(F32))
        o_ref[...] = (mine + g_ref[...].astype(F32)).astype(o_ref.dtype)

    return pl.pallas_call(
        body, name=name, grid=(N_CHIPS, r // tr),
        in_specs=[pl.BlockSpec((None, 2, tr, c), lambda k, i: (k, 0, i, 0)), pl.BlockSpec((None, tr, c), lambda k, i: (k, i, 0))],
        out_specs=pl.BlockSpec((None, tr, c), lambda k, i: (k, i, 0)), out_shape=_sds((N_CHIPS, r, c), x.dtype),
        compiler_params=_params("parallel", "parallel"),
    )(x.reshape(N_CHIPS, 2, r, c), got)


def _chip_exchange(x, *, name):
    def body(x_ref, o_ref, send_sems, recv_sems, local_sem):
        xx, yy, cc = (lax.axis_index(a) for a in AXES)
        mine = 2 * xx + yy
        local = pltpu.make_async_copy(x_ref.at[mine], o_ref.at[mine], local_sem)
        local.start()
        sends = []
        for j, (px, py) in enumerate([(1 - xx, yy), (xx, 1 - yy), (1 - xx, 1 - yy)]):
            cp = pltpu.make_async_remote_copy(
                src_ref=x_ref.at[2 * px + py], dst_ref=o_ref.at[mine], send_sem=send_sems.at[j], recv_sem=recv_sems.at[j],
                device_id=(px, py, cc), device_id_type=pl.DeviceIdType.MESH)
            cp.start()
            sends.append(cp)
        for j, (px, py) in enumerate([(1 - xx, yy), (xx, 1 - yy), (1 - xx, 1 - yy)]):
            pltpu.make_async_remote_copy(
                src_ref=x_ref.at[2 * px + py], dst_ref=o_ref.at[2 * px + py], send_sem=send_sems.at[j],
                recv_sem=recv_sems.at[j], device_id=(px, py, cc), device_id_type=pl.DeviceIdType.MESH).wait_recv()
        for cp in sends:
            cp.wait_send()
        local.wait()

    return pl.pallas_call(
        body, name=name, in_specs=[_HBM], out_specs=_HBM, out_shape=_sds(x.shape, x.dtype),
        scratch_shapes=[pltpu.SemaphoreType.DMA((N_CHIPS - 1,)), pltpu.SemaphoreType.DMA((N_CHIPS - 1,)),
                        pltpu.SemaphoreType.DMA(())],
    )(x)


def _reduce_scatter(x, *, name):
    chip_sums = _pair_add(x, _pair_exchange(x, name=name + "_pair"), name=name + "_pair_add")
    return _sum_slots(_chip_exchange(chip_sums, name=name + "_chips"), name=name + "_sum")


def _all_gather(x, *, name):
    def body(x_ref, o_ref, send_sems, recv_sems, local_sem):
        xx, yy, cc = (lax.axis_index(a) for a in AXES)
        me, sibling = (xx, yy, cc), (xx, yy, 1 - cc)
        chips = [(1 - xx, yy), (xx, 1 - yy), (1 - xx, 1 - yy)]

        def slot(px, py, pc):
            return o_ref.at[4 * px + 2 * py + pc]

        def copy(k, block, to, src=None):
            return pltpu.make_async_remote_copy(
                src_ref=slot(*block) if src is None else src, dst_ref=slot(*block), send_sem=send_sems.at[k],
                recv_sem=recv_sems.at[k], device_id=to, device_id_type=pl.DeviceIdType.MESH)

        local = pltpu.make_async_copy(x_ref, slot(*me), local_sem)
        local.start()
        first = [copy(0, me, sibling, src=x_ref)] + [copy(1 + j, me, (*chip, cc), src=x_ref) for j, chip in enumerate(chips)]
        for cp in first:
            cp.start()
        passed = [copy(4 + j, (*chip, cc), sibling) for j, chip in enumerate(chips)]
        for j, chip in enumerate(chips):
            copy(1 + j, (*chip, cc), me).wait_recv()
            passed[j].start()
        copy(0, sibling, me).wait_recv()
        for j, chip in enumerate(chips):
            copy(4 + j, (*chip, 1 - cc), me).wait_recv()
        for cp in first + passed:
            cp.wait_send()
        local.wait()

    return pl.pallas_call(
        body, name=name, in_specs=[_HBM], out_specs=_HBM, out_shape=_sds((N_DEV,) + tuple(x.shape), x.dtype),
        scratch_shapes=[pltpu.SemaphoreType.DMA((N_DEV - 1,)), pltpu.SemaphoreType.DMA((N_DEV - 1,)),
                        pltpu.SemaphoreType.DMA(())],
    )(x)


def _sum_slots(x, *, name):
    n, r, c = x.shape
    tr = _tile(r, max(PAD_ROWS, (1 << 17) // c), PAD_ROWS)

    def body(x_ref, o_ref):
        acc = x_ref[0].astype(F32)
        for d in range(1, n):
            acc = acc + x_ref[d].astype(F32)
        o_ref[...] = acc

    return pl.pallas_call(
        body, name=name, grid=(r // tr,), in_specs=[pl.BlockSpec((n, tr, c), lambda i: (0, i, 0))],
        out_specs=_row(tr, c), out_shape=_sds((r, c), F32), compiler_params=_params("parallel"),
    )(x)


def _s5_layouts(bbar_r, bbar_i, c_re, c_im):
    eye = jnp.eye(S5_GPB, dtype=F32)

    def b_blocks(bbar):
        bb = bbar.reshape(S5_NB, S5_GPB, S5_STATE, S5_GROUP)
        return jnp.einsum("jgph,gk->jghkp", bb, eye).reshape(S5_NB, S5_GPB * S5_GROUP, S5_GPB * S5_STATE)

    def c_blocks(cc):
        c4 = cc.reshape(S5_NB, S5_GPB, S5_GROUP, S5_STATE)
        return jnp.einsum("jghp,gk->jgpkh", c4, eye).reshape(S5_NB, S5_GPB * S5_STATE, S5_GPB * S5_GROUP)

    bre, bim, cre, cim = b_blocks(bbar_r), b_blocks(bbar_i), c_blocks(c_re), c_blocks(c_im)
    cast = lambda a: a.astype(MXU)
    sw = lambda a: jnp.swapaxes(a, 1, 2).astype(MXU)
    return dict(bre=cast(bre), bim=cast(bim), cre=cast(cre), cim=cast(cim), bre_t=sw(bre), bim_t=sw(bim), cre_t=sw(cre),
                cim_t=sw(cim))


def _b_diag(db):
    d5 = db.reshape(S5_NB, S5_GPB, S5_GROUP, S5_GPB, S5_STATE)
    diag = jnp.stack([d5[:, g, :, g, :] for g in range(S5_GPB)], axis=1)
    return jnp.swapaxes(diag, 2, 3).reshape(S5_GROUPS, S5_STATE * S5_GROUP)


def _c_diag(dc):
    d5 = dc.reshape(S5_NB, S5_GPB, S5_STATE, S5_GPB, S5_GROUP)
    diag = jnp.stack([d5[:, g, :, g, :] for g in range(S5_GPB)], axis=1)
    return jnp.swapaxes(diag, 2, 3).reshape(S5_GROUPS, S5_GROUP, S5_STATE)


def _head_rows(*vecs):
    par = jnp.zeros((8, HP), F32)
    for i, v in enumerate(vecs):
        par = par.at[i, :SSD_HEADS].set(v.astype(F32))
    return par


def _add(acc, r):
    return (acc + r,)


def _layer_fwd(x, mem, w, consts):
    s = {"x": x}
    rep = consts["rep"]
    s["h1"] = h1 = _rmsnorm_fwd(x, w["norm_mix"], name="norm_mix_fwd")
    s["proj"] = proj = _mm(h1, w["w_main"], name="in_proj")
    s["dtr"] = dtr = _mm(h1, w["w_dt"], name="dt_proj")
    ar, ai, ldt = w["s5_a_re"], w["s5_a_im"], w["s5_log_dt"].reshape(S5_GROUPS, 1)
    br, bi = w["s5_b_re"].reshape(S5_GROUPS, -1), w["s5_b_im"].reshape(S5_GROUPS, -1)
    abar_r, abar_i, bbar_r, bbar_i = _s5_prep(ar, ai, ldt, br, bi, rep)
    s["abar"] = abar = (abar_r.reshape(1, S5_CH), abar_i.reshape(1, S5_CH))
    s["lay"] = lay = _s5_layouts(bbar_r, bbar_i, w["s5_c_re"], w["s5_c_im"])
    s["u"] = u = _interleave_rows(proj[:, :D_S5])
    s["sr"], s["si"], s["ypre"], s["yg"] = _s5_fwd(u, *abar, lay["bre"], lay["bim"], lay["cre"], lay["cim"],
                                                   w["s5_d"].reshape(1, D_S5))
    ypre, yg = s["ypre"], s["yg"]
    s["gp"], out_s5 = _mm(yg, w["s5_w_glu"], extras=[ypre], epi=lambda acc, yp: (acc, _gelu(yp) * _sigmoid(acc)),
                          out_dtypes=(F32, MXU), name="s5_glu")
    out_s5 = _deinterleave_rows(out_s5)
    s["par"] = par = _head_rows(w["ssd_dt_bias"], w["ssd_a_log"], w["ssd_d"])
    out_ssd, s["y_ssd"], s["states"], s["conv_pre"] = _ssd_fwd(proj, w["ssd_conv_w"], w["ssd_conv_b"], dtr, par,
                                                               w["ssd_norm"], consts["ssd"])
    s["ycat"] = ycat = jnp.concatenate([out_s5, out_ssd], axis=1)
    s["x1"] = x1 = _mm(ycat, w["w_out"], extras=[x], epi=_add, name="out_proj")
    s["hq"] = hq = _rmsnorm_fwd(x1, w["norm_xattn"], name="norm_xattn_fwd")
    s["mn"] = mn = _rmsnorm_fwd(mem, w["norm_mem"], name="norm_mem_fwd")
    s["q"] = q = _mm(hq, w["xa_wq"], out_dtypes=(MXU,), name="xa_q")
    s["k"] = k = _mm(mn, w["xa_wk"], out_dtypes=(MXU,), name="xa_k")
    s["v"] = v = _mm(mn, w["xa_wv"], out_dtypes=(MXU,), name="xa_v")
    s["o"] = o = _attn_fwd(q, k, v)
    s["x2"] = x2 = _mm(o, w["xa_wo"], extras=[x1], epi=_add, name="xa_o")
    s["hm"] = hm = _rmsnorm_fwd(x2, w["norm_mlp"], name="norm_mlp_fwd")
    s["act"] = _mm(hm, w["mlp_w1"], epi=lambda acc: (jnp.square(jnp.maximum(acc, 0.0)),), out_dtypes=(MXU,),
                   name="mlp_up")
    x3 = _mm(s["act"], w["mlp_w2"], extras=[x2], epi=_add, name="mlp_down")
    return x3, s


def _layer_bwd(dx3, mem, w, s, consts):
    g = {}
    rep = consts["rep"]
    d_a = _mm(dx3, w["mlp_w2"], tb=True, extras=[s["act"]],
              epi=lambda acc, act: (acc * (2.0 * jnp.sqrt(act.astype(F32))),), out_dtypes=(MXU,), name="mlp_down_dx")
    g["mlp_w2"] = _mm(s["act"], dx3, ta=True, name="mlp_down_dw")
    g["mlp_w1"] = _mm(s["hm"], d_a, ta=True, name="mlp_up_dw")
    d_hm = _mm(d_a, w["mlp_w1"], tb=True, name="mlp_up_dx")
    dx2, g["norm_mlp"] = _rmsnorm_bwd(s["x2"], w["norm_mlp"], d_hm, dx3, name="norm_mlp_bwd")
    d_o = _mm(dx2, w["xa_wo"], tb=True, out_dtypes=(MXU,), name="xa_o_dx")
    g["xa_wo"] = _mm(s["o"], dx2, ta=True, name="xa_o_dw")
    dq, dk, dv = _attn_bwd(s["q"], s["k"], s["v"], d_o)
    g["xa_wq"] = _mm(s["hq"], dq, ta=True, name="xa_q_dw")
    d_hq = _mm(dq, w["xa_wq"], tb=True, name="xa_q_dx")
    dx1, g["norm_xattn"] = _rmsnorm_bwd(s["x1"], w["norm_xattn"], d_hq, dx2, name="norm_xattn_bwd")
    g["xa_wk"] = _mm(s["mn"], dk, ta=True, name="xa_k_dw")
    g["xa_wv"] = _mm(s["mn"], dv, ta=True, name="xa_v_dw")
    d_mn_v = _mm(dv, w["xa_wv"], tb=True, name="xa_v_dx")
    d_mn = _mm(dk, w["xa_wk"], tb=True, extras=[d_mn_v], epi=_add, name="xa_k_dx")
    _, g["norm_mem"] = _rmsnorm_bwd(mem, w["norm_mem"], d_mn, None, name="norm_mem_bwd")
    d_ycat = _mm(dx1, w["w_out"], tb=True, name="out_proj_dx")
    g["w_out"] = _mm(s["ycat"], dx1, ta=True, name="out_proj_dw")
    lay, proj, ypre, u = s["lay"], s["proj"], s["ypre"], s["u"]
    d_os5 = _interleave_rows(d_ycat[:, :D_S5])
    d_gp = _s5_gate_bwd(d_os5, s["gp"], ypre)
    g["s5_w_glu"] = _mm(s["yg"], d_gp, ta=True, name="s5_glu_dw")
    d_ypre = _mm(d_gp, w["s5_w_glu"], tb=True, extras=[d_os5, s["gp"], ypre],
                 epi=lambda acc, do, gp, yp: ((acc + do * _sigmoid(gp)) * _gelu_grad(yp),), name="s5_glu_dx")
    du, d_abar_r, d_abar_i, db_re, db_im, dc_re, dc_im, d_d = _s5_bwd(
        d_ypre, u, s["sr"], s["si"], *s["abar"], lay["cre_t"], lay["cim_t"], lay["bre_t"], lay["bim_t"],
        w["s5_d"].reshape(1, D_S5))
    d_bbar_r, d_bbar_i = _b_diag(db_re), _b_diag(db_im)
    g["s5_c_re"], g["s5_c_im"] = _c_diag(dc_re), -_c_diag(dc_im)
    g["s5_d"] = d_d.reshape(S5_GROUPS, S5_GROUP)
    du = _deinterleave_rows(du)
    ar, ai, ldt = w["s5_a_re"], w["s5_a_im"], w["s5_log_dt"].reshape(S5_GROUPS, 1)
    br, bi = w["s5_b_re"].reshape(S5_GROUPS, -1), w["s5_b_im"].reshape(S5_GROUPS, -1)
    d_ar, d_ai, d_ldt, d_br, d_bi = _s5_prep_bwd(
        ar, ai, ldt, br, bi, rep, d_abar_r.reshape(S5_GROUPS, S5_STATE), d_abar_i.reshape(S5_GROUPS, S5_STATE),
        d_bbar_r, d_bbar_i)
    g["s5_a_re"], g["s5_a_im"], g["s5_log_dt"] = d_ar, d_ai, d_ldt.reshape(S5_GROUPS)
    g["s5_b_re"] = d_br.reshape(S5_GROUPS, S5_STATE, S5_GROUP)
    g["s5_b_im"] = d_bi.reshape(S5_GROUPS, S5_STATE, S5_GROUP)
    d_cp, dz, d_dtr, g_ssd_norm, d_par = _ssd_bwd(proj, s["conv_pre"], s["dtr"], s["par"], w["ssd_norm"], s["y_ssd"],
                                                  s["states"], d_ycat, consts["ssd"])
    g["ssd_norm"] = g_ssd_norm
    g["ssd_dt_bias"], g["ssd_a_log"], g["ssd_d"] = (d_par[i, :SSD_HEADS] for i in range(3))
    d_xbc, g["ssd_conv_w"], g["ssd_conv_b"] = _conv_bwd(proj, w["ssd_conv_w"], d_cp)
    d_proj = jnp.concatenate([du, dz, d_xbc], axis=1)
    g_main = _mm(s["h1"], d_proj, ta=True, name="in_proj_dw")
    g_dt = _mm(s["h1"], d_dtr, ta=True, name="dt_proj_dw")
    g["w_in"] = jnp.concatenate([g_main, g_dt[:, :SSD_HEADS]], axis=1)
    d_h1_dt = _mm(d_dtr, w["w_dt"], tb=True, name="dt_proj_dx")
    d_h1 = _mm(d_proj, w["w_main"], tb=True, extras=[d_h1_dt], epi=_add, name="in_proj_dx")
    dx, g["norm_mix"] = _rmsnorm_bwd(s["x"], w["norm_mix"], d_h1, dx1, name="norm_mix_bwd")
    return dx, g


LAYER_WEIGHTS = ("norm_mix", "w_in", "s5_a_re", "s5_a_im", "s5_log_dt", "s5_b_re", "s5_b_im", "s5_c_re", "s5_c_im", "s5_d",
                 "s5_w_glu", "ssd_conv_w", "ssd_conv_b", "ssd_dt_bias", "ssd_a_log", "ssd_d", "ssd_norm", "w_out",
                 "norm_xattn", "norm_mem", "xa_wq", "xa_wk", "xa_wv", "xa_wo", "norm_mlp", "mlp_w1", "mlp_w2")
WEIGHTS = LAYER_WEIGHTS + ("norm_final",)


def _local_step(x, mem, target, weights):
    consts = {
        "ssd": _ssd_consts(),
        "rep": (jnp.arange(S5_STATE)[:, None] == jnp.arange(S5_STATE * S5_GROUP)[None, :] // S5_GROUP).astype(F32),
    }
    layers = []
    for l in range(DEPTH):
        w = {n: weights[n][l] for n in LAYER_WEIGHTS}
        w_in = w["w_in"]
        w["w_main"] = w_in[:, :D_MAIN]
        w["w_dt"] = jnp.pad(w_in[:, D_MAIN:], ((0, 0), (0, HP - SSD_HEADS)))
        layers.append(w)
    saved = []
    for l in range(DEPTH):
        x, s = _layer_fwd(x, mem, layers[l], consts)
        saved.append(s)
    loss, dx, g_final = _loss_head(x, weights["norm_final"], target)
    grads = [None] * DEPTH
    for l in reversed(range(DEPTH)):
        dx, grads[l] = _layer_bwd(dx, mem, layers[l], saved[l], consts)
    out = {n: jnp.stack([grads[l][n].reshape(weights[n].shape[1:]) for l in range(DEPTH)]) for n in LAYER_WEIGHTS}
    out["norm_final"] = g_final.reshape(weights["norm_final"].shape)
    return loss, dx, out


SHARDED = {"w_in": 2, "s5_w_glu": 1, "ssd_conv_w": 2, "w_out": 1, "xa_wq": 1, "xa_wk": 1, "xa_wv": 1, "xa_wo": 1,
           "mlp_w1": 2, "mlp_w2": 1}
EXACT = ("ssd_conv_w",)
ROW_EXCHANGE = tuple(n for n, ax in SHARDED.items() if ax == 1)
OWN_EXCHANGE = tuple(n for n in SHARDED if n not in ROW_EXCHANGE)
REPLICATED = tuple(n for n in WEIGHTS if n not in SHARDED)
LANES = 128
PAD_ROWS = 16


def _to_rows(flat, lead=()):
    n = flat.shape[-1]
    quantum = LANES * PAD_ROWS
    padded = -(-n // quantum) * quantum
    flat = jnp.pad(flat, [(0, 0)] * len(lead) + [(0, padded - n)])
    return flat.reshape(*lead, padded // LANES, LANES)


def _gather_weights(local):
    def assemble(n, seg):
        shp, ax = local[n].shape, SHARDED[n]
        return jnp.moveaxis(seg, 0, ax).reshape(*shp[:ax], N_DEV * shp[ax], *shp[ax + 1:])

    full = {}
    for n in OWN_EXCHANGE:
        payload = local[n] if n in EXACT else local[n].astype(MXU)
        full[n] = assemble(n, _all_gather(payload, name="gather_" + n))
    got = _all_gather(jnp.concatenate([local[n].astype(MXU) for n in ROW_EXCHANGE], axis=1), name="gather_row_sharded")
    off = 0
    for n in ROW_EXCHANGE:
        rows = local[n].shape[1]
        full[n] = assemble(n, got[:, :, off:off + rows])
        off += rows
    return full


def _scatter_grads(grads, local_shapes):
    def shards(n):
        shp, ax = local_shapes[n], SHARDED[n]
        gfull = grads[n].reshape(*shp[:ax], N_DEV, shp[ax], *shp[ax + 1:])
        return jnp.moveaxis(gfull, ax, 0).astype(MXU)

    out = {}
    for n in OWN_EXCHANGE:
        shp = local_shapes[n]
        out[n] = _reduce_scatter(shards(n).reshape(N_DEV, -1, shp[-1]), name="scatter_" + n).reshape(shp)
    payload = jnp.concatenate([shards(n) for n in ROW_EXCHANGE], axis=2)
    summed = _reduce_scatter(payload.reshape(N_DEV, -1, payload.shape[-1]), name="scatter_row_sharded")
    summed = summed.reshape(payload.shape[1:])
    off = 0
    for n in ROW_EXCHANGE:
        rows = local_shapes[n][1]
        out[n] = summed[:, off:off + rows]
        off += rows
    return out


def _allreduce_small(loss, grads):
    parts = [loss.reshape(-1)[:1]] + [grads[n].reshape(-1) for n in REPLICATED]
    payload = _to_rows(jnp.concatenate(parts))
    summed = _sum_slots(_all_gather(payload, name="gather_small_grads"), name="sum_small_grads").reshape(-1)
    out, off = {}, 1
    for n in REPLICATED:
        size = grads[n].size
        out[n] = summed[off:off + size].reshape(grads[n].shape)
        off += size
    return summed[0], out


def kernel(x, mem, norm_mix, w_in, s5_a_re, s5_a_im, s5_log_dt, s5_b_re, s5_b_im, s5_c_re, s5_c_im, s5_d, s5_w_glu, ssd_conv_w, ssd_conv_b, ssd_dt_bias, ssd_a_log, ssd_d, ssd_norm, w_out, norm_xattn, norm_mem, xa_wq, xa_wk, xa_wv, xa_wo, norm_mlp, mlp_w1, mlp_w2, norm_final, loss_target, m_norm_mix, m_w_in, m_s5_a_re, m_s5_a_im, m_s5_log_dt, m_s5_b_re, m_s5_b_im, m_s5_c_re, m_s5_c_im, m_s5_d, m_s5_w_glu, m_ssd_conv_w, m_ssd_conv_b, m_ssd_dt_bias, m_ssd_a_log, m_ssd_d, m_ssd_norm, m_w_out, m_norm_xattn, m_norm_mem, m_xa_wq, m_xa_wk, m_xa_wv, m_xa_wo, m_norm_mlp, m_mlp_w1, m_mlp_w2, m_norm_final, v_norm_mix, v_w_in, v_s5_a_re, v_s5_a_im, v_s5_log_dt, v_s5_b_re, v_s5_b_im, v_s5_c_re, v_s5_c_im, v_s5_d, v_s5_w_glu, v_ssd_conv_w, v_ssd_conv_b, v_ssd_dt_bias, v_ssd_a_log, v_ssd_d, v_ssd_norm, v_w_out, v_norm_xattn, v_norm_mem, v_xa_wq, v_xa_wk, v_xa_wv, v_xa_wo, v_norm_mlp, v_mlp_w1, v_mlp_w2, v_norm_final):
    args = locals()
    local = {n: args[n] for n in WEIGHTS}
    full = dict(local)
    full.update(_gather_weights(local))
    loss, grad_x, grads = _local_step(x[0], mem[0], loss_target[0], full)
    loss, g_small = _allreduce_small(loss, grads)
    g_all = _scatter_grads(grads, {n: local[n].shape for n in SHARDED})
    g_all.update(g_small)
    delta, new_m, new_v = {}, {}, {}
    for n in WEIGHTS:
        delta[n], new_m[n], new_v[n] = _adamw(local[n], g_all[n], args["m_" + n], args["v_" + n], name="adamw_" + n)
    return (loss, grad_x[None], *[g_all[n] for n in WEIGHTS], *[delta[n] for n in WEIGHTS],
            *[new_m[n] for n in WEIGHTS], *[new_v[n] for n in WEIGHTS])
```

```python
import functools
import math

import jax
import jax.numpy as jnp
from jax import lax
from jax.experimental import pallas as pl
from jax.experimental.pallas import tpu as pltpu

F32 = jnp.float32
MXU = jnp.bfloat16
HI = lax.Precision.HIGHEST

D_MODEL = 1024
DEPTH = 4
MEM_LEN = 256
D_S5 = 1024
D_SSD = 1024
S5_GROUP = 16
S5_GROUPS = 64
S5_STATE = 64
S5_CH = S5_GROUPS * S5_STATE
S5_NB = 4
S5_GPB = S5_GROUPS // S5_NB
SSD_HEADDIM = 64
SSD_HEADS = 16
SSD_GROUPS = 4
SSD_STATE = 128
SSD_CONV = 4
SSD_CHUNK = 128
SSD_BC = SSD_GROUPS * SSD_STATE
D_CONV_CH = 2048
D_MAIN = 4096
D_IN_PROJ = D_MAIN + SSD_HEADS
HP = 128
XA_HEADS = 4
XA_HEAD_DIM = 256
D_FF = 4096
EPS = 1e-5
N_DEV = 8
AXES = ("x", "y", "c")

ADAM_LR = 0.001
ADAM_B1 = 0.9
ADAM_B2 = 0.999
ADAM_EPS = 1e-08
ADAM_WD = 0.01
ADAM_STEP = 10

VMEM_LIMIT = 56 * 1024 * 1024


def _params(*sem):
    return pltpu.CompilerParams(dimension_semantics=sem, vmem_limit_bytes=VMEM_LIMIT)


def _tile(n, pref, quantum=128):
    t = (min(pref, n) // quantum) * quantum
    while t >= quantum:
        if n % t == 0:
            return t
        t -= quantum
    return n


def _sds(shape, dtype):
    return jax.ShapeDtypeStruct(tuple(shape), dtype)


def _recip(d):
    r = pl.reciprocal(d, approx=True)
    return r * (2.0 - d * r)


def _sigmoid(x):
    return _recip(1.0 + jnp.exp(-jnp.maximum(x, -80.0)))


def _silu(x):
    return x * _sigmoid(x)


def _silu_grad(x):
    s = _sigmoid(x)
    return s * (1.0 + x * (1.0 - s))


_GELU_C = math.sqrt(2.0 / math.pi)


def _gelu(x):
    return 0.5 * x * (1.0 + jnp.tanh(_GELU_C * (x + 0.044715 * x * x * x)))


def _gelu_grad(x):
    th = jnp.tanh(_GELU_C * (x + 0.044715 * x * x * x))
    return 0.5 * (1.0 + th) + 0.5 * x * (1.0 - th * th) * _GELU_C * (1.0 + 3.0 * 0.044715 * x * x)


def _softplus(x):
    return jnp.maximum(x, 0.0) + jnp.log(1.0 + jnp.exp(-jnp.abs(x)))


_NN = (((1,), (0,)), ((), ()))
_NT = (((1,), (1,)), ((), ()))
_TN = (((0,), (0,)), ((), ()))


def _dot(a, b, dims=_NN, precision=None):
    return lax.dot_general(a, b, dims, precision=precision, preferred_element_type=F32)


def _dot_split(x, w):
    hi = x.astype(MXU)
    lo = (x - hi.astype(F32)).astype(MXU)
    return _dot(hi, w) + _dot(lo, w)


def _rows8(v):
    return jnp.broadcast_to(v, (8, v.shape[1]))


def _mm(a, b, *, ta=False, tb=False, extras=(), epi=None, out_dtypes=(F32,), tm=1024, tn=1024, tk=1024, name):
    m, k = (a.shape[1], a.shape[0]) if ta else a.shape
    n = b.shape[0] if tb else b.shape[1]
    assert k == (b.shape[1] if tb else b.shape[0]), (a.shape, b.shape, ta, tb)
    tm, tn, tk = _tile(m, tm), _tile(n, tn), _tile(k, tk)
    nk = k // tk
    n_ex, n_out = len(extras), len(out_dtypes)
    dims = (((0,) if ta else (1,), (1,) if tb else (0,)), ((), ()))

    def body(a_ref, b_ref, *rest):
        ex_refs, out_refs = rest[:n_ex], rest[n_ex:n_ex + n_out]
        prod = _dot(a_ref[...].astype(MXU), b_ref[...].astype(MXU), dims)

        def finish(total):
            outs = epi(total, *[e[...] for e in ex_refs]) if epi is not None else (total,)
            for o, r in zip(outs, out_refs, strict=True):
                r[...] = o.astype(r.dtype)

        if nk == 1:
            finish(prod)
            return
        acc = rest[-1]
        kk = pl.program_id(2)

        @pl.when(kk == 0)
        def _():
            acc[...] = prod

        @pl.when(jnp.logical_and(kk > 0, kk < nk - 1))
        def _():
            acc[...] += prod

        @pl.when(kk == nk - 1)
        def _():
            finish(acc[...] + prod)

    a_spec = pl.BlockSpec((tk, tm), lambda i, j, kk: (kk, i)) if ta else pl.BlockSpec((tm, tk), lambda i, j, kk: (i, kk))
    b_spec = pl.BlockSpec((tn, tk), lambda i, j, kk: (j, kk)) if tb else pl.BlockSpec((tk, tn), lambda i, j, kk: (kk, j))
    mn_spec = pl.BlockSpec((tm, tn), lambda i, j, kk: (i, j))
    outs = pl.pallas_call(
        body,
        name=name,
        grid=(m // tm, n // tn, nk),
        in_specs=[a_spec, b_spec] + [mn_spec] * n_ex,
        out_specs=[mn_spec] * n_out,
        out_shape=[_sds((m, n), dt) for dt in out_dtypes],
        scratch_shapes=[pltpu.VMEM((tm, tn), F32)] if nk > 1 else [],
        compiler_params=_params("parallel", "parallel", "arbitrary"),
    )(a, b, *extras)
    return outs[0] if n_out == 1 else outs


def _row(tb, w, cb=0):
    return pl.BlockSpec((tb, w), lambda i: (i, cb))


def _const(shape):
    return pl.BlockSpec(shape, lambda i: (0,) * len(shape))


def _rmsnorm_fwd(x, g, *, name):
    t, d = x.shape
    tb = _tile(t, 512, 8)

    def body(x_ref, g_ref, h_ref):
        xv = x_ref[...]
        r = lax.rsqrt(jnp.mean(xv * xv, axis=-1, keepdims=True) + EPS)
        h_ref[...] = (xv * r * g_ref[...]).astype(h_ref.dtype)

    return pl.pallas_call(
        body, name=name, grid=(t // tb,), in_specs=[_row(tb, d), _const((1, d))], out_specs=_row(tb, d),
        out_shape=_sds((t, d), MXU), compiler_params=_params("parallel"),
    )(x, g.reshape(1, d))


def _rmsnorm_bwd(x, g, dh, dres, *, name):
    t, d = x.shape
    tb = _tile(t, 256, 8)
    has_res = dres is not None

    def body(x_ref, g_ref, dh_ref, *rest):
        dx_ref, dg_ref = rest[-2:]

        @pl.when(pl.program_id(0) == 0)
        def _():
            dg_ref[...] = jnp.zeros_like(dg_ref)

        xv = x_ref[...]
        r = lax.rsqrt(jnp.mean(xv * xv, axis=-1, keepdims=True) + EPS)
        xh = xv * r
        dhv = dh_ref[...].astype(F32)
        dg_ref[...] += jnp.sum(dhv * xh, axis=0, keepdims=True)
        dxh = dhv * g_ref[...]
        dx = r * (dxh - xh * jnp.mean(dxh * xh, axis=-1, keepdims=True))
        if has_res:
            dx = dx + rest[0][...]
        dx_ref[...] = dx

    ins = [x, g.reshape(1, d), dh] + ([dres] if has_res else [])
    return pl.pallas_call(
        body, name=name, grid=(t // tb,),
        in_specs=[_row(tb, d), _const((1, d)), _row(tb, d)] + ([_row(tb, d)] if has_res else []),
        out_specs=[_row(tb, d), _const((1, d))],
        out_shape=[_sds((t, d), F32), _sds((1, d), F32)],
        compiler_params=_params("arbitrary"),
    )(*ins)


def _loss_head(x, g, target):
    t, d = x.shape
    tb = _tile(t, 256, 8)

    def body(x_ref, g_ref, tg_ref, loss_ref, dx_ref, dg_ref):
        @pl.when(pl.program_id(0) == 0)
        def _():
            dg_ref[...] = jnp.zeros_like(dg_ref)
            loss_ref[...] = jnp.zeros_like(loss_ref)

        xv, gv = x_ref[...], g_ref[...]
        r = lax.rsqrt(jnp.mean(xv * xv, axis=-1, keepdims=True) + EPS)
        xh = xv * r
        err = xh * gv - tg_ref[...]
        loss_ref[...] += 0.5 * jnp.sum(jnp.mean(err * err, axis=-1, keepdims=True), axis=0, keepdims=True)
        dy = err * (1.0 / d)
        dg_ref[...] += jnp.sum(dy * xh, axis=0, keepdims=True)
        dxh = dy * gv
        dx_ref[...] = r * (dxh - xh * jnp.mean(dxh * xh, axis=-1, keepdims=True))

    return pl.pallas_call(
        body, name="loss_head", grid=(t // tb,),
        in_specs=[_row(tb, d), _const((1, d)), _row(tb, d)],
        out_specs=[_const((1, HP)), _row(tb, d), _const((1, d))],
        out_shape=[_sds((1, HP), F32), _sds((t, d), F32), _sds((1, d), F32)],
        compiler_params=_params("arbitrary"),
    )(x, g.reshape(1, d), target)


def _s5_discretise(ar, ai, ldt, br, bi, rep):
    dt = jnp.exp(ldt)
    mag = jnp.exp(dt * ar)
    abar_r, abar_i = mag * jnp.cos(dt * ai), mag * jnp.sin(dt * ai)
    den = ar * ar + ai * ai
    zr, zi = abar_r - 1.0, abar_i
    fr = (zr * ar + zi * ai) / den
    fi = (zi * ar - zr * ai) / den
    fr_e, fi_e = _dot(fr, rep, precision=HI), _dot(fi, rep, precision=HI)
    return abar_r, abar_i, fr_e * br - fi_e * bi, fr_e * bi + fi_e * br


def _s5_prep(ar, ai, ldt, br, bi, rep):
    g, p = ar.shape
    ph = br.shape[1]

    def body(ar_ref, ai_ref, ldt_ref, br_ref, bi_ref, rep_ref, o0, o1, o2, o3):
        outs = _s5_discretise(ar_ref[...], ai_ref[...], ldt_ref[...], br_ref[...], bi_ref[...], rep_ref[...])
        for o, v in zip((o0, o1, o2, o3), outs):
            o[...] = v

    return pl.pallas_call(
        body, name="s5_prep",
        out_shape=[_sds((g, p), F32), _sds((g, p), F32), _sds((g, ph), F32), _sds((g, ph), F32)],
        compiler_params=pltpu.CompilerParams(vmem_limit_bytes=VMEM_LIMIT),
    )(ar, ai, ldt, br, bi, rep)


def _s5_prep_bwd(ar, ai, ldt, br, bi, rep, d_abar_r, d_abar_i, d_bbar_r, d_bbar_i):
    g, p = ar.shape
    ph = br.shape[1]

    def body(ar_ref, ai_ref, ldt_ref, br_ref, bi_ref, rep_ref, c0, c1, c2, c3, o0, o1, o2, o3, o4):
        rep_v = rep_ref[...]
        _, vjp = jax.vjp(lambda a, b, c, d, e: _s5_discretise(a, b, c, d, e, rep_v),
                         ar_ref[...], ai_ref[...], ldt_ref[...], br_ref[...], bi_ref[...])
        grads = vjp((c0[...], c1[...], c2[...], c3[...]))
        for o, v in zip((o0, o1, o2, o3, o4), grads):
            o[...] = v

    return pl.pallas_call(
        body, name="s5_prep_bwd",
        out_shape=[_sds((g, p), F32), _sds((g, p), F32), _sds((g, 1), F32), _sds((g, ph), F32), _sds((g, ph), F32)],
        compiler_params=pltpu.CompilerParams(vmem_limit_bytes=VMEM_LIMIT),
    )(ar, ai, ldt, br, bi, rep, d_abar_r, d_abar_i, d_bbar_r, d_bbar_i)


SCAN_ROWS = 512


def _scan_rows(t):
    return _tile(t, SCAN_ROWS, 64)


def _interleave_rows(x):
    t, c = x.shape
    tb = _scan_rows(t)
    return x.reshape(t // tb, 8, tb // 8, c).swapaxes(1, 2).reshape(t, c)


def _deinterleave_rows(x):
    t, c = x.shape
    tb = _scan_rows(t)
    return x.reshape(t // tb, tb // 8, 8, c).swapaxes(1, 2).reshape(t, c)


def _cmul(ar, ai, br, bi):
    return ar * br - ai * bi, ar * bi + ai * br


def _segment_carries(fr, fi, ar8, ai8, c_r, c_i, seg, reverse):
    pr, pi = ar8, ai8
    for _ in range(int(math.log2(seg))):
        pr, pi = _cmul(pr, pi, pr, pi)
    row = lax.broadcasted_iota(jnp.int32, fr.shape, 0)
    edge = 7 if reverse else 0
    qr, qi = _cmul(pr, pi, c_r, c_i)
    xr, xi = jnp.where(row == edge, fr + qr, fr), jnp.where(row == edge, fi + qi, fi)
    for sh in (1, 2, 4):
        if reverse:
            keep, amount = row < 8 - sh, 8 - sh
        else:
            keep, amount = row >= sh, sh
        qr, qi = jnp.where(keep, pltpu.roll(xr, amount, 0), 0.0), jnp.where(keep, pltpu.roll(xi, amount, 0), 0.0)
        tr, ti = _cmul(pr, pi, qr, qi)
        xr, xi = xr + tr, xi + ti
        pr, pi = _cmul(pr, pi, pr, pi)
    if reverse:
        in_r, in_i = jnp.where(row == 7, c_r, pltpu.roll(xr, 7, 0)), jnp.where(row == 7, c_i, pltpu.roll(xi, 7, 0))
        return in_r, in_i, xr[0:1, :], xi[0:1, :]
    in_r, in_i = jnp.where(row == 0, c_r, pltpu.roll(xr, 1, 0)), jnp.where(row == 0, c_i, pltpu.roll(xi, 1, 0))
    return in_r, in_i, xr[7:8, :], xi[7:8, :]


def _sweeps(ar8, ai8, dr, di, cr, ci, seg, reverse, emit):
    order = range(seg - 1, -1, -1) if reverse else range(seg)
    rows = lambda j: slice(j * 8, (j + 1) * 8)
    fr, fi = jnp.zeros(ar8.shape, F32), jnp.zeros(ar8.shape, F32)
    for j in order:
        tr, ti = _cmul(ar8, ai8, fr, fi)
        fr, fi = tr + dr[rows(j), :], ti + di[rows(j), :]
    s_r, s_i, out_r, out_i = _segment_carries(fr, fi, ar8, ai8, cr[...], ci[...], seg, reverse)
    cr[...] = out_r
    ci[...] = out_i
    for j in order:
        tr, ti = _cmul(ar8, ai8, s_r, s_i)
        s_r, s_i = tr + dr[rows(j), :], ti + di[rows(j), :]
        emit(j, s_r, s_i)


S5_BK, S5_BN = 256, 1024


def _s5_specs(tb, nt, reverse):
    t_of = (lambda s: nt - 1 - s) if reverse else (lambda s: s)
    return dict(
        small=pl.BlockSpec((tb, S5_BK), lambda c, s: (t_of(s), c)),
        wide=pl.BlockSpec((tb, S5_BN), lambda c, s: (t_of(s), c)),
        halo=pl.BlockSpec((8, S5_BN), lambda c, s: (jnp.maximum(t_of(s) * (tb // 8) - 1, 0), c)),
        vec_w=pl.BlockSpec((1, S5_BN), lambda c, s: (0, c)),
        vec_s=pl.BlockSpec((1, S5_BK), lambda c, s: (0, c)),
        w_in=pl.BlockSpec((None, S5_BK, S5_BN), lambda c, s: (c, 0, 0)),
        w_out=pl.BlockSpec((None, S5_BN, S5_BK), lambda c, s: (c, 0, 0)),
    )


def _s5_fwd(u, ar, ai, bre, bim, cre, cim, d_vec):
    t = u.shape[0]
    tb = _scan_rows(t)
    nt, seg, nb = t // tb, tb // 8, bre.shape[0]
    assert 1 << int(math.log2(seg)) == seg

    def body(u_ref, ar_ref, ai_ref, bre_ref, bim_ref, cre_ref, cim_ref, d_ref, sr_ref, si_ref, yp_ref, yg_ref,
             cr, ci, dr_s, di_s):
        @pl.when(pl.program_id(1) == 0)
        def _():
            cr[...] = jnp.zeros_like(cr)
            ci[...] = jnp.zeros_like(ci)

        uv = u_ref[...]
        ub = uv.astype(MXU)
        dr_s[...] = _dot(ub, bre_ref[...])
        di_s[...] = _dot(ub, bim_ref[...])
        ar8 = jnp.broadcast_to(ar_ref[...], (8, S5_BN))
        ai8 = jnp.broadcast_to(ai_ref[...], (8, S5_BN))

        def emit(j, s_r, s_i):
            sr_ref[j * 8:(j + 1) * 8, :] = s_r
            si_ref[j * 8:(j + 1) * 8, :] = s_i

        _sweeps(ar8, ai8, dr_s, di_s, cr, ci, seg, False, emit)
        yp = _dot(sr_ref[...].astype(MXU), cre_ref[...]) - _dot(si_ref[...].astype(MXU), cim_ref[...]) + d_ref[...] * uv
        yp_ref[...] = yp
        yg_ref[...] = _gelu(yp).astype(yg_ref.dtype)

    sp = _s5_specs(tb, nt, False)
    return pl.pallas_call(
        body, name="s5_fwd", grid=(nb, nt),
        in_specs=[sp["small"], sp["vec_w"], sp["vec_w"], sp["w_in"], sp["w_in"], sp["w_out"], sp["w_out"], sp["vec_s"]],
        out_specs=[sp["wide"], sp["wide"], sp["small"], sp["small"]],
        out_shape=[_sds((t, nb * S5_BN), F32), _sds((t, nb * S5_BN), F32), _sds((t, nb * S5_BK), F32),
                   _sds((t, nb * S5_BK), MXU)],
        scratch_shapes=[pltpu.VMEM((1, S5_BN), F32), pltpu.VMEM((1, S5_BN), F32), pltpu.VMEM((tb, S5_BN), F32),
                        pltpu.VMEM((tb, S5_BN), F32)],
        compiler_params=_params("parallel", "arbitrary"),
    )(u, ar, ai, bre, bim, cre, cim, d_vec)


def _s5_bwd(dyp, u, sr, si, ar, ai, cre_t, cim_t, bre_t, bim_t, d_vec):
    t = u.shape[0]
    tb = _scan_rows(t)
    nt, seg, nb = t // tb, tb // 8, cre_t.shape[0]

    def body(dyp_ref, u_ref, pr_ref, pi_ref, hr_ref, hi_ref, ar_ref, ai_ref, cre_ref, cim_ref, bre_ref, bim_ref, d_ref,
             du_ref, gr_ref, gi_ref, dbr_ref, dbi_ref, dcr_ref, dci_ref, dd_ref, cr, ci, dr_s, di_s, lr_s, li_s):
        step = pl.program_id(1)

        @pl.when(step == 0)
        def _():
            for r in (cr, ci, gr_ref, gi_ref, dbr_ref, dbi_ref, dcr_ref, dci_ref, dd_ref):
                r[...] = jnp.zeros_like(r)

        dyv, uv = dyp_ref[...], u_ref[...]
        dyb, ub = dyv.astype(MXU), uv.astype(MXU)
        dr_s[...] = _dot(dyb, cre_ref[...])
        di_s[...] = -_dot(dyb, cim_ref[...])
        ar8 = jnp.broadcast_to(ar_ref[...], (8, S5_BN))
        ai8 = jnp.broadcast_to(-ai_ref[...], (8, S5_BN))
        first_block = step == nt - 1
        row = lax.broadcasted_iota(jnp.int32, (8, S5_BN), 0)
        acc = [jnp.zeros((8, S5_BN), F32), jnp.zeros((8, S5_BN), F32)]

        def emit(j, s_r, s_i):
            lr_s[j * 8:(j + 1) * 8, :] = s_r
            li_s[j * 8:(j + 1) * 8, :] = s_i
            if j > 0:
                p_r, p_i = pr_ref[(j - 1) * 8:j * 8, :], pi_ref[(j - 1) * 8:j * 8, :]
            else:
                halo_r = jnp.where(first_block, 0.0, hr_ref[7:8, :])
                halo_i = jnp.where(first_block, 0.0, hi_ref[7:8, :])
                p_r = jnp.where(row == 0, halo_r, pltpu.roll(pr_ref[(seg - 1) * 8:seg * 8, :], 1, 0))
                p_i = jnp.where(row == 0, halo_i, pltpu.roll(pi_ref[(seg - 1) * 8:seg * 8, :], 1, 0))
            acc[0] = acc[0] + (p_r * s_r + p_i * s_i)
            acc[1] = acc[1] + (p_r * s_i - p_i * s_r)

        _sweeps(ar8, ai8, dr_s, di_s, cr, ci, seg, True, emit)
        gr_ref[...] += jnp.sum(acc[0], axis=0, keepdims=True)
        gi_ref[...] += jnp.sum(acc[1], axis=0, keepdims=True)
        lrb, lib = lr_s[...].astype(MXU), li_s[...].astype(MXU)
        du_ref[...] = _dot(lrb, bre_ref[...]) + _dot(lib, bim_ref[...]) + d_ref[...] * dyv
        dbr_ref[...] += _dot(ub, lrb, _TN)
        dbi_ref[...] += _dot(ub, lib, _TN)
        dcr_ref[...] += _dot(pr_ref[...].astype(MXU), dyb, _TN)
        dci_ref[...] += _dot(pi_ref[...].astype(MXU), dyb, _TN)
        dd_ref[...] += jnp.sum(dyv * uv, axis=0, keepdims=True)

    sp = _s5_specs(tb, nt, True)
    return pl.pallas_call(
        body, name="s5_bwd", grid=(nb, nt),
        in_specs=[sp["small"], sp["small"], sp["wide"], sp["wide"], sp["halo"], sp["halo"], sp["vec_w"], sp["vec_w"],
                  sp["w_in"], sp["w_in"], sp["w_out"], sp["w_out"], sp["vec_s"]],
        out_specs=[sp["small"], sp["vec_w"], sp["vec_w"], sp["w_in"], sp["w_in"], sp["w_out"], sp["w_out"], sp["vec_s"]],
        out_shape=[_sds((t, nb * S5_BK), F32), _sds((1, nb * S5_BN), F32), _sds((1, nb * S5_BN), F32),
                   _sds((nb, S5_BK, S5_BN), F32), _sds((nb, S5_BK, S5_BN), F32), _sds((nb, S5_BN, S5_BK), F32),
                   _sds((nb, S5_BN, S5_BK), F32), _sds((1, nb * S5_BK), F32)],
        scratch_shapes=[pltpu.VMEM((1, S5_BN), F32), pltpu.VMEM((1, S5_BN), F32)] + [pltpu.VMEM((tb, S5_BN), F32)] * 4,
        compiler_params=_params("parallel", "arbitrary"),
    )(dyp, u, sr, si, sr, si, ar, ai, cre_t, cim_t, bre_t, bim_t, d_vec)


def _s5_gate_bwd(d_ycat, gp, ypre):
    t, d = gp.shape
    tb = _tile(t, 256, 8)

    def body(do_ref, gp_ref, yp_ref, o_ref):
        sg = _sigmoid(gp_ref[...])
        o_ref[...] = (do_ref[...] * _gelu(yp_ref[...]) * sg * (1.0 - sg)).astype(o_ref.dtype)

    return pl.pallas_call(
        body, name="s5_gate_bwd", grid=(t // tb,), in_specs=[_row(tb, d), _row(tb, d), _row(tb, d)],
        out_specs=_row(tb, d), out_shape=_sds((t, d), MXU), compiler_params=_params("parallel"),
    )(d_ycat, gp, ypre)


def _ssd_consts():
    head = jnp.arange(HP)[:, None]
    lane = jnp.arange(D_SSD)[None, :]
    expand = ((lane // SSD_HEADDIM) == head).astype(MXU)
    ll = jnp.arange(SSD_CHUNK)
    tri = (ll[:, None] >= ll[None, :]).astype(F32)
    return expand, expand.T, tri, jnp.eye(HP, dtype=F32)


def _ssd_chunk_terms(cp, dtr, par, expand, tri):
    ln = SSD_CHUNK
    xbc = _silu(cp)
    xs, bm, cm = xbc[:, :D_SSD], xbc[:, D_SSD:D_SSD + SSD_BC], xbc[:, D_SSD + SSD_BC:]
    dt = _softplus(dtr + par[0:1, :])
    a = -jnp.exp(par[1:2, :])
    da = dt * a
    acum = _dot(tri, da, precision=HI)
    acum_t = _dot(da, tri, (((0,), (1,)), ((), ())), precision=HI)
    atot = acum[ln - 1:ln, :]
    dt_e = _dot_split(dt, expand)
    eac_e = _dot_split(jnp.exp(acum), expand)
    dec_e = _dot_split(jnp.exp(atot - acum), expand)
    eat_e = _dot_split(_rows8(jnp.exp(atot)), expand)[0:1, :]
    dsk_e = _dot_split(_rows8(par[2:3, :]), expand)[0:1, :]
    return dict(xs=xs, bm=bm, cm=cm, dt=dt, a=a, acum=acum, acum_t=acum_t, dt_e=dt_e, eac_e=eac_e,
                dec_e=dec_e, eat_e=eat_e, dsk_e=dsk_e)


def _decay_matrix(acum, acum_t, h, mask):
    diff = acum[:, h:h + 1] - acum_t[h:h + 1, :]
    return jnp.where(mask, jnp.exp(jnp.minimum(diff, 0.0)), 0.0)


def _ssd_fwd(proj, conv_w, conv_b, dtr, par, gnorm, consts):
    t = proj.shape[0]
    ln = SSD_CHUNK
    nc = t // ln
    expand, _, tri, _ = consts
    hd2 = 2 * SSD_HEADDIM

    def body(cur_ref, prev_ref, cw_ref, cb_ref, z_ref, dtr_ref, par_ref, g_ref, e_ref, tri_ref,
             out_ref, y_ref, st_ref, cp_ref, state, ext):
        first = pl.program_id(0) == 0

        @pl.when(first)
        def _():
            state[...] = jnp.zeros_like(state)

        st_ref[...] = state[...]
        ext[0:8, :] = jnp.where(first, 0.0, prev_ref[...])
        ext[8:ln + 8, :] = cur_ref[...]
        cpv = jnp.broadcast_to(cb_ref[...], (ln, D_CONV_CH))
        for j in range(SSD_CONV):
            cpv = cpv + cw_ref[SSD_CONV - 1 - j:SSD_CONV - j, :] * ext[8 - j:8 - j + ln, :]
        cp_ref[...] = cpv
        c = _ssd_chunk_terms(cpv, dtr_ref[...], par_ref[...], e_ref[...], tri_ref[...])
        xdt = c["xs"] * c["dt_e"]
        xb, xd = xdt.astype(MXU), (xdt * c["dec_e"]).astype(MXU)
        bb, cb = c["bm"].astype(MXU), c["cm"].astype(MXU)
        mask = lax.broadcasted_iota(jnp.int32, (ln, ln), 0) >= lax.broadcasted_iota(jnp.int32, (ln, ln), 1)
        left = lax.broadcasted_iota(jnp.int32, (ln, hd2), 1) < SSD_HEADDIM
        for g in range(SSD_GROUPS):
            nsl = slice(g * SSD_STATE, (g + 1) * SSD_STATE)
            gsl = slice(g * 256, (g + 1) * 256)
            bg, cg = bb[:, nsl], cb[:, nsl]
            cbm = _dot(cg, bg, _NT)
            st_g = state[:, gsl]
            for pair in range(2):
                h0 = g * 4 + pair * 2
                psl = slice(h0 * SSD_HEADDIM, (h0 + 2) * SSD_HEADDIM)
                m0 = (cbm * _decay_matrix(c["acum"], c["acum_t"], h0, mask)).astype(MXU)
                m1 = (cbm * _decay_matrix(c["acum"], c["acum_t"], h0 + 1, mask)).astype(MXU)
                y_ref[:, psl] = jnp.where(left, _dot(m0, xb[:, psl]), _dot(m1, xb[:, psl]))
            y_ref[:, gsl] += _dot(cg, st_g.astype(MXU)) * c["eac_e"][:, gsl]
            state[:, gsl] = st_g * c["eat_e"][:, gsl] + _dot(bg, xd[:, gsl], _TN)
        y = y_ref[...] + c["dsk_e"] * c["xs"]
        y_ref[...] = y
        y2 = y * _silu(z_ref[...])
        r = lax.rsqrt(jnp.mean(y2 * y2, axis=-1, keepdims=True) + EPS)
        out_ref[...] = (y2 * r * g_ref[...]).astype(out_ref.dtype)

    xbc_block = (D_MAIN - D_CONV_CH) // D_CONV_CH
    return pl.pallas_call(
        body, name="ssd_fwd", grid=(nc,),
        in_specs=[_row(ln, D_CONV_CH, xbc_block),
                  pl.BlockSpec((8, D_CONV_CH), lambda i: (jnp.maximum(i * (ln // 8) - 1, 0), xbc_block)),
                  _const((SSD_CONV, D_CONV_CH)), _const((1, D_CONV_CH)),
                  _row(ln, D_SSD, 1), _row(ln, HP), _const((8, HP)), _const((1, D_SSD)),
                  _const((HP, D_SSD)), _const((ln, ln))],
        out_specs=[_row(ln, D_SSD), _row(ln, D_SSD), pl.BlockSpec((None, SSD_STATE, D_SSD), lambda i: (i, 0, 0)),
                   _row(ln, D_CONV_CH)],
        out_shape=[_sds((t, D_SSD), MXU), _sds((t, D_SSD), F32), _sds((nc, SSD_STATE, D_SSD), F32),
                   _sds((t, D_CONV_CH), F32)],
        scratch_shapes=[pltpu.VMEM((SSD_STATE, D_SSD), F32), pltpu.VMEM((ln + 8, D_CONV_CH), F32)],
        compiler_params=_params("arbitrary"),
    )(proj, proj, conv_w, conv_b.reshape(1, D_CONV_CH), proj, dtr, par, gnorm.reshape(1, D_SSD), expand, tri)


def _ssd_bwd(proj, conv_pre, conv_w, dtr, par, gnorm, y, states, d_ycat, consts):
    t = proj.shape[0]
    ln = SSD_CHUNK
    nc = t // ln
    expand, expand_t, tri, eye = consts
    hd2 = 2 * SSD_HEADDIM

    def body(cp_ref, z_ref, dtr_ref, par_ref, g_ref, y_ref, st_ref, do_ref, e_ref, et_ref, tri_ref, eye_ref,
             cur_ref, prev_ref, cw_ref,
             dxbc_ref, dz_ref, ddt_ref, dg_ref, dpar_ref, dcw_ref, dcb_ref,
             dstate, dx_buf, lane_buf, tot_buf, colsum, dcp_ref, ext, dext):
        @pl.when(pl.program_id(0) == 0)
        def _():
            dstate[...] = jnp.zeros_like(dstate)
            dg_ref[...] = jnp.zeros_like(dg_ref)
            dpar_ref[...] = jnp.zeros_like(dpar_ref)
            dcw_ref[...] = jnp.zeros_like(dcw_ref)
            dcb_ref[...] = jnp.zeros_like(dcb_ref)
            dext[ln:ln + 8, :] = jnp.zeros((8, D_CONV_CH), F32)

        cpv, et_v, tri_v, par_v = cp_ref[...], et_ref[...], tri_ref[...], par_ref[...]
        c = _ssd_chunk_terms(cpv, dtr_ref[...], par_v, e_ref[...], tri_v)
        xs = c["xs"]
        zv, yv, dov = z_ref[...], y_ref[...], do_ref[...]
        sz = _silu(zv)
        y2 = yv * sz
        r = lax.rsqrt(jnp.mean(y2 * y2, axis=-1, keepdims=True) + EPS)
        yh = y2 * r
        dg_ref[...] += jnp.sum(dov * yh, axis=0, keepdims=True)
        dyh = dov * g_ref[...]
        dy2 = r * (dyh - yh * jnp.mean(dyh * yh, axis=-1, keepdims=True))
        dz_ref[...] = dy2 * yv * _silu_grad(zv)
        dy = dy2 * sz

        xdt = xs * c["dt_e"]
        xdf = xdt * c["dec_e"]
        xb, xd = xdt.astype(MXU), xdf.astype(MXU)
        bb, cb = c["bm"].astype(MXU), c["cm"].astype(MXU)
        dyb, dye = dy.astype(MXU), (dy * c["eac_e"]).astype(MXU)
        mask = lax.broadcasted_iota(jnp.int32, (ln, ln), 0) >= lax.broadcasted_iota(jnp.int32, (ln, ln), 1)
        left = lax.broadcasted_iota(jnp.int32, (ln, hd2), 1) < SSD_HEADDIM
        lane_hp = lax.broadcasted_iota(jnp.int32, (ln, HP), 1)
        d_acum = jnp.zeros((ln, HP), F32)
        colsum[...] = jnp.zeros_like(colsum)
        tot_buf[...] = jnp.zeros_like(tot_buf)
        for g in range(SSD_GROUPS):
            nsl = slice(g * SSD_STATE, (g + 1) * SSD_STATE)
            gsl = slice(g * 256, (g + 1) * 256)
            bg, cg = bb[:, nsl], cb[:, nsl]
            cbm = _dot(cg, bg, _NT)
            st_g = st_ref[:, gsl]
            dst_g = dstate[:, gsl]
            stb, dstb = st_g.astype(MXU), dst_g.astype(MXU)
            y_off = _dot(cg, stb)
            bds = _dot(bg, dstb)
            dcb = jnp.zeros((ln, ln), F32)
            for pair in range(2):
                h0 = g * 4 + pair * 2
                psl = slice(h0 * SSD_HEADDIM, (h0 + 2) * SSD_HEADDIM)
                xp, dyp = xb[:, psl], dyb[:, psl]
                dxp = []
                for k in range(2):
                    h = h0 + k
                    lm = _decay_matrix(c["acum"], c["acum_t"], h, mask)
                    mm = cbm * lm
                    half = left if k == 0 else jnp.logical_not(left)
                    dm = _dot(jnp.where(half, dyp, jnp.zeros_like(dyp)), xp, _NT)
                    dcb = dcb + dm * lm
                    gm = dm * mm
                    d_acum = d_acum + jnp.where(lane_hp == h, jnp.sum(gm, axis=1, keepdims=True), 0.0)
                    colsum[h:h + 1, :] = jnp.sum(gm, axis=0, keepdims=True)
                    dxp.append(_dot(mm.astype(MXU), dyp, _TN))
                dx_buf[:, psl] = jnp.where(left, dxp[0], dxp[1])
            dx_buf[:, gsl] += bds * c["dec_e"][:, gsl]
            dcbb = dcb.astype(MXU)
            dc_g = _dot(dcbb, bg) + _dot(dye[:, gsl], stb, _NT)
            db_g = _dot(dcbb, cg, _TN) + _dot(xd[:, gsl], dstb, _NT)
            dcp_ref[:, D_SSD + g * SSD_STATE:D_SSD + (g + 1) * SSD_STATE] = db_g
            dcp_ref[:, D_SSD + SSD_BC + g * SSD_STATE:D_SSD + SSD_BC + (g + 1) * SSD_STATE] = dc_g
            dec_term = xdf[:, gsl] * bds
            lane_buf[:, gsl] = dy[:, gsl] * y_off * c["eac_e"][:, gsl] - dec_term
            tot_buf[0:1, gsl] = (jnp.sum(st_g * dst_g, axis=0, keepdims=True) * c["eat_e"][:, gsl]
                                 + jnp.sum(dec_term, axis=0, keepdims=True))
            dstate[:, gsl] = dst_g * c["eat_e"][:, gsl] + _dot(cg, dye[:, gsl], _TN)
        dx_tot = dx_buf[...]
        d_acum = d_acum + _dot_split(lane_buf[...], et_v) - _dot(colsum[...], eye_ref[...], _TN, precision=HI)
        d_atot = _dot_split(tot_buf[...], et_v)[0:1, :]
        row_hp = lax.broadcasted_iota(jnp.int32, (ln, HP), 0)
        d_acum = d_acum + jnp.where(row_hp == ln - 1, d_atot, 0.0)
        d_da = _dot(tri_v, d_acum, _TN, precision=HI)
        d_dt = d_da * c["a"] + _dot_split(dx_tot * xs, et_v)
        d_dtr = d_dt * _sigmoid(dtr_ref[...] + par_v[0:1, :])
        ddt_ref[...] = d_dtr
        dpar_ref[0:1, :] += jnp.sum(d_dtr, axis=0, keepdims=True)
        dpar_ref[1:2, :] += jnp.sum(d_da * c["dt"], axis=0, keepdims=True) * c["a"]
        dpar_ref[2:3, :] += _dot_split(_rows8(jnp.sum(dy * xs, axis=0, keepdims=True)), et_v)[0:1, :]
        dcp_ref[:, 0:D_SSD] = dx_tot * c["dt_e"] + dy * c["dsk_e"]
        dcv = dcp_ref[...] * _silu_grad(cpv)
        ext[0:8, :] = jnp.where(pl.program_id(0) == nc - 1, 0.0, prev_ref[...])
        ext[8:ln + 8, :] = cur_ref[...]
        dext[0:ln, :] = dcv
        dxbc = jnp.zeros((ln, D_CONV_CH), F32)
        for j in range(SSD_CONV):
            dxbc = dxbc + cw_ref[SSD_CONV - 1 - j:SSD_CONV - j, :] * dext[j:j + ln, :]
            dcw_ref[SSD_CONV - 1 - j:SSD_CONV - j, :] += jnp.sum(dcv * ext[8 - j:8 - j + ln, :], axis=0, keepdims=True)
        dxbc_ref[...] = dxbc
        dcb_ref[...] += jnp.sum(dcv, axis=0, keepdims=True)
        dext[ln:ln + 8, :] = dcv[0:8, :]

    xbc_block = (D_MAIN - D_CONV_CH) // D_CONV_CH
    rev = lambda i: (nc - 1 - i, 0)
    rev1 = lambda i: (nc - 1 - i, 1)
    return pl.pallas_call(
        body, name="ssd_bwd", grid=(nc,),
        in_specs=[pl.BlockSpec((ln, D_CONV_CH), rev), pl.BlockSpec((ln, D_SSD), rev1), pl.BlockSpec((ln, HP), rev),
                  _const((8, HP)), _const((1, D_SSD)), pl.BlockSpec((ln, D_SSD), rev),
                  pl.BlockSpec((None, SSD_STATE, D_SSD), lambda i: (nc - 1 - i, 0, 0)),
                  pl.BlockSpec((ln, D_SSD), rev1),
                  _const((HP, D_SSD)), _const((D_SSD, HP)), _const((ln, ln)), _const((HP, HP)),
                  pl.BlockSpec((ln, D_CONV_CH), lambda i: (nc - 1 - i, xbc_block)),
                  pl.BlockSpec((8, D_CONV_CH), lambda i: (jnp.maximum((nc - 1 - i) * (ln // 8) - 1, 0), xbc_block)),
                  _const((SSD_CONV, D_CONV_CH))],
        out_specs=[pl.BlockSpec((ln, D_CONV_CH), rev), pl.BlockSpec((ln, D_SSD), rev), pl.BlockSpec((ln, HP), rev),
                   _const((1, D_SSD)), _const((8, HP)), _const((SSD_CONV, D_CONV_CH)), _const((1, D_CONV_CH))],
        out_shape=[_sds((t, D_CONV_CH), F32), _sds((t, D_SSD), F32), _sds((t, HP), F32), _sds((1, D_SSD), F32),
                   _sds((8, HP), F32), _sds((SSD_CONV, D_CONV_CH), F32), _sds((1, D_CONV_CH), F32)],
        scratch_shapes=[pltpu.VMEM((SSD_STATE, D_SSD), F32), pltpu.VMEM((ln, D_SSD), F32), pltpu.VMEM((ln, D_SSD), F32),
                        pltpu.VMEM((8, D_SSD), F32), pltpu.VMEM((HP, ln), F32), pltpu.VMEM((ln, D_CONV_CH), F32),
                        pltpu.VMEM((ln + 8, D_CONV_CH), F32), pltpu.VMEM((ln + 8, D_CONV_CH), F32)],
        compiler_params=_params("arbitrary"),
    )(conv_pre, proj, dtr, par, gnorm.reshape(1, D_SSD), y, states, d_ycat, expand, expand_t, tri, eye, proj, proj, conv_w)


def _softmax_rows(s):
    e = jnp.exp(s - jnp.max(s, axis=-1, keepdims=True))
    return e * _recip(jnp.sum(e, axis=-1, keepdims=True))


def _attn_fwd(q, k, v):
    t, d = q.shape
    mlen = k.shape[0]
    tq = _tile(t, 512, 8)
    scale = XA_HEAD_DIM ** -0.5

    def body(q_ref, k_ref, v_ref, o_ref):
        for h in range(XA_HEADS):
            sl = slice(h * XA_HEAD_DIM, (h + 1) * XA_HEAD_DIM)
            p = _softmax_rows(_dot(q_ref[:, sl], k_ref[:, sl], _NT) * scale)
            o_ref[:, sl] = _dot(p.astype(MXU), v_ref[:, sl]).astype(o_ref.dtype)

    return pl.pallas_call(
        body, name="xattn_fwd", grid=(t // tq,),
        in_specs=[_row(tq, d), _const((mlen, d)), _const((mlen, d))], out_specs=_row(tq, d),
        out_shape=_sds((t, d), MXU), compiler_params=_params("parallel"),
    )(q, k, v)


def _attn_bwd(q, k, v, do):
    t, d = q.shape
    mlen = k.shape[0]
    tq = _tile(t, 512, 8)
    scale = XA_HEAD_DIM ** -0.5

    def body(q_ref, k_ref, v_ref, do_ref, dq_ref, dk_ref, dv_ref):
        @pl.when(pl.program_id(0) == 0)
        def _():
            dk_ref[...] = jnp.zeros_like(dk_ref)
            dv_ref[...] = jnp.zeros_like(dv_ref)

        for h in range(XA_HEADS):
            sl = slice(h * XA_HEAD_DIM, (h + 1) * XA_HEAD_DIM)
            qh, kh, vh, doh = q_ref[:, sl], k_ref[:, sl], v_ref[:, sl], do_ref[:, sl]
            p = _softmax_rows(_dot(qh, kh, _NT) * scale)
            dp = _dot(doh, vh, _NT)
            dv_ref[:, sl] += _dot(p.astype(MXU), doh, _TN)
            ds = (p * (dp - jnp.sum(p * dp, axis=-1, keepdims=True)) * scale).astype(MXU)
            dq_ref[:, sl] = _dot(ds, kh).astype(dq_ref.dtype)
            dk_ref[:, sl] += _dot(ds, qh, _TN)

    return pl.pallas_call(
        body, name="xattn_bwd", grid=(t // tq,),
        in_specs=[_row(tq, d), _const((mlen, d)), _const((mlen, d)), _row(tq, d)],
        out_specs=[_row(tq, d), _const((mlen, d)), _const((mlen, d))],
        out_shape=[_sds((t, d), MXU), _sds((mlen, d), F32), _sds((mlen, d), F32)],
        compiler_params=_params("arbitrary"),
    )(q, k, v, do)


def _lane_view(a):
    if a.ndim >= 2 and a.shape[-1] >= 128:
        return a.reshape(-1, a.shape[-1])
    if a.size % 128 == 0:
        return a.reshape(-1, 128)
    return a.reshape(1, -1)


def _adamw(w, g, m, v, *, name):
    shape = w.shape
    w2, g2, m2, v2 = (_lane_view(a) for a in (w, g.reshape(shape), m, v))
    r, c = w2.shape
    tr = _tile(r, max(8, (1 << 18) // max(c, 128)), 8)

    def body(w_ref, g_ref, m_ref, v_ref, d_ref, mo_ref, vo_ref):
        gv = g_ref[...]
        mn = ADAM_B1 * m_ref[...] + (1.0 - ADAM_B1) * gv
        vn = ADAM_B2 * v_ref[...] + (1.0 - ADAM_B2) * (gv * gv)
        m_hat = mn / (1.0 - ADAM_B1 ** ADAM_STEP)
        v_hat = vn / (1.0 - ADAM_B2 ** ADAM_STEP)
        d_ref[...] = -ADAM_LR * (m_hat / (jnp.sqrt(v_hat) + ADAM_EPS) + ADAM_WD * w_ref[...])
        mo_ref[...] = mn
        vo_ref[...] = vn

    outs = pl.pallas_call(
        body, name=name, grid=(r // tr,), in_specs=[_row(tr, c)] * 4, out_specs=[_row(tr, c)] * 3,
        out_shape=[_sds((r, c), F32)] * 3, compiler_params=_params("parallel"),
    )(w2, g2, m2, v2)
    return tuple(o.reshape(shape) for o in outs)


_HBM = pl.BlockSpec(memory_space=pltpu.HBM)


N_CHIPS = 4
CHIPS = ((0, 0), (0, 1), (1, 0), (1, 1))


def _pair_exchange(x, *, name):
    def body(x_ref, o_ref, send_sems, recv_sems):
        xx, yy, cc = (lax.axis_index(a) for a in AXES)
        copies = [
            pltpu.make_async_remote_copy(
                src_ref=x_ref.at[4 * px + 2 * py + (1 - cc)], dst_ref=o_ref.at[k], send_sem=send_sems.at[k],
                recv_sem=recv_sems.at[k], device_id=(xx, yy, 1 - cc), device_id_type=pl.DeviceIdType.MESH)
            for k, (px, py) in enumerate(CHIPS)]
        for cp in copies:
            cp.start()
        for cp in copies:
            cp.wait_recv()
        for cp in copies:
            cp.wait_send()

    return pl.pallas_call(
        body, name=name, in_specs=[_HBM], out_specs=_HBM, out_shape=_sds((N_CHIPS,) + tuple(x.shape[1:]), x.dtype),
        scratch_shapes=[pltpu.SemaphoreType.DMA((N_CHIPS,)), pltpu.SemaphoreType.DMA((N_CHIPS,))],
    )(x)


def _pair_add(x, got, *, name):
    _, r, c = x.shape
    tr = _tile(r, max(PAD_ROWS, (1 << 17) // c), PAD_ROWS)

    def body(x_ref, g_ref, o_ref):
        mine = jnp.where(lax.axis_index("c") == 0, x_ref[0].astype(F32), x_ref[1].astype(F32))
        o_ref[...] = (mine + g_ref[...].astype(F32)).astype(o_ref.dtype)

    return pl.pallas_call(
        body, name=name, grid=(N_CHIPS, r // tr),
        in_specs=[pl.BlockSpec((None, 2, tr, c), lambda k, i: (k, 0, i, 0)), pl.BlockSpec((None, tr, c), lambda k, i: (k, i, 0))],
        out_specs=pl.BlockSpec((None, tr, c), lambda k, i: (k, i, 0)), out_shape=_sds((N_CHIPS, r, c), x.dtype),
        compiler_params=_params("parallel", "parallel"),
    )(x.reshape(N_CHIPS, 2, r, c), got)


def _chip_exchange(x, *, name):
    def body(x_ref, o_ref, send_sems, recv_sems, local_sem):
        xx, yy, cc = (lax.axis_index(a) for a in AXES)
        mine = 2 * xx + yy
        local = pltpu.make_async_copy(x_ref.at[mine], o_ref.at[mine], local_sem)
        local.start()
        sends = []
        for j, (px, py) in enumerate([(1 - xx, yy), (xx, 1 - yy), (1 - xx, 1 - yy)]):
            cp = pltpu.make_async_remote_copy(
                src_ref=x_ref.at[2 * px + py], dst_ref=o_ref.at[mine], send_sem=send_sems.at[j], recv_sem=recv_sems.at[j],
                device_id=(px, py, cc), device_id_type=pl.DeviceIdType.MESH)
            cp.start()
            sends.append(cp)
        for j, (px, py) in enumerate([(1 - xx, yy), (xx, 1 - yy), (1 - xx, 1 - yy)]):
            pltpu.make_async_remote_copy(
                src_ref=x_ref.at[2 * px + py], dst_ref=o_ref.at[2 * px + py], send_sem=send_sems.at[j],
                recv_sem=recv_sems.at[j], device_id=(px, py, cc), device_id_type=pl.DeviceIdType.MESH).wait_recv()
        for cp in sends:
            cp.wait_send()
        local.wait()

    return pl.pallas_call(
        body, name=name, in_specs=[_HBM], out_specs=_HBM, out_shape=_sds(x.shape, x.dtype),
        scratch_shapes=[pltpu.SemaphoreType.DMA((N_CHIPS - 1,)), pltpu.SemaphoreType.DMA((N_CHIPS - 1,)),
                        pltpu.SemaphoreType.DMA(())],
    )(x)


def _reduce_scatter(x, *, name):
    chip_sums = _pair_add(x, _pair_exchange(x, name=name + "_pair"), name=name + "_pair_add")
    return _sum_slots(_chip_exchange(chip_sums, name=name + "_chips"), name=name + "_sum")


def _all_gather(x, *, name):
    def body(x_ref, o_ref, send_sems, recv_sems, local_sem):
        xx, yy, cc = (lax.axis_index(a) for a in AXES)
        me, sibling = (xx, yy, cc), (xx, yy, 1 - cc)
        chips = [(1 - xx, yy), (xx, 1 - yy), (1 - xx, 1 - yy)]

        def slot(px, py, pc):
            return o_ref.at[4 * px + 2 * py + pc]

        def copy(k, block, to, src=None):
            return pltpu.make_async_remote_copy(
                src_ref=slot(*block) if src is None else src, dst_ref=slot(*block), send_sem=send_sems.at[k],
                recv_sem=recv_sems.at[k], device_id=to, device_id_type=pl.DeviceIdType.MESH)

        local = pltpu.make_async_copy(x_ref, slot(*me), local_sem)
        local.start()
        first = [copy(0, me, sibling, src=x_ref)] + [copy(1 + j, me, (*chip, cc), src=x_ref) for j, chip in enumerate(chips)]
        for cp in first:
            cp.start()
        passed = [copy(4 + j, (*chip, cc), sibling) for j, chip in enumerate(chips)]
        for j, chip in enumerate(chips):
            copy(1 + j, (*chip, cc), me).wait_recv()
            passed[j].start()
        copy(0, sibling, me).wait_recv()
        for j, chip in enumerate(chips):
            copy(4 + j, (*chip, 1 - cc), me).wait_recv()
        for cp in first + passed:
            cp.wait_send()
        local.wait()

    return pl.pallas_call(
        body, name=name, in_specs=[_HBM], out_specs=_HBM, out_shape=_sds((N_DEV,) + tuple(x.shape), x.dtype),
        scratch_shapes=[pltpu.SemaphoreType.DMA((N_DEV - 1,)), pltpu.SemaphoreType.DMA((N_DEV - 1,)),
                        pltpu.SemaphoreType.DMA(())],
    )(x)


def _sum_slots(x, *, name):
    n, r, c = x.shape
    tr = _tile(r, max(PAD_ROWS, (1 << 17) // c), PAD_ROWS)

    def body(x_ref, o_ref):
        acc = x_ref[0].astype(F32)
        for d in range(1, n):
            acc = acc + x_ref[d].astype(F32)
        o_ref[...] = acc

    return pl.pallas_call(
        body, name=name, grid=(r // tr,), in_specs=[pl.BlockSpec((n, tr, c), lambda i: (0, i, 0))],
        out_specs=_row(tr, c), out_shape=_sds((r, c), F32), compiler_params=_params("parallel"),
    )(x)


def _s5_layouts(bbar_r, bbar_i, c_re, c_im):
    eye = jnp.eye(S5_GPB, dtype=F32)

    def b_blocks(bbar):
        bb = bbar.reshape(S5_NB, S5_GPB, S5_STATE, S5_GROUP)
        return jnp.einsum("jgph,gk->jghkp", bb, eye).reshape(S5_NB, S5_GPB * S5_GROUP, S5_GPB * S5_STATE)

    def c_blocks(cc):
        c4 = cc.reshape(S5_NB, S5_GPB, S5_GROUP, S5_STATE)
        return jnp.einsum("jghp,gk->jgpkh", c4, eye).reshape(S5_NB, S5_GPB * S5_STATE, S5_GPB * S5_GROUP)

    bre, bim, cre, cim = b_blocks(bbar_r), b_blocks(bbar_i), c_blocks(c_re), c_blocks(c_im)
    cast = lambda a: a.astype(MXU)
    sw = lambda a: jnp.swapaxes(a, 1, 2).astype(MXU)
    return dict(bre=cast(bre), bim=cast(bim), cre=cast(cre), cim=cast(cim), bre_t=sw(bre), bim_t=sw(bim), cre_t=sw(cre),
                cim_t=sw(cim))


def _b_diag(db):
    d5 = db.reshape(S5_NB, S5_GPB, S5_GROUP, S5_GPB, S5_STATE)
    diag = jnp.stack([d5[:, g, :, g, :] for g in range(S5_GPB)], axis=1)
    return jnp.swapaxes(diag, 2, 3).reshape(S5_GROUPS, S5_STATE * S5_GROUP)


def _c_diag(dc):
    d5 = dc.reshape(S5_NB, S5_GPB, S5_STATE, S5_GPB, S5_GROUP)
    diag = jnp.stack([d5[:, g, :, g, :] for g in range(S5_GPB)], axis=1)
    return jnp.swapaxes(diag, 2, 3).reshape(S5_GROUPS, S5_GROUP, S5_STATE)


def _head_rows(*vecs):
    par = jnp.zeros((8, HP), F32)
    for i, v in enumerate(vecs):
        par = par.at[i, :SSD_HEADS].set(v.astype(F32))
    return par


def _add(acc, r):
    return (acc + r,)


def _layer_fwd(x, mem, w, consts):
    s = {"x": x}
    rep = consts["rep"]
    s["h1"] = h1 = _rmsnorm_fwd(x, w["norm_mix"], name="norm_mix_fwd")
    s["proj"] = proj = _mm(h1, w["w_main"], name="in_proj")
    s["dtr"] = dtr = _mm(h1, w["w_dt"], name="dt_proj")
    ar, ai, ldt = w["s5_a_re"], w["s5_a_im"], w["s5_log_dt"].reshape(S5_GROUPS, 1)
    br, bi = w["s5_b_re"].reshape(S5_GROUPS, -1), w["s5_b_im"].reshape(S5_GROUPS, -1)
    abar_r, abar_i, bbar_r, bbar_i = _s5_prep(ar, ai, ldt, br, bi, rep)
    s["abar"] = abar = (abar_r.reshape(1, S5_CH), abar_i.reshape(1, S5_CH))
    s["lay"] = lay = _s5_layouts(bbar_r, bbar_i, w["s5_c_re"], w["s5_c_im"])
    s["u"] = u = _interleave_rows(proj[:, :D_S5])
    s["sr"], s["si"], s["ypre"], s["yg"] = _s5_fwd(u, *abar, lay["bre"], lay["bim"], lay["cre"], lay["cim"],
                                                   w["s5_d"].reshape(1, D_S5))
    ypre, yg = s["ypre"], s["yg"]
    s["gp"], out_s5 = _mm(yg, w["s5_w_glu"], extras=[ypre], epi=lambda acc, yp: (acc, _gelu(yp) * _sigmoid(acc)),
                          out_dtypes=(F32, MXU), name="s5_glu")
    out_s5 = _deinterleave_rows(out_s5)
    s["par"] = par = _head_rows(w["ssd_dt_bias"], w["ssd_a_log"], w["ssd_d"])
    out_ssd, s["y_ssd"], s["states"], s["conv_pre"] = _ssd_fwd(proj, w["ssd_conv_w"], w["ssd_conv_b"], dtr, par,
                                                               w["ssd_norm"], consts["ssd"])
    s["ycat"] = ycat = jnp.concatenate([out_s5, out_ssd], axis=1)
    s["x1"] = x1 = _mm(ycat, w["w_out"], extras=[x], epi=_add, name="out_proj")
    s["hq"] = hq = _rmsnorm_fwd(x1, w["norm_xattn"], name="norm_xattn_fwd")
    s["mn"] = mn = _rmsnorm_fwd(mem, w["norm_mem"], name="norm_mem_fwd")
    s["q"] = q = _mm(hq, w["xa_wq"], out_dtypes=(MXU,), name="xa_q")
    s["k"] = k = _mm(mn, w["xa_wk"], out_dtypes=(MXU,), name="xa_k")
    s["v"] = v = _mm(mn, w["xa_wv"], out_dtypes=(MXU,), name="xa_v")
    s["o"] = o = _attn_fwd(q, k, v)
    s["x2"] = x2 = _mm(o, w["xa_wo"], extras=[x1], epi=_add, name="xa_o")
    s["hm"] = hm = _rmsnorm_fwd(x2, w["norm_mlp"], name="norm_mlp_fwd")
    s["act"] = _mm(hm, w["mlp_w1"], epi=lambda acc: (jnp.square(jnp.maximum(acc, 0.0)),), out_dtypes=(MXU,),
                   name="mlp_up")
    x3 = _mm(s["act"], w["mlp_w2"], extras=[x2], epi=_add, name="mlp_down")
    return x3, s


def _layer_bwd(dx3, mem, w, s, consts):
    g = {}
    rep = consts["rep"]
    d_a = _mm(dx3, w["mlp_w2"], tb=True, extras=[s["act"]],
              epi=lambda acc, act: (acc * (2.0 * jnp.sqrt(act.astype(F32))),), out_dtypes=(MXU,), name="mlp_down_dx")
    g["mlp_w2"] = _mm(s["act"], dx3, ta=True, name="mlp_down_dw")
    g["mlp_w1"] = _mm(s["hm"], d_a, ta=True, name="mlp_up_dw")
    d_hm = _mm(d_a, w["mlp_w1"], tb=True, name="mlp_up_dx")
    dx2, g["norm_mlp"] = _rmsnorm_bwd(s["x2"], w["norm_mlp"], d_hm, dx3, name="norm_mlp_bwd")
    d_o = _mm(dx2, w["xa_wo"], tb=True, out_dtypes=(MXU,), name="xa_o_dx")
    g["xa_wo"] = _mm(s["o"], dx2, ta=True, name="xa_o_dw")
    dq, dk, dv = _attn_bwd(s["q"], s["k"], s["v"], d_o)
    g["xa_wq"] = _mm(s["hq"], dq, ta=True, name="xa_q_dw")
    d_hq = _mm(dq, w["xa_wq"], tb=True, name="xa_q_dx")
    dx1, g["norm_xattn"] = _rmsnorm_bwd(s["x1"], w["norm_xattn"], d_hq, dx2, name="norm_xattn_bwd")
    g["xa_wk"] = _mm(s["mn"], dk, ta=True, name="xa_k_dw")
    g["xa_wv"] = _mm(s["mn"], dv, ta=True, name="xa_v_dw")
    d_mn_v = _mm(dv, w["xa_wv"], tb=True, name="xa_v_dx")
    d_mn = _mm(dk, w["xa_wk"], tb=True, extras=[d_mn_v], epi=_add, name="xa_k_dx")
    _, g["norm_mem"] = _rmsnorm_bwd(mem, w["norm_mem"], d_mn, None, name="norm_mem_bwd")
    d_ycat = _mm(dx1, w["w_out"], tb=True, name="out_proj_dx")
    g["w_out"] = _mm(s["ycat"], dx1, ta=True, name="out_proj_dw")
    lay, proj, ypre, u = s["lay"], s["proj"], s["ypre"], s["u"]
    d_os5 = _interleave_rows(d_ycat[:, :D_S5])
    d_gp = _s5_gate_bwd(d_os5, s["gp"], ypre)
    g["s5_w_glu"] = _mm(s["yg"], d_gp, ta=True, name="s5_glu_dw")
    d_ypre = _mm(d_gp, w["s5_w_glu"], tb=True, extras=[d_os5, s["gp"], ypre],
                 epi=lambda acc, do, gp, yp: ((acc + do * _sigmoid(gp)) * _gelu_grad(yp),), name="s5_glu_dx")
    du, d_abar_r, d_abar_i, db_re, db_im, dc_re, dc_im, d_d = _s5_bwd(
        d_ypre, u, s["sr"], s["si"], *s["abar"], lay["cre_t"], lay["cim_t"], lay["bre_t"], lay["bim_t"],
        w["s5_d"].reshape(1, D_S5))
    d_bbar_r, d_bbar_i = _b_diag(db_re), _b_diag(db_im)
    g["s5_c_re"], g["s5_c_im"] = _c_diag(dc_re), -_c_diag(dc_im)
    g["s5_d"] = d_d.reshape(S5_GROUPS, S5_GROUP)
    du = _deinterleave_rows(du)
    ar, ai, ldt = w["s5_a_re"], w["s5_a_im"], w["s5_log_dt"].reshape(S5_GROUPS, 1)
    br, bi = w["s5_b_re"].reshape(S5_GROUPS, -1), w["s5_b_im"].reshape(S5_GROUPS, -1)
    d_ar, d_ai, d_ldt, d_br, d_bi = _s5_prep_bwd(
        ar, ai, ldt, br, bi, rep, d_abar_r.reshape(S5_GROUPS, S5_STATE), d_abar_i.reshape(S5_GROUPS, S5_STATE),
        d_bbar_r, d_bbar_i)
    g["s5_a_re"], g["s5_a_im"], g["s5_log_dt"] = d_ar, d_ai, d_ldt.reshape(S5_GROUPS)
    g["s5_b_re"] = d_br.reshape(S5_GROUPS, S5_STATE, S5_GROUP)
    g["s5_b_im"] = d_bi.reshape(S5_GROUPS, S5_STATE, S5_GROUP)
    d_xbc, dz, d_dtr, g["ssd_norm"], d_par, g["ssd_conv_w"], g["ssd_conv_b"] = _ssd_bwd(
        proj, s["conv_pre"], w["ssd_conv_w"], s["dtr"], s["par"], w["ssd_norm"], s["y_ssd"], s["states"], d_ycat,
        consts["ssd"])
    g["ssd_dt_bias"], g["ssd_a_log"], g["ssd_d"] = (d_par[i, :SSD_HEADS] for i in range(3))
    d_proj = jnp.concatenate([du, dz, d_xbc], axis=1)
    g_main = _mm(s["h1"], d_proj, ta=True, name="in_proj_dw")
    g_dt = _mm(s["h1"], d_dtr, ta=True, name="dt_proj_dw")
    g["w_in"] = jnp.concatenate([g_main, g_dt[:, :SSD_HEADS]], axis=1)
    d_h1_dt = _mm(d_dtr, w["w_dt"], tb=True, name="dt_proj_dx")
    d_h1 = _mm(d_proj, w["w_main"], tb=True, extras=[d_h1_dt], epi=_add, name="in_proj_dx")
    dx, g["norm_mix"] = _rmsnorm_bwd(s["x"], w["norm_mix"], d_h1, dx1, name="norm_mix_bwd")
    return dx, g


LAYER_WEIGHTS = ("norm_mix", "w_in", "s5_a_re", "s5_a_im", "s5_log_dt", "s5_b_re", "s5_b_im", "s5_c_re", "s5_c_im", "s5_d",
                 "s5_w_glu", "ssd_conv_w", "ssd_conv_b", "ssd_dt_bias", "ssd_a_log", "ssd_d", "ssd_norm", "w_out",
                 "norm_xattn", "norm_mem", "xa_wq", "xa_wk", "xa_wv", "xa_wo", "norm_mlp", "mlp_w1", "mlp_w2")
WEIGHTS = LAYER_WEIGHTS + ("norm_final",)


def _local_step(x, mem, target, weights):
    consts = {
        "ssd": _ssd_consts(),
        "rep": (jnp.arange(S5_STATE)[:, None] == jnp.arange(S5_STATE * S5_GROUP)[None, :] // S5_GROUP).astype(F32),
    }
    layers = []
    for l in range(DEPTH):
        w = {n: weights[n][l] for n in LAYER_WEIGHTS}
        w_in = w["w_in"]
        w["w_main"] = w_in[:, :D_MAIN]
        w["w_dt"] = jnp.pad(w_in[:, D_MAIN:], ((0, 0), (0, HP - SSD_HEADS)))
        layers.append(w)
    saved = []
    for l in range(DEPTH):
        x, s = _layer_fwd(x, mem, layers[l], consts)
        saved.append(s)
    loss, dx, g_final = _loss_head(x, weights["norm_final"], target)
    grads = [None] * DEPTH
    for l in reversed(range(DEPTH)):
        dx, grads[l] = _layer_bwd(dx, mem, layers[l], saved[l], consts)
    out = {n: jnp.stack([grads[l][n].reshape(weights[n].shape[1:]) for l in range(DEPTH)]) for n in LAYER_WEIGHTS}
    out["norm_final"] = g_final.reshape(weights["norm_final"].shape)
    return loss, dx, out


SHARDED = {"w_in": 2, "s5_w_glu": 1, "ssd_conv_w": 2, "w_out": 1, "xa_wq": 1, "xa_wk": 1, "xa_wv": 1, "xa_wo": 1,
           "mlp_w1": 2, "mlp_w2": 1}
EXACT = ("ssd_conv_w",)
ROW_EXCHANGE = tuple(n for n, ax in SHARDED.items() if ax == 1)
OWN_EXCHANGE = tuple(n for n in SHARDED if n not in ROW_EXCHANGE)
REPLICATED = tuple(n for n in WEIGHTS if n not in SHARDED)
LANES = 128
PAD_ROWS = 16


def _to_rows(flat, lead=()):
    n = flat.shape[-1]
    quantum = LANES * PAD_ROWS
    padded = -(-n // quantum) * quantum
    flat = jnp.pad(flat, [(0, 0)] * len(lead) + [(0, padded - n)])
    return flat.reshape(*lead, padded // LANES, LANES)


def _gather_weights(local):
    def assemble(n, seg):
        shp, ax = local[n].shape, SHARDED[n]
        return jnp.moveaxis(seg, 0, ax).reshape(*shp[:ax], N_DEV * shp[ax], *shp[ax + 1:])

    full = {}
    for n in OWN_EXCHANGE:
        payload = local[n] if n in EXACT else local[n].astype(MXU)
        full[n] = assemble(n, _all_gather(payload, name="gather_" + n))
    got = _all_gather(jnp.concatenate([local[n].astype(MXU) for n in ROW_EXCHANGE], axis=1), name="gather_row_sharded")
    off = 0
    for n in ROW_EXCHANGE:
        rows = local[n].shape[1]
        full[n] = assemble(n, got[:, :, off:off + rows])
        off += rows
    return full


def _scatter_grads(grads, local_shapes):
    def shards(n):
        shp, ax = local_shapes[n], SHARDED[n]
        gfull = grads[n].reshape(*shp[:ax], N_DEV, shp[ax], *shp[ax + 1:])
        return jnp.moveaxis(gfull, ax, 0).astype(MXU)

    out = {}
    for n in OWN_EXCHANGE:
        shp = local_shapes[n]
        out[n] = _reduce_scatter(shards(n).reshape(N_DEV, -1, shp[-1]), name="scatter_" + n).reshape(shp)
    payload = jnp.concatenate([shards(n) for n in ROW_EXCHANGE], axis=2)
    summed = _reduce_scatter(payload.reshape(N_DEV, -1, payload.shape[-1]), name="scatter_row_sharded")
    summed = summed.reshape(payload.shape[1:])
    off = 0
    for n in ROW_EXCHANGE:
        rows = local_shapes[n][1]
        out[n] = summed[:, off:off + rows]
        off += rows
    return out


def _allreduce_small(loss, grads):
    parts = [loss.reshape(-1)[:1]] + [grads[n].reshape(-1) for n in REPLICATED]
    payload = _to_rows(jnp.concatenate(parts))
    summed = _sum_slots(_all_gather(payload, name="gather_small_grads"), name="sum_small_grads").reshape(-1)
    out, off = {}, 1
    for n in REPLICATED:
        size = grads[n].size
        out[n] = summed[off:off + size].reshape(grads[n].shape)
        off += size
    return summed[0], out


def kernel(x, mem, norm_mix, w_in, s5_a_re, s5_a_im, s5_log_dt, s5_b_re, s5_b_im, s5_c_re, s5_c_im, s5_d, s5_w_glu, ssd_conv_w, ssd_conv_b, ssd_dt_bias, ssd_a_log, ssd_d, ssd_norm, w_out, norm_xattn, norm_mem, xa_wq, xa_wk, xa_wv, xa_wo, norm_mlp, mlp_w1, mlp_w2, norm_final, loss_target, m_norm_mix, m_w_in, m_s5_a_re, m_s5_a_im, m_s5_log_dt, m_s5_b_re, m_s5_b_im, m_s5_c_re, m_s5_c_im, m_s5_d, m_s5_w_glu, m_ssd_conv_w, m_ssd_conv_b, m_ssd_dt_bias, m_ssd_a_log, m_ssd_d, m_ssd_norm, m_w_out, m_norm_xattn, m_norm_mem, m_xa_wq, m_xa_wk, m_xa_wv, m_xa_wo, m_norm_mlp, m_mlp_w1, m_mlp_w2, m_norm_final, v_norm_mix, v_w_in, v_s5_a_re, v_s5_a_im, v_s5_log_dt, v_s5_b_re, v_s5_b_im, v_s5_c_re, v_s5_c_im, v_s5_d, v_s5_w_glu, v_ssd_conv_w, v_ssd_conv_b, v_ssd_dt_bias, v_ssd_a_log, v_ssd_d, v_ssd_norm, v_w_out, v_norm_xattn, v_norm_mem, v_xa_wq, v_xa_wk, v_xa_wv, v_xa_wo, v_norm_mlp, v_mlp_w1, v_mlp_w2, v_norm_final):
    args = locals()
    local = {n: args[n] for n in WEIGHTS}
    full = dict(local)
    full.update(_gather_weights(local))
    loss, grad_x, grads = _local_step(x[0], mem[0], loss_target[0], full)
    loss, g_small = _allreduce_small(loss, grads)
    g_all = _scatter_grads(grads, {n: local[n].shape for n in SHARDED})
    g_all.update(g_small)
    delta, new_m, new_v = {}, {}, {}
    for n in WEIGHTS:
        delta[n], new_m[n], new_v[n] = _adamw(local[n], g_all[n], args["m_" + n], args["v_" + n], name="adamw_" + n)
    return (loss, grad_x[None], *[g_all[n] for n in WEIGHTS], *[delta[n] for n in WEIGHTS],
            *[new_m[n] for n in WEIGHTS], *[new_v[n] for n in WEIGHTS])
```

```python
import functools
import math

import jax
import jax.numpy as jnp
from jax import lax
from jax.experimental import pallas as pl
from jax.experimental.pallas import tpu as pltpu

F32 = jnp.float32
MXU = jnp.bfloat16
HI = lax.Precision.HIGHEST

D_MODEL = 1024
DEPTH = 4
MEM_LEN = 256
D_S5 = 1024
D_SSD = 1024
S5_GROUP = 16
S5_GROUPS = 64
S5_STATE = 64
S5_CH = S5_GROUPS * S5_STATE
S5_NB = 4
S5_GPB = S5_GROUPS // S5_NB
SSD_HEADDIM = 64
SSD_HEADS = 16
SSD_GROUPS = 4
SSD_STATE = 128
SSD_CONV = 4
SSD_CHUNK = 128
SSD_BC = SSD_GROUPS * SSD_STATE
D_CONV_CH = 2048
D_MAIN = 4096
D_IN_PROJ = D_MAIN + SSD_HEADS
HP = 128
XA_HEADS = 4
XA_HEAD_DIM = 256
D_FF = 4096
EPS = 1e-5
N_DEV = 8
AXES = ("x", "y", "c")

ADAM_LR = 0.001
ADAM_B1 = 0.9
ADAM_B2 = 0.999
ADAM_EPS = 1e-08
ADAM_WD = 0.01
ADAM_STEP = 10

VMEM_LIMIT = 56 * 1024 * 1024


def _params(*sem):
    return pltpu.CompilerParams(dimension_semantics=sem, vmem_limit_bytes=VMEM_LIMIT)


def _tile(n, pref, quantum=128):
    t = (min(pref, n) // quantum) * quantum
    while t >= quantum:
        if n % t == 0:
            return t
        t -= quantum
    return n


def _sds(shape, dtype):
    return jax.ShapeDtypeStruct(tuple(shape), dtype)


def _recip(d):
    r = pl.reciprocal(d, approx=True)
    return r * (2.0 - d * r)


def _sigmoid(x):
    return _recip(1.0 + jnp.exp(-jnp.maximum(x, -80.0)))


def _silu(x):
    return x * _sigmoid(x)


def _silu_grad(x):
    s = _sigmoid(x)
    return s * (1.0 + x * (1.0 - s))


_GELU_C = math.sqrt(2.0 / math.pi)


def _gelu(x):
    return 0.5 * x * (1.0 + jnp.tanh(_GELU_C * (x + 0.044715 * x * x * x)))


def _gelu_grad(x):
    th = jnp.tanh(_GELU_C * (x + 0.044715 * x * x * x))
    return 0.5 * (1.0 + th) + 0.5 * x * (1.0 - th * th) * _GELU_C * (1.0 + 3.0 * 0.044715 * x * x)


def _softplus(x):
    return jnp.maximum(x, 0.0) + jnp.log(1.0 + jnp.exp(-jnp.abs(x)))


_NN = (((1,), (0,)), ((), ()))
_NT = (((1,), (1,)), ((), ()))
_TN = (((0,), (0,)), ((), ()))


def _dot(a, b, dims=_NN, precision=None):
    return lax.dot_general(a, b, dims, precision=precision, preferred_element_type=F32)


def _dot_split(x, w):
    hi = x.astype(MXU)
    lo = (x - hi.astype(F32)).astype(MXU)
    return _dot(hi, w) + _dot(lo, w)


def _rows8(v):
    return jnp.broadcast_to(v, (8, v.shape[1]))


def _mm(a, b, *, ta=False, tb=False, extras=(), epi=None, out_dtypes=(F32,), tm=1024, tn=1024, tk=1024, name):
    m, k = (a.shape[1], a.shape[0]) if ta else a.shape
    n = b.shape[0] if tb else b.shape[1]
    assert k == (b.shape[1] if tb else b.shape[0]), (a.shape, b.shape, ta, tb)
    tm, tn, tk = _tile(m, tm), _tile(n, tn), _tile(k, tk)
    nk = k // tk
    n_ex, n_out = len(extras), len(out_dtypes)
    dims = (((0,) if ta else (1,), (1,) if tb else (0,)), ((), ()))

    def body(a_ref, b_ref, *rest):
        ex_refs, out_refs = rest[:n_ex], rest[n_ex:n_ex + n_out]
        prod = _dot(a_ref[...].astype(MXU), b_ref[...].astype(MXU), dims)

        def finish(total):
            outs = epi(total, *[e[...] for e in ex_refs]) if epi is not None else (total,)
            for o, r in zip(outs, out_refs, strict=True):
                r[...] = o.astype(r.dtype)

        if nk == 1:
            finish(prod)
            return
        acc = rest[-1]
        kk = pl.program_id(2)

        @pl.when(kk == 0)
        def _():
            acc[...] = prod

        @pl.when(jnp.logical_and(kk > 0, kk < nk - 1))
        def _():
            acc[...] += prod

        @pl.when(kk == nk - 1)
        def _():
            finish(acc[...] + prod)

    a_spec = pl.BlockSpec((tk, tm), lambda i, j, kk: (kk, i)) if ta else pl.BlockSpec((tm, tk), lambda i, j, kk: (i, kk))
    b_spec = pl.BlockSpec((tn, tk), lambda i, j, kk: (j, kk)) if tb else pl.BlockSpec((tk, tn), lambda i, j, kk: (kk, j))
    mn_spec = pl.BlockSpec((tm, tn), lambda i, j, kk: (i, j))
    outs = pl.pallas_call(
        body,
        name=name,
        grid=(m // tm, n // tn, nk),
        in_specs=[a_spec, b_spec] + [mn_spec] * n_ex,
        out_specs=[mn_spec] * n_out,
        out_shape=[_sds((m, n), dt) for dt in out_dtypes],
        scratch_shapes=[pltpu.VMEM((tm, tn), F32)] if nk > 1 else [],
        compiler_params=_params("parallel", "parallel", "arbitrary"),
    )(a, b, *extras)
    return outs[0] if n_out == 1 else outs


def _row(tb, w, cb=0):
    return pl.BlockSpec((tb, w), lambda i: (i, cb))


def _const(shape):
    return pl.BlockSpec(shape, lambda i: (0,) * len(shape))


def _rmsnorm_fwd(x, g, *, name):
    t, d = x.shape
    tb = _tile(t, 512, 8)

    def body(x_ref, g_ref, h_ref):
        xv = x_ref[...]
        r = lax.rsqrt(jnp.mean(xv * xv, axis=-1, keepdims=True) + EPS)
        h_ref[...] = (xv * r * g_ref[...]).astype(h_ref.dtype)

    return pl.pallas_call(
        body, name=name, grid=(t // tb,), in_specs=[_row(tb, d), _const((1, d))], out_specs=_row(tb, d),
        out_shape=_sds((t, d), MXU), compiler_params=_params("parallel"),
    )(x, g.reshape(1, d))


def _rmsnorm_bwd(x, g, dh, dres, *, name):
    t, d = x.shape
    tb = _tile(t, 256, 8)
    has_res = dres is not None

    def body(x_ref, g_ref, dh_ref, *rest):
        dx_ref, dxm_ref, dg_ref = rest[-3:]

        @pl.when(pl.program_id(0) == 0)
        def _():
            dg_ref[...] = jnp.zeros_like(dg_ref)

        xv = x_ref[...]
        r = lax.rsqrt(jnp.mean(xv * xv, axis=-1, keepdims=True) + EPS)
        xh = xv * r
        dhv = dh_ref[...].astype(F32)
        dg_ref[...] += jnp.sum(dhv * xh, axis=0, keepdims=True)
        dxh = dhv * g_ref[...]
        dx = r * (dxh - xh * jnp.mean(dxh * xh, axis=-1, keepdims=True))
        if has_res:
            dx = dx + rest[0][...]
        dx_ref[...] = dx
        dxm_ref[...] = dx.astype(dxm_ref.dtype)

    ins = [x, g.reshape(1, d), dh] + ([dres] if has_res else [])
    return pl.pallas_call(
        body, name=name, grid=(t // tb,),
        in_specs=[_row(tb, d), _const((1, d)), _row(tb, d)] + ([_row(tb, d)] if has_res else []),
        out_specs=[_row(tb, d), _row(tb, d), _const((1, d))],
        out_shape=[_sds((t, d), F32), _sds((t, d), MXU), _sds((1, d), F32)],
        compiler_params=_params("arbitrary"),
    )(*ins)


def _loss_head(x, g, target):
    t, d = x.shape
    tb = _tile(t, 256, 8)

    def body(x_ref, g_ref, tg_ref, loss_ref, dx_ref, dxm_ref, dg_ref):
        @pl.when(pl.program_id(0) == 0)
        def _():
            dg_ref[...] = jnp.zeros_like(dg_ref)
            loss_ref[...] = jnp.zeros_like(loss_ref)

        xv, gv = x_ref[...], g_ref[...]
        r = lax.rsqrt(jnp.mean(xv * xv, axis=-1, keepdims=True) + EPS)
        xh = xv * r
        err = xh * gv - tg_ref[...]
        loss_ref[...] += 0.5 * jnp.sum(jnp.mean(err * err, axis=-1, keepdims=True), axis=0, keepdims=True)
        dy = err * (1.0 / d)
        dg_ref[...] += jnp.sum(dy * xh, axis=0, keepdims=True)
        dxh = dy * gv
        dx = r * (dxh - xh * jnp.mean(dxh * xh, axis=-1, keepdims=True))
        dx_ref[...] = dx
        dxm_ref[...] = dx.astype(dxm_ref.dtype)

    return pl.pallas_call(
        body, name="loss_head", grid=(t // tb,),
        in_specs=[_row(tb, d), _const((1, d)), _row(tb, d)],
        out_specs=[_const((1, HP)), _row(tb, d), _row(tb, d), _const((1, d))],
        out_shape=[_sds((1, HP), F32), _sds((t, d), F32), _sds((t, d), MXU), _sds((1, d), F32)],
        compiler_params=_params("arbitrary"),
    )(x, g.reshape(1, d), target)


def _s5_discretise(ar, ai, ldt, br, bi, rep):
    dt = jnp.exp(ldt)
    mag = jnp.exp(dt * ar)
    abar_r, abar_i = mag * jnp.cos(dt * ai), mag * jnp.sin(dt * ai)
    den = ar * ar + ai * ai
    zr, zi = abar_r - 1.0, abar_i
    fr = (zr * ar + zi * ai) / den
    fi = (zi * ar - zr * ai) / den
    fr_e, fi_e = _dot(fr, rep, precision=HI), _dot(fi, rep, precision=HI)
    return abar_r, abar_i, fr_e * br - fi_e * bi, fr_e * bi + fi_e * br


def _s5_prep(ar, ai, ldt, br, bi, rep):
    g, p = ar.shape
    ph = br.shape[1]

    def body(ar_ref, ai_ref, ldt_ref, br_ref, bi_ref, rep_ref, o0, o1, o2, o3):
        outs = _s5_discretise(ar_ref[...], ai_ref[...], ldt_ref[...], br_ref[...], bi_ref[...], rep_ref[...])
        for o, v in zip((o0, o1, o2, o3), outs):
            o[...] = v

    return pl.pallas_call(
        body, name="s5_prep",
        out_shape=[_sds((g, p), F32), _sds((g, p), F32), _sds((g, ph), F32), _sds((g, ph), F32)],
        compiler_params=pltpu.CompilerParams(vmem_limit_bytes=VMEM_LIMIT),
    )(ar, ai, ldt, br, bi, rep)


def _s5_prep_bwd(ar, ai, ldt, br, bi, rep, d_abar_r, d_abar_i, d_bbar_r, d_bbar_i):
    g, p = ar.shape
    ph = br.shape[1]

    def body(ar_ref, ai_ref, ldt_ref, br_ref, bi_ref, rep_ref, c0, c1, c2, c3, o0, o1, o2, o3, o4):
        rep_v = rep_ref[...]
        _, vjp = jax.vjp(lambda a, b, c, d, e: _s5_discretise(a, b, c, d, e, rep_v),
                         ar_ref[...], ai_ref[...], ldt_ref[...], br_ref[...], bi_ref[...])
        grads = vjp((c0[...], c1[...], c2[...], c3[...]))
        for o, v in zip((o0, o1, o2, o3, o4), grads):
            o[...] = v

    return pl.pallas_call(
        body, name="s5_prep_bwd",
        out_shape=[_sds((g, p), F32), _sds((g, p), F32), _sds((g, 1), F32), _sds((g, ph), F32), _sds((g, ph), F32)],
        compiler_params=pltpu.CompilerParams(vmem_limit_bytes=VMEM_LIMIT),
    )(ar, ai, ldt, br, bi, rep, d_abar_r, d_abar_i, d_bbar_r, d_bbar_i)


SCAN_ROWS = 512


def _scan_rows(t):
    return _tile(t, SCAN_ROWS, 64)


def _interleave_rows(x):
    t, c = x.shape
    tb = _scan_rows(t)
    return x.reshape(t // tb, 8, tb // 8, c).swapaxes(1, 2).reshape(t, c)


def _deinterleave_rows(x):
    t, c = x.shape
    tb = _scan_rows(t)
    return x.reshape(t // tb, tb // 8, 8, c).swapaxes(1, 2).reshape(t, c)


def _cmul(ar, ai, br, bi):
    return ar * br - ai * bi, ar * bi + ai * br


def _segment_carries(fr, fi, ar8, ai8, c_r, c_i, seg, reverse):
    pr, pi = ar8, ai8
    for _ in range(int(math.log2(seg))):
        pr, pi = _cmul(pr, pi, pr, pi)
    row = lax.broadcasted_iota(jnp.int32, fr.shape, 0)
    edge = 7 if reverse else 0
    qr, qi = _cmul(pr, pi, c_r, c_i)
    xr, xi = jnp.where(row == edge, fr + qr, fr), jnp.where(row == edge, fi + qi, fi)
    for sh in (1, 2, 4):
        if reverse:
            keep, amount = row < 8 - sh, 8 - sh
        else:
            keep, amount = row >= sh, sh
        qr, qi = jnp.where(keep, pltpu.roll(xr, amount, 0), 0.0), jnp.where(keep, pltpu.roll(xi, amount, 0), 0.0)
        tr, ti = _cmul(pr, pi, qr, qi)
        xr, xi = xr + tr, xi + ti
        pr, pi = _cmul(pr, pi, pr, pi)
    if reverse:
        in_r, in_i = jnp.where(row == 7, c_r, pltpu.roll(xr, 7, 0)), jnp.where(row == 7, c_i, pltpu.roll(xi, 7, 0))
        return in_r, in_i, xr[0:1, :], xi[0:1, :]
    in_r, in_i = jnp.where(row == 0, c_r, pltpu.roll(xr, 1, 0)), jnp.where(row == 0, c_i, pltpu.roll(xi, 1, 0))
    return in_r, in_i, xr[7:8, :], xi[7:8, :]


def _sweeps(ar8, ai8, dr, di, cr, ci, seg, reverse, emit):
    order = range(seg - 1, -1, -1) if reverse else range(seg)
    rows = lambda j: slice(j * 8, (j + 1) * 8)
    fr, fi = jnp.zeros(ar8.shape, F32), jnp.zeros(ar8.shape, F32)
    for j in order:
        tr, ti = _cmul(ar8, ai8, fr, fi)
        fr, fi = tr + dr[rows(j), :], ti + di[rows(j), :]
    s_r, s_i, out_r, out_i = _segment_carries(fr, fi, ar8, ai8, cr[...], ci[...], seg, reverse)
    cr[...] = out_r
    ci[...] = out_i
    for j in order:
        tr, ti = _cmul(ar8, ai8, s_r, s_i)
        s_r, s_i = tr + dr[rows(j), :], ti + di[rows(j), :]
        emit(j, s_r, s_i)


S5_BK, S5_BN = 256, 1024


def _s5_specs(tb, nt, reverse):
    t_of = (lambda s: nt - 1 - s) if reverse else (lambda s: s)
    return dict(
        small=pl.BlockSpec((tb, S5_BK), lambda c, s: (t_of(s), c)),
        wide=pl.BlockSpec((tb, S5_BN), lambda c, s: (t_of(s), c)),
        halo=pl.BlockSpec((8, S5_BN), lambda c, s: (jnp.maximum(t_of(s) * (tb // 8) - 1, 0), c)),
        vec_w=pl.BlockSpec((1, S5_BN), lambda c, s: (0, c)),
        vec_s=pl.BlockSpec((1, S5_BK), lambda c, s: (0, c)),
        w_in=pl.BlockSpec((None, S5_BK, S5_BN), lambda c, s: (c, 0, 0)),
        w_out=pl.BlockSpec((None, S5_BN, S5_BK), lambda c, s: (c, 0, 0)),
    )


def _s5_fwd(u, ar, ai, bre, bim, cre, cim, d_vec):
    t = u.shape[0]
    tb = _scan_rows(t)
    nt, seg, nb = t // tb, tb // 8, bre.shape[0]
    assert 1 << int(math.log2(seg)) == seg

    def body(u_ref, ar_ref, ai_ref, bre_ref, bim_ref, cre_ref, cim_ref, d_ref, sr_ref, si_ref, yp_ref, yg_ref,
             cr, ci, dr_s, di_s):
        @pl.when(pl.program_id(1) == 0)
        def _():
            cr[...] = jnp.zeros_like(cr)
            ci[...] = jnp.zeros_like(ci)

        uv = u_ref[...]
        ub = uv.astype(MXU)
        dr_s[...] = _dot(ub, bre_ref[...])
        di_s[...] = _dot(ub, bim_ref[...])
        ar8 = jnp.broadcast_to(ar_ref[...], (8, S5_BN))
        ai8 = jnp.broadcast_to(ai_ref[...], (8, S5_BN))

        def emit(j, s_r, s_i):
            sr_ref[j * 8:(j + 1) * 8, :] = s_r
            si_ref[j * 8:(j + 1) * 8, :] = s_i

        _sweeps(ar8, ai8, dr_s, di_s, cr, ci, seg, False, emit)
        yp = _dot(sr_ref[...].astype(MXU), cre_ref[...]) - _dot(si_ref[...].astype(MXU), cim_ref[...]) + d_ref[...] * uv
        yp_ref[...] = yp
        yg_ref[...] = _gelu(yp).astype(yg_ref.dtype)

    sp = _s5_specs(tb, nt, False)
    return pl.pallas_call(
        body, name="s5_fwd", grid=(nb, nt),
        in_specs=[sp["small"], sp["vec_w"], sp["vec_w"], sp["w_in"], sp["w_in"], sp["w_out"], sp["w_out"], sp["vec_s"]],
        out_specs=[sp["wide"], sp["wide"], sp["small"], sp["small"]],
        out_shape=[_sds((t, nb * S5_BN), F32), _sds((t, nb * S5_BN), F32), _sds((t, nb * S5_BK), F32),
                   _sds((t, nb * S5_BK), MXU)],
        scratch_shapes=[pltpu.VMEM((1, S5_BN), F32), pltpu.VMEM((1, S5_BN), F32), pltpu.VMEM((tb, S5_BN), F32),
                        pltpu.VMEM((tb, S5_BN), F32)],
        compiler_params=_params("parallel", "arbitrary"),
    )(u, ar, ai, bre, bim, cre, cim, d_vec)


def _s5_bwd(dyp, u, sr, si, ar, ai, cre_t, cim_t, bre_t, bim_t, d_vec):
    t = u.shape[0]
    tb = _scan_rows(t)
    nt, seg, nb = t // tb, tb // 8, cre_t.shape[0]

    def body(dyp_ref, u_ref, pr_ref, pi_ref, hr_ref, hi_ref, ar_ref, ai_ref, cre_ref, cim_ref, bre_ref, bim_ref, d_ref,
             du_ref, gr_ref, gi_ref, dbr_ref, dbi_ref, dcr_ref, dci_ref, dd_ref, cr, ci, dr_s, di_s, lr_s, li_s):
        step = pl.program_id(1)

        @pl.when(step == 0)
        def _():
            for r in (cr, ci, gr_ref, gi_ref, dbr_ref, dbi_ref, dcr_ref, dci_ref, dd_ref):
                r[...] = jnp.zeros_like(r)

        dyv, uv = dyp_ref[...], u_ref[...]
        dyb, ub = dyv.astype(MXU), uv.astype(MXU)
        dr_s[...] = _dot(dyb, cre_ref[...])
        di_s[...] = -_dot(dyb, cim_ref[...])
        ar8 = jnp.broadcast_to(ar_ref[...], (8, S5_BN))
        ai8 = jnp.broadcast_to(-ai_ref[...], (8, S5_BN))
        first_block = step == nt - 1
        row = lax.broadcasted_iota(jnp.int32, (8, S5_BN), 0)
        acc = [jnp.zeros((8, S5_BN), F32), jnp.zeros((8, S5_BN), F32)]

        def emit(j, s_r, s_i):
            lr_s[j * 8:(j + 1) * 8, :] = s_r
            li_s[j * 8:(j + 1) * 8, :] = s_i
            if j > 0:
                p_r, p_i = pr_ref[(j - 1) * 8:j * 8, :], pi_ref[(j - 1) * 8:j * 8, :]
            else:
                halo_r = jnp.where(first_block, 0.0, hr_ref[7:8, :])
                halo_i = jnp.where(first_block, 0.0, hi_ref[7:8, :])
                p_r = jnp.where(row == 0, halo_r, pltpu.roll(pr_ref[(seg - 1) * 8:seg * 8, :], 1, 0))
                p_i = jnp.where(row == 0, halo_i, pltpu.roll(pi_ref[(seg - 1) * 8:seg * 8, :], 1, 0))
            acc[0] = acc[0] + (p_r * s_r + p_i * s_i)
            acc[1] = acc[1] + (p_r * s_i - p_i * s_r)

        _sweeps(ar8, ai8, dr_s, di_s, cr, ci, seg, True, emit)
        gr_ref[...] += jnp.sum(acc[0], axis=0, keepdims=True)
        gi_ref[...] += jnp.sum(acc[1], axis=0, keepdims=True)
        lrb, lib = lr_s[...].astype(MXU), li_s[...].astype(MXU)
        du_ref[...] = (_dot(lrb, bre_ref[...]) + _dot(lib, bim_ref[...]) + d_ref[...] * dyv).astype(du_ref.dtype)
        dbr_ref[...] += _dot(ub, lrb, _TN)
        dbi_ref[...] += _dot(ub, lib, _TN)
        dcr_ref[...] += _dot(pr_ref[...].astype(MXU), dyb, _TN)
        dci_ref[...] += _dot(pi_ref[...].astype(MXU), dyb, _TN)
        dd_ref[...] += jnp.sum(dyv * uv, axis=0, keepdims=True)

    sp = _s5_specs(tb, nt, True)
    return pl.pallas_call(
        body, name="s5_bwd", grid=(nb, nt),
        in_specs=[sp["small"], sp["small"], sp["wide"], sp["wide"], sp["halo"], sp["halo"], sp["vec_w"], sp["vec_w"],
                  sp["w_in"], sp["w_in"], sp["w_out"], sp["w_out"], sp["vec_s"]],
        out_specs=[sp["small"], sp["vec_w"], sp["vec_w"], sp["w_in"], sp["w_in"], sp["w_out"], sp["w_out"], sp["vec_s"]],
        out_shape=[_sds((t, nb * S5_BK), MXU), _sds((1, nb * S5_BN), F32), _sds((1, nb * S5_BN), F32),
                   _sds((nb, S5_BK, S5_BN), F32), _sds((nb, S5_BK, S5_BN), F32), _sds((nb, S5_BN, S5_BK), F32),
                   _sds((nb, S5_BN, S5_BK), F32), _sds((1, nb * S5_BK), F32)],
        scratch_shapes=[pltpu.VMEM((1, S5_BN), F32), pltpu.VMEM((1, S5_BN), F32)] + [pltpu.VMEM((tb, S5_BN), F32)] * 4,
        compiler_params=_params("parallel", "arbitrary"),
    )(dyp, u, sr, si, sr, si, ar, ai, cre_t, cim_t, bre_t, bim_t, d_vec)


def _s5_gate_bwd(d_ycat, gp, ypre):
    t, d = gp.shape
    tb = _tile(t, 256, 8)

    def body(do_ref, gp_ref, yp_ref, o_ref):
        sg = _sigmoid(gp_ref[...])
        o_ref[...] = (do_ref[...] * _gelu(yp_ref[...]) * sg * (1.0 - sg)).astype(o_ref.dtype)

    return pl.pallas_call(
        body, name="s5_gate_bwd", grid=(t // tb,), in_specs=[_row(tb, d), _row(tb, d), _row(tb, d)],
        out_specs=_row(tb, d), out_shape=_sds((t, d), MXU), compiler_params=_params("parallel"),
    )(d_ycat, gp, ypre)


def _ssd_consts():
    head = jnp.arange(HP)[:, None]
    lane = jnp.arange(D_SSD)[None, :]
    expand = ((lane // SSD_HEADDIM) == head).astype(MXU)
    ll = jnp.arange(SSD_CHUNK)
    tri = (ll[:, None] >= ll[None, :]).astype(F32)
    return expand, expand.T, tri, jnp.eye(HP, dtype=F32)


def _ssd_chunk_terms(cp, dtr, par, expand, tri):
    ln = SSD_CHUNK
    xbc = _silu(cp)
    xs, bm, cm = xbc[:, :D_SSD], xbc[:, D_SSD:D_SSD + SSD_BC], xbc[:, D_SSD + SSD_BC:]
    dt = _softplus(dtr + par[0:1, :])
    a = -jnp.exp(par[1:2, :])
    da = dt * a
    acum = _dot(tri, da, precision=HI)
    acum_t = _dot(da, tri, (((0,), (1,)), ((), ())), precision=HI)
    atot = acum[ln - 1:ln, :]
    dt_e = _dot_split(dt, expand)
    eac_e = _dot_split(jnp.exp(acum), expand)
    dec_e = _dot_split(jnp.exp(atot - acum), expand)
    eat_e = _dot_split(_rows8(jnp.exp(atot)), expand)[0:1, :]
    dsk_e = _dot_split(_rows8(par[2:3, :]), expand)[0:1, :]
    return dict(xs=xs, bm=bm, cm=cm, dt=dt, a=a, acum=acum, acum_t=acum_t, dt_e=dt_e, eac_e=eac_e,
                dec_e=dec_e, eat_e=eat_e, dsk_e=dsk_e)


def _decay_matrix(acum, acum_t, h, mask):
    diff = acum[:, h:h + 1] - acum_t[h:h + 1, :]
    return jnp.where(mask, jnp.exp(jnp.minimum(diff, 0.0)), 0.0)


def _ssd_fwd(proj, conv_w, conv_b, dtr, par, gnorm, consts):
    t = proj.shape[0]
    ln = SSD_CHUNK
    nc = t // ln
    expand, _, tri, _ = consts
    hd2 = 2 * SSD_HEADDIM

    def body(cur_ref, prev_ref, cw_ref, cb_ref, z_ref, dtr_ref, par_ref, g_ref, e_ref, tri_ref,
             out_ref, y_ref, st_ref, cp_ref, state, ext):
        first = pl.program_id(0) == 0

        @pl.when(first)
        def _():
            state[...] = jnp.zeros_like(state)

        st_ref[...] = state[...]
        ext[0:8, :] = jnp.where(first, 0.0, prev_ref[...])
        ext[8:ln + 8, :] = cur_ref[...]
        cpv = jnp.broadcast_to(cb_ref[...], (ln, D_CONV_CH))
        for j in range(SSD_CONV):
            cpv = cpv + cw_ref[SSD_CONV - 1 - j:SSD_CONV - j, :] * ext[8 - j:8 - j + ln, :]
        cp_ref[...] = cpv
        c = _ssd_chunk_terms(cpv, dtr_ref[...], par_ref[...], e_ref[...], tri_ref[...])
        xdt = c["xs"] * c["dt_e"]
        xb, xd = xdt.astype(MXU), (xdt * c["dec_e"]).astype(MXU)
        bb, cb = c["bm"].astype(MXU), c["cm"].astype(MXU)
        mask = lax.broadcasted_iota(jnp.int32, (ln, ln), 0) >= lax.broadcasted_iota(jnp.int32, (ln, ln), 1)
        left = lax.broadcasted_iota(jnp.int32, (ln, hd2), 1) < SSD_HEADDIM
        for g in range(SSD_GROUPS):
            nsl = slice(g * SSD_STATE, (g + 1) * SSD_STATE)
            gsl = slice(g * 256, (g + 1) * 256)
            bg, cg = bb[:, nsl], cb[:, nsl]
            cbm = _dot(cg, bg, _NT)
            st_g = state[:, gsl]
            for pair in range(2):
                h0 = g * 4 + pair * 2
                psl = slice(h0 * SSD_HEADDIM, (h0 + 2) * SSD_HEADDIM)
                m0 = (cbm * _decay_matrix(c["acum"], c["acum_t"], h0, mask)).astype(MXU)
                m1 = (cbm * _decay_matrix(c["acum"], c["acum_t"], h0 + 1, mask)).astype(MXU)
                y_ref[:, psl] = jnp.where(left, _dot(m0, xb[:, psl]), _dot(m1, xb[:, psl]))
            y_ref[:, gsl] += _dot(cg, st_g.astype(MXU)) * c["eac_e"][:, gsl]
            state[:, gsl] = st_g * c["eat_e"][:, gsl] + _dot(bg, xd[:, gsl], _TN)
        y = y_ref[...] + c["dsk_e"] * c["xs"]
        y_ref[...] = y
        y2 = y * _silu(z_ref[...])
        r = lax.rsqrt(jnp.mean(y2 * y2, axis=-1, keepdims=True) + EPS)
        out_ref[...] = (y2 * r * g_ref[...]).astype(out_ref.dtype)

    xbc_block = (D_MAIN - D_CONV_CH) // D_CONV_CH
    return pl.pallas_call(
        body, name="ssd_fwd", grid=(nc,),
        in_specs=[_row(ln, D_CONV_CH, xbc_block),
                  pl.BlockSpec((8, D_CONV_CH), lambda i: (jnp.maximum(i * (ln // 8) - 1, 0), xbc_block)),
                  _const((SSD_CONV, D_CONV_CH)), _const((1, D_CONV_CH)),
                  _row(ln, D_SSD, 1), _row(ln, HP), _const((8, HP)), _const((1, D_SSD)),
                  _const((HP, D_SSD)), _const((ln, ln))],
        out_specs=[_row(ln, D_SSD), _row(ln, D_SSD), pl.BlockSpec((None, SSD_STATE, D_SSD), lambda i: (i, 0, 0)),
                   _row(ln, D_CONV_CH)],
        out_shape=[_sds((t, D_SSD), MXU), _sds((t, D_SSD), F32), _sds((nc, SSD_STATE, D_SSD), F32),
                   _sds((t, D_CONV_CH), F32)],
        scratch_shapes=[pltpu.VMEM((SSD_STATE, D_SSD), F32), pltpu.VMEM((ln + 8, D_CONV_CH), F32)],
        compiler_params=_params("arbitrary"),
    )(proj, proj, conv_w, conv_b.reshape(1, D_CONV_CH), proj, dtr, par, gnorm.reshape(1, D_SSD), expand, tri)


def _ssd_bwd(proj, conv_pre, conv_w, dtr, par, gnorm, y, states, d_ycat, consts):
    t = proj.shape[0]
    ln = SSD_CHUNK
    nc = t // ln
    expand, expand_t, tri, eye = consts
    hd2 = 2 * SSD_HEADDIM

    def body(cp_ref, z_ref, dtr_ref, par_ref, g_ref, y_ref, st_ref, do_ref, e_ref, et_ref, tri_ref, eye_ref,
             cur_ref, prev_ref, cw_ref,
             dxbc_ref, dz_ref, ddt_ref, dg_ref, dpar_ref, dcw_ref, dcb_ref,
             dstate, dx_buf, lane_buf, tot_buf, colsum, dcp_ref, ext, dext):
        @pl.when(pl.program_id(0) == 0)
        def _():
            dstate[...] = jnp.zeros_like(dstate)
            dg_ref[...] = jnp.zeros_like(dg_ref)
            dpar_ref[...] = jnp.zeros_like(dpar_ref)
            dcw_ref[...] = jnp.zeros_like(dcw_ref)
            dcb_ref[...] = jnp.zeros_like(dcb_ref)
            dext[ln:ln + 8, :] = jnp.zeros((8, D_CONV_CH), F32)

        cpv, et_v, tri_v, par_v = cp_ref[...], et_ref[...], tri_ref[...], par_ref[...]
        c = _ssd_chunk_terms(cpv, dtr_ref[...], par_v, e_ref[...], tri_v)
        xs = c["xs"]
        zv, yv, dov = z_ref[...], y_ref[...], do_ref[...]
        sz = _silu(zv)
        y2 = yv * sz
        r = lax.rsqrt(jnp.mean(y2 * y2, axis=-1, keepdims=True) + EPS)
        yh = y2 * r
        dg_ref[...] += jnp.sum(dov * yh, axis=0, keepdims=True)
        dyh = dov * g_ref[...]
        dy2 = r * (dyh - yh * jnp.mean(dyh * yh, axis=-1, keepdims=True))
        dz_ref[...] = (dy2 * yv * _silu_grad(zv)).astype(dz_ref.dtype)
        dy = dy2 * sz

        xdt = xs * c["dt_e"]
        xdf = xdt * c["dec_e"]
        xb, xd = xdt.astype(MXU), xdf.astype(MXU)
        bb, cb = c["bm"].astype(MXU), c["cm"].astype(MXU)
        dyb, dye = dy.astype(MXU), (dy * c["eac_e"]).astype(MXU)
        mask = lax.broadcasted_iota(jnp.int32, (ln, ln), 0) >= lax.broadcasted_iota(jnp.int32, (ln, ln), 1)
        left = lax.broadcasted_iota(jnp.int32, (ln, hd2), 1) < SSD_HEADDIM
        lane_hp = lax.broadcasted_iota(jnp.int32, (ln, HP), 1)
        d_acum = jnp.zeros((ln, HP), F32)
        colsum[...] = jnp.zeros_like(colsum)
        tot_buf[...] = jnp.zeros_like(tot_buf)
        for g in range(SSD_GROUPS):
            nsl = slice(g * SSD_STATE, (g + 1) * SSD_STATE)
            gsl = slice(g * 256, (g + 1) * 256)
            bg, cg = bb[:, nsl], cb[:, nsl]
            cbm = _dot(cg, bg, _NT)
            st_g = st_ref[:, gsl]
            dst_g = dstate[:, gsl]
            stb, dstb = st_g.astype(MXU), dst_g.astype(MXU)
            y_off = _dot(cg, stb)
            bds = _dot(bg, dstb)
            dcb = jnp.zeros((ln, ln), F32)
            for pair in range(2):
                h0 = g * 4 + pair * 2
                psl = slice(h0 * SSD_HEADDIM, (h0 + 2) * SSD_HEADDIM)
                xp, dyp = xb[:, psl], dyb[:, psl]
                dxp = []
                for k in range(2):
                    h = h0 + k
                    lm = _decay_matrix(c["acum"], c["acum_t"], h, mask)
                    mm = cbm * lm
                    half = left if k == 0 else jnp.logical_not(left)
                    dm = _dot(jnp.where(half, dyp, jnp.zeros_like(dyp)), xp, _NT)
                    dcb = dcb + dm * lm
                    gm = dm * mm
                    d_acum = d_acum + jnp.where(lane_hp == h, jnp.sum(gm, axis=1, keepdims=True), 0.0)
                    colsum[h:h + 1, :] = jnp.sum(gm, axis=0, keepdims=True)
                    dxp.append(_dot(mm.astype(MXU), dyp, _TN))
                dx_buf[:, psl] = jnp.where(left, dxp[0], dxp[1])
            dx_buf[:, gsl] += bds * c["dec_e"][:, gsl]
            dcbb = dcb.astype(MXU)
            dc_g = _dot(dcbb, bg) + _dot(dye[:, gsl], stb, _NT)
            db_g = _dot(dcbb, cg, _TN) + _dot(xd[:, gsl], dstb, _NT)
            dcp_ref[:, D_SSD + g * SSD_STATE:D_SSD + (g + 1) * SSD_STATE] = db_g
            dcp_ref[:, D_SSD + SSD_BC + g * SSD_STATE:D_SSD + SSD_BC + (g + 1) * SSD_STATE] = dc_g
            dec_term = xdf[:, gsl] * bds
            lane_buf[:, gsl] = dy[:, gsl] * y_off * c["eac_e"][:, gsl] - dec_term
            tot_buf[0:1, gsl] = (jnp.sum(st_g * dst_g, axis=0, keepdims=True) * c["eat_e"][:, gsl]
                                 + jnp.sum(dec_term, axis=0, keepdims=True))
            dstate[:, gsl] = dst_g * c["eat_e"][:, gsl] + _dot(cg, dye[:, gsl], _TN)
        dx_tot = dx_buf[...]
        d_acum = d_acum + _dot_split(lane_buf[...], et_v) - _dot(colsum[...], eye_ref[...], _TN, precision=HI)
        d_atot = _dot_split(tot_buf[...], et_v)[0:1, :]
        row_hp = lax.broadcasted_iota(jnp.int32, (ln, HP), 0)
        d_acum = d_acum + jnp.where(row_hp == ln - 1, d_atot, 0.0)
        d_da = _dot(tri_v, d_acum, _TN, precision=HI)
        d_dt = d_da * c["a"] + _dot_split(dx_tot * xs, et_v)
        d_dtr = d_dt * _sigmoid(dtr_ref[...] + par_v[0:1, :])
        ddt_ref[...] = d_dtr
        dpar_ref[0:1, :] += jnp.sum(d_dtr, axis=0, keepdims=True)
        dpar_ref[1:2, :] += jnp.sum(d_da * c["dt"], axis=0, keepdims=True) * c["a"]
        dpar_ref[2:3, :] += _dot_split(_rows8(jnp.sum(dy * xs, axis=0, keepdims=True)), et_v)[0:1, :]
        dcp_ref[:, 0:D_SSD] = dx_tot * c["dt_e"] + dy * c["dsk_e"]
        dcv = dcp_ref[...] * _silu_grad(cpv)
        ext[0:8, :] = jnp.where(pl.program_id(0) == nc - 1, 0.0, prev_ref[...])
        ext[8:ln + 8, :] = cur_ref[...]
        dext[0:ln, :] = dcv
        dxbc = jnp.zeros((ln, D_CONV_CH), F32)
        for j in range(SSD_CONV):
            dxbc = dxbc + cw_ref[SSD_CONV - 1 - j:SSD_CONV - j, :] * dext[j:j + ln, :]
            dcw_ref[SSD_CONV - 1 - j:SSD_CONV - j, :] += jnp.sum(dcv * ext[8 - j:8 - j + ln, :], axis=0, keepdims=True)
        dxbc_ref[...] = dxbc.astype(dxbc_ref.dtype)
        dcb_ref[...] += jnp.sum(dcv, axis=0, keepdims=True)
        dext[ln:ln + 8, :] = dcv[0:8, :]

    xbc_block = (D_MAIN - D_CONV_CH) // D_CONV_CH
    rev = lambda i: (nc - 1 - i, 0)
    rev1 = lambda i: (nc - 1 - i, 1)
    return pl.pallas_call(
        body, name="ssd_bwd", grid=(nc,),
        in_specs=[pl.BlockSpec((ln, D_CONV_CH), rev), pl.BlockSpec((ln, D_SSD), rev1), pl.BlockSpec((ln, HP), rev),
                  _const((8, HP)), _const((1, D_SSD)), pl.BlockSpec((ln, D_SSD), rev),
                  pl.BlockSpec((None, SSD_STATE, D_SSD), lambda i: (nc - 1 - i, 0, 0)),
                  pl.BlockSpec((ln, D_SSD), rev1),
                  _const((HP, D_SSD)), _const((D_SSD, HP)), _const((ln, ln)), _const((HP, HP)),
                  pl.BlockSpec((ln, D_CONV_CH), lambda i: (nc - 1 - i, xbc_block)),
                  pl.BlockSpec((8, D_CONV_CH), lambda i: (jnp.maximum((nc - 1 - i) * (ln // 8) - 1, 0), xbc_block)),
                  _const((SSD_CONV, D_CONV_CH))],
        out_specs=[pl.BlockSpec((ln, D_CONV_CH), rev), pl.BlockSpec((ln, D_SSD), rev), pl.BlockSpec((ln, HP), rev),
                   _const((1, D_SSD)), _const((8, HP)), _const((SSD_CONV, D_CONV_CH)), _const((1, D_CONV_CH))],
        out_shape=[_sds((t, D_CONV_CH), MXU), _sds((t, D_SSD), MXU), _sds((t, HP), F32), _sds((1, D_SSD), F32),
                   _sds((8, HP), F32), _sds((SSD_CONV, D_CONV_CH), F32), _sds((1, D_CONV_CH), F32)],
        scratch_shapes=[pltpu.VMEM((SSD_STATE, D_SSD), F32), pltpu.VMEM((ln, D_SSD), F32), pltpu.VMEM((ln, D_SSD), F32),
                        pltpu.VMEM((8, D_SSD), F32), pltpu.VMEM((HP, ln), F32), pltpu.VMEM((ln, D_CONV_CH), F32),
                        pltpu.VMEM((ln + 8, D_CONV_CH), F32), pltpu.VMEM((ln + 8, D_CONV_CH), F32)],
        compiler_params=_params("arbitrary"),
    )(conv_pre, proj, dtr, par, gnorm.reshape(1, D_SSD), y, states, d_ycat, expand, expand_t, tri, eye, proj, proj, conv_w)


def _softmax_rows(s):
    e = jnp.exp(s - jnp.max(s, axis=-1, keepdims=True))
    return e * _recip(jnp.sum(e, axis=-1, keepdims=True))


def _attn_fwd(q, k, v):
    t, d = q.shape
    mlen = k.shape[0]
    tq = _tile(t, 512, 8)
    scale = XA_HEAD_DIM ** -0.5

    def body(q_ref, k_ref, v_ref, o_ref):
        for h in range(XA_HEADS):
            sl = slice(h * XA_HEAD_DIM, (h + 1) * XA_HEAD_DIM)
            p = _softmax_rows(_dot(q_ref[:, sl], k_ref[:, sl], _NT) * scale)
            o_ref[:, sl] = _dot(p.astype(MXU), v_ref[:, sl]).astype(o_ref.dtype)

    return pl.pallas_call(
        body, name="xattn_fwd", grid=(t // tq,),
        in_specs=[_row(tq, d), _const((mlen, d)), _const((mlen, d))], out_specs=_row(tq, d),
        out_shape=_sds((t, d), MXU), compiler_params=_params("parallel"),
    )(q, k, v)


def _attn_bwd(q, k, v, do):
    t, d = q.shape
    mlen = k.shape[0]
    tq = _tile(t, 512, 8)
    scale = XA_HEAD_DIM ** -0.5

    def body(q_ref, k_ref, v_ref, do_ref, dq_ref, dk_ref, dv_ref):
        @pl.when(pl.program_id(0) == 0)
        def _():
            dk_ref[...] = jnp.zeros_like(dk_ref)
            dv_ref[...] = jnp.zeros_like(dv_ref)

        for h in range(XA_HEADS):
            sl = slice(h * XA_HEAD_DIM, (h + 1) * XA_HEAD_DIM)
            qh, kh, vh, doh = q_ref[:, sl], k_ref[:, sl], v_ref[:, sl], do_ref[:, sl]
            p = _softmax_rows(_dot(qh, kh, _NT) * scale)
            dp = _dot(doh, vh, _NT)
            dv_ref[:, sl] += _dot(p.astype(MXU), doh, _TN)
            ds = (p * (dp - jnp.sum(p * dp, axis=-1, keepdims=True)) * scale).astype(MXU)
            dq_ref[:, sl] = _dot(ds, kh).astype(dq_ref.dtype)
            dk_ref[:, sl] += _dot(ds, qh, _TN)

    return pl.pallas_call(
        body, name="xattn_bwd", grid=(t // tq,),
        in_specs=[_row(tq, d), _const((mlen, d)), _const((mlen, d)), _row(tq, d)],
        out_specs=[_row(tq, d), _const((mlen, d)), _const((mlen, d))],
        out_shape=[_sds((t, d), MXU), _sds((mlen, d), F32), _sds((mlen, d), F32)],
        compiler_params=_params("arbitrary"),
    )(q, k, v, do)


def _lane_view(a):
    if a.ndim >= 2 and a.shape[-1] >= 128:
        return a.reshape(-1, a.shape[-1])
    if a.size % 128 == 0:
        return a.reshape(-1, 128)
    return a.reshape(1, -1)


def _adamw(w, g, m, v, *, name):
    shape = w.shape
    w2, g2, m2, v2 = (_lane_view(a) for a in (w, g.reshape(shape), m, v))
    r, c = w2.shape
    tr = _tile(r, max(8, (1 << 18) // max(c, 128)), 8)

    def body(w_ref, g_ref, m_ref, v_ref, d_ref, mo_ref, vo_ref):
        gv = g_ref[...]
        mn = ADAM_B1 * m_ref[...] + (1.0 - ADAM_B1) * gv
        vn = ADAM_B2 * v_ref[...] + (1.0 - ADAM_B2) * (gv * gv)
        m_hat = mn / (1.0 - ADAM_B1 ** ADAM_STEP)
        v_hat = vn / (1.0 - ADAM_B2 ** ADAM_STEP)
        d_ref[...] = -ADAM_LR * (m_hat / (jnp.sqrt(v_hat) + ADAM_EPS) + ADAM_WD * w_ref[...])
        mo_ref[...] = mn
        vo_ref[...] = vn

    outs = pl.pallas_call(
        body, name=name, grid=(r // tr,), in_specs=[_row(tr, c)] * 4, out_specs=[_row(tr, c)] * 3,
        out_shape=[_sds((r, c), F32)] * 3, compiler_params=_params("parallel"),
    )(w2, g2, m2, v2)
    return tuple(o.reshape(shape) for o in outs)


_HBM = pl.BlockSpec(memory_space=pltpu.HBM)


N_CHIPS = 4
CHIPS = ((0, 0), (0, 1), (1, 0), (1, 1))


def _pair_exchange(x, *, name):
    def body(x_ref, o_ref, send_sems, recv_sems):
        xx, yy, cc = (lax.axis_index(a) for a in AXES)
        copies = [
            pltpu.make_async_remote_copy(
                src_ref=x_ref.at[4 * px + 2 * py + (1 - cc)], dst_ref=o_ref.at[k], send_sem=send_sems.at[k],
                recv_sem=recv_sems.at[k], device_id=(xx, yy, 1 - cc), device_id_type=pl.DeviceIdType.MESH)
            for k, (px, py) in enumerate(CHIPS)]
        for cp in copies:
            cp.start()
        for cp in copies:
            cp.wait_recv()
        for cp in copies:
            cp.wait_send()

    return pl.pallas_call(
        body, name=name, in_specs=[_HBM], out_specs=_HBM, out_shape=_sds((N_CHIPS,) + tuple(x.shape[1:]), x.dtype),
        scratch_shapes=[pltpu.SemaphoreType.DMA((N_CHIPS,)), pltpu.SemaphoreType.DMA((N_CHIPS,))],
    )(x)


def _pair_add(x, got, *, name):
    _, r, c = x.shape
    tr = _tile(r, max(PAD_ROWS, (1 << 17) // c), PAD_ROWS)

    def body(x_ref, g_ref, o_ref):
        mine = jnp.where(lax.axis_index("c") == 0, x_ref[0].astype(F32), x_ref[1].astype(F32))
        o_ref[...] = (mine + g_ref[...].astype(F32)).astype(o_ref.dtype)

    return pl.pallas_call(
        body, name=name, grid=(N_CHIPS, r // tr),
        in_specs=[pl.BlockSpec((None, 2, tr, c), lambda k, i: (k, 0, i, 0)), pl.BlockSpec((None, tr, c), lambda k, i: (k, i, 0))],
        out_specs=pl.BlockSpec((None, tr, c), lambda k, i: (k, i, 0)), out_shape=_sds((N_CHIPS, r, c), x.dtype),
        compiler_params=_params("parallel", "parallel"),
    )(x.reshape(N_CHIPS, 2, r, c), got)


def _chip_exchange(x, *, name):
    def body(x_ref, o_ref, send_sems, recv_sems, local_sem):
        xx, yy, cc = (lax.axis_index(a) for a in AXES)
        mine = 2 * xx + yy
        local = pltpu.make_async_copy(x_ref.at[mine], o_ref.at[mine], local_sem)
        local.start()
        sends = []
        for j, (px, py) in enumerate([(1 - xx, yy), (xx, 1 - yy), (1 - xx, 1 - yy)]):
            cp = pltpu.make_async_remote_copy(
                src_ref=x_ref.at[2 * px + py], dst_ref=o_ref.at[mine], send_sem=send_sems.at[j], recv_sem=recv_sems.at[j],
                device_id=(px, py, cc), device_id_type=pl.DeviceIdType.MESH)
            cp.start()
            sends.append(cp)
        for j, (px, py) in enumerate([(1 - xx, yy), (xx, 1 - yy), (1 - xx, 1 - yy)]):
            pltpu.make_async_remote_copy(
                src_ref=x_ref.at[2 * px + py], dst_ref=o_ref.at[2 * px + py], send_sem=send_sems.at[j],
                recv_sem=recv_sems.at[j], device_id=(px, py, cc), device_id_type=pl.DeviceIdType.MESH).wait_recv()
        for cp in sends:
            cp.wait_send()
        local.wait()

    return pl.pallas_call(
        body, name=name, in_specs=[_HBM], out_specs=_HBM, out_shape=_sds(x.shape, x.dtype),
        scratch_shapes=[pltpu.SemaphoreType.DMA((N_CHIPS - 1,)), pltpu.SemaphoreType.DMA((N_CHIPS - 1,)),
                        pltpu.SemaphoreType.DMA(())],
    )(x)


def _reduce_scatter(x, *, name):
    chip_sums = _pair_add(x, _pair_exchange(x, name=name + "_pair"), name=name + "_pair_add")
    return _sum_slots(_chip_exchange(chip_sums, name=name + "_chips"), name=name + "_sum")


def _all_gather(x, *, name):
    def body(x_ref, o_ref, send_sems, recv_sems, local_sem):
        xx, yy, cc = (lax.axis_index(a) for a in AXES)
        me, sibling = (xx, yy, cc), (xx, yy, 1 - cc)
        chips = [(1 - xx, yy), (xx, 1 - yy), (1 - xx, 1 - yy)]

        def slot(px, py, pc):
            return o_ref.at[4 * px + 2 * py + pc]

        def copy(k, block, to, src=None):
            return pltpu.make_async_remote_copy(
                src_ref=slot(*block) if src is None else src, dst_ref=slot(*block), send_sem=send_sems.at[k],
                recv_sem=recv_sems.at[k], device_id=to, device_id_type=pl.DeviceIdType.MESH)

        local = pltpu.make_async_copy(x_ref, slot(*me), local_sem)
        local.start()
        first = [copy(0, me, sibling, src=x_ref)] + [copy(1 + j, me, (*chip, cc), src=x_ref) for j, chip in enumerate(chips)]
        for cp in first:
            cp.start()
        passed = [copy(4 + j, (*chip, cc), sibling) for j, chip in enumerate(chips)]
        for j, chip in enumerate(chips):
            copy(1 + j, (*chip, cc), me).wait_recv()
            passed[j].start()
        copy(0, sibling, me).wait_recv()
        for j, chip in enumerate(chips):
            copy(4 + j, (*chip, 1 - cc), me).wait_recv()
        for cp in first + passed:
            cp.wait_send()
        local.wait()

    return pl.pallas_call(
        body, name=name, in_specs=[_HBM], out_specs=_HBM, out_shape=_sds((N_DEV,) + tuple(x.shape), x.dtype),
        scratch_shapes=[pltpu.SemaphoreType.DMA((N_DEV - 1,)), pltpu.SemaphoreType.DMA((N_DEV - 1,)),
                        pltpu.SemaphoreType.DMA(())],
    )(x)


def _sum_slots(x, *, name):
    n, r, c = x.shape
    tr = _tile(r, max(PAD_ROWS, (1 << 17) // c), PAD_ROWS)

    def body(x_ref, o_ref):
        acc = x_ref[0].astype(F32)
        for d in range(1, n):
            acc = acc + x_ref[d].astype(F32)
        o_ref[...] = acc

    return pl.pallas_call(
        body, name=name, grid=(r // tr,), in_specs=[pl.BlockSpec((n, tr, c), lambda i: (0, i, 0))],
        out_specs=_row(tr, c), out_shape=_sds((r, c), F32), compiler_params=_params("parallel"),
    )(x)


def _s5_layouts(bbar_r, bbar_i, c_re, c_im):
    eye = jnp.eye(S5_GPB, dtype=F32)

    def b_blocks(bbar):
        bb = bbar.reshape(S5_NB, S5_GPB, S5_STATE, S5_GROUP)
        return jnp.einsum("jgph,gk->jghkp", bb, eye).reshape(S5_NB, S5_GPB * S5_GROUP, S5_GPB * S5_STATE)

    def c_blocks(cc):
        c4 = cc.reshape(S5_NB, S5_GPB, S5_GROUP, S5_STATE)
        return jnp.einsum("jghp,gk->jgpkh", c4, eye).reshape(S5_NB, S5_GPB * S5_STATE, S5_GPB * S5_GROUP)

    bre, bim, cre, cim = b_blocks(bbar_r), b_blocks(bbar_i), c_blocks(c_re), c_blocks(c_im)
    cast = lambda a: a.astype(MXU)
    sw = lambda a: jnp.swapaxes(a, 1, 2).astype(MXU)
    return dict(bre=cast(bre), bim=cast(bim), cre=cast(cre), cim=cast(cim), bre_t=sw(bre), bim_t=sw(bim), cre_t=sw(cre),
                cim_t=sw(cim))


def _b_diag(db):
    d5 = db.reshape(S5_NB, S5_GPB, S5_GROUP, S5_GPB, S5_STATE)
    diag = jnp.stack([d5[:, g, :, g, :] for g in range(S5_GPB)], axis=1)
    return jnp.swapaxes(diag, 2, 3).reshape(S5_GROUPS, S5_STATE * S5_GROUP)


def _c_diag(dc):
    d5 = dc.reshape(S5_NB, S5_GPB, S5_STATE, S5_GPB, S5_GROUP)
    diag = jnp.stack([d5[:, g, :, g, :] for g in range(S5_GPB)], axis=1)
    return jnp.swapaxes(diag, 2, 3).reshape(S5_GROUPS, S5_GROUP, S5_STATE)


def _head_rows(*vecs):
    par = jnp.zeros((8, HP), F32)
    for i, v in enumerate(vecs):
        par = par.at[i, :SSD_HEADS].set(v.astype(F32))
    return par


def _add(acc, r):
    return (acc + r,)


def _layer_fwd(x, mem, w, consts):
    s = {"x": x}
    rep = consts["rep"]
    s["h1"] = h1 = _rmsnorm_fwd(x, w["norm_mix"], name="norm_mix_fwd")
    s["proj"] = proj = _mm(h1, w["w_main"], name="in_proj")
    s["dtr"] = dtr = _mm(h1, w["w_dt"], name="dt_proj")
    ar, ai, ldt = w["s5_a_re"], w["s5_a_im"], w["s5_log_dt"].reshape(S5_GROUPS, 1)
    br, bi = w["s5_b_re"].reshape(S5_GROUPS, -1), w["s5_b_im"].reshape(S5_GROUPS, -1)
    abar_r, abar_i, bbar_r, bbar_i = _s5_prep(ar, ai, ldt, br, bi, rep)
    s["abar"] = abar = (abar_r.reshape(1, S5_CH), abar_i.reshape(1, S5_CH))
    s["lay"] = lay = _s5_layouts(bbar_r, bbar_i, w["s5_c_re"], w["s5_c_im"])
    s["u"] = u = _interleave_rows(proj[:, :D_S5])
    s["sr"], s["si"], s["ypre"], s["yg"] = _s5_fwd(u, *abar, lay["bre"], lay["bim"], lay["cre"], lay["cim"],
                                                   w["s5_d"].reshape(1, D_S5))
    ypre, yg = s["ypre"], s["yg"]
    s["gp"], out_s5 = _mm(yg, w["s5_w_glu"], extras=[ypre], epi=lambda acc, yp: (acc, _gelu(yp) * _sigmoid(acc)),
                          out_dtypes=(F32, MXU), name="s5_glu")
    out_s5 = _deinterleave_rows(out_s5)
    s["par"] = par = _head_rows(w["ssd_dt_bias"], w["ssd_a_log"], w["ssd_d"])
    out_ssd, s["y_ssd"], s["states"], s["conv_pre"] = _ssd_fwd(proj, w["ssd_conv_w"], w["ssd_conv_b"], dtr, par,
                                                               w["ssd_norm"], consts["ssd"])
    s["ycat"] = ycat = jnp.concatenate([out_s5, out_ssd], axis=1)
    s["x1"] = x1 = _mm(ycat, w["w_out"], extras=[x], epi=_add, name="out_proj")
    s["hq"] = hq = _rmsnorm_fwd(x1, w["norm_xattn"], name="norm_xattn_fwd")
    s["mn"] = mn = _rmsnorm_fwd(mem, w["norm_mem"], name="norm_mem_fwd")
    s["q"] = q = _mm(hq, w["xa_wq"], out_dtypes=(MXU,), name="xa_q")
    s["k"] = k = _mm(mn, w["xa_wk"], out_dtypes=(MXU,), name="xa_k")
    s["v"] = v = _mm(mn, w["xa_wv"], out_dtypes=(MXU,), name="xa_v")
    s["o"] = o = _attn_fwd(q, k, v)
    s["x2"] = x2 = _mm(o, w["xa_wo"], extras=[x1], epi=_add, name="xa_o")
    s["hm"] = hm = _rmsnorm_fwd(x2, w["norm_mlp"], name="norm_mlp_fwd")
    s["act"] = _mm(hm, w["mlp_w1"], epi=lambda acc: (jnp.square(jnp.maximum(acc, 0.0)),), out_dtypes=(MXU,),
                   name="mlp_up")
    x3 = _mm(s["act"], w["mlp_w2"], extras=[x2], epi=_add, name="mlp_down")
    return x3, s


def _layer_bwd(dx3, dx3m, mem, w, s, consts):
    g = {}
    rep = consts["rep"]
    wire = (MXU,)
    d_a = _mm(dx3m, w["mlp_w2"], tb=True, extras=[s["act"]],
              epi=lambda acc, act: (acc * (2.0 * jnp.sqrt(act.astype(F32))),), out_dtypes=(MXU,), name="mlp_down_dx")
    g["mlp_w2"] = _mm(s["act"], dx3m, ta=True, out_dtypes=wire, name="mlp_down_dw")
    g["mlp_w1"] = _mm(s["hm"], d_a, ta=True, out_dtypes=wire, name="mlp_up_dw")
    d_hm = _mm(d_a, w["mlp_w1"], tb=True, name="mlp_up_dx")
    dx2, dx2m, g["norm_mlp"] = _rmsnorm_bwd(s["x2"], w["norm_mlp"], d_hm, dx3, name="norm_mlp_bwd")
    d_o = _mm(dx2m, w["xa_wo"], tb=True, out_dtypes=(MXU,), name="xa_o_dx")
    g["xa_wo"] = _mm(s["o"], dx2m, ta=True, out_dtypes=wire, name="xa_o_dw")
    dq, dk, dv = _attn_bwd(s["q"], s["k"], s["v"], d_o)
    g["xa_wq"] = _mm(s["hq"], dq, ta=True, out_dtypes=wire, name="xa_q_dw")
    d_hq = _mm(dq, w["xa_wq"], tb=True, name="xa_q_dx")
    dx1, dx1m, g["norm_xattn"] = _rmsnorm_bwd(s["x1"], w["norm_xattn"], d_hq, dx2, name="norm_xattn_bwd")
    g["xa_wk"] = _mm(s["mn"], dk, ta=True, out_dtypes=wire, name="xa_k_dw")
    g["xa_wv"] = _mm(s["mn"], dv, ta=True, out_dtypes=wire, name="xa_v_dw")
    d_mn_v = _mm(dv, w["xa_wv"], tb=True, name="xa_v_dx")
    d_mn = _mm(dk, w["xa_wk"], tb=True, extras=[d_mn_v], epi=_add, name="xa_k_dx")
    _, _, g["norm_mem"] = _rmsnorm_bwd(mem, w["norm_mem"], d_mn, None, name="norm_mem_bwd")
    d_ycat = _mm(dx1m, w["w_out"], tb=True, name="out_proj_dx")
    g["w_out"] = _mm(s["ycat"], dx1m, ta=True, out_dtypes=wire, name="out_proj_dw")
    lay, proj, ypre, u = s["lay"], s["proj"], s["ypre"], s["u"]
    d_os5 = _interleave_rows(d_ycat[:, :D_S5])
    d_gp = _s5_gate_bwd(d_os5, s["gp"], ypre)
    g["s5_w_glu"] = _mm(s["yg"], d_gp, ta=True, out_dtypes=wire, name="s5_glu_dw")
    d_ypre = _mm(d_gp, w["s5_w_glu"], tb=True, extras=[d_os5, s["gp"], ypre],
                 epi=lambda acc, do, gp, yp: ((acc + do * _sigmoid(gp)) * _gelu_grad(yp),), name="s5_glu_dx")
    du, d_abar_r, d_abar_i, db_re, db_im, dc_re, dc_im, d_d = _s5_bwd(
        d_ypre, u, s["sr"], s["si"], *s["abar"], lay["cre_t"], lay["cim_t"], lay["bre_t"], lay["bim_t"],
        w["s5_d"].reshape(1, D_S5))
    d_bbar_r, d_bbar_i = _b_diag(db_re), _b_diag(db_im)
    g["s5_c_re"], g["s5_c_im"] = _c_diag(dc_re), -_c_diag(dc_im)
    g["s5_d"] = d_d.reshape(S5_GROUPS, S5_GROUP)
    du = _deinterleave_rows(du)
    ar, ai, ldt = w["s5_a_re"], w["s5_a_im"], w["s5_log_dt"].reshape(S5_GROUPS, 1)
    br, bi = w["s5_b_re"].reshape(S5_GROUPS, -1), w["s5_b_im"].reshape(S5_GROUPS, -1)
    d_ar, d_ai, d_ldt, d_br, d_bi = _s5_prep_bwd(
        ar, ai, ldt, br, bi, rep, d_abar_r.reshape(S5_GROUPS, S5_STATE), d_abar_i.reshape(S5_GROUPS, S5_STATE),
        d_bbar_r, d_bbar_i)
    g["s5_a_re"], g["s5_a_im"], g["s5_log_dt"] = d_ar, d_ai, d_ldt.reshape(S5_GROUPS)
    g["s5_b_re"] = d_br.reshape(S5_GROUPS, S5_STATE, S5_GROUP)
    g["s5_b_im"] = d_bi.reshape(S5_GROUPS, S5_STATE, S5_GROUP)
    d_xbc, dz, d_dtr, g["ssd_norm"], d_par, g["ssd_conv_w"], g["ssd_conv_b"] = _ssd_bwd(
        proj, s["conv_pre"], w["ssd_conv_w"], s["dtr"], s["par"], w["ssd_norm"], s["y_ssd"], s["states"], d_ycat,
        consts["ssd"])
    g["ssd_dt_bias"], g["ssd_a_log"], g["ssd_d"] = (d_par[i, :SSD_HEADS] for i in range(3))
    d_proj = jnp.concatenate([du, dz, d_xbc], axis=1)
    g_main = _mm(s["h1"], d_proj, ta=True, out_dtypes=wire, name="in_proj_dw")
    g_dt = _mm(s["h1"], d_dtr, ta=True, out_dtypes=wire, name="dt_proj_dw")
    g["w_in"] = jnp.concatenate([g_main, g_dt[:, :SSD_HEADS]], axis=1)
    d_h1_dt = _mm(d_dtr, w["w_dt"], tb=True, name="dt_proj_dx")
    d_h1 = _mm(d_proj, w["w_main"], tb=True, extras=[d_h1_dt], epi=_add, name="in_proj_dx")
    dx, dxm, g["norm_mix"] = _rmsnorm_bwd(s["x"], w["norm_mix"], d_h1, dx1, name="norm_mix_bwd")
    return dx, dxm, g


LAYER_WEIGHTS = ("norm_mix", "w_in", "s5_a_re", "s5_a_im", "s5_log_dt", "s5_b_re", "s5_b_im", "s5_c_re", "s5_c_im", "s5_d",
                 "s5_w_glu", "ssd_conv_w", "ssd_conv_b", "ssd_dt_bias", "ssd_a_log", "ssd_d", "ssd_norm", "w_out",
                 "norm_xattn", "norm_mem", "xa_wq", "xa_wk", "xa_wv", "xa_wo", "norm_mlp", "mlp_w1", "mlp_w2")
WEIGHTS = LAYER_WEIGHTS + ("norm_final",)


def _local_step(x, mem, target, weights):
    consts = {
        "ssd": _ssd_consts(),
        "rep": (jnp.arange(S5_STATE)[:, None] == jnp.arange(S5_STATE * S5_GROUP)[None, :] // S5_GROUP).astype(F32),
    }
    layers = []
    for l in range(DEPTH):
        w = {n: weights[n][l] for n in LAYER_WEIGHTS}
        w_in = w["w_in"]
        w["w_main"] = w_in[:, :D_MAIN]
        w["w_dt"] = jnp.pad(w_in[:, D_MAIN:], ((0, 0), (0, HP - SSD_HEADS)))
        layers.append(w)
    saved = []
    for l in range(DEPTH):
        x, s = _layer_fwd(x, mem, layers[l], consts)
        saved.append(s)
    loss, dx, dxm, g_final = _loss_head(x, weights["norm_final"], target)
    grads = [None] * DEPTH
    for l in reversed(range(DEPTH)):
        dx, dxm, grads[l] = _layer_bwd(dx, dxm, mem, layers[l], saved[l], consts)
    out = {n: jnp.stack([grads[l][n].reshape(weights[n].shape[1:]) for l in range(DEPTH)]) for n in LAYER_WEIGHTS}
    out["norm_final"] = g_final.reshape(weights["norm_final"].shape)
    return loss, dx, out


SHARDED = {"w_in": 2, "s5_w_glu": 1, "ssd_conv_w": 2, "w_out": 1, "xa_wq": 1, "xa_wk": 1, "xa_wv": 1, "xa_wo": 1,
           "mlp_w1": 2, "mlp_w2": 1}
EXACT = ("ssd_conv_w",)
ROW_EXCHANGE = tuple(n for n, ax in SHARDED.items() if ax == 1)
OWN_EXCHANGE = tuple(n for n in SHARDED if n not in ROW_EXCHANGE)
REPLICATED = tuple(n for n in WEIGHTS if n not in SHARDED)
LANES = 128
PAD_ROWS = 16


def _to_rows(flat, lead=()):
    n = flat.shape[-1]
    quantum = LANES * PAD_ROWS
    padded = -(-n // quantum) * quantum
    flat = jnp.pad(flat, [(0, 0)] * len(lead) + [(0, padded - n)])
    return flat.reshape(*lead, padded // LANES, LANES)


def _gather_weights(local):
    def assemble(n, seg):
        shp, ax = local[n].shape, SHARDED[n]
        return jnp.moveaxis(seg, 0, ax).reshape(*shp[:ax], N_DEV * shp[ax], *shp[ax + 1:])

    full = {}
    for n in OWN_EXCHANGE:
        payload = local[n] if n in EXACT else local[n].astype(MXU)
        full[n] = assemble(n, _all_gather(payload, name="gather_" + n))
    got = _all_gather(jnp.concatenate([local[n].astype(MXU) for n in ROW_EXCHANGE], axis=1), name="gather_row_sharded")
    off = 0
    for n in ROW_EXCHANGE:
        rows = local[n].shape[1]
        full[n] = assemble(n, got[:, :, off:off + rows])
        off += rows
    return full


def _scatter_grads(grads, local_shapes):
    def shards(n):
        shp, ax = local_shapes[n], SHARDED[n]
        gfull = grads[n].reshape(*shp[:ax], N_DEV, shp[ax], *shp[ax + 1:])
        return jnp.moveaxis(gfull, ax, 0).astype(MXU)

    out = {}
    for n in OWN_EXCHANGE:
        shp = local_shapes[n]
        out[n] = _reduce_scatter(shards(n).reshape(N_DEV, -1, shp[-1]), name="scatter_" + n).reshape(shp)
    payload = jnp.concatenate([shards(n) for n in ROW_EXCHANGE], axis=2)
    summed = _reduce_scatter(payload.reshape(N_DEV, -1, payload.shape[-1]), name="scatter_row_sharded")
    summed = summed.reshape(payload.shape[1:])
    off = 0
    for n in ROW_EXCHANGE:
        rows = local_shapes[n][1]
        out[n] = summed[:, off:off + rows]
        off += rows
    return out


def _allreduce_small(loss, grads):
    parts = [loss.reshape(-1)[:1]] + [grads[n].reshape(-1) for n in REPLICATED]
    payload = _to_rows(jnp.concatenate(parts))
    summed = _sum_slots(_all_gather(payload, name="gather_small_grads"), name="sum_small_grads").reshape(-1)
    out, off = {}, 1
    for n in REPLICATED:
        size = grads[n].size
        out[n] = summed[off:off + size].reshape(grads[n].shape)
        off += size
    return summed[0], out


def kernel(x, mem, norm_mix, w_in, s5_a_re, s5_a_im, s5_log_dt, s5_b_re, s5_b_im, s5_c_re, s5_c_im, s5_d, s5_w_glu, ssd_conv_w, ssd_conv_b, ssd_dt_bias, ssd_a_log, ssd_d, ssd_norm, w_out, norm_xattn, norm_mem, xa_wq, xa_wk, xa_wv, xa_wo, norm_mlp, mlp_w1, mlp_w2, norm_final, loss_target, m_norm_mix, m_w_in, m_s5_a_re, m_s5_a_im, m_s5_log_dt, m_s5_b_re, m_s5_b_im, m_s5_c_re, m_s5_c_im, m_s5_d, m_s5_w_glu, m_ssd_conv_w, m_ssd_conv_b, m_ssd_dt_bias, m_ssd_a_log, m_ssd_d, m_ssd_norm, m_w_out, m_norm_xattn, m_norm_mem, m_xa_wq, m_xa_wk, m_xa_wv, m_xa_wo, m_norm_mlp, m_mlp_w1, m_mlp_w2, m_norm_final, v_norm_mix, v_w_in, v_s5_a_re, v_s5_a_im, v_s5_log_dt, v_s5_b_re, v_s5_b_im, v_s5_c_re, v_s5_c_im, v_s5_d, v_s5_w_glu, v_ssd_conv_w, v_ssd_conv_b, v_ssd_dt_bias, v_ssd_a_log, v_ssd_d, v_ssd_norm, v_w_out, v_norm_xattn, v_norm_mem, v_xa_wq, v_xa_wk, v_xa_wv, v_xa_wo, v_norm_mlp, v_mlp_w1, v_mlp_w2, v_norm_final):
    args = locals()
    local = {n: args[n] for n in WEIGHTS}
    full = dict(local)
    full.update(_gather_weights(local))
    loss, grad_x, grads = _local_step(x[0], mem[0], loss_target[0], full)
    loss, g_small = _allreduce_small(loss, grads)
    g_all = _scatter_grads(grads, {n: local[n].shape for n in SHARDED})
    g_all.update(g_small)
    delta, new_m, new_v = {}, {}, {}
    for n in WEIGHTS:
        delta[n], new_m[n], new_v[n] = _adamw(local[n], g_all[n], args["m_" + n], args["v_" + n], name="adamw_" + n)
    return (loss, grad_x[None], *[g_all[n] for n in WEIGHTS], *[delta[n] for n in WEIGHTS],
            *[new_m[n] for n in WEIGHTS], *[new_v[n] for n in WEIGHTS])
```

```python
import functools
import math

import jax
import jax.numpy as jnp
from jax import lax
from jax.experimental import pallas as pl
from jax.experimental.pallas import tpu as pltpu

F32 = jnp.float32
MXU = jnp.bfloat16
HI = lax.Precision.HIGHEST

D_MODEL = 1024
DEPTH = 4
MEM_LEN = 256
D_S5 = 1024
D_SSD = 1024
S5_GROUP = 16
S5_GROUPS = 64
S5_STATE = 64
S5_CH = S5_GROUPS * S5_STATE
S5_NB = 4
S5_GPB = S5_GROUPS // S5_NB
SSD_HEADDIM = 64
SSD_HEADS = 16
SSD_GROUPS = 4
SSD_STATE = 128
SSD_CONV = 4
SSD_CHUNK = 128
SSD_BC = SSD_GROUPS * SSD_STATE
D_CONV_CH = 2048
D_MAIN = 4096
D_IN_PROJ = D_MAIN + SSD_HEADS
HP = 128
XA_HEADS = 4
XA_HEAD_DIM = 256
D_FF = 4096
EPS = 1e-5
N_DEV = 8
AXES = ("x", "y", "c")

ADAM_LR = 0.001
ADAM_B1 = 0.9
ADAM_B2 = 0.999
ADAM_EPS = 1e-08
ADAM_WD = 0.01
ADAM_STEP = 10

VMEM_LIMIT = 56 * 1024 * 1024


def _params(*sem):
    return pltpu.CompilerParams(dimension_semantics=sem, vmem_limit_bytes=VMEM_LIMIT)


def _tile(n, pref, quantum=128):
    t = (min(pref, n) // quantum) * quantum
    while t >= quantum:
        if n % t == 0:
            return t
        t -= quantum
    return n


def _sds(shape, dtype):
    return jax.ShapeDtypeStruct(tuple(shape), dtype)


def _recip(d):
    r = pl.reciprocal(d, approx=True)
    return r * (2.0 - d * r)


def _sigmoid(x):
    return _recip(1.0 + jnp.exp(-jnp.maximum(x, -80.0)))


def _silu(x):
    return x * _sigmoid(x)


def _silu_grad(x):
    s = _sigmoid(x)
    return s * (1.0 + x * (1.0 - s))


_GELU_C = math.sqrt(2.0 / math.pi)


def _gelu(x):
    return 0.5 * x * (1.0 + jnp.tanh(_GELU_C * (x + 0.044715 * x * x * x)))


def _gelu_grad(x):
    th = jnp.tanh(_GELU_C * (x + 0.044715 * x * x * x))
    return 0.5 * (1.0 + th) + 0.5 * x * (1.0 - th * th) * _GELU_C * (1.0 + 3.0 * 0.044715 * x * x)


def _softplus(x):
    return jnp.maximum(x, 0.0) + jnp.log(1.0 + jnp.exp(-jnp.abs(x)))


_NN = (((1,), (0,)), ((), ()))
_NT = (((1,), (1,)), ((), ()))
_TN = (((0,), (0,)), ((), ()))


def _dot(a, b, dims=_NN, precision=None):
    return lax.dot_general(a, b, dims, precision=precision, preferred_element_type=F32)


def _dot_split(x, w):
    hi = x.astype(MXU)
    lo = (x - hi.astype(F32)).astype(MXU)
    return _dot(hi, w) + _dot(lo, w)


def _rows8(v):
    return jnp.broadcast_to(v, (8, v.shape[1]))


def _mm(a, b, *, ta=False, tb=False, extras=(), vecs=(), epi=None, out_dtypes=(F32,), n_colsum=0, tm=1024, tn=1024,
        tk=1024, name):
    m, k = (a.shape[1], a.shape[0]) if ta else a.shape
    n = b.shape[0] if tb else b.shape[1]
    assert k == (b.shape[1] if tb else b.shape[0]), (a.shape, b.shape, ta, tb)
    tm, tn, tk = _tile(m, tm), _tile(n, tn), _tile(k, tk)
    nk = k // tk
    n_ex, n_out = len(extras) + len(vecs), len(out_dtypes)
    assert n_colsum == 0 or tn == n
    dims = (((0,) if ta else (1,), (1,) if tb else (0,)), ((), ()))

    def body(a_ref, b_ref, *rest):
        ex_refs, out_refs = rest[:n_ex], rest[n_ex:n_ex + n_out]
        sum_refs = rest[n_ex + n_out:n_ex + n_out + n_colsum]
        prod = _dot(a_ref[...].astype(MXU), b_ref[...].astype(MXU), dims)

        def finish(total):
            outs = epi(total, *[e[...] for e in ex_refs]) if epi is not None else (total,)
            for o, r in zip(outs[:n_out], out_refs, strict=True):
                r[...] = o.astype(r.dtype)
            for o, r in zip(outs[n_out:], sum_refs, strict=True):
                @pl.when(pl.program_id(0) == 0)
                def _(o=o, r=r):
                    r[...] = o

                @pl.when(pl.program_id(0) > 0)
                def _(o=o, r=r):
                    r[...] += o

        if nk == 1:
            finish(prod)
            return
        acc = rest[n_ex + n_out + n_colsum]
        kk = pl.program_id(2)

        @pl.when(kk == 0)
        def _():
            acc[...] = prod

        @pl.when(jnp.logical_and(kk > 0, kk < nk - 1))
        def _():
            acc[...] += prod

        @pl.when(kk == nk - 1)
        def _():
            finish(acc[...] + prod)

    a_spec = pl.BlockSpec((tk, tm), lambda i, j, kk: (kk, i)) if ta else pl.BlockSpec((tm, tk), lambda i, j, kk: (i, kk))
    b_spec = pl.BlockSpec((tn, tk), lambda i, j, kk: (j, kk)) if tb else pl.BlockSpec((tk, tn), lambda i, j, kk: (kk, j))
    mn_spec = pl.BlockSpec((tm, tn), lambda i, j, kk: (i, j))
    n_spec = pl.BlockSpec((1, tn), lambda i, j, kk: (0, j))
    outs = pl.pallas_call(
        body,
        name=name,
        grid=(m // tm, n // tn, nk),
        in_specs=[a_spec, b_spec] + [mn_spec] * len(extras) + [n_spec] * len(vecs),
        out_specs=[mn_spec] * n_out + [n_spec] * n_colsum,
        out_shape=[_sds((m, n), dt) for dt in out_dtypes] + [_sds((1, n), F32)] * n_colsum,
        scratch_shapes=[pltpu.VMEM((tm, tn), F32)] if nk > 1 else [],
        compiler_params=_params(*(("arbitrary",) * 3 if n_colsum else ("parallel", "parallel", "arbitrary"))),
    )(a, b, *extras, *vecs)
    return outs[0] if len(outs) == 1 else outs


def _row(tb, w, cb=0):
    return pl.BlockSpec((tb, w), lambda i: (i, cb))


def _const(shape):
    return pl.BlockSpec(shape, lambda i: (0,) * len(shape))


def _rmsnorm_fwd(x, g, *, name):
    t, d = x.shape
    tb = _tile(t, 512, 8)

    def body(x_ref, g_ref, h_ref):
        xv = x_ref[...]
        r = lax.rsqrt(jnp.mean(xv * xv, axis=-1, keepdims=True) + EPS)
        h_ref[...] = (xv * r * g_ref[...]).astype(h_ref.dtype)

    return pl.pallas_call(
        body, name=name, grid=(t // tb,), in_specs=[_row(tb, d), _const((1, d))], out_specs=_row(tb, d),
        out_shape=_sds((t, d), MXU), compiler_params=_params("parallel"),
    )(x, g.reshape(1, d))


def _rmsnorm_bwd(x, g, dh, dres, *, name):
    t, d = x.shape
    tb = _tile(t, 256, 8)
    has_res = dres is not None

    def body(x_ref, g_ref, dh_ref, *rest):
        dx_ref, dxm_ref, dg_ref = rest[-3:]

        @pl.when(pl.program_id(0) == 0)
        def _():
            dg_ref[...] = jnp.zeros_like(dg_ref)

        xv = x_ref[...]
        r = lax.rsqrt(jnp.mean(xv * xv, axis=-1, keepdims=True) + EPS)
        xh = xv * r
        dhv = dh_ref[...].astype(F32)
        dg_ref[...] += jnp.sum(dhv * xh, axis=0, keepdims=True)
        dxh = dhv * g_ref[...]
        dx = r * (dxh - xh * jnp.mean(dxh * xh, axis=-1, keepdims=True))
        if has_res:
            dx = dx + rest[0][...]
        dx_ref[...] = dx
        dxm_ref[...] = dx.astype(dxm_ref.dtype)

    ins = [x, g.reshape(1, d), dh] + ([dres] if has_res else [])
    return pl.pallas_call(
        body, name=name, grid=(t // tb,),
        in_specs=[_row(tb, d), _const((1, d)), _row(tb, d)] + ([_row(tb, d)] if has_res else []),
        out_specs=[_row(tb, d), _row(tb, d), _const((1, d))],
        out_shape=[_sds((t, d), F32), _sds((t, d), MXU), _sds((1, d), F32)],
        compiler_params=_params("arbitrary"),
    )(*ins)


def _norm_bwd_epilogue(dh, x, dres, g):
    r = lax.rsqrt(jnp.mean(x * x, axis=-1, keepdims=True) + EPS)
    xh = x * r
    dxh = dh * g
    dx = r * (dxh - xh * jnp.mean(dxh * xh, axis=-1, keepdims=True)) + dres
    return dx, dx, jnp.sum(dh * xh, axis=0, keepdims=True)


def _loss_head(x, g, target):
    t, d = x.shape
    tb = _tile(t, 256, 8)

    def body(x_ref, g_ref, tg_ref, loss_ref, dx_ref, dxm_ref, dg_ref):
        @pl.when(pl.program_id(0) == 0)
        def _():
            dg_ref[...] = jnp.zeros_like(dg_ref)
            loss_ref[...] = jnp.zeros_like(loss_ref)

        xv, gv = x_ref[...], g_ref[...]
        r = lax.rsqrt(jnp.mean(xv * xv, axis=-1, keepdims=True) + EPS)
        xh = xv * r
        err = xh * gv - tg_ref[...]
        loss_ref[...] += 0.5 * jnp.sum(jnp.mean(err * err, axis=-1, keepdims=True), axis=0, keepdims=True)
        dy = err * (1.0 / d)
        dg_ref[...] += jnp.sum(dy * xh, axis=0, keepdims=True)
        dxh = dy * gv
        dx = r * (dxh - xh * jnp.mean(dxh * xh, axis=-1, keepdims=True))
        dx_ref[...] = dx
        dxm_ref[...] = dx.astype(dxm_ref.dtype)

    return pl.pallas_call(
        body, name="loss_head", grid=(t // tb,),
        in_specs=[_row(tb, d), _const((1, d)), _row(tb, d)],
        out_specs=[_const((1, HP)), _row(tb, d), _row(tb, d), _const((1, d))],
        out_shape=[_sds((1, HP), F32), _sds((t, d), F32), _sds((t, d), MXU), _sds((1, d), F32)],
        compiler_params=_params("arbitrary"),
    )(x, g.reshape(1, d), target)


def _s5_discretise(ar, ai, ldt, br, bi, rep):
    dt = jnp.exp(ldt)
    mag = jnp.exp(dt * ar)
    abar_r, abar_i = mag * jnp.cos(dt * ai), mag * jnp.sin(dt * ai)
    den = ar * ar + ai * ai
    zr, zi = abar_r - 1.0, abar_i
    fr = (zr * ar + zi * ai) / den
    fi = (zi * ar - zr * ai) / den
    fr_e, fi_e = _dot(fr, rep, precision=HI), _dot(fi, rep, precision=HI)
    return abar_r, abar_i, fr_e * br - fi_e * bi, fr_e * bi + fi_e * br


def _s5_prep(ar, ai, ldt, br, bi, rep):
    g, p = ar.shape
    ph = br.shape[1]

    def body(ar_ref, ai_ref, ldt_ref, br_ref, bi_ref, rep_ref, o0, o1, o2, o3):
        outs = _s5_discretise(ar_ref[...], ai_ref[...], ldt_ref[...], br_ref[...], bi_ref[...], rep_ref[...])
        for o, v in zip((o0, o1, o2, o3), outs):
            o[...] = v

    return pl.pallas_call(
        body, name="s5_prep",
        out_shape=[_sds((g, p), F32), _sds((g, p), F32), _sds((g, ph), F32), _sds((g, ph), F32)],
        compiler_params=pltpu.CompilerParams(vmem_limit_bytes=VMEM_LIMIT),
    )(ar, ai, ldt, br, bi, rep)


def _s5_prep_bwd(ar, ai, ldt, br, bi, rep, d_abar_r, d_abar_i, d_bbar_r, d_bbar_i):
    g, p = ar.shape
    ph = br.shape[1]

    def body(ar_ref, ai_ref, ldt_ref, br_ref, bi_ref, rep_ref, c0, c1, c2, c3, o0, o1, o2, o3, o4):
        rep_v = rep_ref[...]
        _, vjp = jax.vjp(lambda a, b, c, d, e: _s5_discretise(a, b, c, d, e, rep_v),
                         ar_ref[...], ai_ref[...], ldt_ref[...], br_ref[...], bi_ref[...])
        grads = vjp((c0[...], c1[...], c2[...], c3[...]))
        for o, v in zip((o0, o1, o2, o3, o4), grads):
            o[...] = v

    return pl.pallas_call(
        body, name="s5_prep_bwd",
        out_shape=[_sds((g, p), F32), _sds((g, p), F32), _sds((g, 1), F32), _sds((g, ph), F32), _sds((g, ph), F32)],
        compiler_params=pltpu.CompilerParams(vmem_limit_bytes=VMEM_LIMIT),
    )(ar, ai, ldt, br, bi, rep, d_abar_r, d_abar_i, d_bbar_r, d_bbar_i)


SCAN_ROWS = 512


def _scan_rows(t):
    return _tile(t, SCAN_ROWS, 64)


def _interleave_rows(x):
    t, c = x.shape
    tb = _scan_rows(t)
    return x.reshape(t // tb, 8, tb // 8, c).swapaxes(1, 2).reshape(t, c)


def _deinterleave_rows(x):
    t, c = x.shape
    tb = _scan_rows(t)
    return x.reshape(t // tb, tb // 8, 8, c).swapaxes(1, 2).reshape(t, c)


def _cmul(ar, ai, br, bi):
    return ar * br - ai * bi, ar * bi + ai * br


def _segment_carries(fr, fi, ar8, ai8, c_r, c_i, seg, reverse):
    pr, pi = ar8, ai8
    for _ in range(int(math.log2(seg))):
        pr, pi = _cmul(pr, pi, pr, pi)
    row = lax.broadcasted_iota(jnp.int32, fr.shape, 0)
    edge = 7 if reverse else 0
    qr, qi = _cmul(pr, pi, c_r, c_i)
    xr, xi = jnp.where(row == edge, fr + qr, fr), jnp.where(row == edge, fi + qi, fi)
    for sh in (1, 2, 4):
        if reverse:
            keep, amount = row < 8 - sh, 8 - sh
        else:
            keep, amount = row >= sh, sh
        qr, qi = jnp.where(keep, pltpu.roll(xr, amount, 0), 0.0), jnp.where(keep, pltpu.roll(xi, amount, 0), 0.0)
        tr, ti = _cmul(pr, pi, qr, qi)
        xr, xi = xr + tr, xi + ti
        pr, pi = _cmul(pr, pi, pr, pi)
    if reverse:
        in_r, in_i = jnp.where(row == 7, c_r, pltpu.roll(xr, 7, 0)), jnp.where(row == 7, c_i, pltpu.roll(xi, 7, 0))
        return in_r, in_i, xr[0:1, :], xi[0:1, :]
    in_r, in_i = jnp.where(row == 0, c_r, pltpu.roll(xr, 1, 0)), jnp.where(row == 0, c_i, pltpu.roll(xi, 1, 0))
    return in_r, in_i, xr[7:8, :], xi[7:8, :]


def _sweeps(ar8, ai8, dr, di, cr, ci, seg, reverse, emit):
    order = range(seg - 1, -1, -1) if reverse else range(seg)
    rows = lambda j: slice(j * 8, (j + 1) * 8)
    fr, fi = jnp.zeros(ar8.shape, F32), jnp.zeros(ar8.shape, F32)
    for j in order:
        tr, ti = _cmul(ar8, ai8, fr, fi)
        fr, fi = tr + dr[rows(j), :], ti + di[rows(j), :]
    s_r, s_i, out_r, out_i = _segment_carries(fr, fi, ar8, ai8, cr[...], ci[...], seg, reverse)
    cr[...] = out_r
    ci[...] = out_i
    for j in order:
        tr, ti = _cmul(ar8, ai8, s_r, s_i)
        s_r, s_i = tr + dr[rows(j), :], ti + di[rows(j), :]
        emit(j, s_r, s_i)


S5_BK, S5_BN = 256, 1024


def _s5_specs(tb, nt, reverse):
    t_of = (lambda s: nt - 1 - s) if reverse else (lambda s: s)
    return dict(
        small=pl.BlockSpec((tb, S5_BK), lambda c, s: (t_of(s), c)),
        wide=pl.BlockSpec((tb, S5_BN), lambda c, s: (t_of(s), c)),
        halo=pl.BlockSpec((8, S5_BN), lambda c, s: (jnp.maximum(t_of(s) * (tb // 8) - 1, 0), c)),
        vec_w=pl.BlockSpec((1, S5_BN), lambda c, s: (0, c)),
        vec_s=pl.BlockSpec((1, S5_BK), lambda c, s: (0, c)),
        w_in=pl.BlockSpec((None, S5_BK, S5_BN), lambda c, s: (c, 0, 0)),
        w_out=pl.BlockSpec((None, S5_BN, S5_BK), lambda c, s: (c, 0, 0)),
    )


def _s5_fwd(u, ar, ai, bre, bim, cre, cim, d_vec):
    t = u.shape[0]
    tb = _scan_rows(t)
    nt, seg, nb = t // tb, tb // 8, bre.shape[0]
    assert 1 << int(math.log2(seg)) == seg

    def body(u_ref, ar_ref, ai_ref, bre_ref, bim_ref, cre_ref, cim_ref, d_ref, sr_ref, si_ref, yp_ref, yg_ref,
             cr, ci, dr_s, di_s):
        @pl.when(pl.program_id(1) == 0)
        def _():
            cr[...] = jnp.zeros_like(cr)
            ci[...] = jnp.zeros_like(ci)

        uv = u_ref[...]
        ub = uv.astype(MXU)
        dr_s[...] = _dot(ub, bre_ref[...])
        di_s[...] = _dot(ub, bim_ref[...])
        ar8 = jnp.broadcast_to(ar_ref[...], (8, S5_BN))
        ai8 = jnp.broadcast_to(ai_ref[...], (8, S5_BN))

        def emit(j, s_r, s_i):
            sr_ref[j * 8:(j + 1) * 8, :] = s_r
            si_ref[j * 8:(j + 1) * 8, :] = s_i

        _sweeps(ar8, ai8, dr_s, di_s, cr, ci, seg, False, emit)
        yp = _dot(sr_ref[...].astype(MXU), cre_ref[...]) - _dot(si_ref[...].astype(MXU), cim_ref[...]) + d_ref[...] * uv
        yp_ref[...] = yp
        yg_ref[...] = _gelu(yp).astype(yg_ref.dtype)

    sp = _s5_specs(tb, nt, False)
    return pl.pallas_call(
        body, name="s5_fwd", grid=(nb, nt),
        in_specs=[sp["small"], sp["vec_w"], sp["vec_w"], sp["w_in"], sp["w_in"], sp["w_out"], sp["w_out"], sp["vec_s"]],
        out_specs=[sp["wide"], sp["wide"], sp["small"], sp["small"]],
        out_shape=[_sds((t, nb * S5_BN), F32), _sds((t, nb * S5_BN), F32), _sds((t, nb * S5_BK), F32),
                   _sds((t, nb * S5_BK), MXU)],
        scratch_shapes=[pltpu.VMEM((1, S5_BN), F32), pltpu.VMEM((1, S5_BN), F32), pltpu.VMEM((tb, S5_BN), F32),
                        pltpu.VMEM((tb, S5_BN), F32)],
        compiler_params=_params("parallel", "arbitrary"),
    )(u, ar, ai, bre, bim, cre, cim, d_vec)


def _s5_bwd(dyp, u, sr, si, ar, ai, cre_t, cim_t, bre_t, bim_t, d_vec):
    t = u.shape[0]
    tb = _scan_rows(t)
    nt, seg, nb = t // tb, tb // 8, cre_t.shape[0]

    def body(dyp_ref, u_ref, pr_ref, pi_ref, hr_ref, hi_ref, ar_ref, ai_ref, cre_ref, cim_ref, bre_ref, bim_ref, d_ref,
             du_ref, gr_ref, gi_ref, dbr_ref, dbi_ref, dcr_ref, dci_ref, dd_ref, cr, ci, dr_s, di_s, lr_s, li_s):
        step = pl.program_id(1)

        @pl.when(step == 0)
        def _():
            for r in (cr, ci, gr_ref, gi_ref, dbr_ref, dbi_ref, dcr_ref, dci_ref, dd_ref):
                r[...] = jnp.zeros_like(r)

        dyv, uv = dyp_ref[...], u_ref[...]
        dyb, ub = dyv.astype(MXU), uv.astype(MXU)
        dr_s[...] = _dot(dyb, cre_ref[...])
        di_s[...] = -_dot(dyb, cim_ref[...])
        ar8 = jnp.broadcast_to(ar_ref[...], (8, S5_BN))
        ai8 = jnp.broadcast_to(-ai_ref[...], (8, S5_BN))
        first_block = step == nt - 1
        row = lax.broadcasted_iota(jnp.int32, (8, S5_BN), 0)
        acc = [jnp.zeros((8, S5_BN), F32), jnp.zeros((8, S5_BN), F32)]

        def emit(j, s_r, s_i):
            lr_s[j * 8:(j + 1) * 8, :] = s_r
            li_s[j * 8:(j + 1) * 8, :] = s_i
            if j > 0:
                p_r, p_i = pr_ref[(j - 1) * 8:j * 8, :], pi_ref[(j - 1) * 8:j * 8, :]
            else:
                halo_r = jnp.where(first_block, 0.0, hr_ref[7:8, :])
                halo_i = jnp.where(first_block, 0.0, hi_ref[7:8, :])
                p_r = jnp.where(row == 0, halo_r, pltpu.roll(pr_ref[(seg - 1) * 8:seg * 8, :], 1, 0))
                p_i = jnp.where(row == 0, halo_i, pltpu.roll(pi_ref[(seg - 1) * 8:seg * 8, :], 1, 0))
            acc[0] = acc[0] + (p_r * s_r + p_i * s_i)
            acc[1] = acc[1] + (p_r * s_i - p_i * s_r)

        _sweeps(ar8, ai8, dr_s, di_s, cr, ci, seg, True, emit)
        gr_ref[...] += jnp.sum(acc[0], axis=0, keepdims=True)
        gi_ref[...] += jnp.sum(acc[1], axis=0, keepdims=True)
        lrb, lib = lr_s[...].astype(MXU), li_s[...].astype(MXU)
        du_ref[...] = (_dot(lrb, bre_ref[...]) + _dot(lib, bim_ref[...]) + d_ref[...] * dyv).astype(du_ref.dtype)
        dbr_ref[...] += _dot(ub, lrb, _TN)
        dbi_ref[...] += _dot(ub, lib, _TN)
        dcr_ref[...] += _dot(pr_ref[...].astype(MXU), dyb, _TN)
        dci_ref[...] += _dot(pi_ref[...].astype(MXU), dyb, _TN)
        dd_ref[...] += jnp.sum(dyv * uv, axis=0, keepdims=True)

    sp = _s5_specs(tb, nt, True)
    return pl.pallas_call(
        body, name="s5_bwd", grid=(nb, nt),
        in_specs=[sp["small"], sp["small"], sp["wide"], sp["wide"], sp["halo"], sp["halo"], sp["vec_w"], sp["vec_w"],
                  sp["w_in"], sp["w_in"], sp["w_out"], sp["w_out"], sp["vec_s"]],
        out_specs=[sp["small"], sp["vec_w"], sp["vec_w"], sp["w_in"], sp["w_in"], sp["w_out"], sp["w_out"], sp["vec_s"]],
        out_shape=[_sds((t, nb * S5_BK), MXU), _sds((1, nb * S5_BN), F32), _sds((1, nb * S5_BN), F32),
                   _sds((nb, S5_BK, S5_BN), F32), _sds((nb, S5_BK, S5_BN), F32), _sds((nb, S5_BN, S5_BK), F32),
                   _sds((nb, S5_BN, S5_BK), F32), _sds((1, nb * S5_BK), F32)],
        scratch_shapes=[pltpu.VMEM((1, S5_BN), F32), pltpu.VMEM((1, S5_BN), F32)] + [pltpu.VMEM((tb, S5_BN), F32)] * 4,
        compiler_params=_params("parallel", "arbitrary"),
    )(dyp, u, sr, si, sr, si, ar, ai, cre_t, cim_t, bre_t, bim_t, d_vec)


def _s5_gate_bwd(d_ycat, gp, ypre):
    t, d = gp.shape
    tb = _tile(t, 256, 8)

    def body(do_ref, gp_ref, yp_ref, o_ref):
        sg = _sigmoid(gp_ref[...])
        o_ref[...] = (do_ref[...] * _gelu(yp_ref[...]) * sg * (1.0 - sg)).astype(o_ref.dtype)

    return pl.pallas_call(
        body, name="s5_gate_bwd", grid=(t // tb,), in_specs=[_row(tb, d), _row(tb, d), _row(tb, d)],
        out_specs=_row(tb, d), out_shape=_sds((t, d), MXU), compiler_params=_params("parallel"),
    )(d_ycat, gp, ypre)


def _ssd_consts():
    head = jnp.arange(HP)[:, None]
    lane = jnp.arange(D_SSD)[None, :]
    expand = ((lane // SSD_HEADDIM) == head).astype(MXU)
    ll = jnp.arange(SSD_CHUNK)
    tri = (ll[:, None] >= ll[None, :]).astype(F32)
    return expand, expand.T, tri, jnp.eye(HP, dtype=F32)


def _ssd_chunk_terms(cp, dtr, par, expand, tri):
    ln = SSD_CHUNK
    xbc = _silu(cp)
    xs, bm, cm = xbc[:, :D_SSD], xbc[:, D_SSD:D_SSD + SSD_BC], xbc[:, D_SSD + SSD_BC:]
    dt = _softplus(dtr + par[0:1, :])
    a = -jnp.exp(par[1:2, :])
    da = dt * a
    acum = _dot(tri, da, precision=HI)
    acum_t = _dot(da, tri, (((0,), (1,)), ((), ())), precision=HI)
    atot = acum[ln - 1:ln, :]
    dt_e = _dot_split(dt, expand)
    eac_e = _dot_split(jnp.exp(acum), expand)
    dec_e = _dot_split(jnp.exp(atot - acum), expand)
    eat_e = _dot_split(_rows8(jnp.exp(atot)), expand)[0:1, :]
    dsk_e = _dot_split(_rows8(par[2:3, :]), expand)[0:1, :]
    return dict(xs=xs, bm=bm, cm=cm, dt=dt, a=a, acum=acum, acum_t=acum_t, dt_e=dt_e, eac_e=eac_e,
                dec_e=dec_e, eat_e=eat_e, dsk_e=dsk_e)


def _decay_matrix(acum, acum_t, h, mask):
    diff = acum[:, h:h + 1] - acum_t[h:h + 1, :]
    return jnp.where(mask, jnp.exp(jnp.minimum(diff, 0.0)), 0.0)


def _ssd_fwd(proj, conv_w, conv_b, dtr, par, gnorm, consts):
    t = proj.shape[0]
    ln = SSD_CHUNK
    nc = t // ln
    expand, _, tri, _ = consts
    hd2 = 2 * SSD_HEADDIM

    def body(cur_ref, prev_ref, cw_ref, cb_ref, z_ref, dtr_ref, par_ref, g_ref, e_ref, tri_ref,
             out_ref, y_ref, st_ref, cp_ref, state, ext):
        first = pl.program_id(0) == 0

        @pl.when(first)
        def _():
            state[...] = jnp.zeros_like(state)

        st_ref[...] = state[...]
        ext[0:8, :] = jnp.where(first, 0.0, prev_ref[...])
        ext[8:ln + 8, :] = cur_ref[...]
        cpv = jnp.broadcast_to(cb_ref[...], (ln, D_CONV_CH))
        for j in range(SSD_CONV):
            cpv = cpv + cw_ref[SSD_CONV - 1 - j:SSD_CONV - j, :] * ext[8 - j:8 - j + ln, :]
        cp_ref[...] = cpv
        c = _ssd_chunk_terms(cpv, dtr_ref[...], par_ref[...], e_ref[...], tri_ref[...])
        xdt = c["xs"] * c["dt_e"]
        xb, xd = xdt.astype(MXU), (xdt * c["dec_e"]).astype(MXU)
        bb, cb = c["bm"].astype(MXU), c["cm"].astype(MXU)
        mask = lax.broadcasted_iota(jnp.int32, (ln, ln), 0) >= lax.broadcasted_iota(jnp.int32, (ln, ln), 1)
        left = lax.broadcasted_iota(jnp.int32, (ln, hd2), 1) < SSD_HEADDIM
        for g in range(SSD_GROUPS):
            nsl = slice(g * SSD_STATE, (g + 1) * SSD_STATE)
            gsl = slice(g * 256, (g + 1) * 256)
            bg, cg = bb[:, nsl], cb[:, nsl]
            cbm = _dot(cg, bg, _NT)
            st_g = state[:, gsl]
            for pair in range(2):
                h0 = g * 4 + pair * 2
                psl = slice(h0 * SSD_HEADDIM, (h0 + 2) * SSD_HEADDIM)
                m0 = (cbm * _decay_matrix(c["acum"], c["acum_t"], h0, mask)).astype(MXU)
                m1 = (cbm * _decay_matrix(c["acum"], c["acum_t"], h0 + 1, mask)).astype(MXU)
                y_ref[:, psl] = jnp.where(left, _dot(m0, xb[:, psl]), _dot(m1, xb[:, psl]))
            y_ref[:, gsl] += _dot(cg, st_g.astype(MXU)) * c["eac_e"][:, gsl]
            state[:, gsl] = st_g * c["eat_e"][:, gsl] + _dot(bg, xd[:, gsl], _TN)
        y = y_ref[...] + c["dsk_e"] * c["xs"]
        y_ref[...] = y
        y2 = y * _silu(z_ref[...])
        r = lax.rsqrt(jnp.mean(y2 * y2, axis=-1, keepdims=True) + EPS)
        out_ref[...] = (y2 * r * g_ref[...]).astype(out_ref.dtype)

    xbc_block = (D_MAIN - D_CONV_CH) // D_CONV_CH
    return pl.pallas_call(
        body, name="ssd_fwd", grid=(nc,),
        in_specs=[_row(ln, D_CONV_CH, xbc_block),
                  pl.BlockSpec((8, D_CONV_CH), lambda i: (jnp.maximum(i * (ln // 8) - 1, 0), xbc_block)),
                  _const((SSD_CONV, D_CONV_CH)), _const((1, D_CONV_CH)),
                  _row(ln, D_SSD, 1), _row(ln, HP), _const((8, HP)), _const((1, D_SSD)),
                  _const((HP, D_SSD)), _const((ln, ln))],
        out_specs=[_row(ln, D_SSD), _row(ln, D_SSD), pl.BlockSpec((None, SSD_STATE, D_SSD), lambda i: (i, 0, 0)),
                   _row(ln, D_CONV_CH)],
        out_shape=[_sds((t, D_SSD), MXU), _sds((t, D_SSD), F32), _sds((nc, SSD_STATE, D_SSD), F32),
                   _sds((t, D_CONV_CH), F32)],
        scratch_shapes=[pltpu.VMEM((SSD_STATE, D_SSD), F32), pltpu.VMEM((ln + 8, D_CONV_CH), F32)],
        compiler_params=_params("arbitrary"),
    )(proj, proj, conv_w, conv_b.reshape(1, D_CONV_CH), proj, dtr, par, gnorm.reshape(1, D_SSD), expand, tri)


def _ssd_bwd(proj, conv_pre, conv_w, dtr, par, gnorm, y, states, d_ycat, consts):
    t = proj.shape[0]
    ln = SSD_CHUNK
    nc = t // ln
    expand, expand_t, tri, eye = consts
    hd2 = 2 * SSD_HEADDIM

    def body(cp_ref, z_ref, dtr_ref, par_ref, g_ref, y_ref, st_ref, do_ref, e_ref, et_ref, tri_ref, eye_ref,
             cur_ref, prev_ref, cw_ref,
             dxbc_ref, dz_ref, ddt_ref, dg_ref, dpar_ref, dcw_ref, dcb_ref,
             dstate, dx_buf, lane_buf, tot_buf, colsum, dcp_ref, ext, dext):
        @pl.when(pl.program_id(0) == 0)
        def _():
            dstate[...] = jnp.zeros_like(dstate)
            dg_ref[...] = jnp.zeros_like(dg_ref)
            dpar_ref[...] = jnp.zeros_like(dpar_ref)
            dcw_ref[...] = jnp.zeros_like(dcw_ref)
            dcb_ref[...] = jnp.zeros_like(dcb_ref)
            dext[ln:ln + 8, :] = jnp.zeros((8, D_CONV_CH), F32)

        cpv, et_v, tri_v, par_v = cp_ref[...], et_ref[...], tri_ref[...], par_ref[...]
        c = _ssd_chunk_terms(cpv, dtr_ref[...], par_v, e_ref[...], tri_v)
        xs = c["xs"]
        zv, yv, dov = z_ref[...], y_ref[...], do_ref[...]
        sz = _silu(zv)
        y2 = yv * sz
        r = lax.rsqrt(jnp.mean(y2 * y2, axis=-1, keepdims=True) + EPS)
        yh = y2 * r
        dg_ref[...] += jnp.sum(dov * yh, axis=0, keepdims=True)
        dyh = dov * g_ref[...]
        dy2 = r * (dyh - yh * jnp.mean(dyh * yh, axis=-1, keepdims=True))
        dz_ref[...] = (dy2 * yv * _silu_grad(zv)).astype(dz_ref.dtype)
        dy = dy2 * sz

        xdt = xs * c["dt_e"]
        xdf = xdt * c["dec_e"]
        xb, xd = xdt.astype(MXU), xdf.astype(MXU)
        bb, cb = c["bm"].astype(MXU), c["cm"].astype(MXU)
        dyb, dye = dy.astype(MXU), (dy * c["eac_e"]).astype(MXU)
        mask = lax.broadcasted_iota(jnp.int32, (ln, ln), 0) >= lax.broadcasted_iota(jnp.int32, (ln, ln), 1)
        left = lax.broadcasted_iota(jnp.int32, (ln, hd2), 1) < SSD_HEADDIM
        lane_hp = lax.broadcasted_iota(jnp.int32, (ln, HP), 1)
        d_acum = jnp.zeros((ln, HP), F32)
        colsum[...] = jnp.zeros_like(colsum)
        tot_buf[...] = jnp.zeros_like(tot_buf)
        for g in range(SSD_GROUPS):
            nsl = slice(g * SSD_STATE, (g + 1) * SSD_STATE)
            gsl = slice(g * 256, (g + 1) * 256)
            bg, cg = bb[:, nsl], cb[:, nsl]
            cbm = _dot(cg, bg, _NT)
            st_g = st_ref[:, gsl]
            dst_g = dstate[:, gsl]
            stb, dstb = st_g.astype(MXU), dst_g.astype(MXU)
            y_off = _dot(cg, stb)
            bds = _dot(bg, dstb)
            dcb = jnp.zeros((ln, ln), F32)
            for pair in range(2):
                h0 = g * 4 + pair * 2
                psl = slice(h0 * SSD_HEADDIM, (h0 + 2) * SSD_HEADDIM)
                xp, dyp = xb[:, psl], dyb[:, psl]
                dxp = []
                for k in range(2):
                    h = h0 + k
                    lm = _decay_matrix(c["acum"], c["acum_t"], h, mask)
                    mm = cbm * lm
                    half = left if k == 0 else jnp.logical_not(left)
                    dm = _dot(jnp.where(half, dyp, jnp.zeros_like(dyp)), xp, _NT)
                    dcb = dcb + dm * lm
                    gm = dm * mm
                    d_acum = d_acum + jnp.where(lane_hp == h, jnp.sum(gm, axis=1, keepdims=True), 0.0)
                    colsum[h:h + 1, :] = jnp.sum(gm, axis=0, keepdims=True)
                    dxp.append(_dot(mm.astype(MXU), dyp, _TN))
                dx_buf[:, psl] = jnp.where(left, dxp[0], dxp[1])
            dx_buf[:, gsl] += bds * c["dec_e"][:, gsl]
            dcbb = dcb.astype(MXU)
            dc_g = _dot(dcbb, bg) + _dot(dye[:, gsl], stb, _NT)
            db_g = _dot(dcbb, cg, _TN) + _dot(xd[:, gsl], dstb, _NT)
            dcp_ref[:, D_SSD + g * SSD_STATE:D_SSD + (g + 1) * SSD_STATE] = db_g
            dcp_ref[:, D_SSD + SSD_BC + g * SSD_STATE:D_SSD + SSD_BC + (g + 1) * SSD_STATE] = dc_g
            dec_term = xdf[:, gsl] * bds
            lane_buf[:, gsl] = dy[:, gsl] * y_off * c["eac_e"][:, gsl] - dec_term
            tot_buf[0:1, gsl] = (jnp.sum(st_g * dst_g, axis=0, keepdims=True) * c["eat_e"][:, gsl]
                                 + jnp.sum(dec_term, axis=0, keepdims=True))
            dstate[:, gsl] = dst_g * c["eat_e"][:, gsl] + _dot(cg, dye[:, gsl], _TN)
        dx_tot = dx_buf[...]
        d_acum = d_acum + _dot_split(lane_buf[...], et_v) - _dot(colsum[...], eye_ref[...], _TN, precision=HI)
        d_atot = _dot_split(tot_buf[...], et_v)[0:1, :]
        row_hp = lax.broadcasted_iota(jnp.int32, (ln, HP), 0)
        d_acum = d_acum + jnp.where(row_hp == ln - 1, d_atot, 0.0)
        d_da = _dot(tri_v, d_acum, _TN, precision=HI)
        d_dt = d_da * c["a"] + _dot_split(dx_tot * xs, et_v)
        d_dtr = d_dt * _sigmoid(dtr_ref[...] + par_v[0:1, :])
        ddt_ref[...] = d_dtr
        dpar_ref[0:1, :] += jnp.sum(d_dtr, axis=0, keepdims=True)
        dpar_ref[1:2, :] += jnp.sum(d_da * c["dt"], axis=0, keepdims=True) * c["a"]
        dpar_ref[2:3, :] += _dot_split(_rows8(jnp.sum(dy * xs, axis=0, keepdims=True)), et_v)[0:1, :]
        dcp_ref[:, 0:D_SSD] = dx_tot * c["dt_e"] + dy * c["dsk_e"]
        dcv = dcp_ref[...] * _silu_grad(cpv)
        ext[0:8, :] = jnp.where(pl.program_id(0) == nc - 1, 0.0, prev_ref[...])
        ext[8:ln + 8, :] = cur_ref[...]
        dext[0:ln, :] = dcv
        dxbc = jnp.zeros((ln, D_CONV_CH), F32)
        for j in range(SSD_CONV):
            dxbc = dxbc + cw_ref[SSD_CONV - 1 - j:SSD_CONV - j, :] * dext[j:j + ln, :]
            dcw_ref[SSD_CONV - 1 - j:SSD_CONV - j, :] += jnp.sum(dcv * ext[8 - j:8 - j + ln, :], axis=0, keepdims=True)
        dxbc_ref[...] = dxbc.astype(dxbc_ref.dtype)
        dcb_ref[...] += jnp.sum(dcv, axis=0, keepdims=True)
        dext[ln:ln + 8, :] = dcv[0:8, :]

    xbc_block = (D_MAIN - D_CONV_CH) // D_CONV_CH
    rev = lambda i: (nc - 1 - i, 0)
    rev1 = lambda i: (nc - 1 - i, 1)
    return pl.pallas_call(
        body, name="ssd_bwd", grid=(nc,),
        in_specs=[pl.BlockSpec((ln, D_CONV_CH), rev), pl.BlockSpec((ln, D_SSD), rev1), pl.BlockSpec((ln, HP), rev),
                  _const((8, HP)), _const((1, D_SSD)), pl.BlockSpec((ln, D_SSD), rev),
                  pl.BlockSpec((None, SSD_STATE, D_SSD), lambda i: (nc - 1 - i, 0, 0)),
                  pl.BlockSpec((ln, D_SSD), rev1),
                  _const((HP, D_SSD)), _const((D_SSD, HP)), _const((ln, ln)), _const((HP, HP)),
                  pl.BlockSpec((ln, D_CONV_CH), lambda i: (nc - 1 - i, xbc_block)),
                  pl.BlockSpec((8, D_CONV_CH), lambda i: (jnp.maximum((nc - 1 - i) * (ln // 8) - 1, 0), xbc_block)),
                  _const((SSD_CONV, D_CONV_CH))],
        out_specs=[pl.BlockSpec((ln, D_CONV_CH), rev), pl.BlockSpec((ln, D_SSD), rev), pl.BlockSpec((ln, HP), rev),
                   _const((1, D_SSD)), _const((8, HP)), _const((SSD_CONV, D_CONV_CH)), _const((1, D_CONV_CH))],
        out_shape=[_sds((t, D_CONV_CH), MXU), _sds((t, D_SSD), MXU), _sds((t, HP), F32), _sds((1, D_SSD), F32),
                   _sds((8, HP), F32), _sds((SSD_CONV, D_CONV_CH), F32), _sds((1, D_CONV_CH), F32)],
        scratch_shapes=[pltpu.VMEM((SSD_STATE, D_SSD), F32), pltpu.VMEM((ln, D_SSD), F32), pltpu.VMEM((ln, D_SSD), F32),
                        pltpu.VMEM((8, D_SSD), F32), pltpu.VMEM((HP, ln), F32), pltpu.VMEM((ln, D_CONV_CH), F32),
                        pltpu.VMEM((ln + 8, D_CONV_CH), F32), pltpu.VMEM((ln + 8, D_CONV_CH), F32)],
        compiler_params=_params("arbitrary"),
    )(conv_pre, proj, dtr, par, gnorm.reshape(1, D_SSD), y, states, d_ycat, expand, expand_t, tri, eye, proj, proj, conv_w)


def _softmax_rows(s):
    e = jnp.exp(s - jnp.max(s, axis=-1, keepdims=True))
    return e * _recip(jnp.sum(e, axis=-1, keepdims=True))


def _attn_fwd(q, k, v):
    t, d = q.shape
    mlen = k.shape[0]
    tq = _tile(t, 512, 8)
    scale = XA_HEAD_DIM ** -0.5

    def body(q_ref, k_ref, v_ref, o_ref):
        for h in range(XA_HEADS):
            sl = slice(h * XA_HEAD_DIM, (h + 1) * XA_HEAD_DIM)
            p = _softmax_rows(_dot(q_ref[:, sl], k_ref[:, sl], _NT) * scale)
            o_ref[:, sl] = _dot(p.astype(MXU), v_ref[:, sl]).astype(o_ref.dtype)

    return pl.pallas_call(
        body, name="xattn_fwd", grid=(t // tq,),
        in_specs=[_row(tq, d), _const((mlen, d)), _const((mlen, d))], out_specs=_row(tq, d),
        out_shape=_sds((t, d), MXU), compiler_params=_params("parallel"),
    )(q, k, v)


def _attn_bwd(q, k, v, do):
    t, d = q.shape
    mlen = k.shape[0]
    tq = _tile(t, 512, 8)
    scale = XA_HEAD_DIM ** -0.5

    def body(q_ref, k_ref, v_ref, do_ref, dq_ref, dk_ref, dv_ref):
        @pl.when(pl.program_id(0) == 0)
        def _():
            dk_ref[...] = jnp.zeros_like(dk_ref)
            dv_ref[...] = jnp.zeros_like(dv_ref)

        for h in range(XA_HEADS):
            sl = slice(h * XA_HEAD_DIM, (h + 1) * XA_HEAD_DIM)
            qh, kh, vh, doh = q_ref[:, sl], k_ref[:, sl], v_ref[:, sl], do_ref[:, sl]
            p = _softmax_rows(_dot(qh, kh, _NT) * scale)
            dp = _dot(doh, vh, _NT)
            dv_ref[:, sl] += _dot(p.astype(MXU), doh, _TN)
            ds = (p * (dp - jnp.sum(p * dp, axis=-1, keepdims=True)) * scale).astype(MXU)
            dq_ref[:, sl] = _dot(ds, kh).astype(dq_ref.dtype)
            dk_ref[:, sl] += _dot(ds, qh, _TN)

    return pl.pallas_call(
        body, name="xattn_bwd", grid=(t // tq,),
        in_specs=[_row(tq, d), _const((mlen, d)), _const((mlen, d)), _row(tq, d)],
        out_specs=[_row(tq, d), _const((mlen, d)), _const((mlen, d))],
        out_shape=[_sds((t, d), MXU), _sds((mlen, d), F32), _sds((mlen, d), F32)],
        compiler_params=_params("arbitrary"),
    )(q, k, v, do)


def _lane_view(a):
    if a.ndim >= 2 and a.shape[-1] >= 128:
        return a.reshape(-1, a.shape[-1])
    if a.size % 128 == 0:
        return a.reshape(-1, 128)
    return a.reshape(1, -1)


def _adamw(w, g, m, v, *, name):
    shape = w.shape
    w2, g2, m2, v2 = (_lane_view(a) for a in (w, g.reshape(shape), m, v))
    r, c = w2.shape
    tr = _tile(r, max(8, (1 << 18) // max(c, 128)), 8)

    def body(w_ref, g_ref, m_ref, v_ref, d_ref, mo_ref, vo_ref):
        gv = g_ref[...]
        mn = ADAM_B1 * m_ref[...] + (1.0 - ADAM_B1) * gv
        vn = ADAM_B2 * v_ref[...] + (1.0 - ADAM_B2) * (gv * gv)
        m_hat = mn / (1.0 - ADAM_B1 ** ADAM_STEP)
        v_hat = vn / (1.0 - ADAM_B2 ** ADAM_STEP)
        d_ref[...] = -ADAM_LR * (m_hat / (jnp.sqrt(v_hat) + ADAM_EPS) + ADAM_WD * w_ref[...])
        mo_ref[...] = mn
        vo_ref[...] = vn

    outs = pl.pallas_call(
        body, name=name, grid=(r // tr,), in_specs=[_row(tr, c)] * 4, out_specs=[_row(tr, c)] * 3,
        out_shape=[_sds((r, c), F32)] * 3, compiler_params=_params("parallel"),
    )(w2, g2, m2, v2)
    return tuple(o.reshape(shape) for o in outs)


_HBM = pl.BlockSpec(memory_space=pltpu.HBM)


N_CHIPS = 4
CHIPS = ((0, 0), (0, 1), (1, 0), (1, 1))


def _pair_exchange(x, *, name):
    def body(x_ref, o_ref, send_sems, recv_sems):
        xx, yy, cc = (lax.axis_index(a) for a in AXES)
        copies = [
            pltpu.make_async_remote_copy(
                src_ref=x_ref.at[4 * px + 2 * py + (1 - cc)], dst_ref=o_ref.at[k], send_sem=send_sems.at[k],
                recv_sem=recv_sems.at[k], device_id=(xx, yy, 1 - cc), device_id_type=pl.DeviceIdType.MESH)
            for k, (px, py) in enumerate(CHIPS)]
        for cp in copies:
            cp.start()
        for cp in copies:
            cp.wait_recv()
        for cp in copies:
            cp.wait_send()

    return pl.pallas_call(
        body, name=name, in_specs=[_HBM], out_specs=_HBM, out_shape=_sds((N_CHIPS,) + tuple(x.shape[1:]), x.dtype),
        scratch_shapes=[pltpu.SemaphoreType.DMA((N_CHIPS,)), pltpu.SemaphoreType.DMA((N_CHIPS,))],
    )(x)


def _pair_add(x, got, *, name):
    _, r, c = x.shape
    tr = _tile(r, max(PAD_ROWS, (1 << 17) // c), PAD_ROWS)

    def body(x_ref, g_ref, o_ref):
        mine = jnp.where(lax.axis_index("c") == 0, x_ref[0].astype(F32), x_ref[1].astype(F32))
        o_ref[...] = (mine + g_ref[...].astype(F32)).astype(o_ref.dtype)

    return pl.pallas_call(
        body, name=name, grid=(N_CHIPS, r // tr),
        in_specs=[pl.BlockSpec((None, 2, tr, c), lambda k, i: (k, 0, i, 0)), pl.BlockSpec((None, tr, c), lambda k, i: (k, i, 0))],
        out_specs=pl.BlockSpec((None, tr, c), lambda k, i: (k, i, 0)), out_shape=_sds((N_CHIPS, r, c), x.dtype),
        compiler_params=_params("parallel", "parallel"),
    )(x.reshape(N_CHIPS, 2, r, c), got)


def _chip_exchange(x, *, name):
    def body(x_ref, o_ref, send_sems, recv_sems, local_sem):
        xx, yy, cc = (lax.axis_index(a) for a in AXES)
        mine = 2 * xx + yy
        local = pltpu.make_async_copy(x_ref.at[mine], o_ref.at[mine], local_sem)
        local.start()
        sends = []
        for j, (px, py) in enumerate([(1 - xx, yy), (xx, 1 - yy), (1 - xx, 1 - yy)]):
            cp = pltpu.make_async_remote_copy(
                src_ref=x_ref.at[2 * px + py], dst_ref=o_ref.at[mine], send_sem=send_sems.at[j], recv_sem=recv_sems.at[j],
                device_id=(px, py, cc), device_id_type=pl.DeviceIdType.MESH)
            cp.start()
            sends.append(cp)
        for j, (px, py) in enumerate([(1 - xx, yy), (xx, 1 - yy), (1 - xx, 1 - yy)]):
            pltpu.make_async_remote_copy(
                src_ref=x_ref.at[2 * px + py], dst_ref=o_ref.at[2 * px + py], send_sem=send_sems.at[j],
                recv_sem=recv_sems.at[j], device_id=(px, py, cc), device_id_type=pl.DeviceIdType.MESH).wait_recv()
        for cp in sends:
            cp.wait_send()
        local.wait()

    return pl.pallas_call(
        body, name=name, in_specs=[_HBM], out_specs=_HBM, out_shape=_sds(x.shape, x.dtype),
        scratch_shapes=[pltpu.SemaphoreType.DMA((N_CHIPS - 1,)), pltpu.SemaphoreType.DMA((N_CHIPS - 1,)),
                        pltpu.SemaphoreType.DMA(())],
    )(x)


def _reduce_scatter(x, *, name):
    chip_sums = _pair_add(x, _pair_exchange(x, name=name + "_pair"), name=name + "_pair_add")
    return _sum_slots(_chip_exchange(chip_sums, name=name + "_chips"), name=name + "_sum")


def _all_gather(x, *, name):
    def body(x_ref, o_ref, send_sems, recv_sems, local_sem):
        xx, yy, cc = (lax.axis_index(a) for a in AXES)
        me, sibling = (xx, yy, cc), (xx, yy, 1 - cc)
        chips = [(1 - xx, yy), (xx, 1 - yy), (1 - xx, 1 - yy)]

        def slot(px, py, pc):
            return o_ref.at[4 * px + 2 * py + pc]

        def copy(k, block, to, src=None):
            return pltpu.make_async_remote_copy(
                src_ref=slot(*block) if src is None else src, dst_ref=slot(*block), send_sem=send_sems.at[k],
                recv_sem=recv_sems.at[k], device_id=to, device_id_type=pl.DeviceIdType.MESH)

        local = pltpu.make_async_copy(x_ref, slot(*me), local_sem)
        local.start()
        first = [copy(0, me, sibling, src=x_ref)] + [copy(1 + j, me, (*chip, cc), src=x_ref) for j, chip in enumerate(chips)]
        for cp in first:
            cp.start()
        passed = [copy(4 + j, (*chip, cc), sibling) for j, chip in enumerate(chips)]
        for j, chip in enumerate(chips):
            copy(1 + j, (*chip, cc), me).wait_recv()
            passed[j].start()
        copy(0, sibling, me).wait_recv()
        for j, chip in enumerate(chips):
            copy(4 + j, (*chip, 1 - cc), me).wait_recv()
        for cp in first + passed:
            cp.wait_send()
        local.wait()

    return pl.pallas_call(
        body, name=name, in_specs=[_HBM], out_specs=_HBM, out_shape=_sds((N_DEV,) + tuple(x.shape), x.dtype),
        scratch_shapes=[pltpu.SemaphoreType.DMA((N_DEV - 1,)), pltpu.SemaphoreType.DMA((N_DEV - 1,)),
                        pltpu.SemaphoreType.DMA(())],
    )(x)


def _sum_slots(x, *, name):
    n, r, c = x.shape
    tr = _tile(r, max(PAD_ROWS, (1 << 17) // c), PAD_ROWS)

    def body(x_ref, o_ref):
        acc = x_ref[0].astype(F32)
        for d in range(1, n):
            acc = acc + x_ref[d].astype(F32)
        o_ref[...] = acc

    return pl.pallas_call(
        body, name=name, grid=(r // tr,), in_specs=[pl.BlockSpec((n, tr, c), lambda i: (0, i, 0))],
        out_specs=_row(tr, c), out_shape=_sds((r, c), F32), compiler_params=_params("parallel"),
    )(x)


def _s5_layouts(bbar_r, bbar_i, c_re, c_im):
    eye = jnp.eye(S5_GPB, dtype=F32)

    def b_blocks(bbar):
        bb = bbar.reshape(S5_NB, S5_GPB, S5_STATE, S5_GROUP)
        return jnp.einsum("jgph,gk->jghkp", bb, eye).reshape(S5_NB, S5_GPB * S5_GROUP, S5_GPB * S5_STATE)

    def c_blocks(cc):
        c4 = cc.reshape(S5_NB, S5_GPB, S5_GROUP, S5_STATE)
        return jnp.einsum("jghp,gk->jgpkh", c4, eye).reshape(S5_NB, S5_GPB * S5_STATE, S5_GPB * S5_GROUP)

    bre, bim, cre, cim = b_blocks(bbar_r), b_blocks(bbar_i), c_blocks(c_re), c_blocks(c_im)
    cast = lambda a: a.astype(MXU)
    sw = lambda a: jnp.swapaxes(a, 1, 2).astype(MXU)
    return dict(bre=cast(bre), bim=cast(bim), cre=cast(cre), cim=cast(cim), bre_t=sw(bre), bim_t=sw(bim), cre_t=sw(cre),
                cim_t=sw(cim))


def _b_diag(db):
    d5 = db.reshape(S5_NB, S5_GPB, S5_GROUP, S5_GPB, S5_STATE)
    diag = jnp.stack([d5[:, g, :, g, :] for g in range(S5_GPB)], axis=1)
    return jnp.swapaxes(diag, 2, 3).reshape(S5_GROUPS, S5_STATE * S5_GROUP)


def _c_diag(dc):
    d5 = dc.reshape(S5_NB, S5_GPB, S5_STATE, S5_GPB, S5_GROUP)
    diag = jnp.stack([d5[:, g, :, g, :] for g in range(S5_GPB)], axis=1)
    return jnp.swapaxes(diag, 2, 3).reshape(S5_GROUPS, S5_GROUP, S5_STATE)


def _head_rows(*vecs):
    par = jnp.zeros((8, HP), F32)
    for i, v in enumerate(vecs):
        par = par.at[i, :SSD_HEADS].set(v.astype(F32))
    return par


def _add(acc, r):
    return (acc + r,)


def _layer_fwd(x, mem, w, consts):
    s = {"x": x}
    rep = consts["rep"]
    s["h1"] = h1 = _rmsnorm_fwd(x, w["norm_mix"], name="norm_mix_fwd")
    s["proj"] = proj = _mm(h1, w["w_main"], name="in_proj")
    s["dtr"] = dtr = _mm(h1, w["w_dt"], name="dt_proj")
    ar, ai, ldt = w["s5_a_re"], w["s5_a_im"], w["s5_log_dt"].reshape(S5_GROUPS, 1)
    br, bi = w["s5_b_re"].reshape(S5_GROUPS, -1), w["s5_b_im"].reshape(S5_GROUPS, -1)
    abar_r, abar_i, bbar_r, bbar_i = _s5_prep(ar, ai, ldt, br, bi, rep)
    s["abar"] = abar = (abar_r.reshape(1, S5_CH), abar_i.reshape(1, S5_CH))
    s["lay"] = lay = _s5_layouts(bbar_r, bbar_i, w["s5_c_re"], w["s5_c_im"])
    s["u"] = u = _interleave_rows(proj[:, :D_S5])
    s["sr"], s["si"], s["ypre"], s["yg"] = _s5_fwd(u, *abar, lay["bre"], lay["bim"], lay["cre"], lay["cim"],
                                                   w["s5_d"].reshape(1, D_S5))
    ypre, yg = s["ypre"], s["yg"]
    s["gp"], out_s5 = _mm(yg, w["s5_w_glu"], extras=[ypre], epi=lambda acc, yp: (acc, _gelu(yp) * _sigmoid(acc)),
                          out_dtypes=(F32, MXU), name="s5_glu")
    out_s5 = _deinterleave_rows(out_s5)
    s["par"] = par = _head_rows(w["ssd_dt_bias"], w["ssd_a_log"], w["ssd_d"])
    out_ssd, s["y_ssd"], s["states"], s["conv_pre"] = _ssd_fwd(proj, w["ssd_conv_w"], w["ssd_conv_b"], dtr, par,
                                                               w["ssd_norm"], consts["ssd"])
    s["ycat"] = ycat = jnp.concatenate([out_s5, out_ssd], axis=1)
    s["x1"] = x1 = _mm(ycat, w["w_out"], extras=[x], epi=_add, name="out_proj")
    s["hq"] = hq = _rmsnorm_fwd(x1, w["norm_xattn"], name="norm_xattn_fwd")
    s["mn"] = mn = _rmsnorm_fwd(mem, w["norm_mem"], name="norm_mem_fwd")
    s["q"] = q = _mm(hq, w["xa_wq"], out_dtypes=(MXU,), name="xa_q")
    s["k"] = k = _mm(mn, w["xa_wk"], out_dtypes=(MXU,), name="xa_k")
    s["v"] = v = _mm(mn, w["xa_wv"], out_dtypes=(MXU,), name="xa_v")
    s["o"] = o = _attn_fwd(q, k, v)
    s["x2"] = x2 = _mm(o, w["xa_wo"], extras=[x1], epi=_add, name="xa_o")
    s["hm"] = hm = _rmsnorm_fwd(x2, w["norm_mlp"], name="norm_mlp_fwd")
    s["act"] = _mm(hm, w["mlp_w1"], epi=lambda acc: (jnp.square(jnp.maximum(acc, 0.0)),), out_dtypes=(MXU,),
                   name="mlp_up")
    x3 = _mm(s["act"], w["mlp_w2"], extras=[x2], epi=_add, name="mlp_down")
    return x3, s


def _layer_bwd(dx3, dx3m, mem, w, s, consts):
    g = {}
    rep = consts["rep"]
    wire = (MXU,)
    d_a = _mm(dx3m, w["mlp_w2"], tb=True, extras=[s["act"]],
              epi=lambda acc, act: (acc * (2.0 * jnp.sqrt(act.astype(F32))),), out_dtypes=(MXU,), name="mlp_down_dx")
    g["mlp_w2"] = _mm(s["act"], dx3m, ta=True, out_dtypes=wire, name="mlp_down_dw")
    g["mlp_w1"] = _mm(s["hm"], d_a, ta=True, out_dtypes=wire, name="mlp_up_dw")
    norm_bwd = dict(epi=_norm_bwd_epilogue, out_dtypes=(F32, MXU), n_colsum=1, tm=512)
    dx2, dx2m, g["norm_mlp"] = _mm(d_a, w["mlp_w1"], tb=True, extras=[s["x2"], dx3], vecs=[w["norm_mlp"].reshape(1, -1)],
                                   name="mlp_up_dx", **norm_bwd)
    d_o = _mm(dx2m, w["xa_wo"], tb=True, out_dtypes=(MXU,), name="xa_o_dx")
    g["xa_wo"] = _mm(s["o"], dx2m, ta=True, out_dtypes=wire, name="xa_o_dw")
    dq, dk, dv = _attn_bwd(s["q"], s["k"], s["v"], d_o)
    g["xa_wq"] = _mm(s["hq"], dq, ta=True, out_dtypes=wire, name="xa_q_dw")
    dx1, dx1m, g["norm_xattn"] = _mm(dq, w["xa_wq"], tb=True, extras=[s["x1"], dx2],
                                     vecs=[w["norm_xattn"].reshape(1, -1)], name="xa_q_dx", **norm_bwd)
    g["xa_wk"] = _mm(s["mn"], dk, ta=True, out_dtypes=wire, name="xa_k_dw")
    g["xa_wv"] = _mm(s["mn"], dv, ta=True, out_dtypes=wire, name="xa_v_dw")
    d_mn_v = _mm(dv, w["xa_wv"], tb=True, name="xa_v_dx")
    d_mn = _mm(dk, w["xa_wk"], tb=True, extras=[d_mn_v], epi=_add, name="xa_k_dx")
    _, _, g["norm_mem"] = _rmsnorm_bwd(mem, w["norm_mem"], d_mn, None, name="norm_mem_bwd")
    d_ycat = _mm(dx1m, w["w_out"], tb=True, name="out_proj_dx")
    g["w_out"] = _mm(s["ycat"], dx1m, ta=True, out_dtypes=wire, name="out_proj_dw")
    lay, proj, ypre, u = s["lay"], s["proj"], s["ypre"], s["u"]
    d_os5 = _interleave_rows(d_ycat[:, :D_S5])
    d_gp = _s5_gate_bwd(d_os5, s["gp"], ypre)
    g["s5_w_glu"] = _mm(s["yg"], d_gp, ta=True, out_dtypes=wire, name="s5_glu_dw")
    d_ypre = _mm(d_gp, w["s5_w_glu"], tb=True, extras=[d_os5, s["gp"], ypre],
                 epi=lambda acc, do, gp, yp: ((acc + do * _sigmoid(gp)) * _gelu_grad(yp),), name="s5_glu_dx")
    du, d_abar_r, d_abar_i, db_re, db_im, dc_re, dc_im, d_d = _s5_bwd(
        d_ypre, u, s["sr"], s["si"], *s["abar"], lay["cre_t"], lay["cim_t"], lay["bre_t"], lay["bim_t"],
        w["s5_d"].reshape(1, D_S5))
    d_bbar_r, d_bbar_i = _b_diag(db_re), _b_diag(db_im)
    g["s5_c_re"], g["s5_c_im"] = _c_diag(dc_re), -_c_diag(dc_im)
    g["s5_d"] = d_d.reshape(S5_GROUPS, S5_GROUP)
    du = _deinterleave_rows(du)
    ar, ai, ldt = w["s5_a_re"], w["s5_a_im"], w["s5_log_dt"].reshape(S5_GROUPS, 1)
    br, bi = w["s5_b_re"].reshape(S5_GROUPS, -1), w["s5_b_im"].reshape(S5_GROUPS, -1)
    d_ar, d_ai, d_ldt, d_br, d_bi = _s5_prep_bwd(
        ar, ai, ldt, br, bi, rep, d_abar_r.reshape(S5_GROUPS, S5_STATE), d_abar_i.reshape(S5_GROUPS, S5_STATE),
        d_bbar_r, d_bbar_i)
    g["s5_a_re"], g["s5_a_im"], g["s5_log_dt"] = d_ar, d_ai, d_ldt.reshape(S5_GROUPS)
    g["s5_b_re"] = d_br.reshape(S5_GROUPS, S5_STATE, S5_GROUP)
    g["s5_b_im"] = d_bi.reshape(S5_GROUPS, S5_STATE, S5_GROUP)
    d_xbc, dz, d_dtr, g["ssd_norm"], d_par, g["ssd_conv_w"], g["ssd_conv_b"] = _ssd_bwd(
        proj, s["conv_pre"], w["ssd_conv_w"], s["dtr"], s["par"], w["ssd_norm"], s["y_ssd"], s["states"], d_ycat,
        consts["ssd"])
    g["ssd_dt_bias"], g["ssd_a_log"], g["ssd_d"] = (d_par[i, :SSD_HEADS] for i in range(3))
    d_proj = jnp.concatenate([du, dz, d_xbc], axis=1)
    g_main = _mm(s["h1"], d_proj, ta=True, out_dtypes=wire, name="in_proj_dw")
    g_dt = _mm(s["h1"], d_dtr, ta=True, out_dtypes=wire, name="dt_proj_dw")
    g["w_in"] = jnp.concatenate([g_main, g_dt[:, :SSD_HEADS]], axis=1)
    d_h1_dt = _mm(d_dtr, w["w_dt"], tb=True, name="dt_proj_dx")
    dx, dxm, g["norm_mix"] = _mm(
        d_proj, w["w_main"], tb=True, extras=[d_h1_dt, s["x"], dx1], vecs=[w["norm_mix"].reshape(1, -1)],
        name="in_proj_dx", **{**norm_bwd, "epi": lambda acc, dt_part, *rest: _norm_bwd_epilogue(acc + dt_part, *rest)})
    return dx, dxm, g


LAYER_WEIGHTS = ("norm_mix", "w_in", "s5_a_re", "s5_a_im", "s5_log_dt", "s5_b_re", "s5_b_im", "s5_c_re", "s5_c_im", "s5_d",
                 "s5_w_glu", "ssd_conv_w", "ssd_conv_b", "ssd_dt_bias", "ssd_a_log", "ssd_d", "ssd_norm", "w_out",
                 "norm_xattn", "norm_mem", "xa_wq", "xa_wk", "xa_wv", "xa_wo", "norm_mlp", "mlp_w1", "mlp_w2")
WEIGHTS = LAYER_WEIGHTS + ("norm_final",)


def _local_step(x, mem, target, weights):
    consts = {
        "ssd": _ssd_consts(),
        "rep": (jnp.arange(S5_STATE)[:, None] == jnp.arange(S5_STATE * S5_GROUP)[None, :] // S5_GROUP).astype(F32),
    }
    layers = []
    for l in range(DEPTH):
        w = {n: weights[n][l] for n in LAYER_WEIGHTS}
        w_in = w["w_in"]
        w["w_main"] = w_in[:, :D_MAIN]
        w["w_dt"] = jnp.pad(w_in[:, D_MAIN:], ((0, 0), (0, HP - SSD_HEADS)))
        layers.append(w)
    saved = []
    for l in range(DEPTH):
        x, s = _layer_fwd(x, mem, layers[l], consts)
        saved.append(s)
    loss, dx, dxm, g_final = _loss_head(x, weights["norm_final"], target)
    grads = [None] * DEPTH
    for l in reversed(range(DEPTH)):
        dx, dxm, grads[l] = _layer_bwd(dx, dxm, mem, layers[l], saved[l], consts)
    out = {n: jnp.stack([grads[l][n].reshape(weights[n].shape[1:]) for l in range(DEPTH)]) for n in LAYER_WEIGHTS}
    out["norm_final"] = g_final.reshape(weights["norm_final"].shape)
    return loss, dx, out


SHARDED = {"w_in": 2, "s5_w_glu": 1, "ssd_conv_w": 2, "w_out": 1, "xa_wq": 1, "xa_wk": 1, "xa_wv": 1, "xa_wo": 1,
           "mlp_w1": 2, "mlp_w2": 1}
EXACT = ("ssd_conv_w",)
ROW_EXCHANGE = tuple(n for n, ax in SHARDED.items() if ax == 1)
OWN_EXCHANGE = tuple(n for n in SHARDED if n not in ROW_EXCHANGE)
REPLICATED = tuple(n for n in WEIGHTS if n not in SHARDED)
LANES = 128
PAD_ROWS = 16


def _gather_weights(local):
    def assemble(n, seg):
        shp, ax = local[n].shape, SHARDED[n]
        return jnp.moveaxis(seg, 0, ax).reshape(*shp[:ax], N_DEV * shp[ax], *shp[ax + 1:])

    full = {}
    for n in OWN_EXCHANGE:
        payload = local[n] if n in EXACT else local[n].astype(MXU)
        full[n] = assemble(n, _all_gather(payload, name="gather_" + n))
    got = _all_gather(jnp.concatenate([local[n].astype(MXU) for n in ROW_EXCHANGE], axis=1), name="gather_row_sharded")
    off = 0
    for n in ROW_EXCHANGE:
        rows = local[n].shape[1]
        full[n] = assemble(n, got[:, :, off:off + rows])
        off += rows
    return full


def _scatter_grads(grads, local_shapes):
    def shards(n):
        shp, ax = local_shapes[n], SHARDED[n]
        gfull = grads[n].reshape(*shp[:ax], N_DEV, shp[ax], *shp[ax + 1:])
        return jnp.moveaxis(gfull, ax, 0).astype(MXU)

    out = {}
    for n in OWN_EXCHANGE:
        shp = local_shapes[n]
        out[n] = _reduce_scatter(shards(n).reshape(N_DEV, -1, shp[-1]), name="scatter_" + n).reshape(shp)
    payload = jnp.concatenate([shards(n) for n in ROW_EXCHANGE], axis=2)
    summed = _reduce_scatter(payload.reshape(N_DEV, -1, payload.shape[-1]), name="scatter_row_sharded")
    summed = summed.reshape(payload.shape[1:])
    off = 0
    for n in ROW_EXCHANGE:
        rows = local_shapes[n][1]
        out[n] = summed[:, off:off + rows]
        off += rows
    return out


def _allreduce_small(loss, grads):
    parts = [loss.reshape(-1)[:1]] + [grads[n].reshape(-1) for n in REPLICATED]
    flat = jnp.concatenate(parts)
    quantum = N_DEV * PAD_ROWS * LANES
    flat = jnp.pad(flat, (0, -flat.shape[0] % quantum))
    mine = _reduce_scatter(flat.reshape(N_DEV, -1, LANES), name="reduce_small_grads")
    summed = _all_gather(mine, name="gather_small_grads").reshape(-1)
    out, off = {}, 1
    for n in REPLICATED:
        size = grads[n].size
        out[n] = summed[off:off + size].reshape(grads[n].shape)
        off += size
    return summed[0], out


def kernel(x, mem, norm_mix, w_in, s5_a_re, s5_a_im, s5_log_dt, s5_b_re, s5_b_im, s5_c_re, s5_c_im, s5_d, s5_w_glu, ssd_conv_w, ssd_conv_b, ssd_dt_bias, ssd_a_log, ssd_d, ssd_norm, w_out, norm_xattn, norm_mem, xa_wq, xa_wk, xa_wv, xa_wo, norm_mlp, mlp_w1, mlp_w2, norm_final, loss_target, m_norm_mix, m_w_in, m_s5_a_re, m_s5_a_im, m_s5_log_dt, m_s5_b_re, m_s5_b_im, m_s5_c_re, m_s5_c_im, m_s5_d, m_s5_w_glu, m_ssd_conv_w, m_ssd_conv_b, m_ssd_dt_bias, m_ssd_a_log, m_ssd_d, m_ssd_norm, m_w_out, m_norm_xattn, m_norm_mem, m_xa_wq, m_xa_wk, m_xa_wv, m_xa_wo, m_norm_mlp, m_mlp_w1, m_mlp_w2, m_norm_final, v_norm_mix, v_w_in, v_s5_a_re, v_s5_a_im, v_s5_log_dt, v_s5_b_re, v_s5_b_im, v_s5_c_re, v_s5_c_im, v_s5_d, v_s5_w_glu, v_ssd_conv_w, v_ssd_conv_b, v_ssd_dt_bias, v_ssd_a_log, v_ssd_d, v_ssd_norm, v_w_out, v_norm_xattn, v_norm_mem, v_xa_wq, v_xa_wk, v_xa_wv, v_xa_wo, v_norm_mlp, v_mlp_w1, v_mlp_w2, v_norm_final):
    args = locals()
    local = {n: args[n] for n in WEIGHTS}
    full = dict(local)
    full.update(_gather_weights(local))
    loss, grad_x, grads = _local_step(x[0], mem[0], loss_target[0], full)
    loss, g_small = _allreduce_small(loss, grads)
    g_all = _scatter_grads(grads, {n: local[n].shape for n in SHARDED})
    g_all.update(g_small)
    delta, new_m, new_v = {}, {}, {}
    for n in WEIGHTS:
        delta[n], new_m[n], new_v[n] = _adamw(local[n], g_all[n], args["m_" + n], args["v_" + n], name="adamw_" + n)
    return (loss, grad_x[None], *[g_all[n] for n in WEIGHTS], *[delta[n] for n in WEIGHTS],
            *[new_m[n] for n in WEIGHTS], *[new_v[n] for n in WEIGHTS])
```

```python
import functools
import math

import jax
import jax.numpy as jnp
from jax import lax
from jax.experimental import pallas as pl
from jax.experimental.pallas import tpu as pltpu

F32 = jnp.float32
MXU = jnp.bfloat16
HI = lax.Precision.HIGHEST

D_MODEL = 1024
DEPTH = 4
MEM_LEN = 256
D_S5 = 1024
D_SSD = 1024
S5_GROUP = 16
S5_GROUPS = 64
S5_STATE = 64
S5_CH = S5_GROUPS * S5_STATE
S5_NB = 4
S5_GPB = S5_GROUPS // S5_NB
SSD_HEADDIM = 64
SSD_HEADS = 16
SSD_GROUPS = 4
SSD_STATE = 128
SSD_CONV = 4
SSD_CHUNK = 128
SSD_BC = SSD_GROUPS * SSD_STATE
D_CONV_CH = 2048
D_MAIN = 4096
D_IN_PROJ = D_MAIN + SSD_HEADS
HP = 128
XA_HEADS = 4
XA_HEAD_DIM = 256
D_FF = 4096
EPS = 1e-5
N_DEV = 8
AXES = ("x", "y", "c")

ADAM_LR = 0.001
ADAM_B1 = 0.9
ADAM_B2 = 0.999
ADAM_EPS = 1e-08
ADAM_WD = 0.01
ADAM_STEP = 10

VMEM_LIMIT = 56 * 1024 * 1024


def _params(*sem):
    return pltpu.CompilerParams(dimension_semantics=sem, vmem_limit_bytes=VMEM_LIMIT)


def _tile(n, pref, quantum=128):
    t = (min(pref, n) // quantum) * quantum
    while t >= quantum:
        if n % t == 0:
            return t
        t -= quantum
    return n


def _sds(shape, dtype):
    return jax.ShapeDtypeStruct(tuple(shape), dtype)


def _recip(d):
    r = pl.reciprocal(d, approx=True)
    return r * (2.0 - d * r)


def _sigmoid(x):
    return _recip(1.0 + jnp.exp(-jnp.maximum(x, -80.0)))


def _silu(x):
    return x * _sigmoid(x)


def _silu_grad(x):
    s = _sigmoid(x)
    return s * (1.0 + x * (1.0 - s))


_GELU_C = math.sqrt(2.0 / math.pi)


def _gelu(x):
    return 0.5 * x * (1.0 + jnp.tanh(_GELU_C * (x + 0.044715 * x * x * x)))


def _gelu_grad(x):
    th = jnp.tanh(_GELU_C * (x + 0.044715 * x * x * x))
    return 0.5 * (1.0 + th) + 0.5 * x * (1.0 - th * th) * _GELU_C * (1.0 + 3.0 * 0.044715 * x * x)


def _softplus(x):
    return jnp.maximum(x, 0.0) + jnp.log(1.0 + jnp.exp(-jnp.abs(x)))


_NN = (((1,), (0,)), ((), ()))
_NT = (((1,), (1,)), ((), ()))
_TN = (((0,), (0,)), ((), ()))


def _dot(a, b, dims=_NN, precision=None):
    return lax.dot_general(a, b, dims, precision=precision, preferred_element_type=F32)


def _dot_split(x, w):
    hi = x.astype(MXU)
    lo = (x - hi.astype(F32)).astype(MXU)
    return _dot(hi, w) + _dot(lo, w)


def _rows8(v):
    return jnp.broadcast_to(v, (8, v.shape[1]))


def _mm(a, b, *, ta=False, tb=False, extras=(), vecs=(), epi=None, out_dtypes=(F32,), n_colsum=0, tm=1024, tn=1024,
        tk=1024, name):
    m, k = (a.shape[1], a.shape[0]) if ta else a.shape
    n = b.shape[0] if tb else b.shape[1]
    assert k == (b.shape[1] if tb else b.shape[0]), (a.shape, b.shape, ta, tb)
    if ta:
        tk = 2 * tk
    tm, tn, tk = _tile(m, tm), _tile(n, tn), _tile(k, tk)
    nk = k // tk
    n_ex, n_out = len(extras) + len(vecs), len(out_dtypes)
    assert n_colsum == 0 or tn == n
    dims = (((0,) if ta else (1,), (1,) if tb else (0,)), ((), ()))

    def body(a_ref, b_ref, *rest):
        ex_refs, out_refs = rest[:n_ex], rest[n_ex:n_ex + n_out]
        sum_refs = rest[n_ex + n_out:n_ex + n_out + n_colsum]
        prod = _dot(a_ref[...].astype(MXU), b_ref[...].astype(MXU), dims)

        def finish(total):
            outs = epi(total, *[e[...] for e in ex_refs]) if epi is not None else (total,)
            for o, r in zip(outs[:n_out], out_refs, strict=True):
                r[...] = o.astype(r.dtype)
            for o, r in zip(outs[n_out:], sum_refs, strict=True):
                @pl.when(pl.program_id(0) == 0)
                def _(o=o, r=r):
                    r[...] = o

                @pl.when(pl.program_id(0) > 0)
                def _(o=o, r=r):
                    r[...] += o

        if nk == 1:
            finish(prod)
            return
        acc = rest[n_ex + n_out + n_colsum]
        kk = pl.program_id(2)

        @pl.when(kk == 0)
        def _():
            acc[...] = prod

        @pl.when(jnp.logical_and(kk > 0, kk < nk - 1))
        def _():
            acc[...] += prod

        @pl.when(kk == nk - 1)
        def _():
            finish(acc[...] + prod)

    a_spec = pl.BlockSpec((tk, tm), lambda i, j, kk: (kk, i)) if ta else pl.BlockSpec((tm, tk), lambda i, j, kk: (i, kk))
    b_spec = pl.BlockSpec((tn, tk), lambda i, j, kk: (j, kk)) if tb else pl.BlockSpec((tk, tn), lambda i, j, kk: (kk, j))
    mn_spec = pl.BlockSpec((tm, tn), lambda i, j, kk: (i, j))
    n_spec = pl.BlockSpec((1, tn), lambda i, j, kk: (0, j))
    outs = pl.pallas_call(
        body,
        name=name,
        grid=(m // tm, n // tn, nk),
        in_specs=[a_spec, b_spec] + [mn_spec] * len(extras) + [n_spec] * len(vecs),
        out_specs=[mn_spec] * n_out + [n_spec] * n_colsum,
        out_shape=[_sds((m, n), dt) for dt in out_dtypes] + [_sds((1, n), F32)] * n_colsum,
        scratch_shapes=[pltpu.VMEM((tm, tn), F32)] if nk > 1 else [],
        compiler_params=_params(*(("arbitrary",) * 3 if n_colsum else ("parallel", "parallel", "arbitrary"))),
    )(a, b, *extras, *vecs)
    return outs[0] if len(outs) == 1 else outs


def _row(tb, w, cb=0):
    return pl.BlockSpec((tb, w), lambda i: (i, cb))


def _const(shape):
    return pl.BlockSpec(shape, lambda i: (0,) * len(shape))


def _rmsnorm_fwd(x, g, *, name):
    t, d = x.shape
    tb = _tile(t, 512, 8)

    def body(x_ref, g_ref, h_ref):
        xv = x_ref[...]
        r = lax.rsqrt(jnp.mean(xv * xv, axis=-1, keepdims=True) + EPS)
        h_ref[...] = (xv * r * g_ref[...]).astype(h_ref.dtype)

    return pl.pallas_call(
        body, name=name, grid=(t // tb,), in_specs=[_row(tb, d), _const((1, d))], out_specs=_row(tb, d),
        out_shape=_sds((t, d), MXU), compiler_params=_params("parallel"),
    )(x, g.reshape(1, d))


def _rmsnorm_bwd(x, g, dh, dres, *, name):
    t, d = x.shape
    tb = _tile(t, 256, 8)
    has_res = dres is not None

    def body(x_ref, g_ref, dh_ref, *rest):
        dx_ref, dxm_ref, dg_ref = rest[-3:]

        @pl.when(pl.program_id(0) == 0)
        def _():
            dg_ref[...] = jnp.zeros_like(dg_ref)

        xv = x_ref[...]
        r = lax.rsqrt(jnp.mean(xv * xv, axis=-1, keepdims=True) + EPS)
        xh = xv * r
        dhv = dh_ref[...].astype(F32)
        dg_ref[...] += jnp.sum(dhv * xh, axis=0, keepdims=True)
        dxh = dhv * g_ref[...]
        dx = r * (dxh - xh * jnp.mean(dxh * xh, axis=-1, keepdims=True))
        if has_res:
            dx = dx + rest[0][...]
        dx_ref[...] = dx
        dxm_ref[...] = dx.astype(dxm_ref.dtype)

    ins = [x, g.reshape(1, d), dh] + ([dres] if has_res else [])
    return pl.pallas_call(
        body, name=name, grid=(t // tb,),
        in_specs=[_row(tb, d), _const((1, d)), _row(tb, d)] + ([_row(tb, d)] if has_res else []),
        out_specs=[_row(tb, d), _row(tb, d), _const((1, d))],
        out_shape=[_sds((t, d), F32), _sds((t, d), MXU), _sds((1, d), F32)],
        compiler_params=_params("arbitrary"),
    )(*ins)


def _norm_bwd_epilogue(dh, x, dres, g):
    r = lax.rsqrt(jnp.mean(x * x, axis=-1, keepdims=True) + EPS)
    xh = x * r
    dxh = dh * g
    dx = r * (dxh - xh * jnp.mean(dxh * xh, axis=-1, keepdims=True)) + dres
    return dx, dx, jnp.sum(dh * xh, axis=0, keepdims=True)


def _loss_head(x, g, target):
    t, d = x.shape
    tb = _tile(t, 256, 8)

    def body(x_ref, g_ref, tg_ref, loss_ref, dx_ref, dxm_ref, dg_ref):
        @pl.when(pl.program_id(0) == 0)
        def _():
            dg_ref[...] = jnp.zeros_like(dg_ref)
            loss_ref[...] = jnp.zeros_like(loss_ref)

        xv, gv = x_ref[...], g_ref[...]
        r = lax.rsqrt(jnp.mean(xv * xv, axis=-1, keepdims=True) + EPS)
        xh = xv * r
        err = xh * gv - tg_ref[...]
        loss_ref[...] += 0.5 * jnp.sum(jnp.mean(err * err, axis=-1, keepdims=True), axis=0, keepdims=True)
        dy = err * (1.0 / d)
        dg_ref[...] += jnp.sum(dy * xh, axis=0, keepdims=True)
        dxh = dy * gv
        dx = r * (dxh - xh * jnp.mean(dxh * xh, axis=-1, keepdims=True))
        dx_ref[...] = dx
        dxm_ref[...] = dx.astype(dxm_ref.dtype)

    return pl.pallas_call(
        body, name="loss_head", grid=(t // tb,),
        in_specs=[_row(tb, d), _const((1, d)), _row(tb, d)],
        out_specs=[_const((1, HP)), _row(tb, d), _row(tb, d), _const((1, d))],
        out_shape=[_sds((1, HP), F32), _sds((t, d), F32), _sds((t, d), MXU), _sds((1, d), F32)],
        compiler_params=_params("arbitrary"),
    )(x, g.reshape(1, d), target)


def _s5_discretise(ar, ai, ldt, br, bi, rep):
    dt = jnp.exp(ldt)
    mag = jnp.exp(dt * ar)
    abar_r, abar_i = mag * jnp.cos(dt * ai), mag * jnp.sin(dt * ai)
    den = ar * ar + ai * ai
    zr, zi = abar_r - 1.0, abar_i
    fr = (zr * ar + zi * ai) / den
    fi = (zi * ar - zr * ai) / den
    fr_e, fi_e = _dot(fr, rep, precision=HI), _dot(fi, rep, precision=HI)
    return abar_r, abar_i, fr_e * br - fi_e * bi, fr_e * bi + fi_e * br


def _s5_prep(ar, ai, ldt, br, bi, rep):
    g, p = ar.shape
    ph = br.shape[1]

    def body(ar_ref, ai_ref, ldt_ref, br_ref, bi_ref, rep_ref, o0, o1, o2, o3):
        outs = _s5_discretise(ar_ref[...], ai_ref[...], ldt_ref[...], br_ref[...], bi_ref[...], rep_ref[...])
        for o, v in zip((o0, o1, o2, o3), outs):
            o[...] = v

    return pl.pallas_call(
        body, name="s5_prep",
        out_shape=[_sds((g, p), F32), _sds((g, p), F32), _sds((g, ph), F32), _sds((g, ph), F32)],
        compiler_params=pltpu.CompilerParams(vmem_limit_bytes=VMEM_LIMIT),
    )(ar, ai, ldt, br, bi, rep)


def _s5_prep_bwd(ar, ai, ldt, br, bi, rep, d_abar_r, d_abar_i, d_bbar_r, d_bbar_i):
    g, p = ar.shape
    ph = br.shape[1]

    def body(ar_ref, ai_ref, ldt_ref, br_ref, bi_ref, rep_ref, c0, c1, c2, c3, o0, o1, o2, o3, o4):
        rep_v = rep_ref[...]
        _, vjp = jax.vjp(lambda a, b, c, d, e: _s5_discretise(a, b, c, d, e, rep_v),
                         ar_ref[...], ai_ref[...], ldt_ref[...], br_ref[...], bi_ref[...])
        grads = vjp((c0[...], c1[...], c2[...], c3[...]))
        for o, v in zip((o0, o1, o2, o3, o4), grads):
            o[...] = v

    return pl.pallas_call(
        body, name="s5_prep_bwd",
        out_shape=[_sds((g, p), F32), _sds((g, p), F32), _sds((g, 1), F32), _sds((g, ph), F32), _sds((g, ph), F32)],
        compiler_params=pltpu.CompilerParams(vmem_limit_bytes=VMEM_LIMIT),
    )(ar, ai, ldt, br, bi, rep, d_abar_r, d_abar_i, d_bbar_r, d_bbar_i)


SCAN_ROWS = 1024


def _scan_rows(t):
    return _tile(t, SCAN_ROWS, 64)


def _interleave_rows(x):
    t, c = x.shape
    tb = _scan_rows(t)
    return x.reshape(t // tb, 8, tb // 8, c).swapaxes(1, 2).reshape(t, c)


def _deinterleave_rows(x):
    t, c = x.shape
    tb = _scan_rows(t)
    return x.reshape(t // tb, tb // 8, 8, c).swapaxes(1, 2).reshape(t, c)


def _cmul(ar, ai, br, bi):
    return ar * br - ai * bi, ar * bi + ai * br


def _segment_carries(fr, fi, ar8, ai8, c_r, c_i, seg, reverse):
    pr, pi = ar8, ai8
    for _ in range(int(math.log2(seg))):
        pr, pi = _cmul(pr, pi, pr, pi)
    row = lax.broadcasted_iota(jnp.int32, fr.shape, 0)
    edge = 7 if reverse else 0
    qr, qi = _cmul(pr, pi, c_r, c_i)
    xr, xi = jnp.where(row == edge, fr + qr, fr), jnp.where(row == edge, fi + qi, fi)
    for sh in (1, 2, 4):
        if reverse:
            keep, amount = row < 8 - sh, 8 - sh
        else:
            keep, amount = row >= sh, sh
        qr, qi = jnp.where(keep, pltpu.roll(xr, amount, 0), 0.0), jnp.where(keep, pltpu.roll(xi, amount, 0), 0.0)
        tr, ti = _cmul(pr, pi, qr, qi)
        xr, xi = xr + tr, xi + ti
        pr, pi = _cmul(pr, pi, pr, pi)
    if reverse:
        in_r, in_i = jnp.where(row == 7, c_r, pltpu.roll(xr, 7, 0)), jnp.where(row == 7, c_i, pltpu.roll(xi, 7, 0))
        return in_r, in_i, xr[0:1, :], xi[0:1, :]
    in_r, in_i = jnp.where(row == 0, c_r, pltpu.roll(xr, 1, 0)), jnp.where(row == 0, c_i, pltpu.roll(xi, 1, 0))
    return in_r, in_i, xr[7:8, :], xi[7:8, :]


def _sweeps(ar8, ai8, dr, di, cr, ci, seg, reverse, emit):
    order = range(seg - 1, -1, -1) if reverse else range(seg)
    rows = lambda j: slice(j * 8, (j + 1) * 8)
    fr, fi = jnp.zeros(ar8.shape, F32), jnp.zeros(ar8.shape, F32)
    for j in order:
        tr, ti = _cmul(ar8, ai8, fr, fi)
        fr, fi = tr + dr[rows(j), :], ti + di[rows(j), :]
    s_r, s_i, out_r, out_i = _segment_carries(fr, fi, ar8, ai8, cr[...], ci[...], seg, reverse)
    cr[...] = out_r
    ci[...] = out_i
    for j in order:
        tr, ti = _cmul(ar8, ai8, s_r, s_i)
        s_r, s_i = tr + dr[rows(j), :], ti + di[rows(j), :]
        emit(j, s_r, s_i)


S5_BK, S5_BN = 256, 1024


def _s5_specs(tb, nt, reverse):
    t_of = (lambda s: nt - 1 - s) if reverse else (lambda s: s)
    return dict(
        small=pl.BlockSpec((tb, S5_BK), lambda c, s: (t_of(s), c)),
        wide=pl.BlockSpec((tb, S5_BN), lambda c, s: (t_of(s), c)),
        halo=pl.BlockSpec((8, S5_BN), lambda c, s: (jnp.maximum(t_of(s) * (tb // 8) - 1, 0), c)),
        vec_w=pl.BlockSpec((1, S5_BN), lambda c, s: (0, c)),
        vec_s=pl.BlockSpec((1, S5_BK), lambda c, s: (0, c)),
        w_in=pl.BlockSpec((None, S5_BK, S5_BN), lambda c, s: (c, 0, 0)),
        w_out=pl.BlockSpec((None, S5_BN, S5_BK), lambda c, s: (c, 0, 0)),
    )


def _s5_fwd(u, ar, ai, bre, bim, cre, cim, d_vec):
    t = u.shape[0]
    tb = _scan_rows(t)
    nt, seg, nb = t // tb, tb // 8, bre.shape[0]
    assert 1 << int(math.log2(seg)) == seg

    def body(u_ref, ar_ref, ai_ref, bre_ref, bim_ref, cre_ref, cim_ref, d_ref, sr_ref, si_ref, yp_ref, yg_ref,
             cr, ci, dr_s, di_s):
        @pl.when(pl.program_id(1) == 0)
        def _():
            cr[...] = jnp.zeros_like(cr)
            ci[...] = jnp.zeros_like(ci)

        uv = u_ref[...]
        ub = uv.astype(MXU)
        dr_s[...] = _dot(ub, bre_ref[...])
        di_s[...] = _dot(ub, bim_ref[...])
        ar8 = jnp.broadcast_to(ar_ref[...], (8, S5_BN))
        ai8 = jnp.broadcast_to(ai_ref[...], (8, S5_BN))

        def emit(j, s_r, s_i):
            sr_ref[j * 8:(j + 1) * 8, :] = s_r
            si_ref[j * 8:(j + 1) * 8, :] = s_i

        _sweeps(ar8, ai8, dr_s, di_s, cr, ci, seg, False, emit)
        yp = _dot(sr_ref[...].astype(MXU), cre_ref[...]) - _dot(si_ref[...].astype(MXU), cim_ref[...]) + d_ref[...] * uv
        yp_ref[...] = yp
        yg_ref[...] = _gelu(yp).astype(yg_ref.dtype)

    sp = _s5_specs(tb, nt, False)
    return pl.pallas_call(
        body, name="s5_fwd", grid=(nb, nt),
        in_specs=[sp["small"], sp["vec_w"], sp["vec_w"], sp["w_in"], sp["w_in"], sp["w_out"], sp["w_out"], sp["vec_s"]],
        out_specs=[sp["wide"], sp["wide"], sp["small"], sp["small"]],
        out_shape=[_sds((t, nb * S5_BN), F32), _sds((t, nb * S5_BN), F32), _sds((t, nb * S5_BK), F32),
                   _sds((t, nb * S5_BK), MXU)],
        scratch_shapes=[pltpu.VMEM((1, S5_BN), F32), pltpu.VMEM((1, S5_BN), F32), pltpu.VMEM((tb, S5_BN), F32),
                        pltpu.VMEM((tb, S5_BN), F32)],
        compiler_params=_params("parallel", "arbitrary"),
    )(u, ar, ai, bre, bim, cre, cim, d_vec)


def _s5_bwd(dyp, u, sr, si, ar, ai, cre_t, cim_t, bre_t, bim_t, d_vec):
    t = u.shape[0]
    tb = _scan_rows(t)
    nt, seg, nb = t // tb, tb // 8, cre_t.shape[0]

    def body(dyp_ref, u_ref, pr_ref, pi_ref, hr_ref, hi_ref, ar_ref, ai_ref, cre_ref, cim_ref, bre_ref, bim_ref, d_ref,
             du_ref, gr_ref, gi_ref, dbr_ref, dbi_ref, dcr_ref, dci_ref, dd_ref, cr, ci, dr_s, di_s, lr_s, li_s):
        step = pl.program_id(1)

        @pl.when(step == 0)
        def _():
            for r in (cr, ci, gr_ref, gi_ref, dbr_ref, dbi_ref, dcr_ref, dci_ref, dd_ref):
                r[...] = jnp.zeros_like(r)

        dyv, uv = dyp_ref[...], u_ref[...]
        dyb, ub = dyv.astype(MXU), uv.astype(MXU)
        dr_s[...] = _dot(dyb, cre_ref[...])
        di_s[...] = -_dot(dyb, cim_ref[...])
        ar8 = jnp.broadcast_to(ar_ref[...], (8, S5_BN))
        ai8 = jnp.broadcast_to(-ai_ref[...], (8, S5_BN))
        first_block = step == nt - 1
        row = lax.broadcasted_iota(jnp.int32, (8, S5_BN), 0)
        acc = [jnp.zeros((8, S5_BN), F32), jnp.zeros((8, S5_BN), F32)]

        def emit(j, s_r, s_i):
            lr_s[j * 8:(j + 1) * 8, :] = s_r
            li_s[j * 8:(j + 1) * 8, :] = s_i
            if j > 0:
                p_r, p_i = pr_ref[(j - 1) * 8:j * 8, :], pi_ref[(j - 1) * 8:j * 8, :]
            else:
                halo_r = jnp.where(first_block, 0.0, hr_ref[7:8, :])
                halo_i = jnp.where(first_block, 0.0, hi_ref[7:8, :])
                p_r = jnp.where(row == 0, halo_r, pltpu.roll(pr_ref[(seg - 1) * 8:seg * 8, :], 1, 0))
                p_i = jnp.where(row == 0, halo_i, pltpu.roll(pi_ref[(seg - 1) * 8:seg * 8, :], 1, 0))
            acc[0] = acc[0] + (p_r * s_r + p_i * s_i)
            acc[1] = acc[1] + (p_r * s_i - p_i * s_r)

        _sweeps(ar8, ai8, dr_s, di_s, cr, ci, seg, True, emit)
        gr_ref[...] += jnp.sum(acc[0], axis=0, keepdims=True)
        gi_ref[...] += jnp.sum(acc[1], axis=0, keepdims=True)
        lrb, lib = lr_s[...].astype(MXU), li_s[...].astype(MXU)
        du_ref[...] = (_dot(lrb, bre_ref[...]) + _dot(lib, bim_ref[...]) + d_ref[...] * dyv).astype(du_ref.dtype)
        dbr_ref[...] += _dot(ub, lrb, _TN)
        dbi_ref[...] += _dot(ub, lib, _TN)
        dcr_ref[...] += _dot(pr_ref[...].astype(MXU), dyb, _TN)
        dci_ref[...] += _dot(pi_ref[...].astype(MXU), dyb, _TN)
        dd_ref[...] += jnp.sum(dyv * uv, axis=0, keepdims=True)

    sp = _s5_specs(tb, nt, True)
    return pl.pallas_call(
        body, name="s5_bwd", grid=(nb, nt),
        in_specs=[sp["small"], sp["small"], sp["wide"], sp["wide"], sp["halo"], sp["halo"], sp["vec_w"], sp["vec_w"],
                  sp["w_in"], sp["w_in"], sp["w_out"], sp["w_out"], sp["vec_s"]],
        out_specs=[sp["small"], sp["vec_w"], sp["vec_w"], sp["w_in"], sp["w_in"], sp["w_out"], sp["w_out"], sp["vec_s"]],
        out_shape=[_sds((t, nb * S5_BK), MXU), _sds((1, nb * S5_BN), F32), _sds((1, nb * S5_BN), F32),
                   _sds((nb, S5_BK, S5_BN), F32), _sds((nb, S5_BK, S5_BN), F32), _sds((nb, S5_BN, S5_BK), F32),
                   _sds((nb, S5_BN, S5_BK), F32), _sds((1, nb * S5_BK), F32)],
        scratch_shapes=[pltpu.VMEM((1, S5_BN), F32), pltpu.VMEM((1, S5_BN), F32)] + [pltpu.VMEM((tb, S5_BN), F32)] * 4,
        compiler_params=_params("parallel", "arbitrary"),
    )(dyp, u, sr, si, sr, si, ar, ai, cre_t, cim_t, bre_t, bim_t, d_vec)


def _s5_gate_bwd(d_ycat, gp, ypre):
    t, d = gp.shape
    tb = _tile(t, 256, 8)

    def body(do_ref, gp_ref, yp_ref, o_ref):
        sg = _sigmoid(gp_ref[...])
        o_ref[...] = (do_ref[...] * _gelu(yp_ref[...]) * sg * (1.0 - sg)).astype(o_ref.dtype)

    return pl.pallas_call(
        body, name="s5_gate_bwd", grid=(t // tb,), in_specs=[_row(tb, d), _row(tb, d), _row(tb, d)],
        out_specs=_row(tb, d), out_shape=_sds((t, d), MXU), compiler_params=_params("parallel"),
    )(d_ycat, gp, ypre)


def _ssd_consts():
    head = jnp.arange(HP)[:, None]
    lane = jnp.arange(D_SSD)[None, :]
    expand = ((lane // SSD_HEADDIM) == head).astype(MXU)
    ll = jnp.arange(SSD_CHUNK)
    tri = (ll[:, None] >= ll[None, :]).astype(F32)
    return expand, expand.T, tri, jnp.eye(HP, dtype=F32)


def _ssd_chunk_terms(cp, dtr, par, expand, tri):
    ln = SSD_CHUNK
    xbc = _silu(cp)
    xs, bm, cm = xbc[:, :D_SSD], xbc[:, D_SSD:D_SSD + SSD_BC], xbc[:, D_SSD + SSD_BC:]
    dt = _softplus(dtr + par[0:1, :])
    a = -jnp.exp(par[1:2, :])
    da = dt * a
    acum = _dot(tri, da, precision=HI)
    acum_t = _dot(da, tri, (((0,), (1,)), ((), ())), precision=HI)
    atot = acum[ln - 1:ln, :]
    dt_e = _dot_split(dt, expand)
    eac_e = _dot_split(jnp.exp(acum), expand)
    dec_e = _dot_split(jnp.exp(atot - acum), expand)
    eat_e = _dot_split(_rows8(jnp.exp(atot)), expand)[0:1, :]
    dsk_e = _dot_split(_rows8(par[2:3, :]), expand)[0:1, :]
    return dict(xs=xs, bm=bm, cm=cm, dt=dt, a=a, acum=acum, acum_t=acum_t, dt_e=dt_e, eac_e=eac_e,
                dec_e=dec_e, eat_e=eat_e, dsk_e=dsk_e)


def _decay_matrix(acum, acum_t, h, mask):
    diff = acum[:, h:h + 1] - acum_t[h:h + 1, :]
    return jnp.where(mask, jnp.exp(jnp.minimum(diff, 0.0)), 0.0)


def _ssd_fwd(proj, conv_w, conv_b, dtr, par, gnorm, consts):
    t = proj.shape[0]
    ln = SSD_CHUNK
    nc = t // ln
    expand, _, tri, _ = consts
    hd2 = 2 * SSD_HEADDIM

    def body(cur_ref, prev_ref, cw_ref, cb_ref, z_ref, dtr_ref, par_ref, g_ref, e_ref, tri_ref,
             out_ref, y_ref, st_ref, cp_ref, state, ext):
        first = pl.program_id(0) == 0

        @pl.when(first)
        def _():
            state[...] = jnp.zeros_like(state)

        st_ref[...] = state[...]
        ext[0:8, :] = jnp.where(first, 0.0, prev_ref[...])
        ext[8:ln + 8, :] = cur_ref[...]
        cpv = jnp.broadcast_to(cb_ref[...], (ln, D_CONV_CH))
        for j in range(SSD_CONV):
            cpv = cpv + cw_ref[SSD_CONV - 1 - j:SSD_CONV - j, :] * ext[8 - j:8 - j + ln, :]
        cp_ref[...] = cpv
        c = _ssd_chunk_terms(cpv, dtr_ref[...], par_ref[...], e_ref[...], tri_ref[...])
        xdt = c["xs"] * c["dt_e"]
        xb, xd = xdt.astype(MXU), (xdt * c["dec_e"]).astype(MXU)
        bb, cb = c["bm"].astype(MXU), c["cm"].astype(MXU)
        mask = lax.broadcasted_iota(jnp.int32, (ln, ln), 0) >= lax.broadcasted_iota(jnp.int32, (ln, ln), 1)
        left = lax.broadcasted_iota(jnp.int32, (ln, hd2), 1) < SSD_HEADDIM
        for g in range(SSD_GROUPS):
            nsl = slice(g * SSD_STATE, (g + 1) * SSD_STATE)
            gsl = slice(g * 256, (g + 1) * 256)
            bg, cg = bb[:, nsl], cb[:, nsl]
            cbm = _dot(cg, bg, _NT)
            st_g = state[:, gsl]
            for pair in range(2):
                h0 = g * 4 + pair * 2
                psl = slice(h0 * SSD_HEADDIM, (h0 + 2) * SSD_HEADDIM)
                m0 = (cbm * _decay_matrix(c["acum"], c["acum_t"], h0, mask)).astype(MXU)
                m1 = (cbm * _decay_matrix(c["acum"], c["acum_t"], h0 + 1, mask)).astype(MXU)
                y_ref[:, psl] = jnp.where(left, _dot(m0, xb[:, psl]), _dot(m1, xb[:, psl]))
            y_ref[:, gsl] += _dot(cg, st_g.astype(MXU)) * c["eac_e"][:, gsl]
            state[:, gsl] = st_g * c["eat_e"][:, gsl] + _dot(bg, xd[:, gsl], _TN)
        y = y_ref[...] + c["dsk_e"] * c["xs"]
        y_ref[...] = y
        y2 = y * _silu(z_ref[...])
        r = lax.rsqrt(jnp.mean(y2 * y2, axis=-1, keepdims=True) + EPS)
        out_ref[...] = (y2 * r * g_ref[...]).astype(out_ref.dtype)

    xbc_block = (D_MAIN - D_CONV_CH) // D_CONV_CH
    return pl.pallas_call(
        body, name="ssd_fwd", grid=(nc,),
        in_specs=[_row(ln, D_CONV_CH, xbc_block),
                  pl.BlockSpec((8, D_CONV_CH), lambda i: (jnp.maximum(i * (ln // 8) - 1, 0), xbc_block)),
                  _const((SSD_CONV, D_CONV_CH)), _const((1, D_CONV_CH)),
                  _row(ln, D_SSD, 1), _row(ln, HP), _const((8, HP)), _const((1, D_SSD)),
                  _const((HP, D_SSD)), _const((ln, ln))],
        out_specs=[_row(ln, D_SSD), _row(ln, D_SSD), pl.BlockSpec((None, SSD_STATE, D_SSD), lambda i: (i, 0, 0)),
                   _row(ln, D_CONV_CH)],
        out_shape=[_sds((t, D_SSD), MXU), _sds((t, D_SSD), F32), _sds((nc, SSD_STATE, D_SSD), F32),
                   _sds((t, D_CONV_CH), F32)],
        scratch_shapes=[pltpu.VMEM((SSD_STATE, D_SSD), F32), pltpu.VMEM((ln + 8, D_CONV_CH), F32)],
        compiler_params=_params("arbitrary"),
    )(proj, proj, conv_w, conv_b.reshape(1, D_CONV_CH), proj, dtr, par, gnorm.reshape(1, D_SSD), expand, tri)


def _ssd_bwd(proj, conv_pre, conv_w, dtr, par, gnorm, y, states, d_ycat, consts):
    t = proj.shape[0]
    ln = SSD_CHUNK
    nc = t // ln
    expand, expand_t, tri, eye = consts
    hd2 = 2 * SSD_HEADDIM

    def body(cp_ref, z_ref, dtr_ref, par_ref, g_ref, y_ref, st_ref, do_ref, e_ref, et_ref, tri_ref, eye_ref,
             cur_ref, prev_ref, cw_ref,
             dxbc_ref, dz_ref, ddt_ref, dg_ref, dpar_ref, dcw_ref, dcb_ref,
             dstate, dx_buf, lane_buf, tot_buf, colsum, dcp_ref, ext, dext):
        @pl.when(pl.program_id(0) == 0)
        def _():
            dstate[...] = jnp.zeros_like(dstate)
            dg_ref[...] = jnp.zeros_like(dg_ref)
            dpar_ref[...] = jnp.zeros_like(dpar_ref)
            dcw_ref[...] = jnp.zeros_like(dcw_ref)
            dcb_ref[...] = jnp.zeros_like(dcb_ref)
            dext[ln:ln + 8, :] = jnp.zeros((8, D_CONV_CH), F32)

        cpv, et_v, tri_v, par_v = cp_ref[...], et_ref[...], tri_ref[...], par_ref[...]
        c = _ssd_chunk_terms(cpv, dtr_ref[...], par_v, e_ref[...], tri_v)
        xs = c["xs"]
        zv, yv, dov = z_ref[...], y_ref[...], do_ref[...]
        sz = _silu(zv)
        y2 = yv * sz
        r = lax.rsqrt(jnp.mean(y2 * y2, axis=-1, keepdims=True) + EPS)
        yh = y2 * r
        dg_ref[...] += jnp.sum(dov * yh, axis=0, keepdims=True)
        dyh = dov * g_ref[...]
        dy2 = r * (dyh - yh * jnp.mean(dyh * yh, axis=-1, keepdims=True))
        dz_ref[...] = (dy2 * yv * _silu_grad(zv)).astype(dz_ref.dtype)
        dy = dy2 * sz

        xdt = xs * c["dt_e"]
        xdf = xdt * c["dec_e"]
        xb, xd = xdt.astype(MXU), xdf.astype(MXU)
        bb, cb = c["bm"].astype(MXU), c["cm"].astype(MXU)
        dyb, dye = dy.astype(MXU), (dy * c["eac_e"]).astype(MXU)
        mask = lax.broadcasted_iota(jnp.int32, (ln, ln), 0) >= lax.broadcasted_iota(jnp.int32, (ln, ln), 1)
        left = lax.broadcasted_iota(jnp.int32, (ln, hd2), 1) < SSD_HEADDIM
        lane_hp = lax.broadcasted_iota(jnp.int32, (ln, HP), 1)
        d_acum = jnp.zeros((ln, HP), F32)
        colsum[...] = jnp.zeros_like(colsum)
        tot_buf[...] = jnp.zeros_like(tot_buf)
        for g in range(SSD_GROUPS):
            nsl = slice(g * SSD_STATE, (g + 1) * SSD_STATE)
            gsl = slice(g * 256, (g + 1) * 256)
            bg, cg = bb[:, nsl], cb[:, nsl]
            cbm = _dot(cg, bg, _NT)
            st_g = st_ref[:, gsl]
            dst_g = dstate[:, gsl]
            stb, dstb = st_g.astype(MXU), dst_g.astype(MXU)
            y_off = _dot(cg, stb)
            bds = _dot(bg, dstb)
            dcb = jnp.zeros((ln, ln), F32)
            for pair in range(2):
                h0 = g * 4 + pair * 2
                psl = slice(h0 * SSD_HEADDIM, (h0 + 2) * SSD_HEADDIM)
                xp, dyp = xb[:, psl], dyb[:, psl]
                dxp = []
                for k in range(2):
                    h = h0 + k
                    lm = _decay_matrix(c["acum"], c["acum_t"], h, mask)
                    mm = cbm * lm
                    half = left if k == 0 else jnp.logical_not(left)
                    dm = _dot(jnp.where(half, dyp, jnp.zeros_like(dyp)), xp, _NT)
                    dcb = dcb + dm * lm
                    gm = dm * mm
                    d_acum = d_acum + jnp.where(lane_hp == h, jnp.sum(gm, axis=1, keepdims=True), 0.0)
                    colsum[h:h + 1, :] = jnp.sum(gm, axis=0, keepdims=True)
                    dxp.append(_dot(mm.astype(MXU), dyp, _TN))
                dx_buf[:, psl] = jnp.where(left, dxp[0], dxp[1])
            dx_buf[:, gsl] += bds * c["dec_e"][:, gsl]
            dcbb = dcb.astype(MXU)
            dc_g = _dot(dcbb, bg) + _dot(dye[:, gsl], stb, _NT)
            db_g = _dot(dcbb, cg, _TN) + _dot(xd[:, gsl], dstb, _NT)
            dcp_ref[:, D_SSD + g * SSD_STATE:D_SSD + (g + 1) * SSD_STATE] = db_g
            dcp_ref[:, D_SSD + SSD_BC + g * SSD_STATE:D_SSD + SSD_BC + (g + 1) * SSD_STATE] = dc_g
            dec_term = xdf[:, gsl] * bds
            lane_buf[:, gsl] = dy[:, gsl] * y_off * c["eac_e"][:, gsl] - dec_term
            tot_buf[0:1, gsl] = (jnp.sum(st_g * dst_g, axis=0, keepdims=True) * c["eat_e"][:, gsl]
                                 + jnp.sum(dec_term, axis=0, keepdims=True))
            dstate[:, gsl] = dst_g * c["eat_e"][:, gsl] + _dot(cg, dye[:, gsl], _TN)
        dx_tot = dx_buf[...]
        d_acum = d_acum + _dot_split(lane_buf[...], et_v) - _dot(colsum[...], eye_ref[...], _TN, precision=HI)
        d_atot = _dot_split(tot_buf[...], et_v)[0:1, :]
        row_hp = lax.broadcasted_iota(jnp.int32, (ln, HP), 0)
        d_acum = d_acum + jnp.where(row_hp == ln - 1, d_atot, 0.0)
        d_da = _dot(tri_v, d_acum, _TN, precision=HI)
        d_dt = d_da * c["a"] + _dot_split(dx_tot * xs, et_v)
        d_dtr = d_dt * _sigmoid(dtr_ref[...] + par_v[0:1, :])
        ddt_ref[...] = d_dtr
        dpar_ref[0:1, :] += jnp.sum(d_dtr, axis=0, keepdims=True)
        dpar_ref[1:2, :] += jnp.sum(d_da * c["dt"], axis=0, keepdims=True) * c["a"]
        dpar_ref[2:3, :] += _dot_split(_rows8(jnp.sum(dy * xs, axis=0, keepdims=True)), et_v)[0:1, :]
        dcp_ref[:, 0:D_SSD] = dx_tot * c["dt_e"] + dy * c["dsk_e"]
        dcv = dcp_ref[...] * _silu_grad(cpv)
        ext[0:8, :] = jnp.where(pl.program_id(0) == nc - 1, 0.0, prev_ref[...])
        ext[8:ln + 8, :] = cur_ref[...]
        dext[0:ln, :] = dcv
        dxbc = jnp.zeros((ln, D_CONV_CH), F32)
        for j in range(SSD_CONV):
            dxbc = dxbc + cw_ref[SSD_CONV - 1 - j:SSD_CONV - j, :] * dext[j:j + ln, :]
            dcw_ref[SSD_CONV - 1 - j:SSD_CONV - j, :] += jnp.sum(dcv * ext[8 - j:8 - j + ln, :], axis=0, keepdims=True)
        dxbc_ref[...] = dxbc.astype(dxbc_ref.dtype)
        dcb_ref[...] += jnp.sum(dcv, axis=0, keepdims=True)
        dext[ln:ln + 8, :] = dcv[0:8, :]

    xbc_block = (D_MAIN - D_CONV_CH) // D_CONV_CH
    rev = lambda i: (nc - 1 - i, 0)
    rev1 = lambda i: (nc - 1 - i, 1)
    return pl.pallas_call(
        body, name="ssd_bwd", grid=(nc,),
        in_specs=[pl.BlockSpec((ln, D_CONV_CH), rev), pl.BlockSpec((ln, D_SSD), rev1), pl.BlockSpec((ln, HP), rev),
                  _const((8, HP)), _const((1, D_SSD)), pl.BlockSpec((ln, D_SSD), rev),
                  pl.BlockSpec((None, SSD_STATE, D_SSD), lambda i: (nc - 1 - i, 0, 0)),
                  pl.BlockSpec((ln, D_SSD), rev1),
                  _const((HP, D_SSD)), _const((D_SSD, HP)), _const((ln, ln)), _const((HP, HP)),
                  pl.BlockSpec((ln, D_CONV_CH), lambda i: (nc - 1 - i, xbc_block)),
                  pl.BlockSpec((8, D_CONV_CH), lambda i: (jnp.maximum((nc - 1 - i) * (ln // 8) - 1, 0), xbc_block)),
                  _const((SSD_CONV, D_CONV_CH))],
        out_specs=[pl.BlockSpec((ln, D_CONV_CH), rev), pl.BlockSpec((ln, D_SSD), rev), pl.BlockSpec((ln, HP), rev),
                   _const((1, D_SSD)), _const((8, HP)), _const((SSD_CONV, D_CONV_CH)), _const((1, D_CONV_CH))],
        out_shape=[_sds((t, D_CONV_CH), MXU), _sds((t, D_SSD), MXU), _sds((t, HP), F32), _sds((1, D_SSD), F32),
                   _sds((8, HP), F32), _sds((SSD_CONV, D_CONV_CH), F32), _sds((1, D_CONV_CH), F32)],
        scratch_shapes=[pltpu.VMEM((SSD_STATE, D_SSD), F32), pltpu.VMEM((ln, D_SSD), F32), pltpu.VMEM((ln, D_SSD), F32),
                        pltpu.VMEM((8, D_SSD), F32), pltpu.VMEM((HP, ln), F32), pltpu.VMEM((ln, D_CONV_CH), F32),
                        pltpu.VMEM((ln + 8, D_CONV_CH), F32), pltpu.VMEM((ln + 8, D_CONV_CH), F32)],
        compiler_params=_params("arbitrary"),
    )(conv_pre, proj, dtr, par, gnorm.reshape(1, D_SSD), y, states, d_ycat, expand, expand_t, tri, eye, proj, proj, conv_w)


def _softmax_rows(s):
    e = jnp.exp(s - jnp.max(s, axis=-1, keepdims=True))
    return e * _recip(jnp.sum(e, axis=-1, keepdims=True))


def _attn_fwd(q, k, v):
    t, d = q.shape
    mlen = k.shape[0]
    tq = _tile(t, 512, 8)
    scale = XA_HEAD_DIM ** -0.5

    def body(q_ref, k_ref, v_ref, o_ref):
        for h in range(XA_HEADS):
            sl = slice(h * XA_HEAD_DIM, (h + 1) * XA_HEAD_DIM)
            p = _softmax_rows(_dot(q_ref[:, sl], k_ref[:, sl], _NT) * scale)
            o_ref[:, sl] = _dot(p.astype(MXU), v_ref[:, sl]).astype(o_ref.dtype)

    return pl.pallas_call(
        body, name="xattn_fwd", grid=(t // tq,),
        in_specs=[_row(tq, d), _const((mlen, d)), _const((mlen, d))], out_specs=_row(tq, d),
        out_shape=_sds((t, d), MXU), compiler_params=_params("parallel"),
    )(q, k, v)


def _attn_bwd(q, k, v, do):
    t, d = q.shape
    mlen = k.shape[0]
    tq = _tile(t, 512, 8)
    scale = XA_HEAD_DIM ** -0.5

    def body(q_ref, k_ref, v_ref, do_ref, dq_ref, dk_ref, dv_ref):
        @pl.when(pl.program_id(0) == 0)
        def _():
            dk_ref[...] = jnp.zeros_like(dk_ref)
            dv_ref[...] = jnp.zeros_like(dv_ref)

        for h in range(XA_HEADS):
            sl = slice(h * XA_HEAD_DIM, (h + 1) * XA_HEAD_DIM)
            qh, kh, vh, doh = q_ref[:, sl], k_ref[:, sl], v_ref[:, sl], do_ref[:, sl]
            p = _softmax_rows(_dot(qh, kh, _NT) * scale)
            dp = _dot(doh, vh, _NT)
            dv_ref[:, sl] += _dot(p.astype(MXU), doh, _TN)
            ds = (p * (dp - jnp.sum(p * dp, axis=-1, keepdims=True)) * scale).astype(MXU)
            dq_ref[:, sl] = _dot(ds, kh).astype(dq_ref.dtype)
            dk_ref[:, sl] += _dot(ds, qh, _TN)

    return pl.pallas_call(
        body, name="xattn_bwd", grid=(t // tq,),
        in_specs=[_row(tq, d), _const((mlen, d)), _const((mlen, d)), _row(tq, d)],
        out_specs=[_row(tq, d), _const((mlen, d)), _const((mlen, d))],
        out_shape=[_sds((t, d), MXU), _sds((mlen, d), F32), _sds((mlen, d), F32)],
        compiler_params=_params("arbitrary"),
    )(q, k, v, do)


def _lane_view(a):
    if a.ndim >= 2 and a.shape[-1] >= 128:
        return a.reshape(-1, a.shape[-1])
    if a.size % 128 == 0:
        return a.reshape(-1, 128)
    return a.reshape(1, -1)


def _adamw(w, g, m, v, *, name):
    shape = w.shape
    w2, g2, m2, v2 = (_lane_view(a) for a in (w, g.reshape(shape), m, v))
    r, c = w2.shape
    tr = _tile(r, max(8, (1 << 18) // max(c, 128)), 8)

    def body(w_ref, g_ref, m_ref, v_ref, d_ref, mo_ref, vo_ref):
        gv = g_ref[...]
        mn = ADAM_B1 * m_ref[...] + (1.0 - ADAM_B1) * gv
        vn = ADAM_B2 * v_ref[...] + (1.0 - ADAM_B2) * (gv * gv)
        m_hat = mn / (1.0 - ADAM_B1 ** ADAM_STEP)
        v_hat = vn / (1.0 - ADAM_B2 ** ADAM_STEP)
        d_ref[...] = -ADAM_LR * (m_hat / (jnp.sqrt(v_hat) + ADAM_EPS) + ADAM_WD * w_ref[...])
        mo_ref[...] = mn
        vo_ref[...] = vn

    outs = pl.pallas_call(
        body, name=name, grid=(r // tr,), in_specs=[_row(tr, c)] * 4, out_specs=[_row(tr, c)] * 3,
        out_shape=[_sds((r, c), F32)] * 3, compiler_params=_params("parallel"),
    )(w2, g2, m2, v2)
    return tuple(o.reshape(shape) for o in outs)


_HBM = pl.BlockSpec(memory_space=pltpu.HBM)


N_CHIPS = 4
CHIPS = ((0, 0), (0, 1), (1, 0), (1, 1))


def _pair_exchange(x, *, name):
    def body(x_ref, o_ref, send_sems, recv_sems):
        xx, yy, cc = (lax.axis_index(a) for a in AXES)
        copies = [
            pltpu.make_async_remote_copy(
                src_ref=x_ref.at[4 * px + 2 * py + (1 - cc)], dst_ref=o_ref.at[k], send_sem=send_sems.at[k],
                recv_sem=recv_sems.at[k], device_id=(xx, yy, 1 - cc), device_id_type=pl.DeviceIdType.MESH)
            for k, (px, py) in enumerate(CHIPS)]
        for cp in copies:
            cp.start()
        for cp in copies:
            cp.wait_recv()
        for cp in copies:
            cp.wait_send()

    return pl.pallas_call(
        body, name=name, in_specs=[_HBM], out_specs=_HBM, out_shape=_sds((N_CHIPS,) + tuple(x.shape[1:]), x.dtype),
        scratch_shapes=[pltpu.SemaphoreType.DMA((N_CHIPS,)), pltpu.SemaphoreType.DMA((N_CHIPS,))],
    )(x)


def _pair_add(x, got, *, name):
    _, r, c = x.shape
    tr = _tile(r, max(PAD_ROWS, (1 << 17) // c), PAD_ROWS)

    def body(x_ref, g_ref, o_ref):
        mine = jnp.where(lax.axis_index("c") == 0, x_ref[0].astype(F32), x_ref[1].astype(F32))
        o_ref[...] = (mine + g_ref[...].astype(F32)).astype(o_ref.dtype)

    return pl.pallas_call(
        body, name=name, grid=(N_CHIPS, r // tr),
        in_specs=[pl.BlockSpec((None, 2, tr, c), lambda k, i: (k, 0, i, 0)), pl.BlockSpec((None, tr, c), lambda k, i: (k, i, 0))],
        out_specs=pl.BlockSpec((None, tr, c), lambda k, i: (k, i, 0)), out_shape=_sds((N_CHIPS, r, c), x.dtype),
        compiler_params=_params("parallel", "parallel"),
    )(x.reshape(N_CHIPS, 2, r, c), got)


def _chip_exchange(x, *, name):
    def body(x_ref, o_ref, send_sems, recv_sems, local_sem):
        xx, yy, cc = (lax.axis_index(a) for a in AXES)
        mine = 2 * xx + yy
        local = pltpu.make_async_copy(x_ref.at[mine], o_ref.at[mine], local_sem)
        local.start()
        sends = []
        for j, (px, py) in enumerate([(1 - xx, yy), (xx, 1 - yy), (1 - xx, 1 - yy)]):
            cp = pltpu.make_async_remote_copy(
                src_ref=x_ref.at[2 * px + py], dst_ref=o_ref.at[mine], send_sem=send_sems.at[j], recv_sem=recv_sems.at[j],
                device_id=(px, py, cc), device_id_type=pl.DeviceIdType.MESH)
            cp.start()
            sends.append(cp)
        for j, (px, py) in enumerate([(1 - xx, yy), (xx, 1 - yy), (1 - xx, 1 - yy)]):
            pltpu.make_async_remote_copy(
                src_ref=x_ref.at[2 * px + py], dst_ref=o_ref.at[2 * px + py], send_sem=send_sems.at[j],
                recv_sem=recv_sems.at[j], device_id=(px, py, cc), device_id_type=pl.DeviceIdType.MESH).wait_recv()
        for cp in sends:
            cp.wait_send()
        local.wait()

    return pl.pallas_call(
        body, name=name, in_specs=[_HBM], out_specs=_HBM, out_shape=_sds(x.shape, x.dtype),
        scratch_shapes=[pltpu.SemaphoreType.DMA((N_CHIPS - 1,)), pltpu.SemaphoreType.DMA((N_CHIPS - 1,)),
                        pltpu.SemaphoreType.DMA(())],
    )(x)


def _reduce_scatter(x, *, name):
    chip_sums = _pair_add(x, _pair_exchange(x, name=name + "_pair"), name=name + "_pair_add")
    return _sum_slots(_chip_exchange(chip_sums, name=name + "_chips"), name=name + "_sum")


def _all_gather(x, *, name):
    def body(x_ref, o_ref, send_sems, recv_sems, local_sem):
        xx, yy, cc = (lax.axis_index(a) for a in AXES)
        me, sibling = (xx, yy, cc), (xx, yy, 1 - cc)
        chips = [(1 - xx, yy), (xx, 1 - yy), (1 - xx, 1 - yy)]

        def slot(px, py, pc):
            return o_ref.at[4 * px + 2 * py + pc]

        def copy(k, block, to, src=None):
            return pltpu.make_async_remote_copy(
                src_ref=slot(*block) if src is None else src, dst_ref=slot(*block), send_sem=send_sems.at[k],
                recv_sem=recv_sems.at[k], device_id=to, device_id_type=pl.DeviceIdType.MESH)

        local = pltpu.make_async_copy(x_ref, slot(*me), local_sem)
        local.start()
        first = [copy(0, me, sibling, src=x_ref)] + [copy(1 + j, me, (*chip, cc), src=x_ref) for j, chip in enumerate(chips)]
        for cp in first:
            cp.start()
        passed = [copy(4 + j, (*chip, cc), sibling) for j, chip in enumerate(chips)]
        for j, chip in enumerate(chips):
            copy(1 + j, (*chip, cc), me).wait_recv()
            passed[j].start()
        copy(0, sibling, me).wait_recv()
        for j, chip in enumerate(chips):
            copy(4 + j, (*chip, 1 - cc), me).wait_recv()
        for cp in first + passed:
            cp.wait_send()
        local.wait()

    return pl.pallas_call(
        body, name=name, in_specs=[_HBM], out_specs=_HBM, out_shape=_sds((N_DEV,) + tuple(x.shape), x.dtype),
        scratch_shapes=[pltpu.SemaphoreType.DMA((N_DEV - 1,)), pltpu.SemaphoreType.DMA((N_DEV - 1,)),
                        pltpu.SemaphoreType.DMA(())],
    )(x)


def _sum_slots(x, *, name):
    n, r, c = x.shape
    tr = _tile(r, max(PAD_ROWS, (1 << 17) // c), PAD_ROWS)

    def body(x_ref, o_ref):
        acc = x_ref[0].astype(F32)
        for d in range(1, n):
            acc = acc + x_ref[d].astype(F32)
        o_ref[...] = acc

    return pl.pallas_call(
        body, name=name, grid=(r // tr,), in_specs=[pl.BlockSpec((n, tr, c), lambda i: (0, i, 0))],
        out_specs=_row(tr, c), out_shape=_sds((r, c), F32), compiler_params=_params("parallel"),
    )(x)


def _s5_layouts(bbar_r, bbar_i, c_re, c_im):
    eye = jnp.eye(S5_GPB, dtype=F32)

    def b_blocks(bbar):
        bb = bbar.reshape(S5_NB, S5_GPB, S5_STATE, S5_GROUP)
        return jnp.einsum("jgph,gk->jghkp", bb, eye).reshape(S5_NB, S5_GPB * S5_GROUP, S5_GPB * S5_STATE)

    def c_blocks(cc):
        c4 = cc.reshape(S5_NB, S5_GPB, S5_GROUP, S5_STATE)
        return jnp.einsum("jghp,gk->jgpkh", c4, eye).reshape(S5_NB, S5_GPB * S5_STATE, S5_GPB * S5_GROUP)

    bre, bim, cre, cim = b_blocks(bbar_r), b_blocks(bbar_i), c_blocks(c_re), c_blocks(c_im)
    cast = lambda a: a.astype(MXU)
    sw = lambda a: jnp.swapaxes(a, 1, 2).astype(MXU)
    return dict(bre=cast(bre), bim=cast(bim), cre=cast(cre), cim=cast(cim), bre_t=sw(bre), bim_t=sw(bim), cre_t=sw(cre),
                cim_t=sw(cim))


def _b_diag(db):
    d5 = db.reshape(S5_NB, S5_GPB, S5_GROUP, S5_GPB, S5_STATE)
    diag = jnp.stack([d5[:, g, :, g, :] for g in range(S5_GPB)], axis=1)
    return jnp.swapaxes(diag, 2, 3).reshape(S5_GROUPS, S5_STATE * S5_GROUP)


def _c_diag(dc):
    d5 = dc.reshape(S5_NB, S5_GPB, S5_STATE, S5_GPB, S5_GROUP)
    diag = jnp.stack([d5[:, g, :, g, :] for g in range(S5_GPB)], axis=1)
    return jnp.swapaxes(diag, 2, 3).reshape(S5_GROUPS, S5_GROUP, S5_STATE)


def _head_rows(*vecs):
    par = jnp.zeros((8, HP), F32)
    for i, v in enumerate(vecs):
        par = par.at[i, :SSD_HEADS].set(v.astype(F32))
    return par


def _add(acc, r):
    return (acc + r,)


def _layer_fwd(x, mem, w, consts):
    s = {"x": x}
    rep = consts["rep"]
    s["h1"] = h1 = _rmsnorm_fwd(x, w["norm_mix"], name="norm_mix_fwd")
    s["proj"] = proj = _mm(h1, w["w_main"], tm=2048, name="in_proj")
    s["dtr"] = dtr = _mm(h1, w["w_dt"], name="dt_proj")
    ar, ai, ldt = w["s5_a_re"], w["s5_a_im"], w["s5_log_dt"].reshape(S5_GROUPS, 1)
    br, bi = w["s5_b_re"].reshape(S5_GROUPS, -1), w["s5_b_im"].reshape(S5_GROUPS, -1)
    abar_r, abar_i, bbar_r, bbar_i = _s5_prep(ar, ai, ldt, br, bi, rep)
    s["abar"] = abar = (abar_r.reshape(1, S5_CH), abar_i.reshape(1, S5_CH))
    s["lay"] = lay = _s5_layouts(bbar_r, bbar_i, w["s5_c_re"], w["s5_c_im"])
    s["u"] = u = _interleave_rows(proj[:, :D_S5])
    s["sr"], s["si"], s["ypre"], s["yg"] = _s5_fwd(u, *abar, lay["bre"], lay["bim"], lay["cre"], lay["cim"],
                                                   w["s5_d"].reshape(1, D_S5))
    ypre, yg = s["ypre"], s["yg"]
    s["gp"], out_s5 = _mm(yg, w["s5_w_glu"], extras=[ypre], epi=lambda acc, yp: (acc, _gelu(yp) * _sigmoid(acc)),
                          out_dtypes=(F32, MXU), name="s5_glu")
    out_s5 = _deinterleave_rows(out_s5)
    s["par"] = par = _head_rows(w["ssd_dt_bias"], w["ssd_a_log"], w["ssd_d"])
    out_ssd, s["y_ssd"], s["states"], s["conv_pre"] = _ssd_fwd(proj, w["ssd_conv_w"], w["ssd_conv_b"], dtr, par,
                                                               w["ssd_norm"], consts["ssd"])
    s["ycat"] = ycat = jnp.concatenate([out_s5, out_ssd], axis=1)
    s["x1"] = x1 = _mm(ycat, w["w_out"], extras=[x], epi=_add, name="out_proj")
    s["hq"] = hq = _rmsnorm_fwd(x1, w["norm_xattn"], name="norm_xattn_fwd")
    s["mn"] = mn = _rmsnorm_fwd(mem, w["norm_mem"], name="norm_mem_fwd")
    s["q"] = q = _mm(hq, w["xa_wq"], out_dtypes=(MXU,), name="xa_q")
    s["k"] = k = _mm(mn, w["xa_wk"], out_dtypes=(MXU,), name="xa_k")
    s["v"] = v = _mm(mn, w["xa_wv"], out_dtypes=(MXU,), name="xa_v")
    s["o"] = o = _attn_fwd(q, k, v)
    s["x2"] = x2 = _mm(o, w["xa_wo"], extras=[x1], epi=_add, name="xa_o")
    s["hm"] = hm = _rmsnorm_fwd(x2, w["norm_mlp"], name="norm_mlp_fwd")
    s["act"] = _mm(hm, w["mlp_w1"], epi=lambda acc: (jnp.square(jnp.maximum(acc, 0.0)),), out_dtypes=(MXU,),
                   tm=2048, name="mlp_up")
    x3 = _mm(s["act"], w["mlp_w2"], extras=[x2], epi=_add, name="mlp_down")
    return x3, s


def _layer_bwd(dx3, dx3m, mem, w, s, consts):
    g = {}
    rep = consts["rep"]
    wire = (MXU,)
    d_a = _mm(dx3m, w["mlp_w2"], tb=True, extras=[s["act"]],
              epi=lambda acc, act: (acc * (2.0 * jnp.sqrt(act.astype(F32))),), out_dtypes=(MXU,), tm=2048,
              name="mlp_down_dx")
    g["mlp_w2"] = _mm(s["act"], dx3m, ta=True, out_dtypes=wire, name="mlp_down_dw")
    g["mlp_w1"] = _mm(s["hm"], d_a, ta=True, out_dtypes=wire, name="mlp_up_dw")
    norm_bwd = dict(epi=_norm_bwd_epilogue, out_dtypes=(F32, MXU), n_colsum=1, tm=512)
    dx2, dx2m, g["norm_mlp"] = _mm(d_a, w["mlp_w1"], tb=True, extras=[s["x2"], dx3], vecs=[w["norm_mlp"].reshape(1, -1)],
                                   name="mlp_up_dx", **norm_bwd)
    d_o = _mm(dx2m, w["xa_wo"], tb=True, out_dtypes=(MXU,), name="xa_o_dx")
    g["xa_wo"] = _mm(s["o"], dx2m, ta=True, out_dtypes=wire, name="xa_o_dw")
    dq, dk, dv = _attn_bwd(s["q"], s["k"], s["v"], d_o)
    g["xa_wq"] = _mm(s["hq"], dq, ta=True, out_dtypes=wire, name="xa_q_dw")
    dx1, dx1m, g["norm_xattn"] = _mm(dq, w["xa_wq"], tb=True, extras=[s["x1"], dx2],
                                     vecs=[w["norm_xattn"].reshape(1, -1)], name="xa_q_dx", **norm_bwd)
    g["xa_wk"] = _mm(s["mn"], dk, ta=True, out_dtypes=wire, name="xa_k_dw")
    g["xa_wv"] = _mm(s["mn"], dv, ta=True, out_dtypes=wire, name="xa_v_dw")
    d_mn_v = _mm(dv, w["xa_wv"], tb=True, name="xa_v_dx")
    d_mn = _mm(dk, w["xa_wk"], tb=True, extras=[d_mn_v], epi=_add, name="xa_k_dx")
    _, _, g["norm_mem"] = _rmsnorm_bwd(mem, w["norm_mem"], d_mn, None, name="norm_mem_bwd")
    d_ycat = _mm(dx1m, w["w_out"], tb=True, name="out_proj_dx")
    g["w_out"] = _mm(s["ycat"], dx1m, ta=True, out_dtypes=wire, name="out_proj_dw")
    lay, proj, ypre, u = s["lay"], s["proj"], s["ypre"], s["u"]
    d_os5 = _interleave_rows(d_ycat[:, :D_S5])
    d_gp = _s5_gate_bwd(d_os5, s["gp"], ypre)
    g["s5_w_glu"] = _mm(s["yg"], d_gp, ta=True, out_dtypes=wire, name="s5_glu_dw")
    d_ypre = _mm(d_gp, w["s5_w_glu"], tb=True, extras=[d_os5, s["gp"], ypre],
                 epi=lambda acc, do, gp, yp: ((acc + do * _sigmoid(gp)) * _gelu_grad(yp),), name="s5_glu_dx")
    du, d_abar_r, d_abar_i, db_re, db_im, dc_re, dc_im, d_d = _s5_bwd(
        d_ypre, u, s["sr"], s["si"], *s["abar"], lay["cre_t"], lay["cim_t"], lay["bre_t"], lay["bim_t"],
        w["s5_d"].reshape(1, D_S5))
    d_bbar_r, d_bbar_i = _b_diag(db_re), _b_diag(db_im)
    g["s5_c_re"], g["s5_c_im"] = _c_diag(dc_re), -_c_diag(dc_im)
    g["s5_d"] = d_d.reshape(S5_GROUPS, S5_GROUP)
    du = _deinterleave_rows(du)
    ar, ai, ldt = w["s5_a_re"], w["s5_a_im"], w["s5_log_dt"].reshape(S5_GROUPS, 1)
    br, bi = w["s5_b_re"].reshape(S5_GROUPS, -1), w["s5_b_im"].reshape(S5_GROUPS, -1)
    d_ar, d_ai, d_ldt, d_br, d_bi = _s5_prep_bwd(
        ar, ai, ldt, br, bi, rep, d_abar_r.reshape(S5_GROUPS, S5_STATE), d_abar_i.reshape(S5_GROUPS, S5_STATE),
        d_bbar_r, d_bbar_i)
    g["s5_a_re"], g["s5_a_im"], g["s5_log_dt"] = d_ar, d_ai, d_ldt.reshape(S5_GROUPS)
    g["s5_b_re"] = d_br.reshape(S5_GROUPS, S5_STATE, S5_GROUP)
    g["s5_b_im"] = d_bi.reshape(S5_GROUPS, S5_STATE, S5_GROUP)
    d_xbc, dz, d_dtr, g["ssd_norm"], d_par, g["ssd_conv_w"], g["ssd_conv_b"] = _ssd_bwd(
        proj, s["conv_pre"], w["ssd_conv_w"], s["dtr"], s["par"], w["ssd_norm"], s["y_ssd"], s["states"], d_ycat,
        consts["ssd"])
    g["ssd_dt_bias"], g["ssd_a_log"], g["ssd_d"] = (d_par[i, :SSD_HEADS] for i in range(3))
    d_proj = jnp.concatenate([du, dz, d_xbc], axis=1)
    g_main = _mm(s["h1"], d_proj, ta=True, out_dtypes=wire, name="in_proj_dw")
    g_dt = _mm(s["h1"], d_dtr, ta=True, out_dtypes=wire, name="dt_proj_dw")
    g["w_in"] = jnp.concatenate([g_main, g_dt[:, :SSD_HEADS]], axis=1)
    d_h1_dt = _mm(d_dtr, w["w_dt"], tb=True, name="dt_proj_dx")
    dx, dxm, g["norm_mix"] = _mm(
        d_proj, w["w_main"], tb=True, extras=[d_h1_dt, s["x"], dx1], vecs=[w["norm_mix"].reshape(1, -1)],
        name="in_proj_dx", **{**norm_bwd, "epi": lambda acc, dt_part, *rest: _norm_bwd_epilogue(acc + dt_part, *rest)})
    return dx, dxm, g


LAYER_WEIGHTS = ("norm_mix", "w_in", "s5_a_re", "s5_a_im", "s5_log_dt", "s5_b_re", "s5_b_im", "s5_c_re", "s5_c_im", "s5_d",
                 "s5_w_glu", "ssd_conv_w", "ssd_conv_b", "ssd_dt_bias", "ssd_a_log", "ssd_d", "ssd_norm", "w_out",
                 "norm_xattn", "norm_mem", "xa_wq", "xa_wk", "xa_wv", "xa_wo", "norm_mlp", "mlp_w1", "mlp_w2")
WEIGHTS = LAYER_WEIGHTS + ("norm_final",)


def _local_step(x, mem, target, weights):
    consts = {
        "ssd": _ssd_consts(),
        "rep": (jnp.arange(S5_STATE)[:, None] == jnp.arange(S5_STATE * S5_GROUP)[None, :] // S5_GROUP).astype(F32),
    }
    layers = []
    for l in range(DEPTH):
        w = {n: weights[n][l] for n in LAYER_WEIGHTS}
        w_in = w["w_in"]
        w["w_main"] = w_in[:, :D_MAIN]
        w["w_dt"] = jnp.pad(w_in[:, D_MAIN:], ((0, 0), (0, HP - SSD_HEADS)))
        layers.append(w)
    saved = []
    for l in range(DEPTH):
        x, s = _layer_fwd(x, mem, layers[l], consts)
        saved.append(s)
    loss, dx, dxm, g_final = _loss_head(x, weights["norm_final"], target)
    grads = [None] * DEPTH
    for l in reversed(range(DEPTH)):
        dx, dxm, grads[l] = _layer_bwd(dx, dxm, mem, layers[l], saved[l], consts)
    out = {n: jnp.stack([grads[l][n].reshape(weights[n].shape[1:]) for l in range(DEPTH)]) for n in LAYER_WEIGHTS}
    out["norm_final"] = g_final.reshape(weights["norm_final"].shape)
    return loss, dx, out


SHARDED = {"w_in": 2, "s5_w_glu": 1, "ssd_conv_w": 2, "w_out": 1, "xa_wq": 1, "xa_wk": 1, "xa_wv": 1, "xa_wo": 1,
           "mlp_w1": 2, "mlp_w2": 1}
EXACT = ("ssd_conv_w",)
ROW_EXCHANGE = tuple(n for n, ax in SHARDED.items() if ax == 1)
OWN_EXCHANGE = tuple(n for n in SHARDED if n not in ROW_EXCHANGE)
REPLICATED = tuple(n for n in WEIGHTS if n not in SHARDED)
LANES = 128
PAD_ROWS = 16


def _gather_weights(local):
    def assemble(n, seg):
        shp, ax = local[n].shape, SHARDED[n]
        return jnp.moveaxis(seg, 0, ax).reshape(*shp[:ax], N_DEV * shp[ax], *shp[ax + 1:])

    full = {}
    for n in OWN_EXCHANGE:
        payload = local[n] if n in EXACT else local[n].astype(MXU)
        full[n] = assemble(n, _all_gather(payload, name="gather_" + n))
    got = _all_gather(jnp.concatenate([local[n].astype(MXU) for n in ROW_EXCHANGE], axis=1), name="gather_row_sharded")
    off = 0
    for n in ROW_EXCHANGE:
        rows = local[n].shape[1]
        full[n] = assemble(n, got[:, :, off:off + rows])
        off += rows
    return full


def _scatter_grads(grads, local_shapes):
    def shards(n):
        shp, ax = local_shapes[n], SHARDED[n]
        gfull = grads[n].reshape(*shp[:ax], N_DEV, shp[ax], *shp[ax + 1:])
        return jnp.moveaxis(gfull, ax, 0).astype(MXU)

    out = {}
    for n in OWN_EXCHANGE:
        shp = local_shapes[n]
        out[n] = _reduce_scatter(shards(n).reshape(N_DEV, -1, shp[-1]), name="scatter_" + n).reshape(shp)
    payload = jnp.concatenate([shards(n) for n in ROW_EXCHANGE], axis=2)
    summed = _reduce_scatter(payload.reshape(N_DEV, -1, payload.shape[-1]), name="scatter_row_sharded")
    summed = summed.reshape(payload.shape[1:])
    off = 0
    for n in ROW_EXCHANGE:
        rows = local_shapes[n][1]
        out[n] = summed[:, off:off + rows]
        off += rows
    return out


def _allreduce_small(loss, grads):
    parts = [loss.reshape(-1)[:1]] + [grads[n].reshape(-1) for n in REPLICATED]
    flat = jnp.concatenate(parts)
    quantum = N_DEV * PAD_ROWS * LANES
    flat = jnp.pad(flat, (0, -flat.shape[0] % quantum))
    mine = _reduce_scatter(flat.reshape(N_DEV, -1, LANES), name="reduce_small_grads")
    summed = _all_gather(mine, name="gather_small_grads").reshape(-1)
    out, off = {}, 1
    for n in REPLICATED:
        size = grads[n].size
        out[n] = summed[off:off + size].reshape(grads[n].shape)
        off += size
    return summed[0], out


def kernel(x, mem, norm_mix, w_in, s5_a_re, s5_a_im, s5_log_dt, s5_b_re, s5_b_im, s5_c_re, s5_c_im, s5_d, s5_w_glu, ssd_conv_w, ssd_conv_b, ssd_dt_bias, ssd_a_log, ssd_d, ssd_norm, w_out, norm_xattn, norm_mem, xa_wq, xa_wk, xa_wv, xa_wo, norm_mlp, mlp_w1, mlp_w2, norm_final, loss_target, m_norm_mix, m_w_in, m_s5_a_re, m_s5_a_im, m_s5_log_dt, m_s5_b_re, m_s5_b_im, m_s5_c_re, m_s5_c_im, m_s5_d, m_s5_w_glu, m_ssd_conv_w, m_ssd_conv_b, m_ssd_dt_bias, m_ssd_a_log, m_ssd_d, m_ssd_norm, m_w_out, m_norm_xattn, m_norm_mem, m_xa_wq, m_xa_wk, m_xa_wv, m_xa_wo, m_norm_mlp, m_mlp_w1, m_mlp_w2, m_norm_final, v_norm_mix, v_w_in, v_s5_a_re, v_s5_a_im, v_s5_log_dt, v_s5_b_re, v_s5_b_im, v_s5_c_re, v_s5_c_im, v_s5_d, v_s5_w_glu, v_ssd_conv_w, v_ssd_conv_b, v_ssd_dt_bias, v_ssd_a_log, v_ssd_d, v_ssd_norm, v_w_out, v_norm_xattn, v_norm_mem, v_xa_wq, v_xa_wk, v_xa_wv, v_xa_wo, v_norm_mlp, v_mlp_w1, v_mlp_w2, v_norm_final):
    args = locals()
    local = {n: args[n] for n in WEIGHTS}
    full = dict(local)
    full.update(_gather_weights(local))
    loss, grad_x, grads = _local_step(x[0], mem[0], loss_target[0], full)
    loss, g_small = _allreduce_small(loss, grads)
    g_all = _scatter_grads(grads, {n: local[n].shape for n in SHARDED})
    g_all.update(g_small)
    delta, new_m, new_v = {}, {}, {}
    for n in WEIGHTS:
        delta[n], new_m[n], new_v[n] = _adamw(local[n], g_all[n], args["m_" + n], args["v_" + n], name="adamw_" + n)
    return (loss, grad_x[None], *[g_all[n] for n in WEIGHTS], *[delta[n] for n in WEIGHTS],
            *[new_m[n] for n in WEIGHTS], *[new_v[n] for n in WEIGHTS])
```

```python
import functools
import math

import jax
import jax.numpy as jnp
from jax import lax
from jax.experimental import pallas as pl
from jax.experimental.pallas import tpu as pltpu

F32 = jnp.float32
MXU = jnp.bfloat16
HI = lax.Precision.HIGHEST

D_MODEL = 1024
DEPTH = 4
MEM_LEN = 256
D_S5 = 1024
D_SSD = 1024
S5_GROUP = 16
S5_GROUPS = 64
S5_STATE = 64
S5_CH = S5_GROUPS * S5_STATE
S5_NB = 4
S5_GPB = S5_GROUPS // S5_NB
SSD_HEADDIM = 64
SSD_HEADS = 16
SSD_GROUPS = 4
SSD_STATE = 128
SSD_CONV = 4
SSD_CHUNK = 128
SSD_BC = SSD_GROUPS * SSD_STATE
D_CONV_CH = 2048
D_MAIN = 4096
D_IN_PROJ = D_MAIN + SSD_HEADS
HP = 128
XA_HEADS = 4
XA_HEAD_DIM = 256
D_FF = 4096
EPS = 1e-5
N_DEV = 8
AXES = ("x", "y", "c")

ADAM_LR = 0.001
ADAM_B1 = 0.9
ADAM_B2 = 0.999
ADAM_EPS = 1e-08
ADAM_WD = 0.01
ADAM_STEP = 10

VMEM_LIMIT = 56 * 1024 * 1024


def _params(*sem):
    return pltpu.CompilerParams(dimension_semantics=sem, vmem_limit_bytes=VMEM_LIMIT)


def _tile(n, pref, quantum=128):
    t = (min(pref, n) // quantum) * quantum
    while t >= quantum:
        if n % t == 0:
            return t
        t -= quantum
    return n


def _sds(shape, dtype):
    return jax.ShapeDtypeStruct(tuple(shape), dtype)


def _recip(d):
    r = pl.reciprocal(d, approx=True)
    return r * (2.0 - d * r)


def _sigmoid(x):
    return _recip(1.0 + jnp.exp(-jnp.maximum(x, -80.0)))


def _silu(x):
    return x * _sigmoid(x)


def _silu_grad(x, s):
    return s * (1.0 + x * (1.0 - s))


_GELU_C = math.sqrt(2.0 / math.pi)


def _gelu(x):
    return 0.5 * x * (1.0 + jnp.tanh(_GELU_C * (x + 0.044715 * x * x * x)))


def _gelu_grad(x):
    th = jnp.tanh(_GELU_C * (x + 0.044715 * x * x * x))
    return 0.5 * (1.0 + th) + 0.5 * x * (1.0 - th * th) * _GELU_C * (1.0 + 3.0 * 0.044715 * x * x)


def _softplus(x):
    return jnp.maximum(x, 0.0) + jnp.log(1.0 + jnp.exp(-jnp.abs(x)))


_NN = (((1,), (0,)), ((), ()))
_NT = (((1,), (1,)), ((), ()))
_TN = (((0,), (0,)), ((), ()))


def _dot(a, b, dims=_NN, precision=None):
    return lax.dot_general(a, b, dims, precision=precision, preferred_element_type=F32)


def _dot_split(x, w):
    hi = x.astype(MXU)
    lo = (x - hi.astype(F32)).astype(MXU)
    return _dot(hi, w) + _dot(lo, w)


def _rows8(v):
    return jnp.broadcast_to(v, (8, v.shape[1]))


def _mm(a, b, *, ta=False, tb=False, extras=(), vecs=(), epi=None, out_dtypes=(F32,), n_colsum=0, tm=1024, tn=1024,
        tk=1024, name):
    m, k = (a.shape[1], a.shape[0]) if ta else a.shape
    n = b.shape[0] if tb else b.shape[1]
    assert k == (b.shape[1] if tb else b.shape[0]), (a.shape, b.shape, ta, tb)
    if ta:
        tk = 4 * tk
    tm, tn, tk = _tile(m, tm), _tile(n, tn), _tile(k, tk)
    nk = k // tk
    n_ex, n_out = len(extras) + len(vecs), len(out_dtypes)
    assert n_colsum == 0 or tn == n
    dims = (((0,) if ta else (1,), (1,) if tb else (0,)), ((), ()))

    def body(a_ref, b_ref, *rest):
        ex_refs, out_refs = rest[:n_ex], rest[n_ex:n_ex + n_out]
        sum_refs = rest[n_ex + n_out:n_ex + n_out + n_colsum]
        prod = _dot(a_ref[...].astype(MXU), b_ref[...].astype(MXU), dims)

        def finish(total):
            outs = epi(total, *[e[...] for e in ex_refs]) if epi is not None else (total,)
            for o, r in zip(outs[:n_out], out_refs, strict=True):
                r[...] = o.astype(r.dtype)
            for o, r in zip(outs[n_out:], sum_refs, strict=True):
                @pl.when(pl.program_id(0) == 0)
                def _(o=o, r=r):
                    r[...] = o

                @pl.when(pl.program_id(0) > 0)
                def _(o=o, r=r):
                    r[...] += o

        if nk == 1:
            finish(prod)
            return
        acc = rest[n_ex + n_out + n_colsum]
        kk = pl.program_id(2)

        @pl.when(kk == 0)
        def _():
            acc[...] = prod

        @pl.when(jnp.logical_and(kk > 0, kk < nk - 1))
        def _():
            acc[...] += prod

        @pl.when(kk == nk - 1)
        def _():
            finish(acc[...] + prod)

    a_spec = pl.BlockSpec((tk, tm), lambda i, j, kk: (kk, i)) if ta else pl.BlockSpec((tm, tk), lambda i, j, kk: (i, kk))
    b_spec = pl.BlockSpec((tn, tk), lambda i, j, kk: (j, kk)) if tb else pl.BlockSpec((tk, tn), lambda i, j, kk: (kk, j))
    mn_spec = pl.BlockSpec((tm, tn), lambda i, j, kk: (i, j))
    n_spec = pl.BlockSpec((1, tn), lambda i, j, kk: (0, j))
    outs = pl.pallas_call(
        body,
        name=name,
        grid=(m // tm, n // tn, nk),
        in_specs=[a_spec, b_spec] + [mn_spec] * len(extras) + [n_spec] * len(vecs),
        out_specs=[mn_spec] * n_out + [n_spec] * n_colsum,
        out_shape=[_sds((m, n), dt) for dt in out_dtypes] + [_sds((1, n), F32)] * n_colsum,
        scratch_shapes=[pltpu.VMEM((tm, tn), F32)] if nk > 1 else [],
        compiler_params=_params(*(("arbitrary",) * 3 if n_colsum else ("parallel", "parallel", "arbitrary"))),
    )(a, b, *extras, *vecs)
    return outs[0] if len(outs) == 1 else outs


def _row(tb, w, cb=0):
    return pl.BlockSpec((tb, w), lambda i: (i, cb))


def _const(shape):
    return pl.BlockSpec(shape, lambda i: (0,) * len(shape))


def _rmsnorm_fwd(x, g, *, name):
    t, d = x.shape
    tb = _tile(t, 512, 8)

    def body(x_ref, g_ref, h_ref):
        xv = x_ref[...]
        r = lax.rsqrt(jnp.mean(xv * xv, axis=-1, keepdims=True) + EPS)
        h_ref[...] = (xv * r * g_ref[...]).astype(h_ref.dtype)

    return pl.pallas_call(
        body, name=name, grid=(t // tb,), in_specs=[_row(tb, d), _const((1, d))], out_specs=_row(tb, d),
        out_shape=_sds((t, d), MXU), compiler_params=_params("parallel"),
    )(x, g.reshape(1, d))


def _rmsnorm_bwd(x, g, dh, dres, *, name):
    t, d = x.shape
    tb = _tile(t, 256, 8)
    has_res = dres is not None

    def body(x_ref, g_ref, dh_ref, *rest):
        dx_ref, dxm_ref, dg_ref = rest[-3:]

        @pl.when(pl.program_id(0) == 0)
        def _():
            dg_ref[...] = jnp.zeros_like(dg_ref)

        xv = x_ref[...]
        r = lax.rsqrt(jnp.mean(xv * xv, axis=-1, keepdims=True) + EPS)
        xh = xv * r
        dhv = dh_ref[...].astype(F32)
        dg_ref[...] += jnp.sum(dhv * xh, axis=0, keepdims=True)
        dxh = dhv * g_ref[...]
        dx = r * (dxh - xh * jnp.mean(dxh * xh, axis=-1, keepdims=True))
        if has_res:
            dx = dx + rest[0][...]
        dx_ref[...] = dx
        dxm_ref[...] = dx.astype(dxm_ref.dtype)

    ins = [x, g.reshape(1, d), dh] + ([dres] if has_res else [])
    return pl.pallas_call(
        body, name=name, grid=(t // tb,),
        in_specs=[_row(tb, d), _const((1, d)), _row(tb, d)] + ([_row(tb, d)] if has_res else []),
        out_specs=[_row(tb, d), _row(tb, d), _const((1, d))],
        out_shape=[_sds((t, d), F32), _sds((t, d), MXU), _sds((1, d), F32)],
        compiler_params=_params("arbitrary"),
    )(*ins)


def _norm_bwd_epilogue(dh, x, dres, g):
    r = lax.rsqrt(jnp.mean(x * x, axis=-1, keepdims=True) + EPS)
    xh = x * r
    dxh = dh * g
    dx = r * (dxh - xh * jnp.mean(dxh * xh, axis=-1, keepdims=True)) + dres
    return dx, dx, jnp.sum(dh * xh, axis=0, keepdims=True)


def _loss_head(x, g, target):
    t, d = x.shape
    tb = _tile(t, 256, 8)

    def body(x_ref, g_ref, tg_ref, loss_ref, dx_ref, dxm_ref, dg_ref):
        @pl.when(pl.program_id(0) == 0)
        def _():
            dg_ref[...] = jnp.zeros_like(dg_ref)
            loss_ref[...] = jnp.zeros_like(loss_ref)

        xv, gv = x_ref[...], g_ref[...]
        r = lax.rsqrt(jnp.mean(xv * xv, axis=-1, keepdims=True) + EPS)
        xh = xv * r
        err = xh * gv - tg_ref[...]
        loss_ref[...] += 0.5 * jnp.sum(jnp.mean(err * err, axis=-1, keepdims=True), axis=0, keepdims=True)
        dy = err * (1.0 / d)
        dg_ref[...] += jnp.sum(dy * xh, axis=0, keepdims=True)
        dxh = dy * gv
        dx = r * (dxh - xh * jnp.mean(dxh * xh, axis=-1, keepdims=True))
        dx_ref[...] = dx
        dxm_ref[...] = dx.astype(dxm_ref.dtype)

    return pl.pallas_call(
        body, name="loss_head", grid=(t // tb,),
        in_specs=[_row(tb, d), _const((1, d)), _row(tb, d)],
        out_specs=[_const((1, HP)), _row(tb, d), _row(tb, d), _const((1, d))],
        out_shape=[_sds((1, HP), F32), _sds((t, d), F32), _sds((t, d), MXU), _sds((1, d), F32)],
        compiler_params=_params("arbitrary"),
    )(x, g.reshape(1, d), target)


def _s5_discretise(ar, ai, ldt, br, bi, rep):
    dt = jnp.exp(ldt)
    mag = jnp.exp(dt * ar)
    abar_r, abar_i = mag * jnp.cos(dt * ai), mag * jnp.sin(dt * ai)
    den = ar * ar + ai * ai
    zr, zi = abar_r - 1.0, abar_i
    fr = (zr * ar + zi * ai) / den
    fi = (zi * ar - zr * ai) / den
    fr_e, fi_e = _dot(fr, rep, precision=HI), _dot(fi, rep, precision=HI)
    return abar_r, abar_i, fr_e * br - fi_e * bi, fr_e * bi + fi_e * br


def _s5_prep(ar, ai, ldt, br, bi, rep):
    g, p = ar.shape
    ph = br.shape[1]

    def body(ar_ref, ai_ref, ldt_ref, br_ref, bi_ref, rep_ref, o0, o1, o2, o3):
        outs = _s5_discretise(ar_ref[...], ai_ref[...], ldt_ref[...], br_ref[...], bi_ref[...], rep_ref[...])
        for o, v in zip((o0, o1, o2, o3), outs):
            o[...] = v

    return pl.pallas_call(
        body, name="s5_prep",
        out_shape=[_sds((g, p), F32), _sds((g, p), F32), _sds((g, ph), F32), _sds((g, ph), F32)],
        compiler_params=pltpu.CompilerParams(vmem_limit_bytes=VMEM_LIMIT),
    )(ar, ai, ldt, br, bi, rep)


def _s5_prep_bwd(ar, ai, ldt, br, bi, rep, d_abar_r, d_abar_i, d_bbar_r, d_bbar_i):
    g, p = ar.shape
    ph = br.shape[1]

    def body(ar_ref, ai_ref, ldt_ref, br_ref, bi_ref, rep_ref, c0, c1, c2, c3, o0, o1, o2, o3, o4):
        rep_v = rep_ref[...]
        _, vjp = jax.vjp(lambda a, b, c, d, e: _s5_discretise(a, b, c, d, e, rep_v),
                         ar_ref[...], ai_ref[...], ldt_ref[...], br_ref[...], bi_ref[...])
        grads = vjp((c0[...], c1[...], c2[...], c3[...]))
        for o, v in zip((o0, o1, o2, o3, o4), grads):
            o[...] = v

    return pl.pallas_call(
        body, name="s5_prep_bwd",
        out_shape=[_sds((g, p), F32), _sds((g, p), F32), _sds((g, 1), F32), _sds((g, ph), F32), _sds((g, ph), F32)],
        compiler_params=pltpu.CompilerParams(vmem_limit_bytes=VMEM_LIMIT),
    )(ar, ai, ldt, br, bi, rep, d_abar_r, d_abar_i, d_bbar_r, d_bbar_i)


SCAN_ROWS = 1024


def _scan_rows(t):
    return _tile(t, SCAN_ROWS, 64)


def _interleave_rows(x):
    t, c = x.shape
    tb = _scan_rows(t)
    return x.reshape(t // tb, 8, tb // 8, c).swapaxes(1, 2).reshape(t, c)


def _deinterleave_rows(x):
    t, c = x.shape
    tb = _scan_rows(t)
    return x.reshape(t // tb, tb // 8, 8, c).swapaxes(1, 2).reshape(t, c)


def _cmul(ar, ai, br, bi):
    return ar * br - ai * bi, ar * bi + ai * br


def _segment_carries(fr, fi, ar8, ai8, c_r, c_i, seg, reverse):
    pr, pi = ar8, ai8
    for _ in range(int(math.log2(seg))):
        pr, pi = _cmul(pr, pi, pr, pi)
    row = lax.broadcasted_iota(jnp.int32, fr.shape, 0)
    edge = 7 if reverse else 0
    qr, qi = _cmul(pr, pi, c_r, c_i)
    xr, xi = jnp.where(row == edge, fr + qr, fr), jnp.where(row == edge, fi + qi, fi)
    for sh in (1, 2, 4):
        if reverse:
            keep, amount = row < 8 - sh, 8 - sh
        else:
            keep, amount = row >= sh, sh
        qr, qi = jnp.where(keep, pltpu.roll(xr, amount, 0), 0.0), jnp.where(keep, pltpu.roll(xi, amount, 0), 0.0)
        tr, ti = _cmul(pr, pi, qr, qi)
        xr, xi = xr + tr, xi + ti
        pr, pi = _cmul(pr, pi, pr, pi)
    if reverse:
        in_r, in_i = jnp.where(row == 7, c_r, pltpu.roll(xr, 7, 0)), jnp.where(row == 7, c_i, pltpu.roll(xi, 7, 0))
        return in_r, in_i, xr[0:1, :], xi[0:1, :]
    in_r, in_i = jnp.where(row == 0, c_r, pltpu.roll(xr, 1, 0)), jnp.where(row == 0, c_i, pltpu.roll(xi, 1, 0))
    return in_r, in_i, xr[7:8, :], xi[7:8, :]


def _sweeps(ar8, ai8, dr, di, cr, ci, seg, reverse, emit):
    order = range(seg - 1, -1, -1) if reverse else range(seg)
    rows = lambda j: slice(j * 8, (j + 1) * 8)
    fr, fi = jnp.zeros(ar8.shape, F32), jnp.zeros(ar8.shape, F32)
    for j in order:
        tr, ti = _cmul(ar8, ai8, fr, fi)
        fr, fi = tr + dr[rows(j), :], ti + di[rows(j), :]
    s_r, s_i, out_r, out_i = _segment_carries(fr, fi, ar8, ai8, cr[...], ci[...], seg, reverse)
    cr[...] = out_r
    ci[...] = out_i
    for j in order:
        tr, ti = _cmul(ar8, ai8, s_r, s_i)
        s_r, s_i = tr + dr[rows(j), :], ti + di[rows(j), :]
        emit(j, s_r, s_i)


S5_BK, S5_BN = 256, 1024


def _s5_specs(tb, nt, reverse):
    t_of = (lambda s: nt - 1 - s) if reverse else (lambda s: s)
    return dict(
        small=pl.BlockSpec((tb, S5_BK), lambda c, s: (t_of(s), c)),
        wide=pl.BlockSpec((tb, S5_BN), lambda c, s: (t_of(s), c)),
        halo=pl.BlockSpec((8, S5_BN), lambda c, s: (jnp.maximum(t_of(s) * (tb // 8) - 1, 0), c)),
        vec_w=pl.BlockSpec((1, S5_BN), lambda c, s: (0, c)),
        vec_s=pl.BlockSpec((1, S5_BK), lambda c, s: (0, c)),
        w_in=pl.BlockSpec((None, S5_BK, S5_BN), lambda c, s: (c, 0, 0)),
        w_out=pl.BlockSpec((None, S5_BN, S5_BK), lambda c, s: (c, 0, 0)),
    )


def _s5_fwd(u, ar, ai, bre, bim, cre, cim, d_vec):
    t = u.shape[0]
    tb = _scan_rows(t)
    nt, seg, nb = t // tb, tb // 8, bre.shape[0]
    assert 1 << int(math.log2(seg)) == seg

    def body(u_ref, ar_ref, ai_ref, bre_ref, bim_ref, cre_ref, cim_ref, d_ref, sr_ref, si_ref, yp_ref, yg_ref,
             cr, ci, dr_s, di_s):
        @pl.when(pl.program_id(1) == 0)
        def _():
            cr[...] = jnp.zeros_like(cr)
            ci[...] = jnp.zeros_like(ci)

        uv = u_ref[...]
        ub = uv.astype(MXU)
        dr_s[...] = _dot(ub, bre_ref[...])
        di_s[...] = _dot(ub, bim_ref[...])
        ar8 = jnp.broadcast_to(ar_ref[...], (8, S5_BN))
        ai8 = jnp.broadcast_to(ai_ref[...], (8, S5_BN))

        def emit(j, s_r, s_i):
            sr_ref[j * 8:(j + 1) * 8, :] = s_r
            si_ref[j * 8:(j + 1) * 8, :] = s_i

        _sweeps(ar8, ai8, dr_s, di_s, cr, ci, seg, False, emit)
        yp = _dot(sr_ref[...].astype(MXU), cre_ref[...]) - _dot(si_ref[...].astype(MXU), cim_ref[...]) + d_ref[...] * uv
        yp_ref[...] = yp
        yg_ref[...] = _gelu(yp).astype(yg_ref.dtype)

    sp = _s5_specs(tb, nt, False)
    return pl.pallas_call(
        body, name="s5_fwd", grid=(nb, nt),
        in_specs=[sp["small"], sp["vec_w"], sp["vec_w"], sp["w_in"], sp["w_in"], sp["w_out"], sp["w_out"], sp["vec_s"]],
        out_specs=[sp["wide"], sp["wide"], sp["small"], sp["small"]],
        out_shape=[_sds((t, nb * S5_BN), F32), _sds((t, nb * S5_BN), F32), _sds((t, nb * S5_BK), F32),
                   _sds((t, nb * S5_BK), MXU)],
        scratch_shapes=[pltpu.VMEM((1, S5_BN), F32), pltpu.VMEM((1, S5_BN), F32), pltpu.VMEM((tb, S5_BN), F32),
                        pltpu.VMEM((tb, S5_BN), F32)],
        compiler_params=_params("parallel", "arbitrary"),
    )(u, ar, ai, bre, bim, cre, cim, d_vec)


def _s5_bwd(dyp, u, sr, si, ar, ai, cre_t, cim_t, bre_t, bim_t, d_vec):
    t = u.shape[0]
    tb = _scan_rows(t)
    nt, seg, nb = t // tb, tb // 8, cre_t.shape[0]

    def body(dyp_ref, u_ref, pr_ref, pi_ref, hr_ref, hi_ref, ar_ref, ai_ref, cre_ref, cim_ref, bre_ref, bim_ref, d_ref,
             du_ref, gr_ref, gi_ref, dbr_ref, dbi_ref, dcr_ref, dci_ref, dd_ref, cr, ci, dr_s, di_s, lr_s, li_s):
        step = pl.program_id(1)

        @pl.when(step == 0)
        def _():
            for r in (cr, ci, gr_ref, gi_ref, dbr_ref, dbi_ref, dcr_ref, dci_ref, dd_ref):
                r[...] = jnp.zeros_like(r)

        dyv, uv = dyp_ref[...], u_ref[...]
        dyb, ub = dyv.astype(MXU), uv.astype(MXU)
        dr_s[...] = _dot(dyb, cre_ref[...])
        di_s[...] = -_dot(dyb, cim_ref[...])
        ar8 = jnp.broadcast_to(ar_ref[...], (8, S5_BN))
        ai8 = jnp.broadcast_to(-ai_ref[...], (8, S5_BN))
        first_block = step == nt - 1
        row = lax.broadcasted_iota(jnp.int32, (8, S5_BN), 0)
        acc = [jnp.zeros((8, S5_BN), F32), jnp.zeros((8, S5_BN), F32)]

        def emit(j, s_r, s_i):
            lr_s[j * 8:(j + 1) * 8, :] = s_r
            li_s[j * 8:(j + 1) * 8, :] = s_i
            if j > 0:
                p_r, p_i = pr_ref[(j - 1) * 8:j * 8, :], pi_ref[(j - 1) * 8:j * 8, :]
            else:
                halo_r = jnp.where(first_block, 0.0, hr_ref[7:8, :])
                halo_i = jnp.where(first_block, 0.0, hi_ref[7:8, :])
                p_r = jnp.where(row == 0, halo_r, pltpu.roll(pr_ref[(seg - 1) * 8:seg * 8, :], 1, 0))
                p_i = jnp.where(row == 0, halo_i, pltpu.roll(pi_ref[(seg - 1) * 8:seg * 8, :], 1, 0))
            acc[0] = acc[0] + (p_r * s_r + p_i * s_i)
            acc[1] = acc[1] + (p_r * s_i - p_i * s_r)

        _sweeps(ar8, ai8, dr_s, di_s, cr, ci, seg, True, emit)
        gr_ref[...] += jnp.sum(acc[0], axis=0, keepdims=True)
        gi_ref[...] += jnp.sum(acc[1], axis=0, keepdims=True)
        lrb, lib = lr_s[...].astype(MXU), li_s[...].astype(MXU)
        du_ref[...] = (_dot(lrb, bre_ref[...]) + _dot(lib, bim_ref[...]) + d_ref[...] * dyv).astype(du_ref.dtype)
        dbr_ref[...] += _dot(ub, lrb, _TN)
        dbi_ref[...] += _dot(ub, lib, _TN)
        dcr_ref[...] += _dot(pr_ref[...].astype(MXU), dyb, _TN)
        dci_ref[...] += _dot(pi_ref[...].astype(MXU), dyb, _TN)
        dd_ref[...] += jnp.sum(dyv * uv, axis=0, keepdims=True)

    sp = _s5_specs(tb, nt, True)
    return pl.pallas_call(
        body, name="s5_bwd", grid=(nb, nt),
        in_specs=[sp["small"], sp["small"], sp["wide"], sp["wide"], sp["halo"], sp["halo"], sp["vec_w"], sp["vec_w"],
                  sp["w_in"], sp["w_in"], sp["w_out"], sp["w_out"], sp["vec_s"]],
        out_specs=[sp["small"], sp["vec_w"], sp["vec_w"], sp["w_in"], sp["w_in"], sp["w_out"], sp["w_out"], sp["vec_s"]],
        out_shape=[_sds((t, nb * S5_BK), MXU), _sds((1, nb * S5_BN), F32), _sds((1, nb * S5_BN), F32),
                   _sds((nb, S5_BK, S5_BN), F32), _sds((nb, S5_BK, S5_BN), F32), _sds((nb, S5_BN, S5_BK), F32),
                   _sds((nb, S5_BN, S5_BK), F32), _sds((1, nb * S5_BK), F32)],
        scratch_shapes=[pltpu.VMEM((1, S5_BN), F32), pltpu.VMEM((1, S5_BN), F32)] + [pltpu.VMEM((tb, S5_BN), F32)] * 4,
        compiler_params=_params("parallel", "arbitrary"),
    )(dyp, u, sr, si, sr, si, ar, ai, cre_t, cim_t, bre_t, bim_t, d_vec)


def _s5_gate_bwd(d_ycat, gp, ypre):
    t, d = gp.shape
    tb = _tile(t, 256, 8)

    def body(do_ref, gp_ref, yp_ref, o_ref):
        sg = _sigmoid(gp_ref[...])
        o_ref[...] = (do_ref[...] * _gelu(yp_ref[...]) * sg * (1.0 - sg)).astype(o_ref.dtype)

    return pl.pallas_call(
        body, name="s5_gate_bwd", grid=(t // tb,), in_specs=[_row(tb, d), _row(tb, d), _row(tb, d)],
        out_specs=_row(tb, d), out_shape=_sds((t, d), MXU), compiler_params=_params("parallel"),
    )(d_ycat, gp, ypre)


def _ssd_consts():
    head = jnp.arange(HP)[:, None]
    lane = jnp.arange(D_SSD)[None, :]
    expand = ((lane // SSD_HEADDIM) == head).astype(MXU)
    ll = jnp.arange(SSD_CHUNK)
    tri = (ll[:, None] >= ll[None, :]).astype(F32)
    return expand, expand.T, tri, jnp.eye(HP, dtype=F32)


def _ssd_chunk_terms(cp, dtr, par, expand, tri):
    ln = SSD_CHUNK
    sig_cp = _sigmoid(cp)
    xbc = cp * sig_cp
    xs, bm, cm = xbc[:, :D_SSD], xbc[:, D_SSD:D_SSD + SSD_BC], xbc[:, D_SSD + SSD_BC:]
    dt = _softplus(dtr + par[0:1, :])
    a = -jnp.exp(par[1:2, :])
    da = dt * a
    acum = _dot(tri, da, precision=HI)
    acum_t = _dot(da, tri, (((0,), (1,)), ((), ())), precision=HI)
    atot = acum[ln - 1:ln, :]
    dt_e = _dot_split(dt, expand)
    eac_e = _dot_split(jnp.exp(acum), expand)
    dec_e = _dot_split(jnp.exp(atot - acum), expand)
    eat_e = _dot_split(_rows8(jnp.exp(atot)), expand)[0:1, :]
    dsk_e = _dot_split(_rows8(par[2:3, :]), expand)[0:1, :]
    return dict(sig_cp=sig_cp, xs=xs, bm=bm, cm=cm, dt=dt, a=a, acum=acum, acum_t=acum_t, dt_e=dt_e, eac_e=eac_e,
                dec_e=dec_e, eat_e=eat_e, dsk_e=dsk_e)


def _decay_matrix(acum, acum_t, h, mask):
    diff = acum[:, h:h + 1] - acum_t[h:h + 1, :]
    return jnp.where(mask, jnp.exp(jnp.minimum(diff, 0.0)), 0.0)


def _ssd_fwd(proj, conv_w, conv_b, dtr, par, gnorm, consts):
    t = proj.shape[0]
    ln = SSD_CHUNK
    nc = t // ln
    expand, _, tri, _ = consts
    hd2 = 2 * SSD_HEADDIM

    def body(cur_ref, prev_ref, cw_ref, cb_ref, z_ref, dtr_ref, par_ref, g_ref, e_ref, tri_ref,
             out_ref, y_ref, st_ref, cp_ref, state, ext):
        first = pl.program_id(0) == 0

        @pl.when(first)
        def _():
            state[...] = jnp.zeros_like(state)

        st_ref[...] = state[...]
        ext[0:8, :] = jnp.where(first, 0.0, prev_ref[...])
        ext[8:ln + 8, :] = cur_ref[...]
        cpv = jnp.broadcast_to(cb_ref[...], (ln, D_CONV_CH))
        for j in range(SSD_CONV):
            cpv = cpv + cw_ref[SSD_CONV - 1 - j:SSD_CONV - j, :] * ext[8 - j:8 - j + ln, :]
        cp_ref[...] = cpv
        c = _ssd_chunk_terms(cpv, dtr_ref[...], par_ref[...], e_ref[...], tri_ref[...])
        xdt = c["xs"] * c["dt_e"]
        xb, xd = xdt.astype(MXU), (xdt * c["dec_e"]).astype(MXU)
        bb, cb = c["bm"].astype(MXU), c["cm"].astype(MXU)
        mask = lax.broadcasted_iota(jnp.int32, (ln, ln), 0) >= lax.broadcasted_iota(jnp.int32, (ln, ln), 1)
        left = lax.broadcasted_iota(jnp.int32, (ln, hd2), 1) < SSD_HEADDIM
        for g in range(SSD_GROUPS):
            nsl = slice(g * SSD_STATE, (g + 1) * SSD_STATE)
            gsl = slice(g * 256, (g + 1) * 256)
            bg, cg = bb[:, nsl], cb[:, nsl]
            cbm = _dot(cg, bg, _NT)
            st_g = state[:, gsl]
            for pair in range(2):
                h0 = g * 4 + pair * 2
                psl = slice(h0 * SSD_HEADDIM, (h0 + 2) * SSD_HEADDIM)
                m0 = (cbm * _decay_matrix(c["acum"], c["acum_t"], h0, mask)).astype(MXU)
                m1 = (cbm * _decay_matrix(c["acum"], c["acum_t"], h0 + 1, mask)).astype(MXU)
                y_ref[:, psl] = jnp.where(left, _dot(m0, xb[:, psl]), _dot(m1, xb[:, psl]))
            y_ref[:, gsl] += _dot(cg, st_g.astype(MXU)) * c["eac_e"][:, gsl]
            state[:, gsl] = st_g * c["eat_e"][:, gsl] + _dot(bg, xd[:, gsl], _TN)
        y = y_ref[...] + c["dsk_e"] * c["xs"]
        y_ref[...] = y
        y2 = y * _silu(z_ref[...])
        r = lax.rsqrt(jnp.mean(y2 * y2, axis=-1, keepdims=True) + EPS)
        out_ref[...] = (y2 * r * g_ref[...]).astype(out_ref.dtype)

    xbc_block = (D_MAIN - D_CONV_CH) // D_CONV_CH
    return pl.pallas_call(
        body, name="ssd_fwd", grid=(nc,),
        in_specs=[_row(ln, D_CONV_CH, xbc_block),
                  pl.BlockSpec((8, D_CONV_CH), lambda i: (jnp.maximum(i * (ln // 8) - 1, 0), xbc_block)),
                  _const((SSD_CONV, D_CONV_CH)), _const((1, D_CONV_CH)),
                  _row(ln, D_SSD, 1), _row(ln, HP), _const((8, HP)), _const((1, D_SSD)),
                  _const((HP, D_SSD)), _const((ln, ln))],
        out_specs=[_row(ln, D_SSD), _row(ln, D_SSD), pl.BlockSpec((None, SSD_STATE, D_SSD), lambda i: (i, 0, 0)),
                   _row(ln, D_CONV_CH)],
        out_shape=[_sds((t, D_SSD), MXU), _sds((t, D_SSD), F32), _sds((nc, SSD_STATE, D_SSD), F32),
                   _sds((t, D_CONV_CH), F32)],
        scratch_shapes=[pltpu.VMEM((SSD_STATE, D_SSD), F32), pltpu.VMEM((ln + 8, D_CONV_CH), F32)],
        compiler_params=_params("arbitrary"),
    )(proj, proj, conv_w, conv_b.reshape(1, D_CONV_CH), proj, dtr, par, gnorm.reshape(1, D_SSD), expand, tri)


def _ssd_bwd(proj, conv_pre, conv_w, dtr, par, gnorm, y, states, d_ycat, consts):
    t = proj.shape[0]
    ln = SSD_CHUNK
    nc = t // ln
    expand, expand_t, tri, eye = consts
    hd2 = 2 * SSD_HEADDIM

    def body(cp_ref, z_ref, dtr_ref, par_ref, g_ref, y_ref, st_ref, do_ref, e_ref, et_ref, tri_ref, eye_ref,
             cur_ref, prev_ref, cw_ref,
             dxbc_ref, dz_ref, ddt_ref, dg_ref, dpar_ref, dcw_ref, dcb_ref,
             dstate, dx_buf, lane_buf, tot_buf, colsum, dcp_ref, ext, dext):
        @pl.when(pl.program_id(0) == 0)
        def _():
            dstate[...] = jnp.zeros_like(dstate)
            dg_ref[...] = jnp.zeros_like(dg_ref)
            dpar_ref[...] = jnp.zeros_like(dpar_ref)
            dcw_ref[...] = jnp.zeros_like(dcw_ref)
            dcb_ref[...] = jnp.zeros_like(dcb_ref)
            dext[ln:ln + 8, :] = jnp.zeros((8, D_CONV_CH), F32)

        cpv, et_v, tri_v, par_v = cp_ref[...], et_ref[...], tri_ref[...], par_ref[...]
        c = _ssd_chunk_terms(cpv, dtr_ref[...], par_v, e_ref[...], tri_v)
        xs = c["xs"]
        zv, yv, dov = z_ref[...], y_ref[...], do_ref[...]
        sig_z = _sigmoid(zv)
        sz = zv * sig_z
        y2 = yv * sz
        r = lax.rsqrt(jnp.mean(y2 * y2, axis=-1, keepdims=True) + EPS)
        yh = y2 * r
        dg_ref[...] += jnp.sum(dov * yh, axis=0, keepdims=True)
        dyh = dov * g_ref[...]
        dy2 = r * (dyh - yh * jnp.mean(dyh * yh, axis=-1, keepdims=True))
        dz_ref[...] = (dy2 * yv * _silu_grad(zv, sig_z)).astype(dz_ref.dtype)
        dy = dy2 * sz

        xdt = xs * c["dt_e"]
        xdf = xdt * c["dec_e"]
        xb, xd = xdt.astype(MXU), xdf.astype(MXU)
        bb, cb = c["bm"].astype(MXU), c["cm"].astype(MXU)
        dyb, dye = dy.astype(MXU), (dy * c["eac_e"]).astype(MXU)
        mask = lax.broadcasted_iota(jnp.int32, (ln, ln), 0) >= lax.broadcasted_iota(jnp.int32, (ln, ln), 1)
        left = lax.broadcasted_iota(jnp.int32, (ln, hd2), 1) < SSD_HEADDIM
        lane_hp = lax.broadcasted_iota(jnp.int32, (ln, HP), 1)
        d_acum = jnp.zeros((ln, HP), F32)
        colsum[...] = jnp.zeros_like(colsum)
        tot_buf[...] = jnp.zeros_like(tot_buf)
        for g in range(SSD_GROUPS):
            nsl = slice(g * SSD_STATE, (g + 1) * SSD_STATE)
            gsl = slice(g * 256, (g + 1) * 256)
            bg, cg = bb[:, nsl], cb[:, nsl]
            cbm = _dot(cg, bg, _NT)
            st_g = st_ref[:, gsl]
            dst_g = dstate[:, gsl]
            stb, dstb = st_g.astype(MXU), dst_g.astype(MXU)
            y_off = _dot(cg, stb)
            bds = _dot(bg, dstb)
            dcb = jnp.zeros((ln, ln), F32)
            for pair in range(2):
                h0 = g * 4 + pair * 2
                psl = slice(h0 * SSD_HEADDIM, (h0 + 2) * SSD_HEADDIM)
                xp, dyp = xb[:, psl], dyb[:, psl]
                dxp = []
                for k in range(2):
                    h = h0 + k
                    lm = _decay_matrix(c["acum"], c["acum_t"], h, mask)
                    mm = cbm * lm
                    half = left if k == 0 else jnp.logical_not(left)
                    dm = _dot(jnp.where(half, dyp, jnp.zeros_like(dyp)), xp, _NT)
                    dcb = dcb + dm * lm
                    gm = dm * mm
                    d_acum = d_acum + jnp.where(lane_hp == h, jnp.sum(gm, axis=1, keepdims=True), 0.0)
                    colsum[h:h + 1, :] = jnp.sum(gm, axis=0, keepdims=True)
                    dxp.append(_dot(mm.astype(MXU), dyp, _TN))
                dx_buf[:, psl] = jnp.where(left, dxp[0], dxp[1])
            dx_buf[:, gsl] += bds * c["dec_e"][:, gsl]
            dcbb = dcb.astype(MXU)
            dc_g = _dot(dcbb, bg) + _dot(dye[:, gsl], stb, _NT)
            db_g = _dot(dcbb, cg, _TN) + _dot(xd[:, gsl], dstb, _NT)
            dcp_ref[:, D_SSD + g * SSD_STATE:D_SSD + (g + 1) * SSD_STATE] = db_g
            dcp_ref[:, D_SSD + SSD_BC + g * SSD_STATE:D_SSD + SSD_BC + (g + 1) * SSD_STATE] = dc_g
            dec_term = xdf[:, gsl] * bds
            lane_buf[:, gsl] = dy[:, gsl] * y_off * c["eac_e"][:, gsl] - dec_term
            tot_buf[0:1, gsl] = (jnp.sum(st_g * dst_g, axis=0, keepdims=True) * c["eat_e"][:, gsl]
                                 + jnp.sum(dec_term, axis=0, keepdims=True))
            dstate[:, gsl] = dst_g * c["eat_e"][:, gsl] + _dot(cg, dye[:, gsl], _TN)
        dx_tot = dx_buf[...]
        d_acum = d_acum + _dot_split(lane_buf[...], et_v) - _dot(colsum[...], eye_ref[...], _TN, precision=HI)
        d_atot = _dot_split(tot_buf[...], et_v)[0:1, :]
        row_hp = lax.broadcasted_iota(jnp.int32, (ln, HP), 0)
        d_acum = d_acum + jnp.where(row_hp == ln - 1, d_atot, 0.0)
        d_da = _dot(tri_v, d_acum, _TN, precision=HI)
        d_dt = d_da * c["a"] + _dot_split(dx_tot * xs, et_v)
        d_dtr = d_dt * _sigmoid(dtr_ref[...] + par_v[0:1, :])
        ddt_ref[...] = d_dtr
        dpar_ref[0:1, :] += jnp.sum(d_dtr, axis=0, keepdims=True)
        dpar_ref[1:2, :] += jnp.sum(d_da * c["dt"], axis=0, keepdims=True) * c["a"]
        dpar_ref[2:3, :] += _dot_split(_rows8(jnp.sum(dy * xs, axis=0, keepdims=True)), et_v)[0:1, :]
        dcp_ref[:, 0:D_SSD] = dx_tot * c["dt_e"] + dy * c["dsk_e"]
        dcv = dcp_ref[...] * _silu_grad(cpv, c["sig_cp"])
        ext[0:8, :] = jnp.where(pl.program_id(0) == nc - 1, 0.0, prev_ref[...])
        ext[8:ln + 8, :] = cur_ref[...]
        dext[0:ln, :] = dcv
        dxbc = jnp.zeros((ln, D_CONV_CH), F32)
        for j in range(SSD_CONV):
            dxbc = dxbc + cw_ref[SSD_CONV - 1 - j:SSD_CONV - j, :] * dext[j:j + ln, :]
            dcw_ref[SSD_CONV - 1 - j:SSD_CONV - j, :] += jnp.sum(dcv * ext[8 - j:8 - j + ln, :], axis=0, keepdims=True)
        dxbc_ref[...] = dxbc.astype(dxbc_ref.dtype)
        dcb_ref[...] += jnp.sum(dcv, axis=0, keepdims=True)
        dext[ln:ln + 8, :] = dcv[0:8, :]

    xbc_block = (D_MAIN - D_CONV_CH) // D_CONV_CH
    rev = lambda i: (nc - 1 - i, 0)
    rev1 = lambda i: (nc - 1 - i, 1)
    return pl.pallas_call(
        body, name="ssd_bwd", grid=(nc,),
        in_specs=[pl.BlockSpec((ln, D_CONV_CH), rev), pl.BlockSpec((ln, D_SSD), rev1), pl.BlockSpec((ln, HP), rev),
                  _const((8, HP)), _const((1, D_SSD)), pl.BlockSpec((ln, D_SSD), rev),
                  pl.BlockSpec((None, SSD_STATE, D_SSD), lambda i: (nc - 1 - i, 0, 0)),
                  pl.BlockSpec((ln, D_SSD), rev1),
                  _const((HP, D_SSD)), _const((D_SSD, HP)), _const((ln, ln)), _const((HP, HP)),
                  pl.BlockSpec((ln, D_CONV_CH), lambda i: (nc - 1 - i, xbc_block)),
                  pl.BlockSpec((8, D_CONV_CH), lambda i: (jnp.maximum((nc - 1 - i) * (ln // 8) - 1, 0), xbc_block)),
                  _const((SSD_CONV, D_CONV_CH))],
        out_specs=[pl.BlockSpec((ln, D_CONV_CH), rev), pl.BlockSpec((ln, D_SSD), rev), pl.BlockSpec((ln, HP), rev),
                   _const((1, D_SSD)), _const((8, HP)), _const((SSD_CONV, D_CONV_CH)), _const((1, D_CONV_CH))],
        out_shape=[_sds((t, D_CONV_CH), MXU), _sds((t, D_SSD), MXU), _sds((t, HP), F32), _sds((1, D_SSD), F32),
                   _sds((8, HP), F32), _sds((SSD_CONV, D_CONV_CH), F32), _sds((1, D_CONV_CH), F32)],
        scratch_shapes=[pltpu.VMEM((SSD_STATE, D_SSD), F32), pltpu.VMEM((ln, D_SSD), F32), pltpu.VMEM((ln, D_SSD), F32),
                        pltpu.VMEM((8, D_SSD), F32), pltpu.VMEM((HP, ln), F32), pltpu.VMEM((ln, D_CONV_CH), F32),
                        pltpu.VMEM((ln + 8, D_CONV_CH), F32), pltpu.VMEM((ln + 8, D_CONV_CH), F32)],
        compiler_params=_params("arbitrary"),
    )(conv_pre, proj, dtr, par, gnorm.reshape(1, D_SSD), y, states, d_ycat, expand, expand_t, tri, eye, proj, proj, conv_w)


def _softmax_rows(s):
    e = jnp.exp(s - jnp.max(s, axis=-1, keepdims=True))
    return e * _recip(jnp.sum(e, axis=-1, keepdims=True))


def _attn_fwd(q, k, v):
    t, d = q.shape
    mlen = k.shape[0]
    tq = _tile(t, 512, 8)
    scale = XA_HEAD_DIM ** -0.5

    def body(q_ref, k_ref, v_ref, o_ref):
        for h in range(XA_HEADS):
            sl = slice(h * XA_HEAD_DIM, (h + 1) * XA_HEAD_DIM)
            p = _softmax_rows(_dot(q_ref[:, sl], k_ref[:, sl], _NT) * scale)
            o_ref[:, sl] = _dot(p.astype(MXU), v_ref[:, sl]).astype(o_ref.dtype)

    return pl.pallas_call(
        body, name="xattn_fwd", grid=(t // tq,),
        in_specs=[_row(tq, d), _const((mlen, d)), _const((mlen, d))], out_specs=_row(tq, d),
        out_shape=_sds((t, d), MXU), compiler_params=_params("parallel"),
    )(q, k, v)


def _attn_bwd(q, k, v, do):
    t, d = q.shape
    mlen = k.shape[0]
    tq = _tile(t, 512, 8)
    scale = XA_HEAD_DIM ** -0.5

    def body(q_ref, k_ref, v_ref, do_ref, dq_ref, dk_ref, dv_ref):
        @pl.when(pl.program_id(0) == 0)
        def _():
            dk_ref[...] = jnp.zeros_like(dk_ref)
            dv_ref[...] = jnp.zeros_like(dv_ref)

        for h in range(XA_HEADS):
            sl = slice(h * XA_HEAD_DIM, (h + 1) * XA_HEAD_DIM)
            qh, kh, vh, doh = q_ref[:, sl], k_ref[:, sl], v_ref[:, sl], do_ref[:, sl]
            p = _softmax_rows(_dot(qh, kh, _NT) * scale)
            dp = _dot(doh, vh, _NT)
            dv_ref[:, sl] += _dot(p.astype(MXU), doh, _TN)
            ds = (p * (dp - jnp.sum(p * dp, axis=-1, keepdims=True)) * scale).astype(MXU)
            dq_ref[:, sl] = _dot(ds, kh).astype(dq_ref.dtype)
            dk_ref[:, sl] += _dot(ds, qh, _TN)

    return pl.pallas_call(
        body, name="xattn_bwd", grid=(t // tq,),
        in_specs=[_row(tq, d), _const((mlen, d)), _const((mlen, d)), _row(tq, d)],
        out_specs=[_row(tq, d), _const((mlen, d)), _const((mlen, d))],
        out_shape=[_sds((t, d), MXU), _sds((mlen, d), F32), _sds((mlen, d), F32)],
        compiler_params=_params("arbitrary"),
    )(q, k, v, do)


def _lane_view(a):
    if a.ndim >= 2 and a.shape[-1] >= 128:
        return a.reshape(-1, a.shape[-1])
    if a.size % 128 == 0:
        return a.reshape(-1, 128)
    return a.reshape(1, -1)


def _adamw(w, g, m, v, *, name):
    shape = w.shape
    w2, g2, m2, v2 = (_lane_view(a) for a in (w, g.reshape(shape), m, v))
    r, c = w2.shape
    tr = _tile(r, max(8, (1 << 18) // max(c, 128)), 8)

    def body(w_ref, g_ref, m_ref, v_ref, d_ref, mo_ref, vo_ref):
        gv = g_ref[...]
        mn = ADAM_B1 * m_ref[...] + (1.0 - ADAM_B1) * gv
        vn = ADAM_B2 * v_ref[...] + (1.0 - ADAM_B2) * (gv * gv)
        m_hat = mn / (1.0 - ADAM_B1 ** ADAM_STEP)
        v_hat = vn / (1.0 - ADAM_B2 ** ADAM_STEP)
        d_ref[...] = -ADAM_LR * (m_hat / (jnp.sqrt(v_hat) + ADAM_EPS) + ADAM_WD * w_ref[...])
        mo_ref[...] = mn
        vo_ref[...] = vn

    outs = pl.pallas_call(
        body, name=name, grid=(r // tr,), in_specs=[_row(tr, c)] * 4, out_specs=[_row(tr, c)] * 3,
        out_shape=[_sds((r, c), F32)] * 3, compiler_params=_params("parallel"),
    )(w2, g2, m2, v2)
    return tuple(o.reshape(shape) for o in outs)


_HBM = pl.BlockSpec(memory_space=pltpu.HBM)


N_CHIPS = 4
CHIPS = ((0, 0), (0, 1), (1, 0), (1, 1))


def _pair_exchange(x, *, name):
    def body(x_ref, o_ref, send_sems, recv_sems):
        xx, yy, cc = (lax.axis_index(a) for a in AXES)
        copies = [
            pltpu.make_async_remote_copy(
                src_ref=x_ref.at[4 * px + 2 * py + (1 - cc)], dst_ref=o_ref.at[k], send_sem=send_sems.at[k],
                recv_sem=recv_sems.at[k], device_id=(xx, yy, 1 - cc), device_id_type=pl.DeviceIdType.MESH)
            for k, (px, py) in enumerate(CHIPS)]
        for cp in copies:
            cp.start()
        for cp in copies:
            cp.wait_recv()
        for cp in copies:
            cp.wait_send()

    return pl.pallas_call(
        body, name=name, in_specs=[_HBM], out_specs=_HBM, out_shape=_sds((N_CHIPS,) + tuple(x.shape[1:]), x.dtype),
        scratch_shapes=[pltpu.SemaphoreType.DMA((N_CHIPS,)), pltpu.SemaphoreType.DMA((N_CHIPS,))],
    )(x)


def _pair_add(x, got, *, name):
    _, r, c = x.shape
    tr = _tile(r, max(PAD_ROWS, (1 << 17) // c), PAD_ROWS)

    def body(x_ref, g_ref, o_ref):
        mine = jnp.where(lax.axis_index("c") == 0, x_ref[0].astype(F32), x_ref[1].astype(F32))
        o_ref[...] = (mine + g_ref[...].astype(F32)).astype(o_ref.dtype)

    return pl.pallas_call(
        body, name=name, grid=(N_CHIPS, r // tr),
        in_specs=[pl.BlockSpec((None, 2, tr, c), lambda k, i: (k, 0, i, 0)), pl.BlockSpec((None, tr, c), lambda k, i: (k, i, 0))],
        out_specs=pl.BlockSpec((None, tr, c), lambda k, i: (k, i, 0)), out_shape=_sds((N_CHIPS, r, c), x.dtype),
        compiler_params=_params("parallel", "parallel"),
    )(x.reshape(N_CHIPS, 2, r, c), got)


def _chip_exchange(x, *, name):
    def body(x_ref, o_ref, send_sems, recv_sems, local_sem):
        xx, yy, cc = (lax.axis_index(a) for a in AXES)
        mine = 2 * xx + yy
        local = pltpu.make_async_copy(x_ref.at[mine], o_ref.at[mine], local_sem)
        local.start()
        sends = []
        for j, (px, py) in enumerate([(1 - xx, yy), (xx, 1 - yy), (1 - xx, 1 - yy)]):
            cp = pltpu.make_async_remote_copy(
                src_ref=x_ref.at[2 * px + py], dst_ref=o_ref.at[mine], send_sem=send_sems.at[j], recv_sem=recv_sems.at[j],
                device_id=(px, py, cc), device_id_type=pl.DeviceIdType.MESH)
            cp.start()
            sends.append(cp)
        for j, (px, py) in enumerate([(1 - xx, yy), (xx, 1 - yy), (1 - xx, 1 - yy)]):
            pltpu.make_async_remote_copy(
                src_ref=x_ref.at[2 * px + py], dst_ref=o_ref.at[2 * px + py], send_sem=send_sems.at[j],
                recv_sem=recv_sems.at[j], device_id=(px, py, cc), device_id_type=pl.DeviceIdType.MESH).wait_recv()
        for cp in sends:
            cp.wait_send()
        local.wait()

    return pl.pallas_call(
        body, name=name, in_specs=[_HBM], out_specs=_HBM, out_shape=_sds(x.shape, x.dtype),
        scratch_shapes=[pltpu.SemaphoreType.DMA((N_CHIPS - 1,)), pltpu.SemaphoreType.DMA((N_CHIPS - 1,)),
                        pltpu.SemaphoreType.DMA(())],
    )(x)


def _reduce_scatter(x, *, name):
    chip_sums = _pair_add(x, _pair_exchange(x, name=name + "_pair"), name=name + "_pair_add")
    return _sum_slots(_chip_exchange(chip_sums, name=name + "_chips"), name=name + "_sum")


def _all_gather(x, *, name):
    def body(x_ref, o_ref, send_sems, recv_sems, local_sem):
        xx, yy, cc = (lax.axis_index(a) for a in AXES)
        me, sibling = (xx, yy, cc), (xx, yy, 1 - cc)
        chips = [(1 - xx, yy), (xx, 1 - yy), (1 - xx, 1 - yy)]

        def slot(px, py, pc):
            return o_ref.at[4 * px + 2 * py + pc]

        def copy(k, block, to, src=None):
            return pltpu.make_async_remote_copy(
                src_ref=slot(*block) if src is None else src, dst_ref=slot(*block), send_sem=send_sems.at[k],
                recv_sem=recv_sems.at[k], device_id=to, device_id_type=pl.DeviceIdType.MESH)

        local = pltpu.make_async_copy(x_ref, slot(*me), local_sem)
        local.start()
        first = [copy(0, me, sibling, src=x_ref)] + [copy(1 + j, me, (*chip, cc), src=x_ref) for j, chip in enumerate(chips)]
        for cp in first:
            cp.start()
        passed = [copy(4 + j, (*chip, cc), sibling) for j, chip in enumerate(chips)]
        for j, chip in enumerate(chips):
            copy(1 + j, (*chip, cc), me).wait_recv()
            passed[j].start()
        copy(0, sibling, me).wait_recv()
        for j, chip in enumerate(chips):
            copy(4 + j, (*chip, 1 - cc), me).wait_recv()
        for cp in first + passed:
            cp.wait_send()
        local.wait()

    return pl.pallas_call(
        body, name=name, in_specs=[_HBM], out_specs=_HBM, out_shape=_sds((N_DEV,) + tuple(x.shape), x.dtype),
        scratch_shapes=[pltpu.SemaphoreType.DMA((N_DEV - 1,)), pltpu.SemaphoreType.DMA((N_DEV - 1,)),
                        pltpu.SemaphoreType.DMA(())],
    )(x)


def _sum_slots(x, *, name):
    n, r, c = x.shape
    tr = _tile(r, max(PAD_ROWS, (1 << 17) // c), PAD_ROWS)

    def body(x_ref, o_ref):
        acc = x_ref[0].astype(F32)
        for d in range(1, n):
            acc = acc + x_ref[d].astype(F32)
        o_ref[...] = acc

    return pl.pallas_call(
        body, name=name, grid=(r // tr,), in_specs=[pl.BlockSpec((n, tr, c), lambda i: (0, i, 0))],
        out_specs=_row(tr, c), out_shape=_sds((r, c), F32), compiler_params=_params("parallel"),
    )(x)


def _s5_layouts(bbar_r, bbar_i, c_re, c_im):
    eye = jnp.eye(S5_GPB, dtype=F32)

    def b_blocks(bbar):
        bb = bbar.reshape(S5_NB, S5_GPB, S5_STATE, S5_GROUP)
        return jnp.einsum("jgph,gk->jghkp", bb, eye).reshape(S5_NB, S5_GPB * S5_GROUP, S5_GPB * S5_STATE)

    def c_blocks(cc):
        c4 = cc.reshape(S5_NB, S5_GPB, S5_GROUP, S5_STATE)
        return jnp.einsum("jghp,gk->jgpkh", c4, eye).reshape(S5_NB, S5_GPB * S5_STATE, S5_GPB * S5_GROUP)

    bre, bim, cre, cim = b_blocks(bbar_r), b_blocks(bbar_i), c_blocks(c_re), c_blocks(c_im)
    cast = lambda a: a.astype(MXU)
    sw = lambda a: jnp.swapaxes(a, 1, 2).astype(MXU)
    return dict(bre=cast(bre), bim=cast(bim), cre=cast(cre), cim=cast(cim), bre_t=sw(bre), bim_t=sw(bim), cre_t=sw(cre),
                cim_t=sw(cim))


def _b_diag(db):
    d5 = db.reshape(S5_NB, S5_GPB, S5_GROUP, S5_GPB, S5_STATE)
    diag = jnp.stack([d5[:, g, :, g, :] for g in range(S5_GPB)], axis=1)
    return jnp.swapaxes(diag, 2, 3).reshape(S5_GROUPS, S5_STATE * S5_GROUP)


def _c_diag(dc):
    d5 = dc.reshape(S5_NB, S5_GPB, S5_STATE, S5_GPB, S5_GROUP)
    diag = jnp.stack([d5[:, g, :, g, :] for g in range(S5_GPB)], axis=1)
    return jnp.swapaxes(diag, 2, 3).reshape(S5_GROUPS, S5_GROUP, S5_STATE)


def _head_rows(*vecs):
    par = jnp.zeros((8, HP), F32)
    for i, v in enumerate(vecs):
        par = par.at[i, :SSD_HEADS].set(v.astype(F32))
    return par


def _add(acc, r):
    return (acc + r,)


def _layer_fwd(x, mem, w, consts):
    s = {"x": x}
    rep = consts["rep"]
    s["h1"] = h1 = _rmsnorm_fwd(x, w["norm_mix"], name="norm_mix_fwd")
    s["proj"] = proj = _mm(h1, w["w_main"], tm=2048, name="in_proj")
    s["dtr"] = dtr = _mm(h1, w["w_dt"], name="dt_proj")
    ar, ai, ldt = w["s5_a_re"], w["s5_a_im"], w["s5_log_dt"].reshape(S5_GROUPS, 1)
    br, bi = w["s5_b_re"].reshape(S5_GROUPS, -1), w["s5_b_im"].reshape(S5_GROUPS, -1)
    abar_r, abar_i, bbar_r, bbar_i = _s5_prep(ar, ai, ldt, br, bi, rep)
    s["abar"] = abar = (abar_r.reshape(1, S5_CH), abar_i.reshape(1, S5_CH))
    s["lay"] = lay = _s5_layouts(bbar_r, bbar_i, w["s5_c_re"], w["s5_c_im"])
    s["u"] = u = _interleave_rows(proj[:, :D_S5])
    s["sr"], s["si"], s["ypre"], s["yg"] = _s5_fwd(u, *abar, lay["bre"], lay["bim"], lay["cre"], lay["cim"],
                                                   w["s5_d"].reshape(1, D_S5))
    ypre, yg = s["ypre"], s["yg"]
    s["gp"], out_s5 = _mm(yg, w["s5_w_glu"], extras=[ypre], epi=lambda acc, yp: (acc, _gelu(yp) * _sigmoid(acc)),
                          out_dtypes=(F32, MXU), name="s5_glu")
    out_s5 = _deinterleave_rows(out_s5)
    s["par"] = par = _head_rows(w["ssd_dt_bias"], w["ssd_a_log"], w["ssd_d"])
    out_ssd, s["y_ssd"], s["states"], s["conv_pre"] = _ssd_fwd(proj, w["ssd_conv_w"], w["ssd_conv_b"], dtr, par,
                                                               w["ssd_norm"], consts["ssd"])
    s["ycat"] = ycat = jnp.concatenate([out_s5, out_ssd], axis=1)
    s["x1"] = x1 = _mm(ycat, w["w_out"], extras=[x], epi=_add, name="out_proj")
    s["hq"] = hq = _rmsnorm_fwd(x1, w["norm_xattn"], name="norm_xattn_fwd")
    s["mn"] = mn = _rmsnorm_fwd(mem, w["norm_mem"], name="norm_mem_fwd")
    s["q"] = q = _mm(hq, w["xa_wq"], out_dtypes=(MXU,), tm=2048, name="xa_q")
    s["k"] = k = _mm(mn, w["xa_wk"], out_dtypes=(MXU,), name="xa_k")
    s["v"] = v = _mm(mn, w["xa_wv"], out_dtypes=(MXU,), name="xa_v")
    s["o"] = o = _attn_fwd(q, k, v)
    s["x2"] = x2 = _mm(o, w["xa_wo"], extras=[x1], epi=_add, name="xa_o")
    s["hm"] = hm = _rmsnorm_fwd(x2, w["norm_mlp"], name="norm_mlp_fwd")
    s["act"] = _mm(hm, w["mlp_w1"], epi=lambda acc: (jnp.square(jnp.maximum(acc, 0.0)),), out_dtypes=(MXU,),
                   tm=2048, name="mlp_up")
    x3 = _mm(s["act"], w["mlp_w2"], extras=[x2], epi=_add, name="mlp_down")
    return x3, s


def _layer_bwd(dx3, dx3m, mem, w, s, consts):
    g = {}
    rep = consts["rep"]
    wire = (MXU,)
    d_a = _mm(dx3m, w["mlp_w2"], tb=True, extras=[s["act"]],
              epi=lambda acc, act: (acc * (2.0 * jnp.sqrt(act.astype(F32))),), out_dtypes=(MXU,), tm=2048,
              name="mlp_down_dx")
    g["mlp_w2"] = _mm(s["act"], dx3m, ta=True, out_dtypes=wire, name="mlp_down_dw")
    g["mlp_w1"] = _mm(s["hm"], d_a, ta=True, out_dtypes=wire, name="mlp_up_dw")
    norm_bwd = dict(epi=_norm_bwd_epilogue, out_dtypes=(F32, MXU), n_colsum=1, tm=512)
    dx2, dx2m, g["norm_mlp"] = _mm(d_a, w["mlp_w1"], tb=True, extras=[s["x2"], dx3], vecs=[w["norm_mlp"].reshape(1, -1)],
                                   name="mlp_up_dx", **norm_bwd)
    d_o = _mm(dx2m, w["xa_wo"], tb=True, out_dtypes=(MXU,), tm=2048, name="xa_o_dx")
    g["xa_wo"] = _mm(s["o"], dx2m, ta=True, out_dtypes=wire, name="xa_o_dw")
    dq, dk, dv = _attn_bwd(s["q"], s["k"], s["v"], d_o)
    g["xa_wq"] = _mm(s["hq"], dq, ta=True, out_dtypes=wire, name="xa_q_dw")
    dx1, dx1m, g["norm_xattn"] = _mm(dq, w["xa_wq"], tb=True, extras=[s["x1"], dx2],
                                     vecs=[w["norm_xattn"].reshape(1, -1)], name="xa_q_dx", **norm_bwd)
    g["xa_wk"] = _mm(s["mn"], dk, ta=True, out_dtypes=wire, name="xa_k_dw")
    g["xa_wv"] = _mm(s["mn"], dv, ta=True, out_dtypes=wire, name="xa_v_dw")
    d_mn_v = _mm(dv, w["xa_wv"], tb=True, name="xa_v_dx")
    d_mn = _mm(dk, w["xa_wk"], tb=True, extras=[d_mn_v], epi=_add, name="xa_k_dx")
    _, _, g["norm_mem"] = _rmsnorm_bwd(mem, w["norm_mem"], d_mn, None, name="norm_mem_bwd")
    d_ycat = _mm(dx1m, w["w_out"], tb=True, name="out_proj_dx")
    g["w_out"] = _mm(s["ycat"], dx1m, ta=True, out_dtypes=wire, name="out_proj_dw")
    lay, proj, ypre, u = s["lay"], s["proj"], s["ypre"], s["u"]
    d_os5 = _interleave_rows(d_ycat[:, :D_S5])
    d_gp = _s5_gate_bwd(d_os5, s["gp"], ypre)
    g["s5_w_glu"] = _mm(s["yg"], d_gp, ta=True, out_dtypes=wire, name="s5_glu_dw")
    d_ypre = _mm(d_gp, w["s5_w_glu"], tb=True, extras=[d_os5, s["gp"], ypre],
                 epi=lambda acc, do, gp, yp: ((acc + do * _sigmoid(gp)) * _gelu_grad(yp),), name="s5_glu_dx")
    du, d_abar_r, d_abar_i, db_re, db_im, dc_re, dc_im, d_d = _s5_bwd(
        d_ypre, u, s["sr"], s["si"], *s["abar"], lay["cre_t"], lay["cim_t"], lay["bre_t"], lay["bim_t"],
        w["s5_d"].reshape(1, D_S5))
    d_bbar_r, d_bbar_i = _b_diag(db_re), _b_diag(db_im)
    g["s5_c_re"], g["s5_c_im"] = _c_diag(dc_re), -_c_diag(dc_im)
    g["s5_d"] = d_d.reshape(S5_GROUPS, S5_GROUP)
    du = _deinterleave_rows(du)
    ar, ai, ldt = w["s5_a_re"], w["s5_a_im"], w["s5_log_dt"].reshape(S5_GROUPS, 1)
    br, bi = w["s5_b_re"].reshape(S5_GROUPS, -1), w["s5_b_im"].reshape(S5_GROUPS, -1)
    d_ar, d_ai, d_ldt, d_br, d_bi = _s5_prep_bwd(
        ar, ai, ldt, br, bi, rep, d_abar_r.reshape(S5_GROUPS, S5_STATE), d_abar_i.reshape(S5_GROUPS, S5_STATE),
        d_bbar_r, d_bbar_i)
    g["s5_a_re"], g["s5_a_im"], g["s5_log_dt"] = d_ar, d_ai, d_ldt.reshape(S5_GROUPS)
    g["s5_b_re"] = d_br.reshape(S5_GROUPS, S5_STATE, S5_GROUP)
    g["s5_b_im"] = d_bi.reshape(S5_GROUPS, S5_STATE, S5_GROUP)
    d_xbc, dz, d_dtr, g["ssd_norm"], d_par, g["ssd_conv_w"], g["ssd_conv_b"] = _ssd_bwd(
        proj, s["conv_pre"], w["ssd_conv_w"], s["dtr"], s["par"], w["ssd_norm"], s["y_ssd"], s["states"], d_ycat,
        consts["ssd"])
    g["ssd_dt_bias"], g["ssd_a_log"], g["ssd_d"] = (d_par[i, :SSD_HEADS] for i in range(3))
    d_proj = jnp.concatenate([du, dz, d_xbc], axis=1)
    g_main = _mm(s["h1"], d_proj, ta=True, out_dtypes=wire, name="in_proj_dw")
    g_dt = _mm(s["h1"], d_dtr, ta=True, out_dtypes=wire, name="dt_proj_dw")
    g["w_in"] = jnp.concatenate([g_main, g_dt[:, :SSD_HEADS]], axis=1)
    d_h1_dt = _mm(d_dtr, w["w_dt"], tb=True, name="dt_proj_dx")
    dx, dxm, g["norm_mix"] = _mm(
        d_proj, w["w_main"], tb=True, extras=[d_h1_dt, s["x"], dx1], vecs=[w["norm_mix"].reshape(1, -1)],
        name="in_proj_dx", **{**norm_bwd, "epi": lambda acc, dt_part, *rest: _norm_bwd_epilogue(acc + dt_part, *rest)})
    return dx, dxm, g


LAYER_WEIGHTS = ("norm_mix", "w_in", "s5_a_re", "s5_a_im", "s5_log_dt", "s5_b_re", "s5_b_im", "s5_c_re", "s5_c_im", "s5_d",
                 "s5_w_glu", "ssd_conv_w", "ssd_conv_b", "ssd_dt_bias", "ssd_a_log", "ssd_d", "ssd_norm", "w_out",
                 "norm_xattn", "norm_mem", "xa_wq", "xa_wk", "xa_wv", "xa_wo", "norm_mlp", "mlp_w1", "mlp_w2")
WEIGHTS = LAYER_WEIGHTS + ("norm_final",)


def _local_step(x, mem, target, weights):
    consts = {
        "ssd": _ssd_consts(),
        "rep": (jnp.arange(S5_STATE)[:, None] == jnp.arange(S5_STATE * S5_GROUP)[None, :] // S5_GROUP).astype(F32),
    }
    layers = []
    for l in range(DEPTH):
        w = {n: weights[n][l] for n in LAYER_WEIGHTS}
        w_in = w["w_in"]
        w["w_main"] = w_in[:, :D_MAIN]
        w["w_dt"] = jnp.pad(w_in[:, D_MAIN:], ((0, 0), (0, HP - SSD_HEADS)))
        layers.append(w)
    saved = []
    for l in range(DEPTH):
        x, s = _layer_fwd(x, mem, layers[l], consts)
        saved.append(s)
    loss, dx, dxm, g_final = _loss_head(x, weights["norm_final"], target)
    grads = [None] * DEPTH
    for l in reversed(range(DEPTH)):
        dx, dxm, grads[l] = _layer_bwd(dx, dxm, mem, layers[l], saved[l], consts)
    out = {n: jnp.stack([grads[l][n].reshape(weights[n].shape[1:]) for l in range(DEPTH)]) for n in LAYER_WEIGHTS}
    out["norm_final"] = g_final.reshape(weights["norm_final"].shape)
    return loss, dx, out


SHARDED = {"w_in": 2, "s5_w_glu": 1, "ssd_conv_w": 2, "w_out": 1, "xa_wq": 1, "xa_wk": 1, "xa_wv": 1, "xa_wo": 1,
           "mlp_w1": 2, "mlp_w2": 1}
EXACT = ("ssd_conv_w",)
ROW_EXCHANGE = tuple(n for n, ax in SHARDED.items() if ax == 1)
OWN_EXCHANGE = tuple(n for n in SHARDED if n not in ROW_EXCHANGE)
REPLICATED = tuple(n for n in WEIGHTS if n not in SHARDED)
LANES = 128
PAD_ROWS = 16


def _gather_weights(local):
    def assemble(n, seg):
        shp, ax = local[n].shape, SHARDED[n]
        return jnp.moveaxis(seg, 0, ax).reshape(*shp[:ax], N_DEV * shp[ax], *shp[ax + 1:])

    full = {}
    for n in OWN_EXCHANGE:
        payload = local[n] if n in EXACT else local[n].astype(MXU)
        full[n] = assemble(n, _all_gather(payload, name="gather_" + n))
    got = _all_gather(jnp.concatenate([local[n].astype(MXU) for n in ROW_EXCHANGE], axis=1), name="gather_row_sharded")
    off = 0
    for n in ROW_EXCHANGE:
        rows = local[n].shape[1]
        full[n] = assemble(n, got[:, :, off:off + rows])
        off += rows
    return full


def _scatter_grads(grads, local_shapes):
    def shards(n):
        shp, ax = local_shapes[n], SHARDED[n]
        gfull = grads[n].reshape(*shp[:ax], N_DEV, shp[ax], *shp[ax + 1:])
        return jnp.moveaxis(gfull, ax, 0).astype(MXU)

    out = {}
    for n in OWN_EXCHANGE:
        shp = local_shapes[n]
        out[n] = _reduce_scatter(shards(n).reshape(N_DEV, -1, shp[-1]), name="scatter_" + n).reshape(shp)
    payload = jnp.concatenate([shards(n) for n in ROW_EXCHANGE], axis=2)
    summed = _reduce_scatter(payload.reshape(N_DEV, -1, payload.shape[-1]), name="scatter_row_sharded")
    summed = summed.reshape(payload.shape[1:])
    off = 0
    for n in ROW_EXCHANGE:
        rows = local_shapes[n][1]
        out[n] = summed[:, off:off + rows]
        off += rows
    return out


def _allreduce_small(loss, grads):
    parts = [loss.reshape(-1)[:1]] + [grads[n].reshape(-1) for n in REPLICATED]
    flat = jnp.concatenate(parts)
    quantum = N_DEV * PAD_ROWS * LANES
    flat = jnp.pad(flat, (0, -flat.shape[0] % quantum))
    mine = _reduce_scatter(flat.reshape(N_DEV, -1, LANES), name="reduce_small_grads")
    summed = _all_gather(mine, name="gather_small_grads").reshape(-1)
    out, off = {}, 1
    for n in REPLICATED:
        size = grads[n].size
        out[n] = summed[off:off + size].reshape(grads[n].shape)
        off += size
    return summed[0], out


def kernel(x, mem, norm_mix, w_in, s5_a_re, s5_a_im, s5_log_dt, s5_b_re, s5_b_im, s5_c_re, s5_c_im, s5_d, s5_w_glu, ssd_conv_w, ssd_conv_b, ssd_dt_bias, ssd_a_log, ssd_d, ssd_norm, w_out, norm_xattn, norm_mem, xa_wq, xa_wk, xa_wv, xa_wo, norm_mlp, mlp_w1, mlp_w2, norm_final, loss_target, m_norm_mix, m_w_in, m_s5_a_re, m_s5_a_im, m_s5_log_dt, m_s5_b_re, m_s5_b_im, m_s5_c_re, m_s5_c_im, m_s5_d, m_s5_w_glu, m_ssd_conv_w, m_ssd_conv_b, m_ssd_dt_bias, m_ssd_a_log, m_ssd_d, m_ssd_norm, m_w_out, m_norm_xattn, m_norm_mem, m_xa_wq, m_xa_wk, m_xa_wv, m_xa_wo, m_norm_mlp, m_mlp_w1, m_mlp_w2, m_norm_final, v_norm_mix, v_w_in, v_s5_a_re, v_s5_a_im, v_s5_log_dt, v_s5_b_re, v_s5_b_im, v_s5_c_re, v_s5_c_im, v_s5_d, v_s5_w_glu, v_ssd_conv_w, v_ssd_conv_b, v_ssd_dt_bias, v_ssd_a_log, v_ssd_d, v_ssd_norm, v_w_out, v_norm_xattn, v_norm_mem, v_xa_wq, v_xa_wk, v_xa_wv, v_xa_wo, v_norm_mlp, v_mlp_w1, v_mlp_w2, v_norm_final):
    args = locals()
    local = {n: args[n] for n in WEIGHTS}
    full = dict(local)
    full.update(_gather_weights(local))
    loss, grad_x, grads = _local_step(x[0], mem[0], loss_target[0], full)
    loss, g_small = _allreduce_small(loss, grads)
    g_all = _scatter_grads(grads, {n: local[n].shape for n in SHARDED})
    g_all.update(g_small)
    delta, new_m, new_v = {}, {}, {}
    for n in WEIGHTS:
        delta[n], new_m[n], new_v[n] = _adamw(local[n], g_all[n], args["m_" + n], args["v_" + n], name="adamw_" + n)
    return (loss, grad_x[None], *[g_all[n] for n in WEIGHTS], *[delta[n] for n in WEIGHTS],
            *[new_m[n] for n in WEIGHTS], *[new_v[n] for n in WEIGHTS])
```

```python
import functools
import math

import jax
import jax.numpy as jnp
from jax import lax
from jax.experimental import pallas as pl
from jax.experimental.pallas import tpu as pltpu

F32 = jnp.float32
MXU = jnp.bfloat16
HI = lax.Precision.HIGHEST

D_MODEL = 1024
DEPTH = 4
MEM_LEN = 256
D_S5 = 1024
D_SSD = 1024
S5_GROUP = 16
S5_GROUPS = 64
S5_STATE = 64
S5_CH = S5_GROUPS * S5_STATE
S5_NB = 4
S5_GPB = S5_GROUPS // S5_NB
SSD_HEADDIM = 64
SSD_HEADS = 16
SSD_GROUPS = 4
SSD_STATE = 128
SSD_CONV = 4
SSD_CHUNK = 128
SSD_BC = SSD_GROUPS * SSD_STATE
D_CONV_CH = 2048
D_MAIN = 4096
D_IN_PROJ = D_MAIN + SSD_HEADS
HP = 128
XA_HEADS = 4
XA_HEAD_DIM = 256
D_FF = 4096
EPS = 1e-5
N_DEV = 8
AXES = ("x", "y", "c")

ADAM_LR = 0.001
ADAM_B1 = 0.9
ADAM_B2 = 0.999
ADAM_EPS = 1e-08
ADAM_WD = 0.01
ADAM_STEP = 10

VMEM_LIMIT = 56 * 1024 * 1024


def _params(*sem):
    return pltpu.CompilerParams(dimension_semantics=sem, vmem_limit_bytes=VMEM_LIMIT)


def _tile(n, pref, quantum=128):
    t = (min(pref, n) // quantum) * quantum
    while t >= quantum:
        if n % t == 0:
            return t
        t -= quantum
    return n


def _sds(shape, dtype):
    return jax.ShapeDtypeStruct(tuple(shape), dtype)


def _recip(d):
    r = pl.reciprocal(d, approx=True)
    return r * (2.0 - d * r)


def _sigmoid(x):
    return _recip(1.0 + jnp.exp(-jnp.maximum(x, -80.0)))


def _silu(x):
    return x * _sigmoid(x)


def _silu_grad(x, s):
    return s * (1.0 + x * (1.0 - s))


_GELU_C = math.sqrt(2.0 / math.pi)


def _gelu(x):
    return 0.5 * x * (1.0 + jnp.tanh(_GELU_C * (x + 0.044715 * x * x * x)))


def _gelu_grad(x):
    th = jnp.tanh(_GELU_C * (x + 0.044715 * x * x * x))
    return 0.5 * (1.0 + th) + 0.5 * x * (1.0 - th * th) * _GELU_C * (1.0 + 3.0 * 0.044715 * x * x)


def _softplus(x):
    return jnp.maximum(x, 0.0) + jnp.log(1.0 + jnp.exp(-jnp.abs(x)))


_NN = (((1,), (0,)), ((), ()))
_NT = (((1,), (1,)), ((), ()))
_TN = (((0,), (0,)), ((), ()))


def _dot(a, b, dims=_NN, precision=None):
    return lax.dot_general(a, b, dims, precision=precision, preferred_element_type=F32)


def _dot_split(x, w):
    hi = x.astype(MXU)
    lo = (x - hi.astype(F32)).astype(MXU)
    return _dot(hi, w) + _dot(lo, w)


def _rows8(v):
    return jnp.broadcast_to(v, (8, v.shape[1]))


def _mm(a, b, *, ta=False, tb=False, extras=(), vecs=(), epi=None, out_dtypes=(F32,), n_colsum=0, tm=1024, tn=1024,
        tk=2048, name):
    m, k = (a.shape[1], a.shape[0]) if ta else a.shape
    n = b.shape[0] if tb else b.shape[1]
    assert k == (b.shape[1] if tb else b.shape[0]), (a.shape, b.shape, ta, tb)
    if ta:
        tk = 2 * tk
    tm, tn, tk = _tile(m, tm), _tile(n, tn), _tile(k, tk)
    nk = k // tk
    n_ex, n_out = len(extras) + len(vecs), len(out_dtypes)
    assert n_colsum == 0 or tn == n
    dims = (((0,) if ta else (1,), (1,) if tb else (0,)), ((), ()))

    def body(a_ref, b_ref, *rest):
        ex_refs, out_refs = rest[:n_ex], rest[n_ex:n_ex + n_out]
        sum_refs = rest[n_ex + n_out:n_ex + n_out + n_colsum]
        prod = _dot(a_ref[...].astype(MXU), b_ref[...].astype(MXU), dims)

        def finish(total):
            outs = epi(total, *[e[...] for e in ex_refs]) if epi is not None else (total,)
            for o, r in zip(outs[:n_out], out_refs, strict=True):
                r[...] = o.astype(r.dtype)
            for o, r in zip(outs[n_out:], sum_refs, strict=True):
                @pl.when(pl.program_id(0) == 0)
                def _(o=o, r=r):
                    r[...] = o

                @pl.when(pl.program_id(0) > 0)
                def _(o=o, r=r):
                    r[...] += o

        if nk == 1:
            finish(prod)
            return
        acc = rest[n_ex + n_out + n_colsum]
        kk = pl.program_id(2)

        @pl.when(kk == 0)
        def _():
            acc[...] = prod

        @pl.when(jnp.logical_and(kk > 0, kk < nk - 1))
        def _():
            acc[...] += prod

        @pl.when(kk == nk - 1)
        def _():
            finish(acc[...] + prod)

    a_spec = pl.BlockSpec((tk, tm), lambda i, j, kk: (kk, i)) if ta else pl.BlockSpec((tm, tk), lambda i, j, kk: (i, kk))
    b_spec = pl.BlockSpec((tn, tk), lambda i, j, kk: (j, kk)) if tb else pl.BlockSpec((tk, tn), lambda i, j, kk: (kk, j))
    mn_spec = pl.BlockSpec((tm, tn), lambda i, j, kk: (i, j))
    n_spec = pl.BlockSpec((1, tn), lambda i, j, kk: (0, j))
    outs = pl.pallas_call(
        body,
        name=name,
        grid=(m // tm, n // tn, nk),
        in_specs=[a_spec, b_spec] + [mn_spec] * len(extras) + [n_spec] * len(vecs),
        out_specs=[mn_spec] * n_out + [n_spec] * n_colsum,
        out_shape=[_sds((m, n), dt) for dt in out_dtypes] + [_sds((1, n), F32)] * n_colsum,
        scratch_shapes=[pltpu.VMEM((tm, tn), F32)] if nk > 1 else [],
        compiler_params=_params(*(("arbitrary",) * 3 if n_colsum else ("parallel", "parallel", "arbitrary"))),
    )(a, b, *extras, *vecs)
    return outs[0] if len(outs) == 1 else outs


def _row(tb, w, cb=0):
    return pl.BlockSpec((tb, w), lambda i: (i, cb))


def _const(shape):
    return pl.BlockSpec(shape, lambda i: (0,) * len(shape))


def _rmsnorm_fwd(x, g, *, name):
    t, d = x.shape
    tb = _tile(t, 512, 8)

    def body(x_ref, g_ref, h_ref):
        xv = x_ref[...]
        r = lax.rsqrt(jnp.mean(xv * xv, axis=-1, keepdims=True) + EPS)
        h_ref[...] = (xv * r * g_ref[...]).astype(h_ref.dtype)

    return pl.pallas_call(
        body, name=name, grid=(t // tb,), in_specs=[_row(tb, d), _const((1, d))], out_specs=_row(tb, d),
        out_shape=_sds((t, d), MXU), compiler_params=_params("parallel"),
    )(x, g.reshape(1, d))


def _rmsnorm_bwd(x, g, dh, dres, *, name):
    t, d = x.shape
    tb = _tile(t, 256, 8)
    has_res = dres is not None

    def body(x_ref, g_ref, dh_ref, *rest):
        dx_ref, dxm_ref, dg_ref = rest[-3:]

        @pl.when(pl.program_id(0) == 0)
        def _():
            dg_ref[...] = jnp.zeros_like(dg_ref)

        xv = x_ref[...]
        r = lax.rsqrt(jnp.mean(xv * xv, axis=-1, keepdims=True) + EPS)
        xh = xv * r
        dhv = dh_ref[...].astype(F32)
        dg_ref[...] += jnp.sum(dhv * xh, axis=0, keepdims=True)
        dxh = dhv * g_ref[...]
        dx = r * (dxh - xh * jnp.mean(dxh * xh, axis=-1, keepdims=True))
        if has_res:
            dx = dx + rest[0][...]
        dx_ref[...] = dx
        dxm_ref[...] = dx.astype(dxm_ref.dtype)

    ins = [x, g.reshape(1, d), dh] + ([dres] if has_res else [])
    return pl.pallas_call(
        body, name=name, grid=(t // tb,),
        in_specs=[_row(tb, d), _const((1, d)), _row(tb, d)] + ([_row(tb, d)] if has_res else []),
        out_specs=[_row(tb, d), _row(tb, d), _const((1, d))],
        out_shape=[_sds((t, d), F32), _sds((t, d), MXU), _sds((1, d), F32)],
        compiler_params=_params("arbitrary"),
    )(*ins)


def _norm_bwd_epilogue(dh, x, dres, g):
    r = lax.rsqrt(jnp.mean(x * x, axis=-1, keepdims=True) + EPS)
    xh = x * r
    dxh = dh * g
    dx = r * (dxh - xh * jnp.mean(dxh * xh, axis=-1, keepdims=True)) + dres
    return dx, dx, jnp.sum(dh * xh, axis=0, keepdims=True)


def _loss_head(x, g, target):
    t, d = x.shape
    tb = _tile(t, 256, 8)

    def body(x_ref, g_ref, tg_ref, loss_ref, dx_ref, dxm_ref, dg_ref):
        @pl.when(pl.program_id(0) == 0)
        def _():
            dg_ref[...] = jnp.zeros_like(dg_ref)
            loss_ref[...] = jnp.zeros_like(loss_ref)

        xv, gv = x_ref[...], g_ref[...]
        r = lax.rsqrt(jnp.mean(xv * xv, axis=-1, keepdims=True) + EPS)
        xh = xv * r
        err = xh * gv - tg_ref[...]
        loss_ref[...] += 0.5 * jnp.sum(jnp.mean(err * err, axis=-1, keepdims=True), axis=0, keepdims=True)
        dy = err * (1.0 / d)
        dg_ref[...] += jnp.sum(dy * xh, axis=0, keepdims=True)
        dxh = dy * gv
        dx = r * (dxh - xh * jnp.mean(dxh * xh, axis=-1, keepdims=True))
        dx_ref[...] = dx
        dxm_ref[...] = dx.astype(dxm_ref.dtype)

    return pl.pallas_call(
        body, name="loss_head", grid=(t // tb,),
        in_specs=[_row(tb, d), _const((1, d)), _row(tb, d)],
        out_specs=[_const((1, HP)), _row(tb, d), _row(tb, d), _const((1, d))],
        out_shape=[_sds((1, HP), F32), _sds((t, d), F32), _sds((t, d), MXU), _sds((1, d), F32)],
        compiler_params=_params("arbitrary"),
    )(x, g.reshape(1, d), target)


def _s5_discretise(ar, ai, ldt, br, bi, rep):
    dt = jnp.exp(ldt)
    mag = jnp.exp(dt * ar)
    abar_r, abar_i = mag * jnp.cos(dt * ai), mag * jnp.sin(dt * ai)
    den = ar * ar + ai * ai
    zr, zi = abar_r - 1.0, abar_i
    fr = (zr * ar + zi * ai) / den
    fi = (zi * ar - zr * ai) / den
    fr_e, fi_e = _dot(fr, rep, precision=HI), _dot(fi, rep, precision=HI)
    return abar_r, abar_i, fr_e * br - fi_e * bi, fr_e * bi + fi_e * br


def _s5_prep(ar, ai, ldt, br, bi, rep):
    g, p = ar.shape
    ph = br.shape[1]

    def body(ar_ref, ai_ref, ldt_ref, br_ref, bi_ref, rep_ref, o0, o1, o2, o3):
        outs = _s5_discretise(ar_ref[...], ai_ref[...], ldt_ref[...], br_ref[...], bi_ref[...], rep_ref[...])
        for o, v in zip((o0, o1, o2, o3), outs):
            o[...] = v

    return pl.pallas_call(
        body, name="s5_prep",
        out_shape=[_sds((g, p), F32), _sds((g, p), F32), _sds((g, ph), F32), _sds((g, ph), F32)],
        compiler_params=pltpu.CompilerParams(vmem_limit_bytes=VMEM_LIMIT),
    )(ar, ai, ldt, br, bi, rep)


def _s5_prep_bwd(ar, ai, ldt, br, bi, rep, d_abar_r, d_abar_i, d_bbar_r, d_bbar_i):
    g, p = ar.shape
    ph = br.shape[1]

    def body(ar_ref, ai_ref, ldt_ref, br_ref, bi_ref, rep_ref, c0, c1, c2, c3, o0, o1, o2, o3, o4):
        rep_v = rep_ref[...]
        _, vjp = jax.vjp(lambda a, b, c, d, e: _s5_discretise(a, b, c, d, e, rep_v),
                         ar_ref[...], ai_ref[...], ldt_ref[...], br_ref[...], bi_ref[...])
        grads = vjp((c0[...], c1[...], c2[...], c3[...]))
        for o, v in zip((o0, o1, o2, o3, o4), grads):
            o[...] = v

    return pl.pallas_call(
        body, name="s5_prep_bwd",
        out_shape=[_sds((g, p), F32), _sds((g, p), F32), _sds((g, 1), F32), _sds((g, ph), F32), _sds((g, ph), F32)],
        compiler_params=pltpu.CompilerParams(vmem_limit_bytes=VMEM_LIMIT),
    )(ar, ai, ldt, br, bi, rep, d_abar_r, d_abar_i, d_bbar_r, d_bbar_i)


SCAN_ROWS = 1024


def _scan_rows(t):
    return _tile(t, SCAN_ROWS, 64)


def _interleave_rows(x):
    t, c = x.shape
    tb = _scan_rows(t)
    return x.reshape(t // tb, 8, tb // 8, c).swapaxes(1, 2).reshape(t, c)


def _deinterleave_rows(x):
    t, c = x.shape
    tb = _scan_rows(t)
    return x.reshape(t // tb, tb // 8, 8, c).swapaxes(1, 2).reshape(t, c)


def _cmul(ar, ai, br, bi):
    return ar * br - ai * bi, ar * bi + ai * br


def _segment_carries(fr, fi, ar8, ai8, c_r, c_i, seg, reverse):
    pr, pi = ar8, ai8
    for _ in range(int(math.log2(seg))):
        pr, pi = _cmul(pr, pi, pr, pi)
    row = lax.broadcasted_iota(jnp.int32, fr.shape, 0)
    edge = 7 if reverse else 0
    qr, qi = _cmul(pr, pi, c_r, c_i)
    xr, xi = jnp.where(row == edge, fr + qr, fr), jnp.where(row == edge, fi + qi, fi)
    for sh in (1, 2, 4):
        if reverse:
            keep, amount = row < 8 - sh, 8 - sh
        else:
            keep, amount = row >= sh, sh
        qr, qi = jnp.where(keep, pltpu.roll(xr, amount, 0), 0.0), jnp.where(keep, pltpu.roll(xi, amount, 0), 0.0)
        tr, ti = _cmul(pr, pi, qr, qi)
        xr, xi = xr + tr, xi + ti
        pr, pi = _cmul(pr, pi, pr, pi)
    if reverse:
        in_r, in_i = jnp.where(row == 7, c_r, pltpu.roll(xr, 7, 0)), jnp.where(row == 7, c_i, pltpu.roll(xi, 7, 0))
        return in_r, in_i, xr[0:1, :], xi[0:1, :]
    in_r, in_i = jnp.where(row == 0, c_r, pltpu.roll(xr, 1, 0)), jnp.where(row == 0, c_i, pltpu.roll(xi, 1, 0))
    return in_r, in_i, xr[7:8, :], xi[7:8, :]


def _sweeps(ar8, ai8, dr, di, cr, ci, seg, reverse, emit):
    order = range(seg - 1, -1, -1) if reverse else range(seg)
    rows = lambda j: slice(j * 8, (j + 1) * 8)
    fr, fi = jnp.zeros(ar8.shape, F32), jnp.zeros(ar8.shape, F32)
    for j in order:
        tr, ti = _cmul(ar8, ai8, fr, fi)
        fr, fi = tr + dr[rows(j), :], ti + di[rows(j), :]
    s_r, s_i, out_r, out_i = _segment_carries(fr, fi, ar8, ai8, cr[...], ci[...], seg, reverse)
    cr[...] = out_r
    ci[...] = out_i
    for j in order:
        tr, ti = _cmul(ar8, ai8, s_r, s_i)
        s_r, s_i = tr + dr[rows(j), :], ti + di[rows(j), :]
        emit(j, s_r, s_i)


S5_BK, S5_BN = 256, 1024


def _s5_specs(tb, nt, reverse):
    t_of = (lambda s: nt - 1 - s) if reverse else (lambda s: s)
    return dict(
        small=pl.BlockSpec((tb, S5_BK), lambda c, s: (t_of(s), c)),
        wide=pl.BlockSpec((tb, S5_BN), lambda c, s: (t_of(s), c)),
        halo=pl.BlockSpec((8, S5_BN), lambda c, s: (jnp.maximum(t_of(s) * (tb // 8) - 1, 0), c)),
        vec_w=pl.BlockSpec((1, S5_BN), lambda c, s: (0, c)),
        vec_s=pl.BlockSpec((1, S5_BK), lambda c, s: (0, c)),
        w_in=pl.BlockSpec((None, S5_BK, S5_BN), lambda c, s: (c, 0, 0)),
        w_out=pl.BlockSpec((None, S5_BN, S5_BK), lambda c, s: (c, 0, 0)),
    )


def _s5_fwd(u, ar, ai, bre, bim, cre, cim, d_vec):
    t = u.shape[0]
    tb = _scan_rows(t)
    nt, seg, nb = t // tb, tb // 8, bre.shape[0]
    assert 1 << int(math.log2(seg)) == seg

    def body(u_ref, ar_ref, ai_ref, bre_ref, bim_ref, cre_ref, cim_ref, d_ref, sr_ref, si_ref, yp_ref, yg_ref,
             cr, ci, dr_s, di_s):
        @pl.when(pl.program_id(1) == 0)
        def _():
            cr[...] = jnp.zeros_like(cr)
            ci[...] = jnp.zeros_like(ci)

        uv = u_ref[...]
        ub = uv.astype(MXU)
        dr_s[...] = _dot(ub, bre_ref[...])
        di_s[...] = _dot(ub, bim_ref[...])
        ar8 = jnp.broadcast_to(ar_ref[...], (8, S5_BN))
        ai8 = jnp.broadcast_to(ai_ref[...], (8, S5_BN))

        def emit(j, s_r, s_i):
            sr_ref[j * 8:(j + 1) * 8, :] = s_r
            si_ref[j * 8:(j + 1) * 8, :] = s_i

        _sweeps(ar8, ai8, dr_s, di_s, cr, ci, seg, False, emit)
        yp = _dot(sr_ref[...].astype(MXU), cre_ref[...]) - _dot(si_ref[...].astype(MXU), cim_ref[...]) + d_ref[...] * uv
        yp_ref[...] = yp
        yg_ref[...] = _gelu(yp).astype(yg_ref.dtype)

    sp = _s5_specs(tb, nt, False)
    return pl.pallas_call(
        body, name="s5_fwd", grid=(nb, nt),
        in_specs=[sp["small"], sp["vec_w"], sp["vec_w"], sp["w_in"], sp["w_in"], sp["w_out"], sp["w_out"], sp["vec_s"]],
        out_specs=[sp["wide"], sp["wide"], sp["small"], sp["small"]],
        out_shape=[_sds((t, nb * S5_BN), F32), _sds((t, nb * S5_BN), F32), _sds((t, nb * S5_BK), F32),
                   _sds((t, nb * S5_BK), MXU)],
        scratch_shapes=[pltpu.VMEM((1, S5_BN), F32), pltpu.VMEM((1, S5_BN), F32), pltpu.VMEM((tb, S5_BN), F32),
                        pltpu.VMEM((tb, S5_BN), F32)],
        compiler_params=_params("parallel", "arbitrary"),
    )(u, ar, ai, bre, bim, cre, cim, d_vec)


def _s5_bwd(dyp, u, sr, si, ar, ai, cre_t, cim_t, bre_t, bim_t, d_vec):
    t = u.shape[0]
    tb = _scan_rows(t)
    nt, seg, nb = t // tb, tb // 8, cre_t.shape[0]

    def body(dyp_ref, u_ref, pr_ref, pi_ref, hr_ref, hi_ref, ar_ref, ai_ref, cre_ref, cim_ref, bre_ref, bim_ref, d_ref,
             du_ref, gr_ref, gi_ref, dbr_ref, dbi_ref, dcr_ref, dci_ref, dd_ref, cr, ci, dr_s, di_s, lr_s, li_s):
        step = pl.program_id(1)

        @pl.when(step == 0)
        def _():
            for r in (cr, ci, gr_ref, gi_ref, dbr_ref, dbi_ref, dcr_ref, dci_ref, dd_ref):
                r[...] = jnp.zeros_like(r)

        dyv, uv = dyp_ref[...], u_ref[...]
        dyb, ub = dyv.astype(MXU), uv.astype(MXU)
        dr_s[...] = _dot(dyb, cre_ref[...])
        di_s[...] = -_dot(dyb, cim_ref[...])
        ar8 = jnp.broadcast_to(ar_ref[...], (8, S5_BN))
        ai8 = jnp.broadcast_to(-ai_ref[...], (8, S5_BN))
        first_block = step == nt - 1
        row = lax.broadcasted_iota(jnp.int32, (8, S5_BN), 0)
        acc = [jnp.zeros((8, S5_BN), F32), jnp.zeros((8, S5_BN), F32)]

        def emit(j, s_r, s_i):
            lr_s[j * 8:(j + 1) * 8, :] = s_r
            li_s[j * 8:(j + 1) * 8, :] = s_i
            if j > 0:
                p_r, p_i = pr_ref[(j - 1) * 8:j * 8, :], pi_ref[(j - 1) * 8:j * 8, :]
            else:
                halo_r = jnp.where(first_block, 0.0, hr_ref[7:8, :])
                halo_i = jnp.where(first_block, 0.0, hi_ref[7:8, :])
                p_r = jnp.where(row == 0, halo_r, pltpu.roll(pr_ref[(seg - 1) * 8:seg * 8, :], 1, 0))
                p_i = jnp.where(row == 0, halo_i, pltpu.roll(pi_ref[(seg - 1) * 8:seg * 8, :], 1, 0))
            acc[0] = acc[0] + (p_r * s_r + p_i * s_i)
            acc[1] = acc[1] + (p_r * s_i - p_i * s_r)

        _sweeps(ar8, ai8, dr_s, di_s, cr, ci, seg, True, emit)
        gr_ref[...] += jnp.sum(acc[0], axis=0, keepdims=True)
        gi_ref[...] += jnp.sum(acc[1], axis=0, keepdims=True)
        lrb, lib = lr_s[...].astype(MXU), li_s[...].astype(MXU)
        du_ref[...] = (_dot(lrb, bre_ref[...]) + _dot(lib, bim_ref[...]) + d_ref[...] * dyv).astype(du_ref.dtype)
        dbr_ref[...] += _dot(ub, lrb, _TN)
        dbi_ref[...] += _dot(ub, lib, _TN)
        dcr_ref[...] += _dot(pr_ref[...].astype(MXU), dyb, _TN)
        dci_ref[...] += _dot(pi_ref[...].astype(MXU), dyb, _TN)
        dd_ref[...] += jnp.sum(dyv * uv, axis=0, keepdims=True)

    sp = _s5_specs(tb, nt, True)
    return pl.pallas_call(
        body, name="s5_bwd", grid=(nb, nt),
        in_specs=[sp["small"], sp["small"], sp["wide"], sp["wide"], sp["halo"], sp["halo"], sp["vec_w"], sp["vec_w"],
                  sp["w_in"], sp["w_in"], sp["w_out"], sp["w_out"], sp["vec_s"]],
        out_specs=[sp["small"], sp["vec_w"], sp["vec_w"], sp["w_in"], sp["w_in"], sp["w_out"], sp["w_out"], sp["vec_s"]],
        out_shape=[_sds((t, nb * S5_BK), MXU), _sds((1, nb * S5_BN), F32), _sds((1, nb * S5_BN), F32),
                   _sds((nb, S5_BK, S5_BN), F32), _sds((nb, S5_BK, S5_BN), F32), _sds((nb, S5_BN, S5_BK), F32),
                   _sds((nb, S5_BN, S5_BK), F32), _sds((1, nb * S5_BK), F32)],
        scratch_shapes=[pltpu.VMEM((1, S5_BN), F32), pltpu.VMEM((1, S5_BN), F32)] + [pltpu.VMEM((tb, S5_BN), F32)] * 4,
        compiler_params=_params("parallel", "arbitrary"),
    )(dyp, u, sr, si, sr, si, ar, ai, cre_t, cim_t, bre_t, bim_t, d_vec)


def _s5_gate_bwd(d_ycat, gp, ypre):
    t, d = gp.shape
    tb = _tile(t, 256, 8)

    def body(do_ref, gp_ref, yp_ref, o_ref):
        sg = _sigmoid(gp_ref[...])
        o_ref[...] = (do_ref[...] * _gelu(yp_ref[...]) * sg * (1.0 - sg)).astype(o_ref.dtype)

    return pl.pallas_call(
        body, name="s5_gate_bwd", grid=(t // tb,), in_specs=[_row(tb, d), _row(tb, d), _row(tb, d)],
        out_specs=_row(tb, d), out_shape=_sds((t, d), MXU), compiler_params=_params("parallel"),
    )(d_ycat, gp, ypre)


def _ssd_consts():
    head = jnp.arange(HP)[:, None]
    lane = jnp.arange(D_SSD)[None, :]
    expand = ((lane // SSD_HEADDIM) == head).astype(MXU)
    ll = jnp.arange(SSD_CHUNK)
    tri = (ll[:, None] >= ll[None, :]).astype(F32)
    return expand, expand.T, tri, jnp.eye(HP, dtype=F32)


def _ssd_chunk_terms(cp, dtr, par, expand, tri):
    ln = SSD_CHUNK
    sig_cp = _sigmoid(cp)
    xbc = cp * sig_cp
    xs, bm, cm = xbc[:, :D_SSD], xbc[:, D_SSD:D_SSD + SSD_BC], xbc[:, D_SSD + SSD_BC:]
    dt = _softplus(dtr + par[0:1, :])
    a = -jnp.exp(par[1:2, :])
    da = dt * a
    acum = _dot(tri, da, precision=HI)
    acum_t = _dot(da, tri, (((0,), (1,)), ((), ())), precision=HI)
    atot = acum[ln - 1:ln, :]
    dt_e = _dot_split(dt, expand)
    eac_e = _dot_split(jnp.exp(acum), expand)
    dec_e = _dot_split(jnp.exp(atot - acum), expand)
    eat_e = _dot_split(_rows8(jnp.exp(atot)), expand)[0:1, :]
    dsk_e = _dot_split(_rows8(par[2:3, :]), expand)[0:1, :]
    return dict(sig_cp=sig_cp, xs=xs, bm=bm, cm=cm, dt=dt, a=a, acum=acum, acum_t=acum_t, dt_e=dt_e, eac_e=eac_e,
                dec_e=dec_e, eat_e=eat_e, dsk_e=dsk_e)


def _decay_matrix(acum, acum_t, h, mask):
    diff = acum[:, h:h + 1] - acum_t[h:h + 1, :]
    return jnp.where(mask, jnp.exp(jnp.minimum(diff, 0.0)), 0.0)


def _ssd_fwd(proj, conv_w, conv_b, dtr, par, gnorm, consts):
    t = proj.shape[0]
    ln = SSD_CHUNK
    nc = t // ln
    expand, _, tri, _ = consts
    hd2 = 2 * SSD_HEADDIM

    def body(cur_ref, prev_ref, cw_ref, cb_ref, z_ref, dtr_ref, par_ref, g_ref, e_ref, tri_ref,
             out_ref, y_ref, st_ref, cp_ref, state, ext):
        first = pl.program_id(0) == 0

        @pl.when(first)
        def _():
            state[...] = jnp.zeros_like(state)

        st_ref[...] = state[...]
        ext[0:8, :] = jnp.where(first, 0.0, prev_ref[...])
        ext[8:ln + 8, :] = cur_ref[...]
        cpv = jnp.broadcast_to(cb_ref[...], (ln, D_CONV_CH))
        for j in range(SSD_CONV):
            cpv = cpv + cw_ref[SSD_CONV - 1 - j:SSD_CONV - j, :] * ext[8 - j:8 - j + ln, :]
        cp_ref[...] = cpv
        c = _ssd_chunk_terms(cpv, dtr_ref[...], par_ref[...], e_ref[...], tri_ref[...])
        xdt = c["xs"] * c["dt_e"]
        xb, xd = xdt.astype(MXU), (xdt * c["dec_e"]).astype(MXU)
        bb, cb = c["bm"].astype(MXU), c["cm"].astype(MXU)
        mask = lax.broadcasted_iota(jnp.int32, (ln, ln), 0) >= lax.broadcasted_iota(jnp.int32, (ln, ln), 1)
        left = lax.broadcasted_iota(jnp.int32, (ln, hd2), 1) < SSD_HEADDIM
        for g in range(SSD_GROUPS):
            nsl = slice(g * SSD_STATE, (g + 1) * SSD_STATE)
            gsl = slice(g * 256, (g + 1) * 256)
            bg, cg = bb[:, nsl], cb[:, nsl]
            cbm = _dot(cg, bg, _NT)
            st_g = state[:, gsl]
            for pair in range(2):
                h0 = g * 4 + pair * 2
                psl = slice(h0 * SSD_HEADDIM, (h0 + 2) * SSD_HEADDIM)
                m0 = (cbm * _decay_matrix(c["acum"], c["acum_t"], h0, mask)).astype(MXU)
                m1 = (cbm * _decay_matrix(c["acum"], c["acum_t"], h0 + 1, mask)).astype(MXU)
                y_ref[:, psl] = jnp.where(left, _dot(m0, xb[:, psl]), _dot(m1, xb[:, psl]))
            y_ref[:, gsl] += _dot(cg, st_g.astype(MXU)) * c["eac_e"][:, gsl]
            state[:, gsl] = st_g * c["eat_e"][:, gsl] + _dot(bg, xd[:, gsl], _TN)
        y = y_ref[...] + c["dsk_e"] * c["xs"]
        y_ref[...] = y
        y2 = y * _silu(z_ref[...])
        r = lax.rsqrt(jnp.mean(y2 * y2, axis=-1, keepdims=True) + EPS)
        out_ref[...] = (y2 * r * g_ref[...]).astype(out_ref.dtype)

    xbc_block = (D_MAIN - D_CONV_CH) // D_CONV_CH
    return pl.pallas_call(
        body, name="ssd_fwd", grid=(nc,),
        in_specs=[_row(ln, D_CONV_CH, xbc_block),
                  pl.BlockSpec((8, D_CONV_CH), lambda i: (jnp.maximum(i * (ln // 8) - 1, 0), xbc_block)),
                  _const((SSD_CONV, D_CONV_CH)), _const((1, D_CONV_CH)),
                  _row(ln, D_SSD, 1), _row(ln, HP), _const((8, HP)), _const((1, D_SSD)),
                  _const((HP, D_SSD)), _const((ln, ln))],
        out_specs=[_row(ln, D_SSD), _row(ln, D_SSD), pl.BlockSpec((None, SSD_STATE, D_SSD), lambda i: (i, 0, 0)),
                   _row(ln, D_CONV_CH)],
        out_shape=[_sds((t, D_SSD), MXU), _sds((t, D_SSD), F32), _sds((nc, SSD_STATE, D_SSD), F32),
                   _sds((t, D_CONV_CH), F32)],
        scratch_shapes=[pltpu.VMEM((SSD_STATE, D_SSD), F32), pltpu.VMEM((ln + 8, D_CONV_CH), F32)],
        compiler_params=_params("arbitrary"),
    )(proj, proj, conv_w, conv_b.reshape(1, D_CONV_CH), proj, dtr, par, gnorm.reshape(1, D_SSD), expand, tri)


def _ssd_bwd(proj, conv_pre, conv_w, dtr, par, gnorm, y, states, d_ycat, consts):
    t = proj.shape[0]
    ln = SSD_CHUNK
    nc = t // ln
    expand, expand_t, tri, eye = consts
    hd2 = 2 * SSD_HEADDIM

    def body(cp_ref, z_ref, dtr_ref, par_ref, g_ref, y_ref, st_ref, do_ref, e_ref, et_ref, tri_ref, eye_ref,
             cur_ref, prev_ref, cw_ref,
             dxbc_ref, dz_ref, ddt_ref, dg_ref, dpar_ref, dcw_ref, dcb_ref,
             dstate, dx_buf, lane_buf, tot_buf, colsum, dcp_ref, ext, dext):
        @pl.when(pl.program_id(0) == 0)
        def _():
            dstate[...] = jnp.zeros_like(dstate)
            dg_ref[...] = jnp.zeros_like(dg_ref)
            dpar_ref[...] = jnp.zeros_like(dpar_ref)
            dcw_ref[...] = jnp.zeros_like(dcw_ref)
            dcb_ref[...] = jnp.zeros_like(dcb_ref)
            dext[ln:ln + 8, :] = jnp.zeros((8, D_CONV_CH), F32)

        cpv, et_v, tri_v, par_v = cp_ref[...], et_ref[...], tri_ref[...], par_ref[...]
        c = _ssd_chunk_terms(cpv, dtr_ref[...], par_v, e_ref[...], tri_v)
        xs = c["xs"]
        zv, yv, dov = z_ref[...], y_ref[...], do_ref[...]
        sig_z = _sigmoid(zv)
        sz = zv * sig_z
        y2 = yv * sz
        r = lax.rsqrt(jnp.mean(y2 * y2, axis=-1, keepdims=True) + EPS)
        yh = y2 * r
        dg_ref[...] += jnp.sum(dov * yh, axis=0, keepdims=True)
        dyh = dov * g_ref[...]
        dy2 = r * (dyh - yh * jnp.mean(dyh * yh, axis=-1, keepdims=True))
        dz_ref[...] = (dy2 * yv * _silu_grad(zv, sig_z)).astype(dz_ref.dtype)
        dy = dy2 * sz

        xdt = xs * c["dt_e"]
        xdf = xdt * c["dec_e"]
        xb, xd = xdt.astype(MXU), xdf.astype(MXU)
        bb, cb = c["bm"].astype(MXU), c["cm"].astype(MXU)
        dyb, dye = dy.astype(MXU), (dy * c["eac_e"]).astype(MXU)
        mask = lax.broadcasted_iota(jnp.int32, (ln, ln), 0) >= lax.broadcasted_iota(jnp.int32, (ln, ln), 1)
        left = lax.broadcasted_iota(jnp.int32, (ln, hd2), 1) < SSD_HEADDIM
        lane_hp = lax.broadcasted_iota(jnp.int32, (ln, HP), 1)
        d_acum = jnp.zeros((ln, HP), F32)
        colsum[...] = jnp.zeros_like(colsum)
        tot_buf[...] = jnp.zeros_like(tot_buf)
        for g in range(SSD_GROUPS):
            nsl = slice(g * SSD_STATE, (g + 1) * SSD_STATE)
            gsl = slice(g * 256, (g + 1) * 256)
            bg, cg = bb[:, nsl], cb[:, nsl]
            cbm = _dot(cg, bg, _NT)
            st_g = st_ref[:, gsl]
            dst_g = dstate[:, gsl]
            stb, dstb = st_g.astype(MXU), dst_g.astype(MXU)
            y_off = _dot(cg, stb)
            bds = _dot(bg, dstb)
            dcb = jnp.zeros((ln, ln), F32)
            for pair in range(2):
                h0 = g * 4 + pair * 2
                psl = slice(h0 * SSD_HEADDIM, (h0 + 2) * SSD_HEADDIM)
                xp, dyp = xb[:, psl], dyb[:, psl]
                dxp = []
                for k in range(2):
                    h = h0 + k
                    lm = _decay_matrix(c["acum"], c["acum_t"], h, mask)
                    mm = cbm * lm
                    half = left if k == 0 else jnp.logical_not(left)
                    dm = _dot(jnp.where(half, dyp, jnp.zeros_like(dyp)), xp, _NT)
                    dcb = dcb + dm * lm
                    gm = dm * mm
                    d_acum = d_acum + jnp.where(lane_hp == h, jnp.sum(gm, axis=1, keepdims=True), 0.0)
                    colsum[h:h + 1, :] = jnp.sum(gm, axis=0, keepdims=True)
                    dxp.append(_dot(mm.astype(MXU), dyp, _TN))
                dx_buf[:, psl] = jnp.where(left, dxp[0], dxp[1])
            dx_buf[:, gsl] += bds * c["dec_e"][:, gsl]
            dcbb = dcb.astype(MXU)
            dc_g = _dot(dcbb, bg) + _dot(dye[:, gsl], stb, _NT)
            db_g = _dot(dcbb, cg, _TN) + _dot(xd[:, gsl], dstb, _NT)
            dcp_ref[:, D_SSD + g * SSD_STATE:D_SSD + (g + 1) * SSD_STATE] = db_g
            dcp_ref[:, D_SSD + SSD_BC + g * SSD_STATE:D_SSD + SSD_BC + (g + 1) * SSD_STATE] = dc_g
            dec_term = xdf[:, gsl] * bds
            lane_buf[:, gsl] = dy[:, gsl] * y_off * c["eac_e"][:, gsl] - dec_term
            tot_buf[0:1, gsl] = (jnp.sum(st_g * dst_g, axis=0, keepdims=True) * c["eat_e"][:, gsl]
                                 + jnp.sum(dec_term, axis=0, keepdims=True))
            dstate[:, gsl] = dst_g * c["eat_e"][:, gsl] + _dot(cg, dye[:, gsl], _TN)
        dx_tot = dx_buf[...]
        d_acum = d_acum + _dot_split(lane_buf[...], et_v) - _dot(colsum[...], eye_ref[...], _TN, precision=HI)
        d_atot = _dot_split(tot_buf[...], et_v)[0:1, :]
        row_hp = lax.broadcasted_iota(jnp.int32, (ln, HP), 0)
        d_acum = d_acum + jnp.where(row_hp == ln - 1, d_atot, 0.0)
        d_da = _dot(tri_v, d_acum, _TN, precision=HI)
        d_dt = d_da * c["a"] + _dot_split(dx_tot * xs, et_v)
        d_dtr = d_dt * _sigmoid(dtr_ref[...] + par_v[0:1, :])
        ddt_ref[...] = d_dtr
        dpar_ref[0:1, :] += jnp.sum(d_dtr, axis=0, keepdims=True)
        dpar_ref[1:2, :] += jnp.sum(d_da * c["dt"], axis=0, keepdims=True) * c["a"]
        dpar_ref[2:3, :] += _dot_split(_rows8(jnp.sum(dy * xs, axis=0, keepdims=True)), et_v)[0:1, :]
        dcp_ref[:, 0:D_SSD] = dx_tot * c["dt_e"] + dy * c["dsk_e"]
        dcv = dcp_ref[...] * _silu_grad(cpv, c["sig_cp"])
        ext[0:8, :] = jnp.where(pl.program_id(0) == nc - 1, 0.0, prev_ref[...])
        ext[8:ln + 8, :] = cur_ref[...]
        dext[0:ln, :] = dcv
        dxbc = jnp.zeros((ln, D_CONV_CH), F32)
        for j in range(SSD_CONV):
            dxbc = dxbc + cw_ref[SSD_CONV - 1 - j:SSD_CONV - j, :] * dext[j:j + ln, :]
            dcw_ref[SSD_CONV - 1 - j:SSD_CONV - j, :] += jnp.sum(dcv * ext[8 - j:8 - j + ln, :], axis=0, keepdims=True)
        dxbc_ref[...] = dxbc.astype(dxbc_ref.dtype)
        dcb_ref[...] += jnp.sum(dcv, axis=0, keepdims=True)
        dext[ln:ln + 8, :] = dcv[0:8, :]

    xbc_block = (D_MAIN - D_CONV_CH) // D_CONV_CH
    rev = lambda i: (nc - 1 - i, 0)
    rev1 = lambda i: (nc - 1 - i, 1)
    return pl.pallas_call(
        body, name="ssd_bwd", grid=(nc,),
        in_specs=[pl.BlockSpec((ln, D_CONV_CH), rev), pl.BlockSpec((ln, D_SSD), rev1), pl.BlockSpec((ln, HP), rev),
                  _const((8, HP)), _const((1, D_SSD)), pl.BlockSpec((ln, D_SSD), rev),
                  pl.BlockSpec((None, SSD_STATE, D_SSD), lambda i: (nc - 1 - i, 0, 0)),
                  pl.BlockSpec((ln, D_SSD), rev1),
                  _const((HP, D_SSD)), _const((D_SSD, HP)), _const((ln, ln)), _const((HP, HP)),
                  pl.BlockSpec((ln, D_CONV_CH), lambda i: (nc - 1 - i, xbc_block)),
                  pl.BlockSpec((8, D_CONV_CH), lambda i: (jnp.maximum((nc - 1 - i) * (ln // 8) - 1, 0), xbc_block)),
                  _const((SSD_CONV, D_CONV_CH))],
        out_specs=[pl.BlockSpec((ln, D_CONV_CH), rev), pl.BlockSpec((ln, D_SSD), rev), pl.BlockSpec((ln, HP), rev),
                   _const((1, D_SSD)), _const((8, HP)), _const((SSD_CONV, D_CONV_CH)), _const((1, D_CONV_CH))],
        out_shape=[_sds((t, D_CONV_CH), MXU), _sds((t, D_SSD), MXU), _sds((t, HP), F32), _sds((1, D_SSD), F32),
                   _sds((8, HP), F32), _sds((SSD_CONV, D_CONV_CH), F32), _sds((1, D_CONV_CH), F32)],
        scratch_shapes=[pltpu.VMEM((SSD_STATE, D_SSD), F32), pltpu.VMEM((ln, D_SSD), F32), pltpu.VMEM((ln, D_SSD), F32),
                        pltpu.VMEM((8, D_SSD), F32), pltpu.VMEM((HP, ln), F32), pltpu.VMEM((ln, D_CONV_CH), F32),
                        pltpu.VMEM((ln + 8, D_CONV_CH), F32), pltpu.VMEM((ln + 8, D_CONV_CH), F32)],
        compiler_params=_params("arbitrary"),
    )(conv_pre, proj, dtr, par, gnorm.reshape(1, D_SSD), y, states, d_ycat, expand, expand_t, tri, eye, proj, proj, conv_w)


def _softmax_rows(s):
    e = jnp.exp(s - jnp.max(s, axis=-1, keepdims=True))
    return e * _recip(jnp.sum(e, axis=-1, keepdims=True))


def _attn_fwd(q, k, v):
    t, d = q.shape
    mlen = k.shape[0]
    tq = _tile(t, 512, 8)
    scale = XA_HEAD_DIM ** -0.5

    def body(q_ref, k_ref, v_ref, o_ref):
        for h in range(XA_HEADS):
            sl = slice(h * XA_HEAD_DIM, (h + 1) * XA_HEAD_DIM)
            p = _softmax_rows(_dot(q_ref[:, sl], k_ref[:, sl], _NT) * scale)
            o_ref[:, sl] = _dot(p.astype(MXU), v_ref[:, sl]).astype(o_ref.dtype)

    return pl.pallas_call(
        body, name="xattn_fwd", grid=(t // tq,),
        in_specs=[_row(tq, d), _const((mlen, d)), _const((mlen, d))], out_specs=_row(tq, d),
        out_shape=_sds((t, d), MXU), compiler_params=_params("parallel"),
    )(q, k, v)


def _attn_bwd(q, k, v, do):
    t, d = q.shape
    mlen = k.shape[0]
    tq = _tile(t, 512, 8)
    scale = XA_HEAD_DIM ** -0.5

    def body(q_ref, k_ref, v_ref, do_ref, dq_ref, dk_ref, dv_ref):
        @pl.when(pl.program_id(0) == 0)
        def _():
            dk_ref[...] = jnp.zeros_like(dk_ref)
            dv_ref[...] = jnp.zeros_like(dv_ref)

        for h in range(XA_HEADS):
            sl = slice(h * XA_HEAD_DIM, (h + 1) * XA_HEAD_DIM)
            qh, kh, vh, doh = q_ref[:, sl], k_ref[:, sl], v_ref[:, sl], do_ref[:, sl]
            p = _softmax_rows(_dot(qh, kh, _NT) * scale)
            dp = _dot(doh, vh, _NT)
            dv_ref[:, sl] += _dot(p.astype(MXU), doh, _TN)
            ds = (p * (dp - jnp.sum(p * dp, axis=-1, keepdims=True)) * scale).astype(MXU)
            dq_ref[:, sl] = _dot(ds, kh).astype(dq_ref.dtype)
            dk_ref[:, sl] += _dot(ds, qh, _TN)

    return pl.pallas_call(
        body, name="xattn_bwd", grid=(t // tq,),
        in_specs=[_row(tq, d), _const((mlen, d)), _const((mlen, d)), _row(tq, d)],
        out_specs=[_row(tq, d), _const((mlen, d)), _const((mlen, d))],
        out_shape=[_sds((t, d), MXU), _sds((mlen, d), F32), _sds((mlen, d), F32)],
        compiler_params=_params("arbitrary"),
    )(q, k, v, do)


def _lane_view(a):
    if a.ndim >= 2 and a.shape[-1] >= 128:
        return a.reshape(-1, a.shape[-1])
    if a.size % 128 == 0:
        return a.reshape(-1, 128)
    return a.reshape(1, -1)


def _adamw(w, g, m, v, *, name):
    shape = w.shape
    w2, g2, m2, v2 = (_lane_view(a) for a in (w, g.reshape(shape), m, v))
    r, c = w2.shape
    tr = _tile(r, max(8, (1 << 18) // max(c, 128)), 8)

    def body(w_ref, g_ref, m_ref, v_ref, d_ref, mo_ref, vo_ref):
        gv = g_ref[...]
        mn = ADAM_B1 * m_ref[...] + (1.0 - ADAM_B1) * gv
        vn = ADAM_B2 * v_ref[...] + (1.0 - ADAM_B2) * (gv * gv)
        m_hat = mn / (1.0 - ADAM_B1 ** ADAM_STEP)
        v_hat = vn / (1.0 - ADAM_B2 ** ADAM_STEP)
        d_ref[...] = -ADAM_LR * (m_hat / (jnp.sqrt(v_hat) + ADAM_EPS) + ADAM_WD * w_ref[...])
        mo_ref[...] = mn
        vo_ref[...] = vn

    outs = pl.pallas_call(
        body, name=name, grid=(r // tr,), in_specs=[_row(tr, c)] * 4, out_specs=[_row(tr, c)] * 3,
        out_shape=[_sds((r, c), F32)] * 3, compiler_params=_params("parallel"),
    )(w2, g2, m2, v2)
    return tuple(o.reshape(shape) for o in outs)


_HBM = pl.BlockSpec(memory_space=pltpu.HBM)


N_CHIPS = 4
CHIPS = ((0, 0), (0, 1), (1, 0), (1, 1))


def _pair_exchange(x, *, name):
    def body(x_ref, o_ref, send_sems, recv_sems):
        xx, yy, cc = (lax.axis_index(a) for a in AXES)
        copies = [
            pltpu.make_async_remote_copy(
                src_ref=x_ref.at[4 * px + 2 * py + (1 - cc)], dst_ref=o_ref.at[k], send_sem=send_sems.at[k],
                recv_sem=recv_sems.at[k], device_id=(xx, yy, 1 - cc), device_id_type=pl.DeviceIdType.MESH)
            for k, (px, py) in enumerate(CHIPS)]
        for cp in copies:
            cp.start()
        for cp in copies:
            cp.wait_recv()
        for cp in copies:
            cp.wait_send()

    return pl.pallas_call(
        body, name=name, in_specs=[_HBM], out_specs=_HBM, out_shape=_sds((N_CHIPS,) + tuple(x.shape[1:]), x.dtype),
        scratch_shapes=[pltpu.SemaphoreType.DMA((N_CHIPS,)), pltpu.SemaphoreType.DMA((N_CHIPS,))],
    )(x)


def _pair_add(x, got, *, name):
    _, r, c = x.shape
    tr = _tile(r, max(PAD_ROWS, (1 << 17) // c), PAD_ROWS)

    def body(x_ref, g_ref, o_ref):
        mine = jnp.where(lax.axis_index("c") == 0, x_ref[0].astype(F32), x_ref[1].astype(F32))
        o_ref[...] = (mine + g_ref[...].astype(F32)).astype(o_ref.dtype)

    return pl.pallas_call(
        body, name=name, grid=(N_CHIPS, r // tr),
        in_specs=[pl.BlockSpec((None, 2, tr, c), lambda k, i: (k, 0, i, 0)), pl.BlockSpec((None, tr, c), lambda k, i: (k, i, 0))],
        out_specs=pl.BlockSpec((None, tr, c), lambda k, i: (k, i, 0)), out_shape=_sds((N_CHIPS, r, c), x.dtype),
        compiler_params=_params("parallel", "parallel"),
    )(x.reshape(N_CHIPS, 2, r, c), got)


def _chip_exchange(x, *, name):
    def body(x_ref, o_ref, send_sems, recv_sems, local_sem):
        xx, yy, cc = (lax.axis_index(a) for a in AXES)
        mine = 2 * xx + yy
        local = pltpu.make_async_copy(x_ref.at[mine], o_ref.at[mine], local_sem)
        local.start()
        sends = []
        for j, (px, py) in enumerate([(1 - xx, yy), (xx, 1 - yy), (1 - xx, 1 - yy)]):
            cp = pltpu.make_async_remote_copy(
                src_ref=x_ref.at[2 * px + py], dst_ref=o_ref.at[mine], send_sem=send_sems.at[j], recv_sem=recv_sems.at[j],
                device_id=(px, py, cc), device_id_type=pl.DeviceIdType.MESH)
            cp.start()
            sends.append(cp)
        for j, (px, py) in enumerate([(1 - xx, yy), (xx, 1 - yy), (1 - xx, 1 - yy)]):
            pltpu.make_async_remote_copy(
                src_ref=x_ref.at[2 * px + py], dst_ref=o_ref.at[2 * px + py], send_sem=send_sems.at[j],
                recv_sem=recv_sems.at[j], device_id=(px, py, cc), device_id_type=pl.DeviceIdType.MESH).wait_recv()
        for cp in sends:
            cp.wait_send()
        local.wait()

    return pl.pallas_call(
        body, name=name, in_specs=[_HBM], out_specs=_HBM, out_shape=_sds(x.shape, x.dtype),
        scratch_shapes=[pltpu.SemaphoreType.DMA((N_CHIPS - 1,)), pltpu.SemaphoreType.DMA((N_CHIPS - 1,)),
                        pltpu.SemaphoreType.DMA(())],
    )(x)


def _reduce_scatter(x, *, name):
    chip_sums = _pair_add(x, _pair_exchange(x, name=name + "_pair"), name=name + "_pair_add")
    return _sum_slots(_chip_exchange(chip_sums, name=name + "_chips"), name=name + "_sum")


def _all_gather(x, *, name):
    def body(x_ref, o_ref, send_sems, recv_sems, local_sem):
        xx, yy, cc = (lax.axis_index(a) for a in AXES)
        me, sibling = (xx, yy, cc), (xx, yy, 1 - cc)
        chips = [(1 - xx, yy), (xx, 1 - yy), (1 - xx, 1 - yy)]

        def slot(px, py, pc):
            return o_ref.at[4 * px + 2 * py + pc]

        def copy(k, block, to, src=None):
            return pltpu.make_async_remote_copy(
                src_ref=slot(*block) if src is None else src, dst_ref=slot(*block), send_sem=send_sems.at[k],
                recv_sem=recv_sems.at[k], device_id=to, device_id_type=pl.DeviceIdType.MESH)

        local = pltpu.make_async_copy(x_ref, slot(*me), local_sem)
        local.start()
        first = [copy(0, me, sibling, src=x_ref)] + [copy(1 + j, me, (*chip, cc), src=x_ref) for j, chip in enumerate(chips)]
        for cp in first:
            cp.start()
        passed = [copy(4 + j, (*chip, cc), sibling) for j, chip in enumerate(chips)]
        for j, chip in enumerate(chips):
            copy(1 + j, (*chip, cc), me).wait_recv()
            passed[j].start()
        copy(0, sibling, me).wait_recv()
        for j, chip in enumerate(chips):
            copy(4 + j, (*chip, 1 - cc), me).wait_recv()
        for cp in first + passed:
            cp.wait_send()
        local.wait()

    return pl.pallas_call(
        body, name=name, in_specs=[_HBM], out_specs=_HBM, out_shape=_sds((N_DEV,) + tuple(x.shape), x.dtype),
        scratch_shapes=[pltpu.SemaphoreType.DMA((N_DEV - 1,)), pltpu.SemaphoreType.DMA((N_DEV - 1,)),
                        pltpu.SemaphoreType.DMA(())],
    )(x)


def _sum_slots(x, *, name):
    n, r, c = x.shape
    tr = _tile(r, max(PAD_ROWS, (1 << 17) // c), PAD_ROWS)

    def body(x_ref, o_ref):
        acc = x_ref[0].astype(F32)
        for d in range(1, n):
            acc = acc + x_ref[d].astype(F32)
        o_ref[...] = acc

    return pl.pallas_call(
        body, name=name, grid=(r // tr,), in_specs=[pl.BlockSpec((n, tr, c), lambda i: (0, i, 0))],
        out_specs=_row(tr, c), out_shape=_sds((r, c), F32), compiler_params=_params("parallel"),
    )(x)


def _s5_layouts(bbar_r, bbar_i, c_re, c_im):
    eye = jnp.eye(S5_GPB, dtype=F32)

    def b_blocks(bbar):
        bb = bbar.reshape(S5_NB, S5_GPB, S5_STATE, S5_GROUP)
        return jnp.einsum("jgph,gk->jghkp", bb, eye).reshape(S5_NB, S5_GPB * S5_GROUP, S5_GPB * S5_STATE)

    def c_blocks(cc):
        c4 = cc.reshape(S5_NB, S5_GPB, S5_GROUP, S5_STATE)
        return jnp.einsum("jghp,gk->jgpkh", c4, eye).reshape(S5_NB, S5_GPB * S5_STATE, S5_GPB * S5_GROUP)

    bre, bim, cre, cim = b_blocks(bbar_r), b_blocks(bbar_i), c_blocks(c_re), c_blocks(c_im)
    cast = lambda a: a.astype(MXU)
    sw = lambda a: jnp.swapaxes(a, 1, 2).astype(MXU)
    return dict(bre=cast(bre), bim=cast(bim), cre=cast(cre), cim=cast(cim), bre_t=sw(bre), bim_t=sw(bim), cre_t=sw(cre),
                cim_t=sw(cim))


def _b_diag(db):
    d5 = db.reshape(S5_NB, S5_GPB, S5_GROUP, S5_GPB, S5_STATE)
    diag = jnp.stack([d5[:, g, :, g, :] for g in range(S5_GPB)], axis=1)
    return jnp.swapaxes(diag, 2, 3).reshape(S5_GROUPS, S5_STATE * S5_GROUP)


def _c_diag(dc):
    d5 = dc.reshape(S5_NB, S5_GPB, S5_STATE, S5_GPB, S5_GROUP)
    diag = jnp.stack([d5[:, g, :, g, :] for g in range(S5_GPB)], axis=1)
    return jnp.swapaxes(diag, 2, 3).reshape(S5_GROUPS, S5_GROUP, S5_STATE)


def _head_rows(*vecs):
    par = jnp.zeros((8, HP), F32)
    for i, v in enumerate(vecs):
        par = par.at[i, :SSD_HEADS].set(v.astype(F32))
    return par


def _add(acc, r):
    return (acc + r,)


def _layer_fwd(x, mem, w, consts):
    s = {"x": x}
    rep = consts["rep"]
    s["h1"] = h1 = _rmsnorm_fwd(x, w["norm_mix"], name="norm_mix_fwd")
    s["proj"] = proj = _mm(h1, w["w_main"], tm=2048, name="in_proj")
    s["dtr"] = dtr = _mm(h1, w["w_dt"], name="dt_proj")
    ar, ai, ldt = w["s5_a_re"], w["s5_a_im"], w["s5_log_dt"].reshape(S5_GROUPS, 1)
    br, bi = w["s5_b_re"].reshape(S5_GROUPS, -1), w["s5_b_im"].reshape(S5_GROUPS, -1)
    abar_r, abar_i, bbar_r, bbar_i = _s5_prep(ar, ai, ldt, br, bi, rep)
    s["abar"] = abar = (abar_r.reshape(1, S5_CH), abar_i.reshape(1, S5_CH))
    s["lay"] = lay = _s5_layouts(bbar_r, bbar_i, w["s5_c_re"], w["s5_c_im"])
    s["u"] = u = _interleave_rows(proj[:, :D_S5])
    s["sr"], s["si"], s["ypre"], s["yg"] = _s5_fwd(u, *abar, lay["bre"], lay["bim"], lay["cre"], lay["cim"],
                                                   w["s5_d"].reshape(1, D_S5))
    ypre, yg = s["ypre"], s["yg"]
    s["gp"], out_s5 = _mm(yg, w["s5_w_glu"], extras=[ypre], epi=lambda acc, yp: (acc, _gelu(yp) * _sigmoid(acc)),
                          out_dtypes=(F32, MXU), name="s5_glu")
    out_s5 = _deinterleave_rows(out_s5)
    s["par"] = par = _head_rows(w["ssd_dt_bias"], w["ssd_a_log"], w["ssd_d"])
    out_ssd, s["y_ssd"], s["states"], s["conv_pre"] = _ssd_fwd(proj, w["ssd_conv_w"], w["ssd_conv_b"], dtr, par,
                                                               w["ssd_norm"], consts["ssd"])
    s["ycat"] = ycat = jnp.concatenate([out_s5, out_ssd], axis=1)
    s["x1"] = x1 = _mm(ycat, w["w_out"], extras=[x], epi=_add, name="out_proj")
    s["hq"] = hq = _rmsnorm_fwd(x1, w["norm_xattn"], name="norm_xattn_fwd")
    s["mn"] = mn = _rmsnorm_fwd(mem, w["norm_mem"], name="norm_mem_fwd")
    s["q"] = q = _mm(hq, w["xa_wq"], out_dtypes=(MXU,), tm=2048, name="xa_q")
    s["k"] = k = _mm(mn, w["xa_wk"], out_dtypes=(MXU,), name="xa_k")
    s["v"] = v = _mm(mn, w["xa_wv"], out_dtypes=(MXU,), name="xa_v")
    s["o"] = o = _attn_fwd(q, k, v)
    s["x2"] = x2 = _mm(o, w["xa_wo"], extras=[x1], epi=_add, name="xa_o")
    s["hm"] = hm = _rmsnorm_fwd(x2, w["norm_mlp"], name="norm_mlp_fwd")
    s["act"] = _mm(hm, w["mlp_w1"], epi=lambda acc: (jnp.square(jnp.maximum(acc, 0.0)),), out_dtypes=(MXU,),
                   tm=2048, name="mlp_up")
    x3 = _mm(s["act"], w["mlp_w2"], extras=[x2], epi=_add, name="mlp_down")
    return x3, s


def _layer_bwd(dx3, dx3m, mem, w, s, consts):
    g = {}
    rep = consts["rep"]
    wire = (MXU,)
    d_a = _mm(dx3m, w["mlp_w2"], tb=True, extras=[s["act"]],
              epi=lambda acc, act: (acc * (2.0 * jnp.sqrt(act.astype(F32))),), out_dtypes=(MXU,), tm=2048,
              name="mlp_down_dx")
    g["mlp_w2"] = _mm(s["act"], dx3m, ta=True, out_dtypes=wire, name="mlp_down_dw")
    g["mlp_w1"] = _mm(s["hm"], d_a, ta=True, out_dtypes=wire, name="mlp_up_dw")
    norm_bwd = dict(epi=_norm_bwd_epilogue, out_dtypes=(F32, MXU), n_colsum=1, tm=512)
    dx2, dx2m, g["norm_mlp"] = _mm(d_a, w["mlp_w1"], tb=True, extras=[s["x2"], dx3], vecs=[w["norm_mlp"].reshape(1, -1)],
                                   name="mlp_up_dx", **norm_bwd)
    d_o = _mm(dx2m, w["xa_wo"], tb=True, out_dtypes=(MXU,), tm=2048, name="xa_o_dx")
    g["xa_wo"] = _mm(s["o"], dx2m, ta=True, out_dtypes=wire, name="xa_o_dw")
    dq, dk, dv = _attn_bwd(s["q"], s["k"], s["v"], d_o)
    g["xa_wq"] = _mm(s["hq"], dq, ta=True, out_dtypes=wire, name="xa_q_dw")
    dx1, dx1m, g["norm_xattn"] = _mm(dq, w["xa_wq"], tb=True, extras=[s["x1"], dx2],
                                     vecs=[w["norm_xattn"].reshape(1, -1)], name="xa_q_dx", **norm_bwd)
    g["xa_wk"] = _mm(s["mn"], dk, ta=True, out_dtypes=wire, name="xa_k_dw")
    g["xa_wv"] = _mm(s["mn"], dv, ta=True, out_dtypes=wire, name="xa_v_dw")
    d_mn_v = _mm(dv, w["xa_wv"], tb=True, name="xa_v_dx")
    d_mn = _mm(dk, w["xa_wk"], tb=True, extras=[d_mn_v], epi=_add, name="xa_k_dx")
    _, _, g["norm_mem"] = _rmsnorm_bwd(mem, w["norm_mem"], d_mn, None, name="norm_mem_bwd")
    d_ycat = _mm(dx1m, w["w_out"], tb=True, name="out_proj_dx")
    g["w_out"] = _mm(s["ycat"], dx1m, ta=True, out_dtypes=wire, name="out_proj_dw")
    lay, proj, ypre, u = s["lay"], s["proj"], s["ypre"], s["u"]
    d_os5 = _interleave_rows(d_ycat[:, :D_S5])
    d_gp = _s5_gate_bwd(d_os5, s["gp"], ypre)
    g["s5_w_glu"] = _mm(s["yg"], d_gp, ta=True, out_dtypes=wire, name="s5_glu_dw")
    d_ypre = _mm(d_gp, w["s5_w_glu"], tb=True, extras=[d_os5, s["gp"], ypre],
                 epi=lambda acc, do, gp, yp: ((acc + do * _sigmoid(gp)) * _gelu_grad(yp),), name="s5_glu_dx")
    du, d_abar_r, d_abar_i, db_re, db_im, dc_re, dc_im, d_d = _s5_bwd(
        d_ypre, u, s["sr"], s["si"], *s["abar"], lay["cre_t"], lay["cim_t"], lay["bre_t"], lay["bim_t"],
        w["s5_d"].reshape(1, D_S5))
    d_bbar_r, d_bbar_i = _b_diag(db_re), _b_diag(db_im)
    g["s5_c_re"], g["s5_c_im"] = _c_diag(dc_re), -_c_diag(dc_im)
    g["s5_d"] = d_d.reshape(S5_GROUPS, S5_GROUP)
    du = _deinterleave_rows(du)
    ar, ai, ldt = w["s5_a_re"], w["s5_a_im"], w["s5_log_dt"].reshape(S5_GROUPS, 1)
    br, bi = w["s5_b_re"].reshape(S5_GROUPS, -1), w["s5_b_im"].reshape(S5_GROUPS, -1)
    d_ar, d_ai, d_ldt, d_br, d_bi = _s5_prep_bwd(
        ar, ai, ldt, br, bi, rep, d_abar_r.reshape(S5_GROUPS, S5_STATE), d_abar_i.reshape(S5_GROUPS, S5_STATE),
        d_bbar_r, d_bbar_i)
    g["s5_a_re"], g["s5_a_im"], g["s5_log_dt"] = d_ar, d_ai, d_ldt.reshape(S5_GROUPS)
    g["s5_b_re"] = d_br.reshape(S5_GROUPS, S5_STATE, S5_GROUP)
    g["s5_b_im"] = d_bi.reshape(S5_GROUPS, S5_STATE, S5_GROUP)
    d_xbc, dz, d_dtr, g["ssd_norm"], d_par, g["ssd_conv_w"], g["ssd_conv_b"] = _ssd_bwd(
        proj, s["conv_pre"], w["ssd_conv_w"], s["dtr"], s["par"], w["ssd_norm"], s["y_ssd"], s["states"], d_ycat,
        consts["ssd"])
    g["ssd_dt_bias"], g["ssd_a_log"], g["ssd_d"] = (d_par[i, :SSD_HEADS] for i in range(3))
    d_proj = jnp.concatenate([du, dz, d_xbc], axis=1)
    g_main = _mm(s["h1"], d_proj, ta=True, out_dtypes=wire, name="in_proj_dw")
    g_dt = _mm(s["h1"], d_dtr, ta=True, out_dtypes=wire, name="dt_proj_dw")
    g["w_in"] = jnp.concatenate([g_main, g_dt[:, :SSD_HEADS]], axis=1)
    d_h1_dt = _mm(d_dtr, w["w_dt"], tb=True, name="dt_proj_dx")
    dx, dxm, g["norm_mix"] = _mm(
        d_proj, w["w_main"], tb=True, extras=[d_h1_dt, s["x"], dx1], vecs=[w["norm_mix"].reshape(1, -1)],
        name="in_proj_dx", **{**norm_bwd, "epi": lambda acc, dt_part, *rest: _norm_bwd_epilogue(acc + dt_part, *rest)})
    return dx, dxm, g


LAYER_WEIGHTS = ("norm_mix", "w_in", "s5_a_re", "s5_a_im", "s5_log_dt", "s5_b_re", "s5_b_im", "s5_c_re", "s5_c_im", "s5_d",
                 "s5_w_glu", "ssd_conv_w", "ssd_conv_b", "ssd_dt_bias", "ssd_a_log", "ssd_d", "ssd_norm", "w_out",
                 "norm_xattn", "norm_mem", "xa_wq", "xa_wk", "xa_wv", "xa_wo", "norm_mlp", "mlp_w1", "mlp_w2")
WEIGHTS = LAYER_WEIGHTS + ("norm_final",)


def _local_step(x, mem, target, weights):
    consts = {
        "ssd": _ssd_consts(),
        "rep": (jnp.arange(S5_STATE)[:, None] == jnp.arange(S5_STATE * S5_GROUP)[None, :] // S5_GROUP).astype(F32),
    }
    layers = []
    for l in range(DEPTH):
        w = {n: weights[n][l] for n in LAYER_WEIGHTS}
        w_in = w["w_in"]
        w["w_main"] = w_in[:, :D_MAIN]
        w["w_dt"] = jnp.pad(w_in[:, D_MAIN:], ((0, 0), (0, HP - SSD_HEADS)))
        layers.append(w)
    saved = []
    for l in range(DEPTH):
        x, s = _layer_fwd(x, mem, layers[l], consts)
        saved.append(s)
    loss, dx, dxm, g_final = _loss_head(x, weights["norm_final"], target)
    grads = [None] * DEPTH
    for l in reversed(range(DEPTH)):
        dx, dxm, grads[l] = _layer_bwd(dx, dxm, mem, layers[l], saved[l], consts)
    out = {n: jnp.stack([grads[l][n].reshape(weights[n].shape[1:]) for l in range(DEPTH)]) for n in LAYER_WEIGHTS}
    out["norm_final"] = g_final.reshape(weights["norm_final"].shape)
    return loss, dx, out


SHARDED = {"w_in": 2, "s5_w_glu": 1, "ssd_conv_w": 2, "w_out": 1, "xa_wq": 1, "xa_wk": 1, "xa_wv": 1, "xa_wo": 1,
           "mlp_w1": 2, "mlp_w2": 1}
EXACT = ("ssd_conv_w",)
ROW_EXCHANGE = tuple(n for n, ax in SHARDED.items() if ax == 1)
OWN_EXCHANGE = tuple(n for n in SHARDED if n not in ROW_EXCHANGE)
REPLICATED = tuple(n for n in WEIGHTS if n not in SHARDED)
LANES = 128
PAD_ROWS = 16


def _gather_weights(local):
    def assemble(n, seg):
        shp, ax = local[n].shape, SHARDED[n]
        return jnp.moveaxis(seg, 0, ax).reshape(*shp[:ax], N_DEV * shp[ax], *shp[ax + 1:])

    full = {}
    for n in OWN_EXCHANGE:
        payload = local[n] if n in EXACT else local[n].astype(MXU)
        full[n] = assemble(n, _all_gather(payload, name="gather_" + n))
    got = _all_gather(jnp.concatenate([local[n].astype(MXU) for n in ROW_EXCHANGE], axis=1), name="gather_row_sharded")
    off = 0
    for n in ROW_EXCHANGE:
        rows = local[n].shape[1]
        full[n] = assemble(n, got[:, :, off:off + rows])
        off += rows
    return full


def _scatter_grads(grads, local_shapes):
    def shards(n):
        shp, ax = local_shapes[n], SHARDED[n]
        gfull = grads[n].reshape(*shp[:ax], N_DEV, shp[ax], *shp[ax + 1:])
        return jnp.moveaxis(gfull, ax, 0).astype(MXU)

    out = {}
    for n in OWN_EXCHANGE:
        shp = local_shapes[n]
        out[n] = _reduce_scatter(shards(n).reshape(N_DEV, -1, shp[-1]), name="scatter_" + n).reshape(shp)
    payload = jnp.concatenate([shards(n) for n in ROW_EXCHANGE], axis=2)
    summed = _reduce_scatter(payload.reshape(N_DEV, -1, payload.shape[-1]), name="scatter_row_sharded")
    summed = summed.reshape(payload.shape[1:])
    off = 0
    for n in ROW_EXCHANGE:
        rows = local_shapes[n][1]
        out[n] = summed[:, off:off + rows]
        off += rows
    return out


def _allreduce_small(loss, grads):
    parts = [loss.reshape(-1)[:1]] + [grads[n].reshape(-1) for n in REPLICATED]
    flat = jnp.concatenate(parts)
    quantum = N_DEV * PAD_ROWS * LANES
    flat = jnp.pad(flat, (0, -flat.shape[0] % quantum))
    mine = _reduce_scatter(flat.reshape(N_DEV, -1, LANES), name="reduce_small_grads")
    summed = _all_gather(mine, name="gather_small_grads").reshape(-1)
    out, off = {}, 1
    for n in REPLICATED:
        size = grads[n].size
        out[n] = summed[off:off + size].reshape(grads[n].shape)
        off += size
    return summed[0], out


def kernel(x, mem, norm_mix, w_in, s5_a_re, s5_a_im, s5_log_dt, s5_b_re, s5_b_im, s5_c_re, s5_c_im, s5_d, s5_w_glu, ssd_conv_w, ssd_conv_b, ssd_dt_bias, ssd_a_log, ssd_d, ssd_norm, w_out, norm_xattn, norm_mem, xa_wq, xa_wk, xa_wv, xa_wo, norm_mlp, mlp_w1, mlp_w2, norm_final, loss_target, m_norm_mix, m_w_in, m_s5_a_re, m_s5_a_im, m_s5_log_dt, m_s5_b_re, m_s5_b_im, m_s5_c_re, m_s5_c_im, m_s5_d, m_s5_w_glu, m_ssd_conv_w, m_ssd_conv_b, m_ssd_dt_bias, m_ssd_a_log, m_ssd_d, m_ssd_norm, m_w_out, m_norm_xattn, m_norm_mem, m_xa_wq, m_xa_wk, m_xa_wv, m_xa_wo, m_norm_mlp, m_mlp_w1, m_mlp_w2, m_norm_final, v_norm_mix, v_w_in, v_s5_a_re, v_s5_a_im, v_s5_log_dt, v_s5_b_re, v_s5_b_im, v_s5_c_re, v_s5_c_im, v_s5_d, v_s5_w_glu, v_ssd_conv_w, v_ssd_conv_b, v_ssd_dt_bias, v_ssd_a_log, v_ssd_d, v_ssd_norm, v_w_out, v_norm_xattn, v_norm_mem, v_xa_wq, v_xa_wk, v_xa_wv, v_xa_wo, v_norm_mlp, v_mlp_w1, v_mlp_w2, v_norm_final):
    args = locals()
    local = {n: args[n] for n in WEIGHTS}
    full = dict(local)
    full.update(_gather_weights(local))
    loss, grad_x, grads = _local_step(x[0], mem[0], loss_target[0], full)
    loss, g_small = _allreduce_small(loss, grads)
    g_all = _scatter_grads(grads, {n: local[n].shape for n in SHARDED})
    g_all.update(g_small)
    delta, new_m, new_v = {}, {}, {}
    for n in WEIGHTS:
        delta[n], new_m[n], new_v[n] = _adamw(local[n], g_all[n], args["m_" + n], args["v_" + n], name="adamw_" + n)
    return (loss, grad_x[None], *[g_all[n] for n in WEIGHTS], *[delta[n] for n in WEIGHTS],
            *[new_m[n] for n in WEIGHTS], *[new_v[n] for n in WEIGHTS])
```
